```python
import jax, jax.numpy as jnp
from jax import lax
import numpy as np

D_MODEL = 2048
BATCH = 32
SEQ = 256
DEPTH = 1
DEC_BATCH = 4
DEC_SEQ = 1024
PAST_LEN = 512

GRID_W = 64
N_ATT_HEADS = 16
HEAD_DIM = 64
ATT_WIDTH = N_ATT_HEADS * HEAD_DIM
LRU_WIDTH = D_MODEL - ATT_WIDTH
LRU_BLOCKS = 16
LRU_BLOCK = LRU_WIDTH // LRU_BLOCKS
LRU_C = 8.0
CONV_W = 4
CONV_LEFT = 2
WIN_H_MAX = 8
WIN_W = 16
N_EXPERTS = 16
CAPACITY_FACTOR = 2
D_FF_EXPERT = 2048
Q_BLOCK = 128
EPS = 1e-6
NEG_INF = -1e30
IN_COLS = 3 * ATT_WIDTH + 2 * LRU_WIDTH
IN_SPLITS = [ATT_WIDTH, 2 * ATT_WIDTH, 3 * ATT_WIDTH, 3 * ATT_WIDTH + LRU_WIDTH]

kernel_name = 'hybrid_na_rglru_ec_diffusion_step'


def rms_norm(x, w):
    xf = x.astype(jnp.float32)
    y = xf * lax.rsqrt(jnp.mean(xf * xf, axis=-1, keepdims=True) + EPS)
    return (y * w.astype(jnp.float32)).astype(x.dtype)


def adaln(cvec, w_mod, b_mod):
    m = jnp.einsum('bd,dc->bc', jax.nn.silu(cvec), w_mod) + b_mod
    return jnp.split(m, 6, axis=-1)


def modulate(h, shift, scale):
    return h * (1.0 + scale[:, None, :]) + shift[:, None, :]


def qkv_heads(q, k, v, q_norm, k_norm):
    B, T, _ = q.shape
    sh = lambda t: t.reshape(B, T, N_ATT_HEADS, HEAD_DIM).transpose(0, 2, 1, 3)
    return rms_norm(sh(q), q_norm), rms_norm(sh(k), k_norm), sh(v)


def context_attention(q, k, v):
    B, H, S, dh = q.shape
    nb = S // Q_BLOCK
    scale = HEAD_DIM ** -0.5
    qb = q.reshape(B, H, nb, Q_BLOCK, dh).transpose(2, 0, 1, 3, 4)

    def block(qi):
        s = jnp.einsum('bhqd,bhkd->bhqk', qi, k).astype(jnp.float32) * scale
        p = jax.nn.softmax(s, axis=-1).astype(v.dtype)
        return jnp.einsum('bhqk,bhkd->bhqd', p, v)

    o = lax.map(block, qb)
    return o.transpose(1, 2, 0, 3, 4).reshape(B, H, S, dh)


def neighbourhood_attention(q, k, v, k_ctx, v_ctx, rpb):
    B, H, T, dh = q.shape
    rows = T // GRID_W
    kh = min(WIN_H_MAX, rows)
    scale = HEAD_DIM ** -0.5
    qg = q.reshape(B, H, rows, GRID_W, dh)
    kg = k.reshape(B, H, rows, GRID_W, dh)
    vg = v.reshape(B, H, rows, GRID_W, dh)
    r = jnp.arange(rows)
    row_start = jnp.clip(r - kh // 2, 0, rows - kh)
    row_idx = row_start[:, None] + jnp.arange(kh)[None, :]
    k_win = kg[:, :, row_idx].reshape(B, H, rows, kh * GRID_W, dh)
    v_win = vg[:, :, row_idx].reshape(B, H, rows, kh * GRID_W, dh)
    col = jnp.arange(GRID_W)
    col_start = jnp.clip(col - WIN_W // 2, 0, GRID_W - WIN_W)
    col_ok = (col[None, :] >= col_start[:, None]) & (col[None, :] < col_start[:, None] + WIN_W)
    dr_i = (row_idx - r[:, None]) + (WIN_H_MAX - 1)
    dc_i = jnp.clip(col[None, :] - col[:, None], -(WIN_W - 1), WIN_W - 1) + (WIN_W - 1)
    bias = rpb[:, dr_i[:, None, :, None], dc_i[None, :, None, :]].astype(jnp.float32)
    bias = jnp.where(col_ok[None, None, :, None, :], bias, NEG_INF)
    bias = bias.reshape(H, rows, GRID_W, kh * GRID_W)
    s_win = jnp.einsum('bhrqd,bhrkd->bhrqk', qg, k_win).astype(jnp.float32) * scale + bias[None]
    s_ctx = jnp.einsum('bhrqd,bhpd->bhrqp', qg, k_ctx).astype(jnp.float32) * scale
    p = jax.nn.softmax(jnp.concatenate([s_win, s_ctx], axis=-1), axis=-1).astype(v.dtype)
    nw = kh * GRID_W
    o = (jnp.einsum('bhrqk,bhrkd->bhrqd', p[..., :nw], v_win)
         + jnp.einsum('bhrqp,bhpd->bhrqd', p[..., nw:], v_ctx))
    return o.reshape(B, H, T, dh)


def centred_conv(x, w, b):
    T = x.shape[1]
    xp = jnp.pad(x, ((0, 0), (CONV_LEFT, CONV_W - 1 - CONV_LEFT), (0, 0)))
    out = xp[:, 0:T] * w[0]
    for j in range(1, CONV_W):
        out = out + xp[:, j:j + T] * w[j]
    return out + b


def rglru_coeffs(x, w_a, b_a, w_x, b_x, lam):
    B, T, _ = x.shape
    xb = x.reshape(B, T, LRU_BLOCKS, LRU_BLOCK)
    r = jax.nn.sigmoid(jnp.einsum('btnc,ncd->btnd', xb, w_a.astype(jnp.float32)).reshape(B, T, LRU_WIDTH) + b_a)
    i = jax.nn.sigmoid(jnp.einsum('btnc,ncd->btnd', xb, w_x.astype(jnp.float32)).reshape(B, T, LRU_WIDTH) + b_x)
    log_a = -LRU_C * r * jax.nn.softplus(-lam.astype(jnp.float32))
    a = jnp.exp(log_a)
    u = jnp.sqrt(-jnp.expm1(2.0 * log_a)) * (i * x)
    return a, u


def linear_scan(a, u, h0, reverse):
    def step(h, au):
        at, ut = au
        h = at * h + ut
        return h, h
    hT, hs = lax.scan(step, h0, (a.transpose(1, 0, 2), u.transpose(1, 0, 2)), reverse=reverse)
    return hs.transpose(1, 0, 2), hT


def rglru_mixer(xr, gr, conv_w, conv_b, w_a, b_a, w_x, b_x, lam, h0_f, h0_b):
    xc = centred_conv(xr, conv_w, conv_b).astype(jnp.float32)
    a_f, u_f = rglru_coeffs(xc, w_a[0], b_a[0], w_x[0], b_x[0], lam[0])
    a_b, u_b = rglru_coeffs(xc, w_a[1], b_a[1], w_x[1], b_x[1], lam[1])
    h_f, hT_f = linear_scan(a_f, u_f, h0_f.astype(jnp.float32), False)
    h_b, hT_b = linear_scan(a_b, u_b, h0_b.astype(jnp.float32), True)
    y = (h_f + h_b).astype(xr.dtype) * jax.nn.gelu(gr)
    return y, hT_f, hT_b


def expert_choice_ffn(h, w_router, w_gate, w_up, w_down):
    B, T, D = h.shape
    n = B * T
    cap = CAPACITY_FACTOR * n // N_EXPERTS
    xt = h.reshape(n, D)
    aff = jax.nn.softmax(jnp.einsum('nd,de->ne', xt, w_router).astype(jnp.float32), axis=-1)
    g, idx = lax.top_k(aff.T, cap)
    xe = xt[idx]
    he = jax.nn.silu(jnp.einsum('ecd,edf->ecf', xe, w_gate)) * jnp.einsum('ecd,edf->ecf', xe, w_up)
    ye = jnp.einsum('ecf,efd->ecd', he, w_down) * g[..., None].astype(h.dtype)
    out = jnp.zeros_like(xt).at[idx.reshape(-1)].add(ye.reshape(-1, D))
    return out.reshape(B, T, D)


def _layer(x, cvec, ctx_k, ctx_v, h0_f, h0_b, w_mod, b_mod, norm_mix, w_in, q_norm, k_norm, rpb,
           conv_w, conv_b, lru_w_a, lru_b_a, lru_w_x, lru_b_x, lru_lambda, w_out, norm_ffn,
           w_router, w_gate, w_up, w_down, latent):
    B, T, _ = x.shape
    sa, ca, ga, sf, cf, gf = adaln(cvec, w_mod, b_mod)
    h = modulate(rms_norm(x, norm_mix), sa, ca)
    proj = jnp.einsum('btd,dc->btc', h, w_in)
    q, k, v, xr, gr = jnp.split(proj, IN_SPLITS, axis=-1)
    q, k, v = qkv_heads(q, k, v, q_norm, k_norm)
    if latent:
        att = neighbourhood_attention(q, k, v, ctx_k, ctx_v, rpb)
    else:
        att = context_attention(q, k, v)
    att = att.transpose(0, 2, 1, 3).reshape(B, T, ATT_WIDTH)
    lru, hT_f, hT_b = rglru_mixer(xr, gr, conv_w, conv_b, lru_w_a, lru_b_a, lru_w_x, lru_b_x,
                                  lru_lambda, h0_f, h0_b)
    mix = jnp.einsum('btc,cd->btd', jnp.concatenate([att, lru], axis=-1), w_out)
    x = x + ga[:, None, :] * mix
    h = modulate(rms_norm(x, norm_ffn), sf, cf)
    x = x + gf[:, None, :] * expert_choice_ffn(h, w_router, w_gate, w_up, w_down)
    return x, k, v, hT_f, hT_b


def setup_inputs(seed: int = 0) -> dict:
    key = jax.random.key(seed)
    ks = jax.random.split(key, 32)
    f32 = jnp.float32
    nrm = lambda k, shape, s: s * jax.random.normal(k, shape, f32)
    a0 = jax.random.uniform(ks[21], (DEPTH, 2, LRU_WIDTH), f32, 0.9, 0.999)
    return {
        'x_prompt': nrm(ks[0], (BATCH, SEQ, D_MODEL), 1.0),
        'x_sample': nrm(ks[1], (DEC_BATCH, DEC_SEQ, D_MODEL), 1.0),
        'cache_k': nrm(ks[2], (DEC_BATCH, DEPTH, N_ATT_HEADS, PAST_LEN, HEAD_DIM), 1.0),
        'cache_v': nrm(ks[3], (DEC_BATCH, DEPTH, N_ATT_HEADS, PAST_LEN, HEAD_DIM), 1.0),
        'state_fwd': nrm(ks[4], (DEC_BATCH, DEPTH, LRU_WIDTH), 0.5),
        'state_bwd': nrm(ks[5], (DEC_BATCH, DEPTH, LRU_WIDTH), 0.5),
        'c': nrm(ks[6], (DEC_BATCH, D_MODEL), 1.0),
        'c_ctx': nrm(ks[7], (D_MODEL,), 1.0),
        'w_mod': nrm(ks[8], (DEPTH, D_MODEL, 6 * D_MODEL), 0.5 * D_MODEL ** -0.5),
        'b_mod': nrm(ks[9], (DEPTH, 6 * D_MODEL), 0.02),
        'norm_mix': 1.0 + nrm(ks[10], (DEPTH, D_MODEL), 0.02),
        'w_in': nrm(ks[11], (DEPTH, D_MODEL, IN_COLS), D_MODEL ** -0.5),
        'q_norm': 1.0 + nrm(ks[12], (DEPTH, HEAD_DIM), 0.02),
        'k_norm': 1.0 + nrm(ks[13], (DEPTH, HEAD_DIM), 0.02),
        'rpb': nrm(ks[14], (DEPTH, N_ATT_HEADS, 2 * WIN_H_MAX - 1, 2 * WIN_W - 1), 0.1),
        'conv_w': nrm(ks[15], (DEPTH, CONV_W, LRU_WIDTH), 0.5),
        'conv_b': nrm(ks[16], (DEPTH, LRU_WIDTH), 0.02),
        'lru_w_a': nrm(ks[17], (DEPTH, 2, LRU_BLOCKS, LRU_BLOCK, LRU_BLOCK), LRU_BLOCK ** -0.5),
        'lru_b_a': nrm(ks[18], (DEPTH, 2, LRU_WIDTH), 0.02),
        'lru_w_x': nrm(ks[19], (DEPTH, 2, LRU_BLOCKS, LRU_BLOCK, LRU_BLOCK), LRU_BLOCK ** -0.5),
        'lru_b_x': nrm(ks[20], (DEPTH, 2, LRU_WIDTH), 0.02),
        'lru_lambda': jnp.log(a0) - jnp.log1p(-a0),
        'w_out': nrm(ks[22], (DEPTH, D_MODEL, D_MODEL), D_MODEL ** -0.5),
        'norm_ffn': 1.0 + nrm(ks[23], (DEPTH, D_MODEL), 0.02),
        'w_router': nrm(ks[24], (DEPTH, D_MODEL, N_EXPERTS), D_MODEL ** -0.5),
        'w_gate': nrm(ks[25], (DEPTH, N_EXPERTS, D_MODEL, D_FF_EXPERT), D_MODEL ** -0.5),
        'w_up': nrm(ks[26], (DEPTH, N_EXPERTS, D_MODEL, D_FF_EXPERT), D_MODEL ** -0.5),
        'w_down': nrm(ks[27], (DEPTH, N_EXPERTS, D_FF_EXPERT, D_MODEL), D_FF_EXPERT ** -0.5),
    }


def reference(x_prompt, x_sample, cache_k, cache_v, state_fwd, state_bwd, c, c_ctx,
              w_mod, b_mod, norm_mix, w_in, q_norm, k_norm, rpb, conv_w, conv_b,
              lru_w_a, lru_b_a, lru_w_x, lru_b_x, lru_lambda, w_out, norm_ffn,
              w_router, w_gate, w_up, w_down):
    def layer_weights(l):
        return (w_mod[l], b_mod[l], norm_mix[l], w_in[l], q_norm[l], k_norm[l], rpb[l],
                conv_w[l], conv_b[l], lru_w_a[l], lru_b_a[l], lru_w_x[l], lru_b_x[l],
                lru_lambda[l], w_out[l], norm_ffn[l], w_router[l], w_gate[l], w_up[l], w_down[l])

    y_prompt = x_prompt
    zeros = jnp.zeros((x_prompt.shape[0], LRU_WIDTH), jnp.float32)
    ks_, vs_, fs_, bs_ = [], [], [], []
    for l in range(DEPTH):
        y_prompt, k_l, v_l, f_l, b_l = _layer(y_prompt, c_ctx[None, :], None, None, zeros, zeros,
                                              *layer_weights(l), latent=False)
        ks_.append(k_l)
        vs_.append(v_l)
        fs_.append(f_l)
        bs_.append(b_l)
    new_cache_k = jnp.stack(ks_, axis=1)
    new_cache_v = jnp.stack(vs_, axis=1)
    new_state_fwd = jnp.stack(fs_, axis=1)
    new_state_bwd = jnp.stack(bs_, axis=1)

    y_sample = x_sample
    for l in range(DEPTH):
        y_sample = _layer(y_sample, c, cache_k[:, l], cache_v[:, l], state_fwd[:, l], state_bwd[:, l],
                          *layer_weights(l), latent=True)[0]

    return (y_prompt, y_sample, new_cache_k, new_cache_v, new_state_fwd, new_state_bwd)
```

```python
import functools

import jax
import jax.numpy as jnp
from jax import lax
from jax.experimental import pallas as pl
from jax.experimental.pallas import tpu as pltpu

f32 = jnp.float32
bf16 = jnp.bfloat16
i32 = jnp.int32

D = 2048
N_HEADS = 16
HEAD_DIM = 64
ATT_W = N_HEADS * HEAD_DIM
LRU_W = D - ATT_W
LRU_BLOCK = 64
LRU_C = 8.0
GRID_W = 64
WIN_H = 8
WIN_W = 16
N_EXPERTS = 16
CAPACITY_FACTOR = 2
D_FF = 2048
EPS = 1e-6
NEG_INF = -1e30
IN_COLS = 3 * ATT_W + 2 * LRU_W

LANES = 128
SUBLANES = 8
MXU_DIM = 256
UNSELECTED = -(1 << 30)

_MIB = 1024 * 1024


def _cparams(sem, vmem_mib):
    return pltpu.CompilerParams(dimension_semantics=sem, vmem_limit_bytes=vmem_mib * _MIB)


def _mods_body(c_ref, w_ref, b_ref, o_ref):
    c = c_ref[...]
    s = (c * jax.nn.sigmoid(c)).astype(bf16)
    o_ref[...] = jnp.dot(s, w_ref[...].astype(bf16), preferred_element_type=f32) + b_ref[...]


def _adaln_mods(cvecs, w_mod, b_mod):
    tn = 1024
    return pl.pallas_call(
        _mods_body,
        grid=(6 * D // tn,),
        in_specs=[pl.BlockSpec((SUBLANES, D), lambda j: (0, 0)),
                  pl.BlockSpec((D, tn), lambda j: (0, j)),
                  pl.BlockSpec((1, tn), lambda j: (0, j))],
        out_specs=pl.BlockSpec((SUBLANES, tn), lambda j: (0, j)),
        out_shape=jax.ShapeDtypeStruct((SUBLANES, 6 * D), f32),
        compiler_params=_cparams(("arbitrary",), 40),
        name="adaln_mods",
    )(cvecs, w_mod, b_mod)


_TN_IN = 512


def _inproj_body(x_ref, mods_ref, nw_ref, w_ref, qn_ref, kn_ref, g_ref, o_ref, h_ref):
    j = pl.program_id(1)

    @pl.when(j == 0)
    def _():
        x = x_ref[...]
        ms = jnp.mean(x * x, axis=-1, keepdims=True)
        y = x * lax.rsqrt(ms + EPS) * nw_ref[...]
        m = mods_ref[0]
        h_ref[...] = (y * (1.0 + m[1:2]) + m[0:1]).astype(bf16)

    acc = jnp.dot(h_ref[...], w_ref[...].astype(bf16), preferred_element_type=f32)
    n_qk_tiles = 2 * ATT_W // _TN_IN

    @pl.when(j < n_qk_tiles)
    def _():
        sq = acc * acc
        hi = sq.astype(bf16)
        lo = (sq - hi.astype(f32)).astype(bf16)
        g = g_ref[...]
        ms = (jnp.dot(hi, g, preferred_element_type=f32)
              + jnp.dot(lo, g, preferred_element_type=f32))
        nw = jnp.where(j < n_qk_tiles // 2, qn_ref[...], kn_ref[...])
        o_ref[...] = acc * lax.rsqrt(ms + EPS) * nw

    @pl.when(j >= n_qk_tiles)
    def _():
        o_ref[...] = acc


def _inproj(x2d, mods, norm_w, w_in, qn_t, kn_t, gmat, per_tile_rows):
    n = x2d.shape[0]
    tm = 1024
    row = (lambda i: 1 + i) if per_tile_rows else (lambda i: 0)
    return pl.pallas_call(
        _inproj_body,
        grid=(n // tm, IN_COLS // _TN_IN),
        in_specs=[pl.BlockSpec((tm, D), lambda i, j: (i, 0)),
                  pl.BlockSpec((1, 6, D), lambda i, j: (row(i), 0, 0)),
                  pl.BlockSpec((1, D), lambda i, j: (0, 0)),
                  pl.BlockSpec((D, _TN_IN), lambda i, j: (0, j)),
                  pl.BlockSpec((1, _TN_IN), lambda i, j: (0, 0)),
                  pl.BlockSpec((1, _TN_IN), lambda i, j: (0, 0)),
                  pl.BlockSpec((_TN_IN, _TN_IN), lambda i, j: (0, 0))],
        out_specs=pl.BlockSpec((tm, _TN_IN), lambda i, j: (i, j)),
        out_shape=jax.ShapeDtypeStruct((n, IN_COLS), f32),
        scratch_shapes=[pltpu.VMEM((tm, D), bf16)],
        compiler_params=_cparams(("arbitrary", "arbitrary"), 52),
        name="in_projection",
    )(x2d, mods, norm_w, w_in, qn_t, kn_t, gmat)


def _softmax_pv(s_list, v_list):
    m = s_list[0].max(axis=-1, keepdims=True)
    for s in s_list[1:]:
        m = jnp.maximum(m, s.max(axis=-1, keepdims=True))
    l = None
    o = None
    for s, v in zip(s_list, v_list):
        e = jnp.exp(s - m)
        ls = e.sum(axis=-1, keepdims=True)
        os_ = jnp.dot(e.astype(bf16), v, preferred_element_type=f32)
        l = ls if l is None else l + ls
        o = os_ if o is None else o + os_
    return o, l


def _nt_dot(a, b):
    return lax.dot_general(a, b, (((1,), (1,)), ((), ())), preferred_element_type=f32)


def _ctx_attn_body(q_ref, k_ref, v_ref, o_ref):
    scale = HEAD_DIM ** -0.5
    lane = lax.broadcasted_iota(i32, (1, LANES), 1)
    for p in range(ATT_W // LANES):
        cols = slice(p * LANES, (p + 1) * LANES)
        q = q_ref[:, cols]
        k = k_ref[:, cols].astype(bf16)
        v = v_ref[:, cols]
        out = None
        for hh in range(2):
            hm = (lane >= hh * HEAD_DIM) & (lane < (hh + 1) * HEAD_DIM)
            qh = jnp.where(hm, q, 0.0).astype(bf16)
            vh = jnp.where(hm, v, 0.0).astype(bf16)
            s = _nt_dot(qh, k) * scale
            o, l = _softmax_pv([s], [vh])
            o = o / l
            out = o if out is None else out + o
        o_ref[:, cols] = out.astype(bf16)


def _ctx_attention(proj, batch, seq):
    n = proj.shape[0]
    nqb = ATT_W // ATT_W
    return pl.pallas_call(
        _ctx_attn_body,
        grid=(batch,),
        in_specs=[pl.BlockSpec((seq, ATT_W), lambda b: (b, 0)),
                  pl.BlockSpec((seq, ATT_W), lambda b: (b, nqb)),
                  pl.BlockSpec((seq, ATT_W), lambda b: (b, 2 * nqb))],
        out_specs=pl.BlockSpec((seq, ATT_W), lambda b: (b, 0)),
        out_shape=jax.ShapeDtypeStruct((n, ATT_W), bf16),
        compiler_params=_cparams(("arbitrary",), 32),
        name="context_attention",
    )(proj, proj, proj)


_NA_ROWS = 16
_NA_QROWS = 8
_NA_KROWS = 12


def _na_bias(rpb):
    r = jnp.arange(_NA_ROWS)
    row_start = jnp.clip(r - WIN_H // 2, 0, _NA_ROWS - WIN_H)
    col = jnp.arange(GRID_W)
    col_start = jnp.clip(col - WIN_W // 2, 0, GRID_W - WIN_W)
    col_ok = (col[None, :] >= col_start[:, None]) & (col[None, :] < col_start[:, None] + WIN_W)
    dc = jnp.clip(col[None, :] - col[:, None], -(WIN_W - 1), WIN_W - 1) + (WIN_W - 1)
    halves = []
    for j in range(2):
        rq = j * _NA_QROWS + jnp.arange(_NA_QROWS)
        rk = j * (_NA_ROWS - _NA_KROWS) + jnp.arange(_NA_KROWS)
        rs = row_start[rq]
        row_ok = (rk[None, :] >= rs[:, None]) & (rk[None, :] < rs[:, None] + WIN_H)
        dr = jnp.clip(rk[None, :] - rq[:, None] + (WIN_H - 1), 0, 2 * WIN_H - 2)
        b = rpb[:, dr[:, None, :, None], dc[None, :, None, :]].astype(f32)
        ok = row_ok[:, None, :, None] & col_ok[None, :, None, :]
        b = jnp.where(ok[None], b, NEG_INF)
        halves.append(b.reshape(N_HEADS, _NA_QROWS * GRID_W, _NA_KROWS * GRID_W))
    return jnp.stack(halves, axis=1)


def _lat_attn_body(q_ref, k_ref, v_ref, ck_ref, cv_ref, bias_ref, o_ref):
    scale = HEAD_DIM ** -0.5
    lane = lax.broadcasted_iota(i32, (1, LANES), 1)
    nq = _NA_QROWS * GRID_W
    nk = _NA_KROWS * GRID_W
    kstep = (_NA_ROWS - _NA_KROWS) * GRID_W
    ck = ck_ref[0].astype(bf16)
    cv = cv_ref[0]
    for j in range(2):
        q = q_ref[j * nq:(j + 1) * nq, :]
        k = k_ref[j * kstep:j * kstep + nk, :].astype(bf16)
        v = v_ref[j * kstep:j * kstep + nk, :]
        out = None
        for hh in range(2):
            hm = (lane >= hh * HEAD_DIM) & (lane < (hh + 1) * HEAD_DIM)
            qh = jnp.where(hm, q, 0.0).astype(bf16)
            s_win = _nt_dot(qh, k) * scale + bias_ref[hh, j]
            s_ctx = _nt_dot(qh, ck) * scale
            vh = jnp.where(hm, v, 0.0).astype(bf16)
            cvh = jnp.where(hm, cv, 0.0).astype(bf16)
            o, l = _softmax_pv([s_win, s_ctx], [vh, cvh])
            o = o / l
            out = o if out is None else out + o
        o_ref[j * nq:(j + 1) * nq, :] = out.astype(bf16)


def _lat_attention(proj, ck, cv, bias, batch, seq):
    n = proj.shape[0]
    past = ck.shape[1]
    npair = ATT_W // LANES
    nq = _NA_QROWS * GRID_W
    nk = _NA_KROWS * GRID_W
    return pl.pallas_call(
        _lat_attn_body,
        grid=(npair, batch),
        in_specs=[pl.BlockSpec((seq, LANES), lambda p, b: (b, p)),
                  pl.BlockSpec((seq, LANES), lambda p, b: (b, npair + p)),
                  pl.BlockSpec((seq, LANES), lambda p, b: (b, 2 * npair + p)),
                  pl.BlockSpec((1, past, LANES), lambda p, b: (b, 0, p)),
                  pl.BlockSpec((1, past, LANES), lambda p, b: (b, 0, p)),
                  pl.BlockSpec((2, 2, nq, nk), lambda p, b: (p, 0, 0, 0))],
        out_specs=pl.BlockSpec((seq, LANES), lambda p, b: (b, p)),
        out_shape=jax.ShapeDtypeStruct((n, ATT_W), bf16),
        compiler_params=_cparams(("arbitrary", "arbitrary"), 48),
        name="neighbourhood_attention",
    )(proj, proj, proj, ck, cv, bias)


_LRU_TC = 256
_GATE_ROWS = 256
_LRU_HALVES = _LRU_TC // LANES


def _gelu_tanh(x):
    c = 0.7978845608028654
    return x * (0.5 * (1.0 + jnp.tanh(c * (x + 0.044715 * (x * x * x)))))


def _lru_body(xr_ref, gr_ref, w_ref, cw_ref, cb_ref, ba_ref, bx_ref, lam_ref, h0f_ref, h0b_ref,
              y_ref, htf_ref, htb_ref, xc_ref, af_ref, uf_ref, ab_ref, ub_ref, *, T, nb):
    nslab = _LRU_HALVES * nb // SUBLANES
    R = nb * T
    x = xr_ref[...]
    t = lax.broadcasted_iota(i32, (R, 1), 0) & (T - 1)
    cw = cw_ref[...]
    xc = jnp.where(t >= 2, pltpu.roll(x, 2, 0), 0.0) * cw[0:1]
    xc = xc + jnp.where(t >= 1, pltpu.roll(x, 1, 0), 0.0) * cw[1:2]
    xc = xc + x * cw[2:3]
    xc = xc + jnp.where(t < T - 1, pltpu.roll(x, R - 1, 0), 0.0) * cw[3:4]
    xc_ref[...] = xc + cb_ref[...]

    nl = -lam_ref[...]
    sp = jnp.maximum(nl, 0.0) + jnp.log1p(jnp.exp(-jnp.abs(nl)))
    w = w_ref[0]
    dirs = ((af_ref, uf_ref), (ab_ref, ub_ref))
    for b in range(nb):
        for ci in range(T // _GATE_ROWS):
            r0 = b * T + ci * _GATE_ROWS
            xcc = xc_ref[r0:r0 + _GATE_ROWS, :]
            z = jnp.dot(xcc.astype(bf16), w, preferred_element_type=f32)
            for d, (a_ref, u_ref) in enumerate(dirs):
                c0 = d * 2 * _LRU_TC
                r = jax.nn.sigmoid(z[:, c0:c0 + _LRU_TC] + ba_ref[d:d + 1])
                g = jax.nn.sigmoid(z[:, c0 + _LRU_TC:c0 + 2 * _LRU_TC] + bx_ref[d:d + 1])
                log_a = (-LRU_C * r) * sp[d:d + 1]
                a = jnp.exp(log_a)
                u = jnp.sqrt(jnp.tanh(-log_a) * (1.0 + a * a)) * (g * xcc)
                for hf in range(_LRU_HALVES):
                    s, sub = divmod(hf * nb + b, SUBLANES)
                    rows = pl.ds(ci * _GATE_ROWS * SUBLANES + sub, _GATE_ROWS, stride=SUBLANES)
                    a_ref[s, rows, :] = a[:, hf * LANES:(hf + 1) * LANES]
                    u_ref[s, rows, :] = u[:, hf * LANES:(hf + 1) * LANES]

    def step(i, carry):
        rf = pl.ds(pl.multiple_of(i * SUBLANES, SUBLANES), SUBLANES)
        rb = pl.ds(pl.multiple_of((T - 1 - i) * SUBLANES, SUBLANES), SUBLANES)
        new = []
        for s in range(nslab):
            hf_ = af_ref[s, rf, :] * carry[2 * s] + uf_ref[s, rf, :]
            uf_ref[s, rf, :] = hf_
            hb_ = ab_ref[s, rb, :] * carry[2 * s + 1] + ub_ref[s, rb, :]
            ub_ref[s, rb, :] = hb_
            new += [hf_, hb_]
        return tuple(new)

    init = []
    for s in range(nslab):
        init += [h0f_ref[:, s * LANES:(s + 1) * LANES], h0b_ref[:, s * LANES:(s + 1) * LANES]]
    fin = lax.fori_loop(0, T, step, tuple(init), unroll=8)
    for s in range(nslab):
        htf_ref[:, s * LANES:(s + 1) * LANES] = fin[2 * s]
        htb_ref[:, s * LANES:(s + 1) * LANES] = fin[2 * s + 1]

    for b in range(nb):
        for hf in range(_LRU_HALVES):
            s, sub = divmod(hf * nb + b, SUBLANES)
            rows = pl.ds(sub, T, stride=SUBLANES)
            hs = uf_ref[s, rows, :] + ub_ref[s, rows, :]
            g = gr_ref[b * T:(b + 1) * T, hf * LANES:(hf + 1) * LANES]
            y_ref[b * T:(b + 1) * T, hf * LANES:(hf + 1) * LANES] = (
                hs * _gelu_tanh(g)).astype(bf16)


def _rglru(proj, wcat, conv_w, conv_b, b_a, b_x, lam, h0f, h0b, T, nb):
    n = proj.shape[0]
    nslab = _LRU_HALVES * nb // SUBLANES
    W = nslab * LANES
    R = nb * T
    nct = LRU_W // _LRU_TC
    xr0 = 3 * ATT_W // _LRU_TC
    gr0 = xr0 + nct
    chan = lambda k: pl.BlockSpec((k, _LRU_TC), lambda g, j: (0, j))
    slab = pl.BlockSpec((SUBLANES, W), lambda g, j: (g, j))
    body = functools.partial(_lru_body, T=T, nb=nb)
    return pl.pallas_call(
        body,
        grid=(n // R, nct),
        in_specs=[pl.BlockSpec((R, _LRU_TC), lambda g, j: (g, xr0 + j)),
                  pl.BlockSpec((R, _LRU_TC), lambda g, j: (g, gr0 + j)),
                  pl.BlockSpec((1, _LRU_TC, 4 * _LRU_TC), lambda g, j: (j, 0, 0)),
                  chan(4), chan(1), chan(2), chan(2), chan(2), slab, slab],
        out_specs=[pl.BlockSpec((R, _LRU_TC), lambda g, j: (g, j)), slab, slab],
        out_shape=[jax.ShapeDtypeStruct((n, LRU_W), bf16),
                   jax.ShapeDtypeStruct(h0f.shape, f32),
                   jax.ShapeDtypeStruct(h0b.shape, f32)],
        scratch_shapes=[pltpu.VMEM((R, _LRU_TC), f32)]
        + [pltpu.VMEM((nslab, T * SUBLANES, LANES), f32) for _ in range(4)],
        compiler_params=_cparams(("arbitrary", "arbitrary"), 56),
        name="rg_lru",
    )(proj, proj, wcat, conv_w, conv_b, b_a, b_x, lam, h0f, h0b)


def _lru_gate_weights(w_a, w_x):
    per_tile = _LRU_TC // LRU_BLOCK
    nct = LRU_W // _LRU_TC
    eye = jnp.eye(per_tile, dtype=f32)

    def bd(w):
        w = w.reshape(nct, per_tile, LRU_BLOCK, LRU_BLOCK)
        m = w[:, :, :, None, :] * eye[None, :, None, :, None]
        return m.reshape(nct, _LRU_TC, _LRU_TC)

    return jnp.concatenate([bd(w_a[0]), bd(w_x[0]), bd(w_a[1]), bd(w_x[1])], axis=-1).astype(bf16)


def _outproj_body(att_ref, lru_ref, w_ref, x_ref, mods_ref, nw_ref, wr_ref,
                  x1_ref, h2_ref, aff_ref):
    mix = (jnp.dot(att_ref[...], w_ref[0:ATT_W, :], preferred_element_type=f32)
           + jnp.dot(lru_ref[...], w_ref[ATT_W:D, :], preferred_element_type=f32))
    m = mods_ref[0]
    x1 = x_ref[...] + m[2:3] * mix
    x1_ref[...] = x1
    ms = jnp.mean(x1 * x1, axis=-1, keepdims=True)
    h2 = (x1 * lax.rsqrt(ms + EPS) * nw_ref[...]) * (1.0 + m[4:5]) + m[3:4]
    h2_ref[...] = h2
    logits = _nt_dot(wr_ref[...], h2.astype(bf16))
    e = jnp.exp(logits - logits.max(axis=0, keepdims=True))
    aff_ref[...] = e / e.sum(axis=0, keepdims=True)


def _outproj(att, lru, w_out_b, x2d, mods, norm_w, w_router_t, tiles_per_row):
    n = x2d.shape[0]
    tm = 512
    row = (lambda i: 1 + i // tiles_per_row) if tiles_per_row else (lambda i: 0)
    return pl.pallas_call(
        _outproj_body,
        grid=(n // tm,),
        in_specs=[pl.BlockSpec((tm, ATT_W), lambda i: (i, 0)),
                  pl.BlockSpec((tm, LRU_W), lambda i: (i, 0)),
                  pl.BlockSpec((D, D), lambda i: (0, 0)),
                  pl.BlockSpec((tm, D), lambda i: (i, 0)),
                  pl.BlockSpec((1, 6, D), lambda i: (row(i), 0, 0)),
                  pl.BlockSpec((1, D), lambda i: (0, 0)),
                  pl.BlockSpec((N_EXPERTS, D), lambda i: (0, 0))],
        out_specs=[pl.BlockSpec((tm, D), lambda i: (i, 0)),
                   pl.BlockSpec((tm, D), lambda i: (i, 0)),
                   pl.BlockSpec((N_EXPERTS, tm), lambda i: (0, i))],
        out_shape=[jax.ShapeDtypeStruct((n, D), f32),
                   jax.ShapeDtypeStruct((n, D), f32),
                   jax.ShapeDtypeStruct((N_EXPERTS, n), f32)],
        compiler_params=_cparams(("arbitrary",), 52),
        name="out_projection_router",
    )(att, lru, w_out_b, x2d, mods, norm_w, w_router_t)


def _prefix_sum_lanes(x, lane, n):
    s = 1
    while s < n:
        x = x + jnp.where(lane >= s, pltpu.roll(x, s, 1), 0)
        s *= 2
    return x


def _route_body(aff_ref, idx_ref, g_ref, pos_ref, sp_ref, *, n, cap):
    aff = aff_ref[...]
    bits = pltpu.bitcast(aff, i32)
    lane = lax.broadcasted_iota(i32, (N_EXPERTS, n), 1)

    def search(i, thr):
        cand = thr | jnp.left_shift(jnp.int32(1), 30 - i)
        cnt = jnp.sum(jnp.where(bits >= cand, 1.0, 0.0), axis=1, keepdims=True)
        return jnp.where(cnt >= cap, cand, thr)

    thr = lax.fori_loop(0, 31, search, jnp.zeros((N_EXPERTS, 1), i32))
    gt = bits > thr
    eq = bits == thr
    need = cap - jnp.sum(jnp.where(gt, 1.0, 0.0), axis=1, keepdims=True).astype(i32)
    eq_rank = _prefix_sum_lanes(eq.astype(i32), lane, n)
    sel = gt | (eq & (eq_rank <= need))
    sel_i = sel.astype(i32)
    pos = _prefix_sum_lanes(sel_i, lane, n) - sel_i
    pos_ref[...] = pos
    sp_ref[...] = jnp.where(sel, pos, UNSELECTED)

    key = jnp.where(sel, lane - pos, -1)
    idx = lane
    g = aff
    b = 0
    while (1 << b) < n:
        s = 1 << b
        k_sh = pltpu.roll(key, n - s, 1)
        take = (k_sh >= 0) & ((jnp.right_shift(k_sh, b) & 1) == 1)
        stay = (key >= 0) & ((jnp.right_shift(key, b) & 1) == 0)
        idx = jnp.where(take, pltpu.roll(idx, n - s, 1), idx)
        g = jnp.where(take, pltpu.roll(g, n - s, 1), g)
        key = jnp.where(take, k_sh, jnp.where(stay, key, -1))
        b += 1
    idx_ref[...] = idx[:, :cap]
    g_ref[...] = g[:, :cap]


def _route(aff_t, cap):
    n = aff_t.shape[1]
    body = functools.partial(_route_body, n=n, cap=cap)
    full = lambda shape: pl.BlockSpec(shape, lambda: (0, 0))
    return pl.pallas_call(
        body,
        in_specs=[full((N_EXPERTS, n))],
        out_specs=[full((N_EXPERTS, cap)), full((N_EXPERTS, cap)),
                   full((N_EXPERTS, n)), full((N_EXPERTS, n))],
        out_shape=[jax.ShapeDtypeStruct((N_EXPERTS, cap), i32),
                   jax.ShapeDtypeStruct((N_EXPERTS, cap), f32),
                   jax.ShapeDtypeStruct((N_EXPERTS, n), i32),
                   jax.ShapeDtypeStruct((N_EXPERTS, n), i32)],
        compiler_params=pltpu.CompilerParams(vmem_limit_bytes=32 * _MIB),
        name="expert_choice_routing",
    )(aff_t)


_TF = 256
_FFN_ROWS = 512


def _expert_body(idx_ref, h2_hbm, g_ref, wg_ref, wu_ref, wd_ref, o_ref, xe_ref, xb_ref, sem,
                 *, cap, nf):
    e = pl.program_id(0)
    f = pl.program_id(1)

    def row_copy(tok, r):
        return pltpu.make_async_copy(h2_hbm.at[pl.ds(tok, 1), :], xe_ref.at[pl.ds(r, 1), :], sem)

    @pl.when(f == 0)
    def _():
        def issue(r, c):
            row_copy(idx_ref[e, r], r).start()
            return c

        lax.fori_loop(0, cap, issue, 0)

        def wait(r, c):
            row_copy(0, r).wait()
            return c

        lax.fori_loop(0, cap, wait, 0)
        xb_ref[...] = xe_ref[...].astype(bf16)

    xb = xb_ref[...]
    gp = jnp.dot(xb, wg_ref[0].astype(bf16), preferred_element_type=f32)
    up = jnp.dot(xb, wu_ref[0].astype(bf16), preferred_element_type=f32)
    he = ((gp * jax.nn.sigmoid(gp)) * up).astype(bf16)
    wd = wd_ref[0].astype(bf16)
    for rc in range(cap // _FFN_ROWS):
        rows = slice(rc * _FFN_ROWS, (rc + 1) * _FFN_ROWS)
        contrib = jnp.dot(he[rows, :], wd, preferred_element_type=f32)

        @pl.when(f == 0)
        def _():
            o_ref[0, rows, :] = contrib

        @pl.when(f > 0)
        def _():
            o_ref[0, rows, :] += contrib

    @pl.when(f == nf - 1)
    def _():
        o_ref[0] = o_ref[0] * g_ref[0]


def _expert_ffn(idx, h2, g3, w_gate, w_up, w_down):
    cap = idx.shape[1]
    nf = D_FF // _TF
    body = functools.partial(_expert_body, cap=cap, nf=nf)
    grid_spec = pltpu.PrefetchScalarGridSpec(
        num_scalar_prefetch=1,
        grid=(N_EXPERTS, nf),
        in_specs=[pl.BlockSpec(memory_space=pl.ANY),
                  pl.BlockSpec((1, cap, 1), lambda e, f, idx: (e, 0, 0)),
                  pl.BlockSpec((1, D, _TF), lambda e, f, idx: (e, 0, f)),
                  pl.BlockSpec((1, D, _TF), lambda e, f, idx: (e, 0, f)),
                  pl.BlockSpec((1, _TF, D), lambda e, f, idx: (e, f, 0))],
        out_specs=pl.BlockSpec((1, cap, D), lambda e, f, idx: (e, 0, 0)),
        scratch_shapes=[pltpu.VMEM((cap, D), f32), pltpu.VMEM((cap, D), bf16),
                        pltpu.SemaphoreType.DMA(())],
    )
    return pl.pallas_call(
        body,
        grid_spec=grid_spec,
        out_shape=jax.ShapeDtypeStruct((N_EXPERTS, cap, D), f32),
        compiler_params=_cparams(("arbitrary", "arbitrary"), 56),
        name="expert_ffn",
    )(idx, h2, g3, w_gate, w_up, w_down)


_TT = 256
_KB = 256


def _combine_body(offs_ref, ye_hbm, sp_ref, x1_ref, mods_ref, o_ref, stage_ref, acc_ref, sem,
                  *, cap):
    j = pl.program_id(0)

    @pl.when(j == 0)
    def _():
        stage_ref[...] = jnp.zeros_like(stage_ref)

    def row_copy(src, dst):
        return pltpu.make_async_copy(ye_hbm.at[pl.ds(src, 1), :], stage_ref.at[pl.ds(dst, 1), :], sem)

    bases = []
    offs = []
    base = jnp.int32(0)
    for e in range(N_EXPERTS):
        off = offs_ref[e, j]
        cnt = offs_ref[e, j + 1] - off
        bases.append(base)
        offs.append(off)

        def issue(r, c, e=e, off=off, base=base):
            row_copy(e * cap + off + r, base + r).start()
            return c

        lax.fori_loop(0, cnt, issue, 0)
        base = base + cnt
    total = base

    def wait(r, c):
        row_copy(0, r).wait()
        return c

    lax.fori_loop(0, total, wait, 0)

    sp = sp_ref[...]
    kiota = lax.broadcasted_iota(i32, (1, _KB), 1)
    acc_ref[...] = jnp.zeros_like(acc_ref)

    def kblock(kb, c):
        hit = None
        for e in range(N_EXPERTS):
            local = sp[:, e:e + 1] + (bases[e] - offs[e] - kb * _KB)
            h = local == kiota
            hit = h if hit is None else (hit | h)
        s = jnp.where(hit, 1.0, 0.0).astype(bf16)
        rows = stage_ref[pl.ds(pl.multiple_of(kb * _KB, _KB), _KB), :]
        hi = rows.astype(bf16)
        lo = (rows - hi.astype(f32)).astype(bf16)
        acc_ref[...] += (jnp.dot(s, hi, preferred_element_type=f32)
                         + jnp.dot(s, lo, preferred_element_type=f32))
        return c

    lax.fori_loop(0, (total + _KB - 1) // _KB, kblock, 0)
    o_ref[...] = x1_ref[...] + mods_ref[0][5:6] * acc_ref[...]


def _combine(offs, ye_flat, sp_n, x1, mods, cap, tiles_per_row):
    n = x1.shape[0]
    row = (lambda j: 1 + j // tiles_per_row) if tiles_per_row else (lambda j: 0)
    body = functools.partial(_combine_body, cap=cap)
    grid_spec = pltpu.PrefetchScalarGridSpec(
        num_scalar_prefetch=1,
        grid=(n // _TT,),
        in_specs=[pl.BlockSpec(memory_space=pl.ANY),
                  pl.BlockSpec((_TT, N_EXPERTS), lambda j, offs: (j, 0)),
                  pl.BlockSpec((_TT, D), lambda j, offs: (j, 0)),
                  pl.BlockSpec((1, 6, D), lambda j, offs: (row(j), 0, 0))],
        out_specs=pl.BlockSpec((_TT, D), lambda j, offs: (j, 0)),
        scratch_shapes=[pltpu.VMEM((N_EXPERTS * _TT, D), f32), pltpu.VMEM((_TT, D), f32),
                        pltpu.SemaphoreType.DMA(())],
    )
    return pl.pallas_call(
        body,
        grid_spec=grid_spec,
        out_shape=jax.ShapeDtypeStruct((n, D), f32),
        compiler_params=_cparams(("arbitrary",), 56),
        name="expert_combine",
    )(offs, ye_flat, sp_n, x1, mods)


def _moe(x1, h2, aff_t, mods, w_gate, w_up, w_down, tiles_per_row):
    n = x1.shape[0]
    cap = CAPACITY_FACTOR * n // N_EXPERTS
    idx, g, pos, sp = _route(aff_t, cap)
    ye = _expert_ffn(idx, h2, g[:, :, None], w_gate, w_up, w_down)
    offs = jnp.concatenate([pos[:, ::_TT], jnp.full((N_EXPERTS, 1), cap, i32)], axis=1)
    return _combine(offs, ye.reshape(N_EXPERTS * cap, D), sp.T, x1, mods, cap, tiles_per_row)


def kernel(x_prompt, x_sample, cache_k, cache_v, state_fwd, state_bwd, c, c_ctx, w_mod, b_mod,
           norm_mix, w_in, q_norm, k_norm, rpb, conv_w, conv_b, lru_w_a, lru_b_a, lru_w_x,
           lru_b_x, lru_lambda, w_out, norm_ffn, w_router, w_gate, w_up, w_down):
    B, S, _ = x_prompt.shape
    DB, T, _ = x_sample.shape
    past = cache_k.shape[3]
    l = 0

    cvecs = jnp.concatenate([c_ctx[None, :], c, jnp.zeros((SUBLANES - 1 - DB, D), f32)], axis=0)
    mods = _adaln_mods(cvecs, w_mod[l], b_mod[l][None, :]).reshape(SUBLANES, 6, D)

    heads_per_tile = _TN_IN // HEAD_DIM
    qn_t = jnp.tile(q_norm[l], heads_per_tile)[None, :]
    kn_t = jnp.tile(k_norm[l], heads_per_tile)[None, :]
    gmat = (jnp.kron(jnp.eye(heads_per_tile, dtype=f32), jnp.ones((HEAD_DIM, HEAD_DIM), f32))
            / HEAD_DIM).astype(bf16)
    nmix = norm_mix[l][None, :]
    nffn = norm_ffn[l][None, :]
    w_out_b = w_out[l].astype(bf16)
    w_router_t = w_router[l].T.astype(bf16)
    wcat = _lru_gate_weights(lru_w_a[l], lru_w_x[l])
    cb = conv_b[l][None, :]

    xp = x_prompt.reshape(B * S, D)
    xs = x_sample.reshape(DB * T, D)

    proj_c = _inproj(xp, mods, nmix, w_in[l], qn_t, kn_t, gmat, per_tile_rows=False)
    att_c = _ctx_attention(proj_c, B, S)
    zeros_c = jnp.zeros((B, LRU_W), f32)
    lru_c, htf, htb = _rglru(proj_c, wcat, conv_w[l], cb, lru_b_a[l], lru_b_x[l], lru_lambda[l],
                             zeros_c, zeros_c, T=S, nb=SUBLANES)
    x1_c, h2_c, aff_c = _outproj(att_c, lru_c, w_out_b, xp, mods, nffn, w_router_t, 0)
    y_c = _moe(x1_c, h2_c, aff_c, mods, w_gate[l], w_up[l], w_down[l], 0)

    heads = lambda a: a.reshape(B, S, N_HEADS, HEAD_DIM).transpose(0, 2, 1, 3)[:, None]
    new_k = heads(proj_c[:, ATT_W:2 * ATT_W])
    new_v = heads(proj_c[:, 2 * ATT_W:3 * ATT_W])

    nct = LRU_W // _LRU_TC

    def slabs(h0):
        return (h0.reshape(DB, nct, _LRU_HALVES, LANES).transpose(2, 0, 1, 3)
                .reshape(_LRU_HALVES * DB, nct * LANES))

    proj_l = _inproj(xs, mods, nmix, w_in[l], qn_t, kn_t, gmat, per_tile_rows=True)
    tok_major = lambda a: a[:, l].transpose(0, 2, 1, 3).reshape(DB, past, ATT_W)
    att_l = _lat_attention(proj_l, tok_major(cache_k), tok_major(cache_v), _na_bias(rpb[l]), DB, T)
    h0f_l = slabs(state_fwd[:, l])
    h0b_l = slabs(state_bwd[:, l])
    lru_l, _, _ = _rglru(proj_l, wcat, conv_w[l], cb, lru_b_a[l], lru_b_x[l], lru_lambda[l],
                         h0f_l, h0b_l, T=T, nb=DB)
    x1_l, h2_l, aff_l = _outproj(att_l, lru_l, w_out_b, xs, mods, nffn, w_router_t, T // 512)
    y_l = _moe(x1_l, h2_l, aff_l, mods, w_gate[l], w_up[l], w_down[l], T // _TT)

    return (y_c.reshape(B, S, D), y_l.reshape(DB, T, D), new_k, new_v,
            htf[:, None, :], htb[:, None, :])
```

```python
import functools

import jax
import jax.numpy as jnp
from jax import lax
from jax.experimental import pallas as pl
from jax.experimental.pallas import tpu as pltpu

f32 = jnp.float32
bf16 = jnp.bfloat16
i32 = jnp.int32

D = 2048
N_HEADS = 16
HEAD_DIM = 64
ATT_W = N_HEADS * HEAD_DIM
LRU_W = D - ATT_W
LRU_BLOCK = 64
LRU_C = 8.0
GRID_W = 64
WIN_H = 8
WIN_W = 16
N_EXPERTS = 16
CAPACITY_FACTOR = 2
D_FF = 2048
EPS = 1e-6
NEG_INF = -1e30
IN_COLS = 3 * ATT_W + 2 * LRU_W

LANES = 128
SUBLANES = 8
MXU_DIM = 256
UNSELECTED = -(1 << 30)

_MIB = 1024 * 1024


def _cparams(sem, vmem_mib):
    return pltpu.CompilerParams(dimension_semantics=sem, vmem_limit_bytes=vmem_mib * _MIB)


def _mods_body(c_ref, w_ref, b_ref, o_ref):
    c = c_ref[...]
    s = (c * jax.nn.sigmoid(c)).astype(bf16)
    o_ref[...] = jnp.dot(s, w_ref[...].astype(bf16), preferred_element_type=f32) + b_ref[...]


def _adaln_mods(cvecs, w_mod, b_mod):
    tn = 1024
    return pl.pallas_call(
        _mods_body,
        grid=(6 * D // tn,),
        in_specs=[pl.BlockSpec((SUBLANES, D), lambda j: (0, 0)),
                  pl.BlockSpec((D, tn), lambda j: (0, j)),
                  pl.BlockSpec((1, tn), lambda j: (0, j))],
        out_specs=pl.BlockSpec((SUBLANES, tn), lambda j: (0, j)),
        out_shape=jax.ShapeDtypeStruct((SUBLANES, 6 * D), f32),
        compiler_params=_cparams(("arbitrary",), 40),
        name="adaln_mods",
    )(cvecs, w_mod, b_mod)


_TN_IN = 512


def _inproj_body(x_ref, mods_ref, nw_ref, w_ref, qn_ref, kn_ref, g_ref, o_ref, h_ref):
    j = pl.program_id(1)

    @pl.when(j == 0)
    def _():
        x = x_ref[...]
        ms = jnp.mean(x * x, axis=-1, keepdims=True)
        y = x * lax.rsqrt(ms + EPS) * nw_ref[...]
        m = mods_ref[0]
        h_ref[...] = (y * (1.0 + m[1:2]) + m[0:1]).astype(bf16)

    acc = jnp.dot(h_ref[...], w_ref[...].astype(bf16), preferred_element_type=f32)
    n_qk_tiles = 2 * ATT_W // _TN_IN

    @pl.when(j < n_qk_tiles)
    def _():
        sq = acc * acc
        hi = sq.astype(bf16)
        lo = (sq - hi.astype(f32)).astype(bf16)
        g = g_ref[...]
        ms = (jnp.dot(hi, g, preferred_element_type=f32)
              + jnp.dot(lo, g, preferred_element_type=f32))
        nw = jnp.where(j < n_qk_tiles // 2, qn_ref[...], kn_ref[...])
        o_ref[...] = acc * lax.rsqrt(ms + EPS) * nw

    @pl.when(j >= n_qk_tiles)
    def _():
        o_ref[...] = acc


def _inproj(x2d, mods, norm_w, w_in, qn_t, kn_t, gmat, per_tile_rows):
    n = x2d.shape[0]
    tm = 1024
    row = (lambda i: 1 + i) if per_tile_rows else (lambda i: 0)
    return pl.pallas_call(
        _inproj_body,
        grid=(n // tm, IN_COLS // _TN_IN),
        in_specs=[pl.BlockSpec((tm, D), lambda i, j: (i, 0)),
                  pl.BlockSpec((1, 6, D), lambda i, j: (row(i), 0, 0)),
                  pl.BlockSpec((1, D), lambda i, j: (0, 0)),
                  pl.BlockSpec((D, _TN_IN), lambda i, j: (0, j)),
                  pl.BlockSpec((1, _TN_IN), lambda i, j: (0, 0)),
                  pl.BlockSpec((1, _TN_IN), lambda i, j: (0, 0)),
                  pl.BlockSpec((_TN_IN, _TN_IN), lambda i, j: (0, 0))],
        out_specs=pl.BlockSpec((tm, _TN_IN), lambda i, j: (i, j)),
        out_shape=jax.ShapeDtypeStruct((n, IN_COLS), f32),
        scratch_shapes=[pltpu.VMEM((tm, D), bf16)],
        compiler_params=_cparams(("arbitrary", "arbitrary"), 52),
        name="in_projection",
    )(x2d, mods, norm_w, w_in, qn_t, kn_t, gmat)


def _softmax_pv(s_list, v_list):
    m = s_list[0].max(axis=-1, keepdims=True)
    for s in s_list[1:]:
        m = jnp.maximum(m, s.max(axis=-1, keepdims=True))
    l = None
    o = None
    for s, v in zip(s_list, v_list):
        e = jnp.exp(s - m)
        ls = e.sum(axis=-1, keepdims=True)
        os_ = jnp.dot(e.astype(bf16), v, preferred_element_type=f32)
        l = ls if l is None else l + ls
        o = os_ if o is None else o + os_
    return o, l


def _nt_dot(a, b):
    return lax.dot_general(a, b, (((1,), (1,)), ((), ())), preferred_element_type=f32)


def _ctx_attn_body(q_ref, k_ref, v_ref, o_ref):
    scale = HEAD_DIM ** -0.5
    lane = lax.broadcasted_iota(i32, (1, LANES), 1)
    for p in range(ATT_W // LANES):
        cols = slice(p * LANES, (p + 1) * LANES)
        q = q_ref[:, cols]
        k = k_ref[:, cols].astype(bf16)
        v = v_ref[:, cols]
        out = None
        for hh in range(2):
            hm = (lane >= hh * HEAD_DIM) & (lane < (hh + 1) * HEAD_DIM)
            qh = jnp.where(hm, q, 0.0).astype(bf16)
            vh = jnp.where(hm, v, 0.0).astype(bf16)
            s = _nt_dot(qh, k) * scale
            o, l = _softmax_pv([s], [vh])
            o = o / l
            out = o if out is None else out + o
        o_ref[:, cols] = out.astype(bf16)


def _ctx_attention(proj, batch, seq):
    n = proj.shape[0]
    nqb = ATT_W // ATT_W
    return pl.pallas_call(
        _ctx_attn_body,
        grid=(batch,),
        in_specs=[pl.BlockSpec((seq, ATT_W), lambda b: (b, 0)),
                  pl.BlockSpec((seq, ATT_W), lambda b: (b, nqb)),
                  pl.BlockSpec((seq, ATT_W), lambda b: (b, 2 * nqb))],
        out_specs=pl.BlockSpec((seq, ATT_W), lambda b: (b, 0)),
        out_shape=jax.ShapeDtypeStruct((n, ATT_W), bf16),
        compiler_params=_cparams(("arbitrary",), 32),
        name="context_attention",
    )(proj, proj, proj)


_NA_ROWS = 16
_NA_QROWS = 8
_NA_KROWS = 12


def _pad_rpb(rpb):
    nd, nc = 2 * WIN_H - 1, 2 * WIN_W - 1
    return jnp.pad(rpb.astype(f32), ((0, 0), (0, 2 * SUBLANES - nd), (0, LANES - nc)))


def _build_na_bias(rpb_ref, bias_ref):
    qc = lax.broadcasted_iota(i32, (GRID_W, LANES), 0)
    lane = lax.broadcasted_iota(i32, (GRID_W, LANES), 1)
    kc = lane & (GRID_W - 1)
    cs = jnp.clip(qc - WIN_W // 2, 0, GRID_W - WIN_W)
    col_ok = (kc >= cs) & (kc < cs + WIN_W)
    left = lane < GRID_W
    neg = jnp.full((GRID_W, LANES), NEG_INF, f32)
    for hh in range(2):
        tbl = rpb_ref[hh]
        tiles = {}

        def toeplitz(d, odd):
            if (d, odd) not in tiles:
                row = jnp.broadcast_to(tbl[d:d + 1, :], (GRID_W, LANES))
                shift = (LANES - (WIN_W - 1) + odd * GRID_W) % LANES
                t = pltpu.roll(row, shift, 1, stride=1, stride_axis=0)
                tiles[(d, odd)] = jnp.where(col_ok, t, NEG_INF)
            return tiles[(d, odd)]

        for j in range(2):
            for rq in range(_NA_QROWS):
                rq_abs = j * _NA_QROWS + rq
                rs = min(max(rq_abs - WIN_H // 2, 0), _NA_ROWS - WIN_H)
                for kp in range(_NA_KROWS // 2):
                    parts = []
                    for odd in range(2):
                        rk_abs = j * (_NA_ROWS - _NA_KROWS) + 2 * kp + odd
                        if rs <= rk_abs < rs + WIN_H:
                            parts.append(toeplitz(rk_abs - rq_abs + WIN_H - 1, odd))
                        else:
                            parts.append(neg)
                    bias_ref[hh, j, rq * GRID_W:(rq + 1) * GRID_W,
                             kp * LANES:(kp + 1) * LANES] = jnp.where(left, parts[0], parts[1])


def _lat_attn_body(q_ref, k_ref, v_ref, ck_ref, cv_ref, rpb_ref, o_ref, bias_ref):
    @pl.when(pl.program_id(1) == 0)
    def _():
        _build_na_bias(rpb_ref, bias_ref)

    scale = HEAD_DIM ** -0.5
    lane = lax.broadcasted_iota(i32, (1, LANES), 1)
    nq = _NA_QROWS * GRID_W
    nk = _NA_KROWS * GRID_W
    kstep = (_NA_ROWS - _NA_KROWS) * GRID_W
    ck = ck_ref[0].astype(bf16)
    cv = cv_ref[0]
    for j in range(2):
        q = q_ref[j * nq:(j + 1) * nq, :]
        k = k_ref[j * kstep:j * kstep + nk, :].astype(bf16)
        v = v_ref[j * kstep:j * kstep + nk, :]
        out = None
        for hh in range(2):
            hm = (lane >= hh * HEAD_DIM) & (lane < (hh + 1) * HEAD_DIM)
            qh = jnp.where(hm, q, 0.0).astype(bf16)
            s_win = _nt_dot(qh, k) * scale + bias_ref[hh, j]
            s_ctx = _nt_dot(qh, ck) * scale
            vh = jnp.where(hm, v, 0.0).astype(bf16)
            cvh = jnp.where(hm, cv, 0.0).astype(bf16)
            o, l = _softmax_pv([s_win, s_ctx], [vh, cvh])
            o = o / l
            out = o if out is None else out + o
        o_ref[j * nq:(j + 1) * nq, :] = out.astype(bf16)


def _lat_attention(proj, ck, cv, rpb_p, batch, seq):
    n = proj.shape[0]
    past = ck.shape[1]
    npair = ATT_W // LANES
    nq = _NA_QROWS * GRID_W
    nk = _NA_KROWS * GRID_W
    return pl.pallas_call(
        _lat_attn_body,
        grid=(npair, batch),
        in_specs=[pl.BlockSpec((seq, LANES), lambda p, b: (b, p)),
                  pl.BlockSpec((seq, LANES), lambda p, b: (b, npair + p)),
                  pl.BlockSpec((seq, LANES), lambda p, b: (b, 2 * npair + p)),
                  pl.BlockSpec((1, past, LANES), lambda p, b: (b, 0, p)),
                  pl.BlockSpec((1, past, LANES), lambda p, b: (b, 0, p)),
                  pl.BlockSpec((2, 2 * SUBLANES, LANES), lambda p, b: (p, 0, 0))],
        out_specs=pl.BlockSpec((seq, LANES), lambda p, b: (b, p)),
        out_shape=jax.ShapeDtypeStruct((n, ATT_W), bf16),
        scratch_shapes=[pltpu.VMEM((2, 2, nq, nk), f32)],
        compiler_params=_cparams(("arbitrary", "arbitrary"), 48),
        name="neighbourhood_attention",
    )(proj, proj, proj, ck, cv, rpb_p)


_LRU_TC = 256
_GATE_ROWS = 256
_LRU_HALVES = _LRU_TC // LANES


def _gelu_tanh(x):
    c = 0.7978845608028654
    return x * (0.5 * (1.0 + jnp.tanh(c * (x + 0.044715 * (x * x * x)))))


def _lru_body(xr_ref, gr_ref, w_ref, cw_ref, cb_ref, ba_ref, bx_ref, lam_ref, h0f_ref, h0b_ref,
              y_ref, htf_ref, htb_ref, xc_ref, af_ref, uf_ref, ab_ref, ub_ref, *, T, nb):
    nslab = _LRU_HALVES * nb // SUBLANES
    R = nb * T
    x = xr_ref[...]
    t = lax.broadcasted_iota(i32, (R, 1), 0) & (T - 1)
    cw = cw_ref[...]
    xc = jnp.where(t >= 2, pltpu.roll(x, 2, 0), 0.0) * cw[0:1]
    xc = xc + jnp.where(t >= 1, pltpu.roll(x, 1, 0), 0.0) * cw[1:2]
    xc = xc + x * cw[2:3]
    xc = xc + jnp.where(t < T - 1, pltpu.roll(x, R - 1, 0), 0.0) * cw[3:4]
    xc_ref[...] = xc + cb_ref[...]

    nl = -lam_ref[...]
    sp = jnp.maximum(nl, 0.0) + jnp.log1p(jnp.exp(-jnp.abs(nl)))
    w = w_ref[0]
    dirs = ((af_ref, uf_ref), (ab_ref, ub_ref))
    for b in range(nb):
        for ci in range(T // _GATE_ROWS):
            r0 = b * T + ci * _GATE_ROWS
            xcc = xc_ref[r0:r0 + _GATE_ROWS, :]
            z = jnp.dot(xcc.astype(bf16), w, preferred_element_type=f32)
            for d, (a_ref, u_ref) in enumerate(dirs):
                c0 = d * 2 * _LRU_TC
                r = jax.nn.sigmoid(z[:, c0:c0 + _LRU_TC] + ba_ref[d:d + 1])
                g = jax.nn.sigmoid(z[:, c0 + _LRU_TC:c0 + 2 * _LRU_TC] + bx_ref[d:d + 1])
                log_a = (-LRU_C * r) * sp[d:d + 1]
                a = jnp.exp(log_a)
                u = jnp.sqrt(jnp.tanh(-log_a) * (1.0 + a * a)) * (g * xcc)
                for hf in range(_LRU_HALVES):
                    s, sub = divmod(hf * nb + b, SUBLANES)
                    rows = pl.ds(ci * _GATE_ROWS * SUBLANES + sub, _GATE_ROWS, stride=SUBLANES)
                    a_ref[s, rows, :] = a[:, hf * LANES:(hf + 1) * LANES]
                    u_ref[s, rows, :] = u[:, hf * LANES:(hf + 1) * LANES]

    def step(i, carry):
        rf = pl.ds(pl.multiple_of(i * SUBLANES, SUBLANES), SUBLANES)
        rb = pl.ds(pl.multiple_of((T - 1 - i) * SUBLANES, SUBLANES), SUBLANES)
        new = []
        for s in range(nslab):
            hf_ = af_ref[s, rf, :] * carry[2 * s] + uf_ref[s, rf, :]
            uf_ref[s, rf, :] = hf_
            hb_ = ab_ref[s, rb, :] * carry[2 * s + 1] + ub_ref[s, rb, :]
            ub_ref[s, rb, :] = hb_
            new += [hf_, hb_]
        return tuple(new)

    init = []
    for s in range(nslab):
        init += [h0f_ref[:, s * LANES:(s + 1) * LANES], h0b_ref[:, s * LANES:(s + 1) * LANES]]
    fin = lax.fori_loop(0, T, step, tuple(init), unroll=8)
    for s in range(nslab):
        htf_ref[:, s * LANES:(s + 1) * LANES] = fin[2 * s]
        htb_ref[:, s * LANES:(s + 1) * LANES] = fin[2 * s + 1]

    for b in range(nb):
        for hf in range(_LRU_HALVES):
            s, sub = divmod(hf * nb + b, SUBLANES)
            rows = pl.ds(sub, T, stride=SUBLANES)
            hs = uf_ref[s, rows, :] + ub_ref[s, rows, :]
            g = gr_ref[b * T:(b + 1) * T, hf * LANES:(hf + 1) * LANES]
            y_ref[b * T:(b + 1) * T, hf * LANES:(hf + 1) * LANES] = (
                hs * _gelu_tanh(g)).astype(bf16)


def _rglru(proj, wcat, conv_w, conv_b, b_a, b_x, lam, h0f, h0b, T, nb):
    n = proj.shape[0]
    nslab = _LRU_HALVES * nb // SUBLANES
    W = nslab * LANES
    R = nb * T
    nct = LRU_W // _LRU_TC
    xr0 = 3 * ATT_W // _LRU_TC
    gr0 = xr0 + nct
    chan = lambda k: pl.BlockSpec((k, _LRU_TC), lambda g, j: (0, j))
    slab = pl.BlockSpec((SUBLANES, W), lambda g, j: (g, j))
    body = functools.partial(_lru_body, T=T, nb=nb)
    return pl.pallas_call(
        body,
        grid=(n // R, nct),
        in_specs=[pl.BlockSpec((R, _LRU_TC), lambda g, j: (g, xr0 + j)),
                  pl.BlockSpec((R, _LRU_TC), lambda g, j: (g, gr0 + j)),
                  pl.BlockSpec((1, _LRU_TC, 4 * _LRU_TC), lambda g, j: (j, 0, 0)),
                  chan(4), chan(1), chan(2), chan(2), chan(2), slab, slab],
        out_specs=[pl.BlockSpec((R, _LRU_TC), lambda g, j: (g, j)), slab, slab],
        out_shape=[jax.ShapeDtypeStruct((n, LRU_W), bf16),
                   jax.ShapeDtypeStruct(h0f.shape, f32),
                   jax.ShapeDtypeStruct(h0b.shape, f32)],
        scratch_shapes=[pltpu.VMEM((R, _LRU_TC), f32)]
        + [pltpu.VMEM((nslab, T * SUBLANES, LANES), f32) for _ in range(4)],
        compiler_params=_cparams(("arbitrary", "arbitrary"), 56),
        name="rg_lru",
    )(proj, proj, wcat, conv_w, conv_b, b_a, b_x, lam, h0f, h0b)


def _lru_gate_weights(w_a, w_x):
    per_tile = _LRU_TC // LRU_BLOCK
    nct = LRU_W // _LRU_TC
    eye = jnp.eye(per_tile, dtype=f32)

    def bd(w):
        w = w.reshape(nct, per_tile, LRU_BLOCK, LRU_BLOCK)
        m = w[:, :, :, None, :] * eye[None, :, None, :, None]
        return m.reshape(nct, _LRU_TC, _LRU_TC)

    return jnp.concatenate([bd(w_a[0]), bd(w_x[0]), bd(w_a[1]), bd(w_x[1])], axis=-1).astype(bf16)


def _outproj_body(att_ref, lru_ref, w_ref, x_ref, mods_ref, nw_ref, wr_ref,
                  x1_ref, h2_ref, aff_ref):
    mix = (jnp.dot(att_ref[...], w_ref[0:ATT_W, :], preferred_element_type=f32)
           + jnp.dot(lru_ref[...], w_ref[ATT_W:D, :], preferred_element_type=f32))
    m = mods_ref[0]
    x1 = x_ref[...] + m[2:3] * mix
    x1_ref[...] = x1
    ms = jnp.mean(x1 * x1, axis=-1, keepdims=True)
    h2 = (x1 * lax.rsqrt(ms + EPS) * nw_ref[...]) * (1.0 + m[4:5]) + m[3:4]
    h2_ref[...] = h2
    logits = _nt_dot(wr_ref[...], h2.astype(bf16))
    e = jnp.exp(logits - logits.max(axis=0, keepdims=True))
    aff_ref[...] = e / e.sum(axis=0, keepdims=True)


def _outproj(att, lru, w_out_b, x2d, mods, norm_w, w_router_t, tiles_per_row):
    n = x2d.shape[0]
    tm = 512
    row = (lambda i: 1 + i // tiles_per_row) if tiles_per_row else (lambda i: 0)
    return pl.pallas_call(
        _outproj_body,
        grid=(n // tm,),
        in_specs=[pl.BlockSpec((tm, ATT_W), lambda i: (i, 0)),
                  pl.BlockSpec((tm, LRU_W), lambda i: (i, 0)),
                  pl.BlockSpec((D, D), lambda i: (0, 0)),
                  pl.BlockSpec((tm, D), lambda i: (i, 0)),
                  pl.BlockSpec((1, 6, D), lambda i: (row(i), 0, 0)),
                  pl.BlockSpec((1, D), lambda i: (0, 0)),
                  pl.BlockSpec((N_EXPERTS, D), lambda i: (0, 0))],
        out_specs=[pl.BlockSpec((tm, D), lambda i: (i, 0)),
                   pl.BlockSpec((tm, D), lambda i: (i, 0)),
                   pl.BlockSpec((N_EXPERTS, tm), lambda i: (0, i))],
        out_shape=[jax.ShapeDtypeStruct((n, D), f32),
                   jax.ShapeDtypeStruct((n, D), f32),
                   jax.ShapeDtypeStruct((N_EXPERTS, n), f32)],
        compiler_params=_cparams(("arbitrary",), 52),
        name="out_projection_router",
    )(att, lru, w_out_b, x2d, mods, norm_w, w_router_t)


def _prefix_sum_lanes(x, lane, n):
    s = 1
    while s < n:
        x = x + jnp.where(lane >= s, pltpu.roll(x, s, 1), 0)
        s *= 2
    return x


def _route_body(aff_ref, idx_ref, g_ref, pos_ref, sp_ref, *, n, cap):
    aff = aff_ref[...]
    bits = pltpu.bitcast(aff, i32)
    lane = lax.broadcasted_iota(i32, (N_EXPERTS, n), 1)

    def search(i, thr):
        cand = thr | jnp.left_shift(jnp.int32(1), 30 - i)
        cnt = jnp.sum(jnp.where(bits >= cand, 1.0, 0.0), axis=1, keepdims=True)
        return jnp.where(cnt >= cap, cand, thr)

    thr = lax.fori_loop(0, 31, search, jnp.zeros((N_EXPERTS, 1), i32))
    gt = bits > thr
    eq = bits == thr
    need = cap - jnp.sum(jnp.where(gt, 1.0, 0.0), axis=1, keepdims=True).astype(i32)
    eq_rank = _prefix_sum_lanes(eq.astype(i32), lane, n)
    sel = gt | (eq & (eq_rank <= need))
    sel_i = sel.astype(i32)
    pos = _prefix_sum_lanes(sel_i, lane, n) - sel_i
    pos_ref[...] = pos
    sp_ref[...] = jnp.where(sel, pos, UNSELECTED)

    key = jnp.where(sel, lane - pos, -1)
    idx = lane
    g = aff
    b = 0
    while (1 << b) < n:
        s = 1 << b
        k_sh = pltpu.roll(key, n - s, 1)
        take = (k_sh >= 0) & ((jnp.right_shift(k_sh, b) & 1) == 1)
        stay = (key >= 0) & ((jnp.right_shift(key, b) & 1) == 0)
        idx = jnp.where(take, pltpu.roll(idx, n - s, 1), idx)
        g = jnp.where(take, pltpu.roll(g, n - s, 1), g)
        key = jnp.where(take, k_sh, jnp.where(stay, key, -1))
        b += 1
    idx_ref[...] = idx[:, :cap]
    g_ref[...] = g[:, :cap]


def _route(aff_t, cap):
    n = aff_t.shape[1]
    body = functools.partial(_route_body, n=n, cap=cap)
    full = lambda shape: pl.BlockSpec(shape, lambda: (0, 0))
    return pl.pallas_call(
        body,
        in_specs=[full((N_EXPERTS, n))],
        out_specs=[full((N_EXPERTS, cap)), full((N_EXPERTS, cap)),
                   full((N_EXPERTS, n)), full((N_EXPERTS, n))],
        out_shape=[jax.ShapeDtypeStruct((N_EXPERTS, cap), i32),
                   jax.ShapeDtypeStruct((N_EXPERTS, cap), f32),
                   jax.ShapeDtypeStruct((N_EXPERTS, n), i32),
                   jax.ShapeDtypeStruct((N_EXPERTS, n), i32)],
        compiler_params=pltpu.CompilerParams(vmem_limit_bytes=32 * _MIB),
        name="expert_choice_routing",
    )(aff_t)


_TF = 512
_TN_DOWN = 512


def _expert_body(idx_ref, h2_hbm, g_ref, wg_ref, wu_ref, wd_ref, o_ref, xe_ref, xb_ref, he_ref,
                 sem, *, cap, na, nd):
    e = pl.program_id(0)
    s = pl.program_id(1)
    chunk = cap // nd

    def row_copy(tok, r):
        return pltpu.make_async_copy(h2_hbm.at[pl.ds(tok, 1), :], xe_ref.at[pl.ds(r, 1), :], sem)

    def wait_all_rows():
        pltpu.make_async_copy(h2_hbm.at[pl.ds(0, cap), :], xe_ref, sem).wait()

    @pl.when((e == 0) & (s == 0))
    def _():
        def issue(r, c):
            row_copy(idx_ref[0, r], r).start()
            return c

        lax.fori_loop(0, cap, issue, 0, unroll=8)

    @pl.when(s == 0)
    def _():
        wait_all_rows()
        xb_ref[...] = xe_ref[...].astype(bf16)

    @pl.when(s < na)
    def _():
        xb = xb_ref[...]
        gp = jnp.dot(xb, wg_ref[0].astype(bf16), preferred_element_type=f32)
        up = jnp.dot(xb, wu_ref[0].astype(bf16), preferred_element_type=f32)
        he = ((gp * jax.nn.sigmoid(gp)) * up).astype(bf16)
        for k in range(na):
            @pl.when(s == k)
            def _(k=k):
                he_ref[:, k * _TF:(k + 1) * _TF] = he

    @pl.when(s >= na)
    def _():
        t = s - na
        nxt = jnp.minimum(e + 1, N_EXPERTS - 1)
        for r in range(chunk):
            row_copy(idx_ref[nxt, t * chunk + r], t * chunk + r).start()
        acc = jnp.dot(he_ref[...], wd_ref[0].astype(bf16), preferred_element_type=f32)
        o_ref[0] = acc * g_ref[0]

    @pl.when((e == N_EXPERTS - 1) & (s == na + nd - 1))
    def _():
        wait_all_rows()


def _expert_ffn(idx, h2, g3, w_gate, w_up, w_down):
    cap = idx.shape[1]
    na = D_FF // _TF
    nd = D // _TN_DOWN
    body = functools.partial(_expert_body, cap=cap, na=na, nd=nd)
    up_tile = lambda e, s, idx: (e, 0, jnp.minimum(s, na - 1))
    down_tile = lambda e, s, idx: (e, 0, jnp.maximum(s - na, 0))
    grid_spec = pltpu.PrefetchScalarGridSpec(
        num_scalar_prefetch=1,
        grid=(N_EXPERTS, na + nd),
        in_specs=[pl.BlockSpec(memory_space=pl.ANY),
                  pl.BlockSpec((1, cap, 1), lambda e, s, idx: (e, 0, 0)),
                  pl.BlockSpec((1, D, _TF), up_tile),
                  pl.BlockSpec((1, D, _TF), up_tile),
                  pl.BlockSpec((1, D_FF, _TN_DOWN), down_tile)],
        out_specs=pl.BlockSpec((1, cap, _TN_DOWN), down_tile),
        scratch_shapes=[pltpu.VMEM((cap, D), f32), pltpu.VMEM((cap, D), bf16),
                        pltpu.VMEM((cap, D_FF), bf16), pltpu.SemaphoreType.DMA(())],
    )
    return pl.pallas_call(
        body,
        grid_spec=grid_spec,
        out_shape=jax.ShapeDtypeStruct((N_EXPERTS, cap, D), f32),
        compiler_params=_cparams(("arbitrary", "arbitrary"), 60),
        name="expert_ffn",
    )(idx, h2, g3, w_gate, w_up, w_down)


_TT = 128
_KB = 256
_RUN_MAX = _TT + 2 * SUBLANES
_RUN_PIECES = tuple(SUBLANES << i for i in reversed(range((_RUN_MAX // SUBLANES).bit_length())))


def _combine_body(offs_ref, ye_hbm, sp_ref, x1_ref, mods_ref, o_ref, stage_ref, acc_ref, sem,
                  *, cap):
    j = pl.program_id(0)
    slot = j & 1

    def tile_copies(jj, sl, start):
        bases, offs = [], []
        base = jnp.int32(0)
        for e in range(N_EXPERTS):
            off = offs_ref[e, jj]
            end = offs_ref[e, jj + 1]
            off8 = (off // SUBLANES) * SUBLANES
            length = jnp.where(end > off, ((end + SUBLANES - 1) // SUBLANES) * SUBLANES - off8, 0)
            bases.append(base)
            offs.append(off8)
            src = e * cap + off8
            dst = base
            for size in _RUN_PIECES:
                piece = length & size

                @pl.when(piece != 0)
                def _(src=src, dst=dst, size=size):
                    cp = pltpu.make_async_copy(
                        ye_hbm.at[pl.ds(pl.multiple_of(src, SUBLANES), size), :],
                        stage_ref.at[sl, pl.ds(pl.multiple_of(dst, SUBLANES), size), :],
                        sem.at[sl])
                    if start:
                        cp.start()
                    else:
                        cp.wait()

                src = src + piece
                dst = dst + piece
            base = base + length
        return bases, offs, base

    @pl.when(j == 0)
    def _():
        stage_ref[...] = jnp.zeros_like(stage_ref)
        tile_copies(0, 0, True)

    @pl.when(j + 1 < pl.num_programs(0))
    def _():
        tile_copies(j + 1, 1 - slot, True)

    bases, offs, total = tile_copies(j, slot, False)

    sp = sp_ref[...]
    kiota = lax.broadcasted_iota(i32, (1, _KB), 1)
    acc_ref[...] = jnp.zeros_like(acc_ref)

    def kblock(kb, c):
        hit = None
        for e in range(N_EXPERTS):
            local = sp[:, e:e + 1] + (bases[e] - offs[e] - kb * _KB)
            h = local == kiota
            hit = h if hit is None else (hit | h)
        s = jnp.where(hit, 1.0, 0.0).astype(bf16)
        rows = stage_ref[slot, pl.ds(pl.multiple_of(kb * _KB, _KB), _KB), :]
        hi = rows.astype(bf16)
        lo = (rows - hi.astype(f32)).astype(bf16)
        acc_ref[...] += (jnp.dot(s, hi, preferred_element_type=f32)
                         + jnp.dot(s, lo, preferred_element_type=f32))
        return c

    lax.fori_loop(0, (total + _KB - 1) // _KB, kblock, 0)
    o_ref[...] = x1_ref[...] + mods_ref[0][5:6] * acc_ref[...]


def _combine(offs, ye_flat, sp_n, x1, mods, cap, tiles_per_row):
    n = x1.shape[0]
    row = (lambda j: 1 + j // tiles_per_row) if tiles_per_row else (lambda j: 0)
    body = functools.partial(_combine_body, cap=cap)
    grid_spec = pltpu.PrefetchScalarGridSpec(
        num_scalar_prefetch=1,
        grid=(n // _TT,),
        in_specs=[pl.BlockSpec(memory_space=pl.ANY),
                  pl.BlockSpec((_TT, N_EXPERTS), lambda j, offs: (j, 0)),
                  pl.BlockSpec((_TT, D), lambda j, offs: (j, 0)),
                  pl.BlockSpec((1, 6, D), lambda j, offs: (row(j), 0, 0))],
        out_specs=pl.BlockSpec((_TT, D), lambda j, offs: (j, 0)),
        scratch_shapes=[pltpu.VMEM((2, N_EXPERTS * _RUN_MAX, D), f32), pltpu.VMEM((_TT, D), f32),
                        pltpu.SemaphoreType.DMA((2,))],
    )
    return pl.pallas_call(
        body,
        grid_spec=grid_spec,
        out_shape=jax.ShapeDtypeStruct((n, D), f32),
        compiler_params=_cparams(("arbitrary",), 48),
        name="expert_combine",
    )(offs, ye_flat, sp_n, x1, mods)


def _moe(x1, h2, aff_t, mods, w_gate, w_up, w_down, tiles_per_row):
    n = x1.shape[0]
    cap = CAPACITY_FACTOR * n // N_EXPERTS
    idx, g, pos, sp = _route(aff_t, cap)
    ye = _expert_ffn(idx, h2, g[:, :, None], w_gate, w_up, w_down)
    offs = jnp.concatenate([pos[:, ::_TT], jnp.full((N_EXPERTS, 1), cap, i32)], axis=1)
    return _combine(offs, ye.reshape(N_EXPERTS * cap, D), sp.T, x1, mods, cap, tiles_per_row)


def kernel(x_prompt, x_sample, cache_k, cache_v, state_fwd, state_bwd, c, c_ctx, w_mod, b_mod,
           norm_mix, w_in, q_norm, k_norm, rpb, conv_w, conv_b, lru_w_a, lru_b_a, lru_w_x,
           lru_b_x, lru_lambda, w_out, norm_ffn, w_router, w_gate, w_up, w_down):
    B, S, _ = x_prompt.shape
    DB, T, _ = x_sample.shape
    past = cache_k.shape[3]
    l = 0

    cvecs = jnp.concatenate([c_ctx[None, :], c, jnp.zeros((SUBLANES - 1 - DB, D), f32)], axis=0)
    mods = _adaln_mods(cvecs, w_mod[l], b_mod[l][None, :]).reshape(SUBLANES, 6, D)

    heads_per_tile = _TN_IN // HEAD_DIM
    qn_t = jnp.tile(q_norm[l], heads_per_tile)[None, :]
    kn_t = jnp.tile(k_norm[l], heads_per_tile)[None, :]
    gmat = (jnp.kron(jnp.eye(heads_per_tile, dtype=f32), jnp.ones((HEAD_DIM, HEAD_DIM), f32))
            / HEAD_DIM).astype(bf16)
    nmix = norm_mix[l][None, :]
    nffn = norm_ffn[l][None, :]
    w_out_b = w_out[l].astype(bf16)
    w_router_t = w_router[l].T.astype(bf16)
    wcat = _lru_gate_weights(lru_w_a[l], lru_w_x[l])
    cb = conv_b[l][None, :]

    xp = x_prompt.reshape(B * S, D)
    xs = x_sample.reshape(DB * T, D)

    proj_c = _inproj(xp, mods, nmix, w_in[l], qn_t, kn_t, gmat, per_tile_rows=False)
    att_c = _ctx_attention(proj_c, B, S)
    zeros_c = jnp.zeros((B, LRU_W), f32)
    lru_c, htf, htb = _rglru(proj_c, wcat, conv_w[l], cb, lru_b_a[l], lru_b_x[l], lru_lambda[l],
                             zeros_c, zeros_c, T=S, nb=SUBLANES)
    x1_c, h2_c, aff_c = _outproj(att_c, lru_c, w_out_b, xp, mods, nffn, w_router_t, 0)
    y_c = _moe(x1_c, h2_c, aff_c, mods, w_gate[l], w_up[l], w_down[l], 0)

    heads = lambda a: a.reshape(B, S, N_HEADS, HEAD_DIM).transpose(0, 2, 1, 3)[:, None]
    new_k = heads(proj_c[:, ATT_W:2 * ATT_W])
    new_v = heads(proj_c[:, 2 * ATT_W:3 * ATT_W])

    nct = LRU_W // _LRU_TC

    def slabs(h0):
        return (h0.reshape(DB, nct, _LRU_HALVES, LANES).transpose(2, 0, 1, 3)
                .reshape(_LRU_HALVES * DB, nct * LANES))

    proj_l = _inproj(xs, mods, nmix, w_in[l], qn_t, kn_t, gmat, per_tile_rows=True)
    tok_major = lambda a: a[:, l].transpose(0, 2, 1, 3).reshape(DB, past, ATT_W)
    att_l = _lat_attention(proj_l, tok_major(cache_k), tok_major(cache_v), _pad_rpb(rpb[l]), DB, T)
    h0f_l = slabs(state_fwd[:, l])
    h0b_l = slabs(state_bwd[:, l])
    lru_l, _, _ = _rglru(proj_l, wcat, conv_w[l], cb, lru_b_a[l], lru_b_x[l], lru_lambda[l],
                         h0f_l, h0b_l, T=T, nb=DB)
    x1_l, h2_l, aff_l = _outproj(att_l, lru_l, w_out_b, xs, mods, nffn, w_router_t, T // 512)
    y_l = _moe(x1_l, h2_l, aff_l, mods, w_gate[l], w_up[l], w_down[l], T // _TT)

    return (y_c.reshape(B, S, D), y_l.reshape(DB, T, D), new_k, new_v,
            htf[:, None, :], htb[:, None, :])
```

```python
import functools

import jax
import jax.numpy as jnp
from jax import lax
from jax.experimental import pallas as pl
from jax.experimental.pallas import tpu as pltpu

f32 = jnp.float32
bf16 = jnp.bfloat16
i32 = jnp.int32

D = 2048
N_HEADS = 16
HEAD_DIM = 64
ATT_W = N_HEADS * HEAD_DIM
LRU_W = D - ATT_W
LRU_BLOCK = 64
LRU_C = 8.0
GRID_W = 64
WIN_H = 8
WIN_W = 16
N_EXPERTS = 16
CAPACITY_FACTOR = 2
D_FF = 2048
EPS = 1e-6
NEG_INF = -1e30
IN_COLS = 3 * ATT_W + 2 * LRU_W

LANES = 128
SUBLANES = 8
MXU_DIM = 256
UNSELECTED = -(1 << 30)

_MIB = 1024 * 1024


def _cparams(sem, vmem_mib):
    return pltpu.CompilerParams(dimension_semantics=sem, vmem_limit_bytes=vmem_mib * _MIB)


def _mods_body(c_ref, w_ref, b_ref, o_ref):
    c = c_ref[...]
    s = (c * jax.nn.sigmoid(c)).astype(bf16)
    o_ref[...] = jnp.dot(s, w_ref[...].astype(bf16), preferred_element_type=f32) + b_ref[...]


def _adaln_mods(cvecs, w_mod, b_mod):
    tn = 1024
    return pl.pallas_call(
        _mods_body,
        grid=(6 * D // tn,),
        in_specs=[pl.BlockSpec((SUBLANES, D), lambda j: (0, 0)),
                  pl.BlockSpec((D, tn), lambda j: (0, j)),
                  pl.BlockSpec((1, tn), lambda j: (0, j))],
        out_specs=pl.BlockSpec((SUBLANES, tn), lambda j: (0, j)),
        out_shape=jax.ShapeDtypeStruct((SUBLANES, 6 * D), f32),
        compiler_params=_cparams(("arbitrary",), 40),
        name="adaln_mods",
    )(cvecs, w_mod, b_mod)


_TN_IN = 512
_NORM_ROWS = 32


def _inproj_body(x_ref, mods_ref, nw_ref, w_ref, qn_ref, kn_ref, g_ref, o_ref, h_ref):
    j = pl.program_id(1)

    @pl.when(j == 0)
    def _():
        m = mods_ref[0]
        gain = 1.0 + m[1:2]
        shift = m[0:1]
        nw = nw_ref[...]

        def norm_rows(c, carry):
            rows = pl.ds(pl.multiple_of(c * _NORM_ROWS, _NORM_ROWS), _NORM_ROWS)
            x = x_ref[rows, :]
            ms = jnp.mean(x * x, axis=-1, keepdims=True)
            y = x * lax.rsqrt(ms + EPS) * nw
            h_ref[rows, :] = (y * gain + shift).astype(bf16)
            return carry

        lax.fori_loop(0, h_ref.shape[0] // _NORM_ROWS, norm_rows, 0)

    acc = jnp.dot(h_ref[...], w_ref[...], preferred_element_type=f32)
    n_qk_tiles = 2 * ATT_W // _TN_IN

    @pl.when(j < n_qk_tiles)
    def _():
        sq = acc * acc
        hi = sq.astype(bf16)
        lo = (sq - hi.astype(f32)).astype(bf16)
        g = g_ref[...]
        ms = (jnp.dot(hi, g, preferred_element_type=f32)
              + jnp.dot(lo, g, preferred_element_type=f32))
        nw = jnp.where(j < n_qk_tiles // 2, qn_ref[...], kn_ref[...])
        o_ref[...] = acc * lax.rsqrt(ms + EPS) * nw

    @pl.when(j >= n_qk_tiles)
    def _():
        o_ref[...] = acc


def _inproj(x2d, mods, norm_w, w_in, qn_t, kn_t, gmat, per_tile_rows):
    n = x2d.shape[0]
    tm = 1024
    row = (lambda i: 1 + i) if per_tile_rows else (lambda i: 0)
    return pl.pallas_call(
        _inproj_body,
        grid=(n // tm, IN_COLS // _TN_IN),
        in_specs=[pl.BlockSpec((tm, D), lambda i, j: (i, 0)),
                  pl.BlockSpec((1, 6, D), lambda i, j: (row(i), 0, 0)),
                  pl.BlockSpec((1, D), lambda i, j: (0, 0)),
                  pl.BlockSpec((D, _TN_IN), lambda i, j: (0, j)),
                  pl.BlockSpec((1, _TN_IN), lambda i, j: (0, 0)),
                  pl.BlockSpec((1, _TN_IN), lambda i, j: (0, 0)),
                  pl.BlockSpec((_TN_IN, _TN_IN), lambda i, j: (0, 0))],
        out_specs=pl.BlockSpec((tm, _TN_IN), lambda i, j: (i, j)),
        out_shape=jax.ShapeDtypeStruct((n, IN_COLS), f32),
        scratch_shapes=[pltpu.VMEM((tm, D), bf16)],
        compiler_params=_cparams(("arbitrary", "arbitrary"), 52),
        name="in_projection",
    )(x2d, mods, norm_w, w_in, qn_t, kn_t, gmat)


def _softmax_pv(s_list, v_list):
    m = s_list[0].max(axis=-1, keepdims=True)
    for s in s_list[1:]:
        m = jnp.maximum(m, s.max(axis=-1, keepdims=True))
    l = None
    o = None
    for s, v in zip(s_list, v_list):
        e = jnp.exp(s - m)
        ls = e.sum(axis=-1, keepdims=True)
        os_ = jnp.dot(e.astype(bf16), v, preferred_element_type=f32)
        l = ls if l is None else l + ls
        o = os_ if o is None else o + os_
    return o, l


def _nt_dot(a, b):
    return lax.dot_general(a, b, (((1,), (1,)), ((), ())), preferred_element_type=f32)


_QK_SCALE = HEAD_DIM ** -0.5


def _ctx_attn_body(q_ref, k_ref, v_ref, o_ref, nk_ref, nv_ref):
    lane = lax.broadcasted_iota(i32, (1, LANES), 1)
    for p in range(ATT_W // LANES):
        cols = slice(p * LANES, (p + 1) * LANES)
        q = q_ref[:, cols] * _QK_SCALE
        kf = k_ref[:, cols]
        k = kf.astype(bf16)
        v = v_ref[:, cols]
        nk_ref[0, 0, 2 * p] = kf[:, :HEAD_DIM]
        nk_ref[0, 0, 2 * p + 1] = pltpu.roll(kf, HEAD_DIM, 1)[:, :HEAD_DIM]
        nv_ref[0, 0, 2 * p] = v[:, :HEAD_DIM]
        nv_ref[0, 0, 2 * p + 1] = pltpu.roll(v, HEAD_DIM, 1)[:, :HEAD_DIM]
        out = None
        for hh in range(2):
            hm = (lane >= hh * HEAD_DIM) & (lane < (hh + 1) * HEAD_DIM)
            qh = jnp.where(hm, q, 0.0).astype(bf16)
            vh = jnp.where(hm, v, 0.0).astype(bf16)
            s = _nt_dot(qh, k)
            o, l = _softmax_pv([s], [vh])
            o = o / l
            out = o if out is None else out + o
        o_ref[:, cols] = out.astype(bf16)


def _ctx_attention(proj, batch, seq):
    n = proj.shape[0]
    cache = pl.BlockSpec((1, 1, N_HEADS, seq, HEAD_DIM), lambda b: (b, 0, 0, 0, 0))
    cache_shape = jax.ShapeDtypeStruct((batch, 1, N_HEADS, seq, HEAD_DIM), f32)
    return pl.pallas_call(
        _ctx_attn_body,
        grid=(batch,),
        in_specs=[pl.BlockSpec((seq, ATT_W), lambda b: (b, 0)),
                  pl.BlockSpec((seq, ATT_W), lambda b: (b, 1)),
                  pl.BlockSpec((seq, ATT_W), lambda b: (b, 2))],
        out_specs=[pl.BlockSpec((seq, ATT_W), lambda b: (b, 0)), cache, cache],
        out_shape=[jax.ShapeDtypeStruct((n, ATT_W), bf16), cache_shape, cache_shape],
        compiler_params=_cparams(("arbitrary",), 32),
        name="context_attention",
    )(proj, proj, proj)


_NA_ROWS = 16
_NA_QROWS = 8
_NA_KROWS = 12


def _pad_rpb(rpb):
    nd, nc = 2 * WIN_H - 1, 2 * WIN_W - 1
    return jnp.pad(rpb.astype(f32), ((0, 0), (0, 2 * SUBLANES - nd), (0, LANES - nc)))


def _build_na_bias(rpb_ref, bias_ref):
    qc = lax.broadcasted_iota(i32, (GRID_W, LANES), 0)
    lane = lax.broadcasted_iota(i32, (GRID_W, LANES), 1)
    kc = lane & (GRID_W - 1)
    cs = jnp.clip(qc - WIN_W // 2, 0, GRID_W - WIN_W)
    col_ok = (kc >= cs) & (kc < cs + WIN_W)
    left = lane < GRID_W
    neg = jnp.full((GRID_W, LANES), NEG_INF, f32)
    for hh in range(2):
        tbl = rpb_ref[hh]
        tiles = {}

        def toeplitz(d, odd):
            if (d, odd) not in tiles:
                row = jnp.broadcast_to(tbl[d:d + 1, :], (GRID_W, LANES))
                shift = (LANES - (WIN_W - 1) + odd * GRID_W) % LANES
                t = pltpu.roll(row, shift, 1, stride=1, stride_axis=0)
                tiles[(d, odd)] = jnp.where(col_ok, t, NEG_INF)
            return tiles[(d, odd)]

        for j in range(2):
            for rq in range(_NA_QROWS):
                rq_abs = j * _NA_QROWS + rq
                rs = min(max(rq_abs - WIN_H // 2, 0), _NA_ROWS - WIN_H)
                for kp in range(_NA_KROWS // 2):
                    parts = []
                    for odd in range(2):
                        rk_abs = j * (_NA_ROWS - _NA_KROWS) + 2 * kp + odd
                        if rs <= rk_abs < rs + WIN_H:
                            parts.append(toeplitz(rk_abs - rq_abs + WIN_H - 1, odd))
                        else:
                            parts.append(neg)
                    bias_ref[hh, j, rq * GRID_W:(rq + 1) * GRID_W,
                             kp * LANES:(kp + 1) * LANES] = jnp.where(left, parts[0], parts[1])


def _lat_attn_body(q_ref, k_ref, v_ref, ck_ref, cv_ref, rpb_ref, o_ref, bias_ref):
    @pl.when(pl.program_id(1) == 0)
    def _():
        _build_na_bias(rpb_ref, bias_ref)

    lane = lax.broadcasted_iota(i32, (1, LANES), 1)
    nq = _NA_QROWS * GRID_W
    nk = _NA_KROWS * GRID_W
    kstep = (_NA_ROWS - _NA_KROWS) * GRID_W
    src = lax.broadcasted_iota(i32, (HEAD_DIM, LANES), 0)
    dst = lax.broadcasted_iota(i32, (HEAD_DIM, LANES), 1)
    ck = None
    cv_h = []
    for hh in range(2):
        place = jnp.where(dst == src + hh * HEAD_DIM, 1.0, 0.0).astype(bf16)
        part = jnp.dot(ck_ref[0, 0, hh].astype(bf16), place, preferred_element_type=f32)
        ck = part if ck is None else ck + part
        cv_h.append(jnp.dot(cv_ref[0, 0, hh].astype(bf16), place,
                            preferred_element_type=f32).astype(bf16))
    ck = ck.astype(bf16)
    for j in range(2):
        q = q_ref[j * nq:(j + 1) * nq, :] * _QK_SCALE
        k = k_ref[j * kstep:j * kstep + nk, :].astype(bf16)
        v = v_ref[j * kstep:j * kstep + nk, :]
        out = None
        for hh in range(2):
            hm = (lane >= hh * HEAD_DIM) & (lane < (hh + 1) * HEAD_DIM)
            qh = jnp.where(hm, q, 0.0).astype(bf16)
            s_win = _nt_dot(qh, k) + bias_ref[hh, j]
            s_ctx = _nt_dot(qh, ck)
            vh = jnp.where(hm, v, 0.0).astype(bf16)
            cvh = cv_h[hh]
            o, l = _softmax_pv([s_win, s_ctx], [vh, cvh])
            o = o / l
            out = o if out is None else out + o
        o_ref[j * nq:(j + 1) * nq, :] = out.astype(bf16)


def _lat_attention(proj, ck, cv, rpb_p, layer, batch, seq):
    n = proj.shape[0]
    past = ck.shape[3]
    cache = pl.BlockSpec((1, 1, 2, past, HEAD_DIM), lambda p, b: (b, layer, p, 0, 0))
    npair = ATT_W // LANES
    nq = _NA_QROWS * GRID_W
    nk = _NA_KROWS * GRID_W
    return pl.pallas_call(
        _lat_attn_body,
        grid=(npair, batch),
        in_specs=[pl.BlockSpec((seq, LANES), lambda p, b: (b, p)),
                  pl.BlockSpec((seq, LANES), lambda p, b: (b, npair + p)),
                  pl.BlockSpec((seq, LANES), lambda p, b: (b, 2 * npair + p)),
                  cache, cache,
                  pl.BlockSpec((2, 2 * SUBLANES, LANES), lambda p, b: (p, 0, 0))],
        out_specs=pl.BlockSpec((seq, LANES), lambda p, b: (b, p)),
        out_shape=jax.ShapeDtypeStruct((n, ATT_W), bf16),
        scratch_shapes=[pltpu.VMEM((2, 2, nq, nk), f32)],
        compiler_params=_cparams(("arbitrary", "arbitrary"), 48),
        name="neighbourhood_attention",
    )(proj, proj, proj, ck, cv, rpb_p)


_LRU_TC = 256
_GATE_ROWS = 256
_LRU_HALVES = _LRU_TC // LANES


def _gelu_tanh(x):
    c = 0.7978845608028654
    return x * (0.5 * (1.0 + jnp.tanh(c * (x + 0.044715 * (x * x * x)))))


def _lru_body(xr_ref, gr_ref, w_ref, cw_ref, cb_ref, ba_ref, bx_ref, lam_ref, h0f_ref, h0b_ref,
              y_ref, htf_ref, htb_ref, xc_ref, af_ref, uf_ref, ab_ref, ub_ref, *, T, nb):
    nslab = _LRU_HALVES * nb // SUBLANES
    R = nb * T
    x = xr_ref[...]
    t = lax.broadcasted_iota(i32, (R, 1), 0) & (T - 1)
    cw = cw_ref[...]
    xc = jnp.where(t >= 2, pltpu.roll(x, 2, 0), 0.0) * cw[0:1]
    xc = xc + jnp.where(t >= 1, pltpu.roll(x, 1, 0), 0.0) * cw[1:2]
    xc = xc + x * cw[2:3]
    xc = xc + jnp.where(t < T - 1, pltpu.roll(x, R - 1, 0), 0.0) * cw[3:4]
    xc_ref[...] = xc + cb_ref[...]

    nl = -lam_ref[...]
    sp = jnp.maximum(nl, 0.0) + jnp.log1p(jnp.exp(-jnp.abs(nl)))
    half_c_sp = (0.5 * LRU_C) * sp
    w = w_ref[0]
    dirs = ((af_ref, uf_ref), (ab_ref, ub_ref))
    for b in range(nb):
        for ci in range(T // _GATE_ROWS):
            r0 = b * T + ci * _GATE_ROWS
            xcc = xc_ref[r0:r0 + _GATE_ROWS, :]
            half_x = 0.5 * xcc
            z = jnp.dot(xcc.astype(bf16), w, preferred_element_type=f32)
            for d, (a_ref, u_ref) in enumerate(dirs):
                c0 = d * 2 * _LRU_TC
                tr = jnp.tanh(0.5 * (z[:, c0:c0 + _LRU_TC] + ba_ref[d:d + 1]))
                tg = jnp.tanh(0.5 * (z[:, c0 + _LRU_TC:c0 + 2 * _LRU_TC] + bx_ref[d:d + 1]))
                neg_log_a = (1.0 + tr) * half_c_sp[d:d + 1]
                a = jnp.exp(-neg_log_a)
                u = jnp.sqrt(jnp.tanh(neg_log_a) * (1.0 + a * a)) * ((1.0 + tg) * half_x)
                for hf in range(_LRU_HALVES):
                    s, sub = divmod(hf * nb + b, SUBLANES)
                    rows = pl.ds(ci * _GATE_ROWS * SUBLANES + sub, _GATE_ROWS, stride=SUBLANES)
                    a_ref[s, rows, :] = a[:, hf * LANES:(hf + 1) * LANES]
                    u_ref[s, rows, :] = u[:, hf * LANES:(hf + 1) * LANES]

    def step(i, carry):
        rf = pl.ds(pl.multiple_of(i * SUBLANES, SUBLANES), SUBLANES)
        rb = pl.ds(pl.multiple_of((T - 1 - i) * SUBLANES, SUBLANES), SUBLANES)
        new = []
        for s in range(nslab):
            hf_ = af_ref[s, rf, :] * carry[2 * s] + uf_ref[s, rf, :]
            uf_ref[s, rf, :] = hf_
            hb_ = ab_ref[s, rb, :] * carry[2 * s + 1] + ub_ref[s, rb, :]
            ub_ref[s, rb, :] = hb_
            new += [hf_, hb_]
        return tuple(new)

    init = []
    for s in range(nslab):
        init += [h0f_ref[:, s * LANES:(s + 1) * LANES], h0b_ref[:, s * LANES:(s + 1) * LANES]]
    fin = lax.fori_loop(0, T, step, tuple(init), unroll=8)
    for s in range(nslab):
        htf_ref[:, s * LANES:(s + 1) * LANES] = fin[2 * s]
        htb_ref[:, s * LANES:(s + 1) * LANES] = fin[2 * s + 1]

    for b in range(nb):
        for hf in range(_LRU_HALVES):
            s, sub = divmod(hf * nb + b, SUBLANES)
            rows = pl.ds(sub, T, stride=SUBLANES)
            hs = uf_ref[s, rows, :] + ub_ref[s, rows, :]
            g = gr_ref[b * T:(b + 1) * T, hf * LANES:(hf + 1) * LANES]
            y_ref[b * T:(b + 1) * T, hf * LANES:(hf + 1) * LANES] = (
                hs * _gelu_tanh(g)).astype(bf16)


def _rglru(proj, wcat, conv_w, conv_b, b_a, b_x, lam, h0f, h0b, T, nb):
    n = proj.shape[0]
    nslab = _LRU_HALVES * nb // SUBLANES
    W = nslab * LANES
    R = nb * T
    nct = LRU_W // _LRU_TC
    xr0 = 3 * ATT_W // _LRU_TC
    gr0 = xr0 + nct
    chan = lambda k: pl.BlockSpec((k, _LRU_TC), lambda g, j: (0, j))
    slab = pl.BlockSpec((SUBLANES, W), lambda g, j: (g, j))
    body = functools.partial(_lru_body, T=T, nb=nb)
    return pl.pallas_call(
        body,
        grid=(n // R, nct),
        in_specs=[pl.BlockSpec((R, _LRU_TC), lambda g, j: (g, xr0 + j)),
                  pl.BlockSpec((R, _LRU_TC), lambda g, j: (g, gr0 + j)),
                  pl.BlockSpec((1, _LRU_TC, 4 * _LRU_TC), lambda g, j: (j, 0, 0)),
                  chan(4), chan(1), chan(2), chan(2), chan(2), slab, slab],
        out_specs=[pl.BlockSpec((R, _LRU_TC), lambda g, j: (g, j)), slab, slab],
        out_shape=[jax.ShapeDtypeStruct((n, LRU_W), bf16),
                   jax.ShapeDtypeStruct(h0f.shape, f32),
                   jax.ShapeDtypeStruct(h0b.shape, f32)],
        scratch_shapes=[pltpu.VMEM((R, _LRU_TC), f32)]
        + [pltpu.VMEM((nslab, T * SUBLANES, LANES), f32) for _ in range(4)],
        compiler_params=_cparams(("arbitrary", "arbitrary"), 56),
        name="rg_lru",
    )(proj, proj, wcat, conv_w, conv_b, b_a, b_x, lam, h0f, h0b)


def _lru_gate_weights(w_a, w_x):
    per_tile = _LRU_TC // LRU_BLOCK
    nct = LRU_W // _LRU_TC
    eye = jnp.eye(per_tile, dtype=f32)

    def bd(w):
        w = w.reshape(nct, per_tile, LRU_BLOCK, LRU_BLOCK)
        m = w[:, :, :, None, :] * eye[None, :, None, :, None]
        return m.reshape(nct, _LRU_TC, _LRU_TC)

    return jnp.concatenate([bd(w_a[0]), bd(w_x[0]), bd(w_a[1]), bd(w_x[1])], axis=-1).astype(bf16)


def _outproj_body(att_ref, lru_ref, w_ref, x_ref, mods_ref, nw_ref, wr_ref,
                  x1_ref, h2_ref, aff_ref):
    mix = (jnp.dot(att_ref[...], w_ref[0:ATT_W, :], preferred_element_type=f32)
           + jnp.dot(lru_ref[...], w_ref[ATT_W:D, :], preferred_element_type=f32))
    m = mods_ref[0]
    x1 = x_ref[...] + m[2:3] * mix
    x1_ref[...] = x1
    ms = jnp.mean(x1 * x1, axis=-1, keepdims=True)
    h2 = (x1 * lax.rsqrt(ms + EPS) * nw_ref[...]) * (1.0 + m[4:5]) + m[3:4]
    h2_ref[...] = h2
    logits = _nt_dot(wr_ref[...], h2.astype(bf16))
    e = jnp.exp(logits - logits.max(axis=0, keepdims=True))
    aff_ref[...] = e / e.sum(axis=0, keepdims=True)


def _outproj(att, lru, w_out_b, x2d, mods, norm_w, w_router_t, tiles_per_row):
    n = x2d.shape[0]
    tm = 512
    row = (lambda i: 1 + i // tiles_per_row) if tiles_per_row else (lambda i: 0)
    return pl.pallas_call(
        _outproj_body,
        grid=(n // tm,),
        in_specs=[pl.BlockSpec((tm, ATT_W), lambda i: (i, 0)),
                  pl.BlockSpec((tm, LRU_W), lambda i: (i, 0)),
                  pl.BlockSpec((D, D), lambda i: (0, 0)),
                  pl.BlockSpec((tm, D), lambda i: (i, 0)),
                  pl.BlockSpec((1, 6, D), lambda i: (row(i), 0, 0)),
                  pl.BlockSpec((1, D), lambda i: (0, 0)),
                  pl.BlockSpec((N_EXPERTS, D), lambda i: (0, 0))],
        out_specs=[pl.BlockSpec((tm, D), lambda i: (i, 0)),
                   pl.BlockSpec((tm, D), lambda i: (i, 0)),
                   pl.BlockSpec((N_EXPERTS, tm), lambda i: (0, i))],
        out_shape=[jax.ShapeDtypeStruct((n, D), f32),
                   jax.ShapeDtypeStruct((n, D), f32),
                   jax.ShapeDtypeStruct((N_EXPERTS, n), f32)],
        compiler_params=_cparams(("arbitrary",), 52),
        name="out_projection_router",
    )(att, lru, w_out_b, x2d, mods, norm_w, w_router_t)


def _prefix_sum_lanes(x, lane, n):
    s = 1
    while s < n:
        x = x + jnp.where(lane >= s, pltpu.roll(x, s, 1), 0)
        s *= 2
    return x


def _route_body(aff_ref, idx_ref, g_ref, pos_ref, sp_ref, *, n, cap):
    aff = aff_ref[...]
    bits = pltpu.bitcast(aff, i32)
    lane = lax.broadcasted_iota(i32, (N_EXPERTS, n), 1)

    def search(i, thr):
        cand = thr | jnp.left_shift(jnp.int32(1), 30 - i)
        cnt = jnp.sum(jnp.where(bits >= cand, 1.0, 0.0), axis=1, keepdims=True)
        return jnp.where(cnt >= cap, cand, thr)

    thr = lax.fori_loop(0, 31, search, jnp.zeros((N_EXPERTS, 1), i32))
    gt = bits > thr
    eq = bits == thr
    need = cap - jnp.sum(jnp.where(gt, 1.0, 0.0), axis=1, keepdims=True).astype(i32)
    eq_rank = _prefix_sum_lanes(eq.astype(i32), lane, n)
    sel = gt | (eq & (eq_rank <= need))
    sel_i = sel.astype(i32)
    pos = _prefix_sum_lanes(sel_i, lane, n) - sel_i
    pos_ref[...] = pos
    sp_ref[...] = jnp.where(sel, pos, UNSELECTED)

    key = jnp.where(sel, lane - pos, -1)
    idx = lane
    g = aff
    b = 0
    while (1 << b) < n:
        s = 1 << b
        k_sh = pltpu.roll(key, n - s, 1)
        take = (k_sh >= 0) & ((jnp.right_shift(k_sh, b) & 1) == 1)
        stay = (key >= 0) & ((jnp.right_shift(key, b) & 1) == 0)
        idx = jnp.where(take, pltpu.roll(idx, n - s, 1), idx)
        g = jnp.where(take, pltpu.roll(g, n - s, 1), g)
        key = jnp.where(take, k_sh, jnp.where(stay, key, -1))
        b += 1
    idx_ref[...] = idx[:, :cap]
    g_ref[...] = g[:, :cap]


def _route(aff_t, cap):
    n = aff_t.shape[1]
    body = functools.partial(_route_body, n=n, cap=cap)
    full = lambda shape: pl.BlockSpec(shape, lambda: (0, 0))
    return pl.pallas_call(
        body,
        in_specs=[full((N_EXPERTS, n))],
        out_specs=[full((N_EXPERTS, cap)), full((N_EXPERTS, cap)),
                   full((N_EXPERTS, n)), full((N_EXPERTS, n))],
        out_shape=[jax.ShapeDtypeStruct((N_EXPERTS, cap), i32),
                   jax.ShapeDtypeStruct((N_EXPERTS, cap), f32),
                   jax.ShapeDtypeStruct((N_EXPERTS, n), i32),
                   jax.ShapeDtypeStruct((N_EXPERTS, n), i32)],
        compiler_params=pltpu.CompilerParams(vmem_limit_bytes=32 * _MIB),
        name="expert_choice_routing",
    )(aff_t)


_TF = 512
_TN_DOWN = 512


def _expert_body(idx_ref, h2_hbm, g_ref, wg_ref, wu_ref, wd_ref, o_ref, xe_ref, xb_ref, he_ref,
                 sem, *, cap, na, nd):
    e = pl.program_id(0)
    s = pl.program_id(1)
    chunk = cap // nd

    def row_copy(tok, r):
        return pltpu.make_async_copy(h2_hbm.at[pl.ds(tok, 1), :], xe_ref.at[pl.ds(r, 1), :], sem)

    def wait_all_rows():
        pltpu.make_async_copy(h2_hbm.at[pl.ds(0, cap), :], xe_ref, sem).wait()

    @pl.when((e == 0) & (s == 0))
    def _():
        def issue(r, c):
            row_copy(idx_ref[0, r], r).start()
            return c

        lax.fori_loop(0, cap, issue, 0, unroll=8)

    @pl.when(s == 0)
    def _():
        wait_all_rows()
        xb_ref[...] = xe_ref[...].astype(bf16)

    @pl.when(s < na)
    def _():
        xb = xb_ref[...]
        gp = jnp.dot(xb, wg_ref[0].astype(bf16), preferred_element_type=f32)
        up = jnp.dot(xb, wu_ref[0].astype(bf16), preferred_element_type=f32)
        he = ((gp * jax.nn.sigmoid(gp)) * up).astype(bf16)
        for k in range(na):
            @pl.when(s == k)
            def _(k=k):
                he_ref[:, k * _TF:(k + 1) * _TF] = he

    @pl.when(s >= na)
    def _():
        t = s - na
        nxt = jnp.minimum(e + 1, N_EXPERTS - 1)
        for r in range(chunk):
            row_copy(idx_ref[nxt, t * chunk + r], t * chunk + r).start()
        acc = jnp.dot(he_ref[...], wd_ref[0].astype(bf16), preferred_element_type=f32)
        o_ref[0] = acc * g_ref[0]

    @pl.when((e == N_EXPERTS - 1) & (s == na + nd - 1))
    def _():
        wait_all_rows()


def _expert_ffn(idx, h2, g3, w_gate, w_up, w_down):
    cap = idx.shape[1]
    na = D_FF // _TF
    nd = D // _TN_DOWN
    body = functools.partial(_expert_body, cap=cap, na=na, nd=nd)
    up_tile = lambda e, s, idx: (e, 0, jnp.minimum(s, na - 1))
    down_tile = lambda e, s, idx: (e, 0, jnp.maximum(s - na, 0))
    grid_spec = pltpu.PrefetchScalarGridSpec(
        num_scalar_prefetch=1,
        grid=(N_EXPERTS, na + nd),
        in_specs=[pl.BlockSpec(memory_space=pl.ANY),
                  pl.BlockSpec((1, cap, 1), lambda e, s, idx: (e, 0, 0)),
                  pl.BlockSpec((1, D, _TF), up_tile),
                  pl.BlockSpec((1, D, _TF), up_tile),
                  pl.BlockSpec((1, D_FF, _TN_DOWN), down_tile)],
        out_specs=pl.BlockSpec((1, cap, _TN_DOWN), down_tile),
        scratch_shapes=[pltpu.VMEM((cap, D), f32), pltpu.VMEM((cap, D), bf16),
                        pltpu.VMEM((cap, D_FF), bf16), pltpu.SemaphoreType.DMA(())],
    )
    return pl.pallas_call(
        body,
        grid_spec=grid_spec,
        out_shape=jax.ShapeDtypeStruct((N_EXPERTS, cap, D), f32),
        compiler_params=_cparams(("arbitrary", "arbitrary"), 60),
        name="expert_ffn",
    )(idx, h2, g3, w_gate, w_up, w_down)


_TT = 128
_KB = 256
_RUN_MAX = _TT + 2 * SUBLANES
_RUN_PIECES = tuple(SUBLANES << i for i in reversed(range((_RUN_MAX // SUBLANES).bit_length())))
_STAGE_ROWS = N_EXPERTS * _RUN_MAX
_STAGE_PIECES = tuple(SUBLANES << i
                      for i in reversed(range((_STAGE_ROWS // SUBLANES).bit_length())))


def _combine_body(offs_ref, ye_hbm, sp_ref, x1_ref, mods_ref, o_ref, stage_ref, acc_ref, sem,
                  *, cap):
    j = pl.program_id(0)
    slot = j & 1

    def wait_rows(total, sl):
        for size in _STAGE_PIECES:
            @pl.when((total & size) != 0)
            def _(size=size):
                pltpu.make_async_copy(ye_hbm.at[pl.ds(0, size), :],
                                      stage_ref.at[sl, pl.ds(0, size), :], sem.at[sl]).wait()

    def tile_copies(jj, sl, start):
        bases, offs = [], []
        base = jnp.int32(0)
        for e in range(N_EXPERTS):
            off = offs_ref[e, jj]
            end = offs_ref[e, jj + 1]
            off8 = (off // SUBLANES) * SUBLANES
            length = jnp.where(end > off, ((end + SUBLANES - 1) // SUBLANES) * SUBLANES - off8, 0)
            bases.append(base)
            offs.append(off8)
            src = e * cap + off8
            dst = base
            for size in _RUN_PIECES if start else ():
                piece = length & size

                @pl.when(piece != 0)
                def _(src=src, dst=dst, size=size):
                    pltpu.make_async_copy(
                        ye_hbm.at[pl.ds(pl.multiple_of(src, SUBLANES), size), :],
                        stage_ref.at[sl, pl.ds(pl.multiple_of(dst, SUBLANES), size), :],
                        sem.at[sl]).start()

                src = src + piece
                dst = dst + piece
            base = base + length
        return bases, offs, base

    @pl.when(j == 0)
    def _():
        stage_ref[...] = jnp.zeros_like(stage_ref)
        tile_copies(0, 0, True)

    @pl.when(j + 1 < pl.num_programs(0))
    def _():
        tile_copies(j + 1, 1 - slot, True)

    bases, offs, total = tile_copies(j, slot, False)
    wait_rows(total, slot)

    sp = sp_ref[...]
    kiota = lax.broadcasted_iota(i32, (1, _KB), 1)
    acc_ref[...] = jnp.zeros_like(acc_ref)

    def kblock(kb, c):
        hit = None
        for e in range(N_EXPERTS):
            local = sp[:, e:e + 1] + (bases[e] - offs[e] - kb * _KB)
            h = local == kiota
            hit = h if hit is None else (hit | h)
        s = jnp.where(hit, 1.0, 0.0).astype(bf16)
        rows = stage_ref[slot, pl.ds(pl.multiple_of(kb * _KB, _KB), _KB), :]
        hi = rows.astype(bf16)
        lo = (rows - hi.astype(f32)).astype(bf16)
        acc_ref[...] += (jnp.dot(s, hi, preferred_element_type=f32)
                         + jnp.dot(s, lo, preferred_element_type=f32))
        return c

    lax.fori_loop(0, (total + _KB - 1) // _KB, kblock, 0)
    o_ref[...] = x1_ref[...] + mods_ref[0][5:6] * acc_ref[...]


def _combine(offs, ye_flat, sp_n, x1, mods, cap, tiles_per_row):
    n = x1.shape[0]
    row = (lambda j: 1 + j // tiles_per_row) if tiles_per_row else (lambda j: 0)
    body = functools.partial(_combine_body, cap=cap)
    grid_spec = pltpu.PrefetchScalarGridSpec(
        num_scalar_prefetch=1,
        grid=(n // _TT,),
        in_specs=[pl.BlockSpec(memory_space=pl.ANY),
                  pl.BlockSpec((_TT, N_EXPERTS), lambda j, offs: (j, 0)),
                  pl.BlockSpec((_TT, D), lambda j, offs: (j, 0)),
                  pl.BlockSpec((1, 6, D), lambda j, offs: (row(j), 0, 0))],
        out_specs=pl.BlockSpec((_TT, D), lambda j, offs: (j, 0)),
        scratch_shapes=[pltpu.VMEM((2, N_EXPERTS * _RUN_MAX, D), f32), pltpu.VMEM((_TT, D), f32),
                        pltpu.SemaphoreType.DMA((2,))],
    )
    return pl.pallas_call(
        body,
        grid_spec=grid_spec,
        out_shape=jax.ShapeDtypeStruct((n, D), f32),
        compiler_params=_cparams(("arbitrary",), 48),
        name="expert_combine",
    )(offs, ye_flat, sp_n, x1, mods)


def _moe(x1, h2, aff_t, mods, w_gate, w_up, w_down, tiles_per_row):
    n = x1.shape[0]
    cap = CAPACITY_FACTOR * n // N_EXPERTS
    idx, g, pos, sp = _route(aff_t, cap)
    ye = _expert_ffn(idx, h2, g[:, :, None], w_gate, w_up, w_down)
    offs = jnp.concatenate([pos[:, ::_TT], jnp.full((N_EXPERTS, 1), cap, i32)], axis=1)
    return _combine(offs, ye.reshape(N_EXPERTS * cap, D), sp.T, x1, mods, cap, tiles_per_row)


def kernel(x_prompt, x_sample, cache_k, cache_v, state_fwd, state_bwd, c, c_ctx, w_mod, b_mod,
           norm_mix, w_in, q_norm, k_norm, rpb, conv_w, conv_b, lru_w_a, lru_b_a, lru_w_x,
           lru_b_x, lru_lambda, w_out, norm_ffn, w_router, w_gate, w_up, w_down):
    B, S, _ = x_prompt.shape
    DB, T, _ = x_sample.shape
    l = 0

    cvecs = jnp.concatenate([c_ctx[None, :], c, jnp.zeros((SUBLANES - 1 - DB, D), f32)], axis=0)
    mods = _adaln_mods(cvecs, w_mod[l], b_mod[l][None, :]).reshape(SUBLANES, 6, D)

    heads_per_tile = _TN_IN // HEAD_DIM
    qn_t = jnp.tile(q_norm[l], heads_per_tile)[None, :]
    kn_t = jnp.tile(k_norm[l], heads_per_tile)[None, :]
    gmat = (jnp.kron(jnp.eye(heads_per_tile, dtype=f32), jnp.ones((HEAD_DIM, HEAD_DIM), f32))
            / HEAD_DIM).astype(bf16)
    nmix = norm_mix[l][None, :]
    nffn = norm_ffn[l][None, :]
    w_in_b = w_in[l].astype(bf16)
    w_out_b = w_out[l].astype(bf16)
    w_router_t = w_router[l].T.astype(bf16)
    wcat = _lru_gate_weights(lru_w_a[l], lru_w_x[l])
    cb = conv_b[l][None, :]

    xp = x_prompt.reshape(B * S, D)
    xs = x_sample.reshape(DB * T, D)

    proj_c = _inproj(xp, mods, nmix, w_in_b, qn_t, kn_t, gmat, per_tile_rows=False)
    att_c, new_k, new_v = _ctx_attention(proj_c, B, S)
    zeros_c = jnp.zeros((B, LRU_W), f32)
    lru_c, htf, htb = _rglru(proj_c, wcat, conv_w[l], cb, lru_b_a[l], lru_b_x[l], lru_lambda[l],
                             zeros_c, zeros_c, T=S, nb=SUBLANES)
    x1_c, h2_c, aff_c = _outproj(att_c, lru_c, w_out_b, xp, mods, nffn, w_router_t, 0)
    y_c = _moe(x1_c, h2_c, aff_c, mods, w_gate[l], w_up[l], w_down[l], 0)


    nct = LRU_W // _LRU_TC

    def slabs(h0):
        return (h0.reshape(DB, nct, _LRU_HALVES, LANES).transpose(2, 0, 1, 3)
                .reshape(_LRU_HALVES * DB, nct * LANES))

    proj_l = _inproj(xs, mods, nmix, w_in_b, qn_t, kn_t, gmat, per_tile_rows=True)
    att_l = _lat_attention(proj_l, cache_k, cache_v, _pad_rpb(rpb[l]), l, DB, T)
    h0f_l = slabs(state_fwd[:, l])
    h0b_l = slabs(state_bwd[:, l])
    lru_l, _, _ = _rglru(proj_l, wcat, conv_w[l], cb, lru_b_a[l], lru_b_x[l], lru_lambda[l],
                         h0f_l, h0b_l, T=T, nb=DB)
    x1_l, h2_l, aff_l = _outproj(att_l, lru_l, w_out_b, xs, mods, nffn, w_router_t, T // 512)
    y_l = _moe(x1_l, h2_l, aff_l, mods, w_gate[l], w_up[l], w_down[l], T // _TT)

    return (y_c.reshape(B, S, D), y_l.reshape(DB, T, D), new_k, new_v,
            htf[:, None, :], htb[:, None, :])
```

```python
import functools

import jax
import jax.numpy as jnp
from jax import lax
from jax.experimental import pallas as pl
from jax.experimental.pallas import tpu as pltpu

f32 = jnp.float32
bf16 = jnp.bfloat16
i32 = jnp.int32

D = 2048
N_HEADS = 16
HEAD_DIM = 64
ATT_W = N_HEADS * HEAD_DIM
LRU_W = D - ATT_W
LRU_BLOCK = 64
LRU_C = 8.0
GRID_W = 64
WIN_H = 8
WIN_W = 16
N_EXPERTS = 16
CAPACITY_FACTOR = 2
D_FF = 2048
EPS = 1e-6
NEG_INF = -1e30
IN_COLS = 3 * ATT_W + 2 * LRU_W

LANES = 128
SUBLANES = 8
MXU_DIM = 256
UNSELECTED = -(1 << 30)

_MIB = 1024 * 1024


def _cparams(sem, vmem_mib):
    return pltpu.CompilerParams(dimension_semantics=sem, vmem_limit_bytes=vmem_mib * _MIB)


def _mods_body(c_ref, w_ref, b_ref, o_ref):
    c = c_ref[...]
    s = (c * jax.nn.sigmoid(c)).astype(bf16)
    o_ref[...] = jnp.dot(s, w_ref[...].astype(bf16), preferred_element_type=f32) + b_ref[...]


def _adaln_mods(cvecs, w_mod, b_mod):
    tn = 1024
    return pl.pallas_call(
        _mods_body,
        grid=(6 * D // tn,),
        in_specs=[pl.BlockSpec((SUBLANES, D), lambda j: (0, 0)),
                  pl.BlockSpec((D, tn), lambda j: (0, j)),
                  pl.BlockSpec((1, tn), lambda j: (0, j))],
        out_specs=pl.BlockSpec((SUBLANES, tn), lambda j: (0, j)),
        out_shape=jax.ShapeDtypeStruct((SUBLANES, 6 * D), f32),
        compiler_params=_cparams(("arbitrary",), 40),
        name="adaln_mods",
    )(cvecs, w_mod, b_mod)


_TN_IN = 512
_NORM_ROWS = 16


def _inproj_body(x_ref, mods_ref, nw_ref, w_ref, qn_ref, kn_ref, g_ref, o_ref, h_ref):
    j = pl.program_id(1)

    @pl.when(j == 0)
    def _():
        m = mods_ref[0]
        gain = 1.0 + m[1:2]
        shift = m[0:1]
        nw = nw_ref[...]

        def norm_rows(c, carry):
            rows = pl.ds(pl.multiple_of(c * _NORM_ROWS, _NORM_ROWS), _NORM_ROWS)
            x = x_ref[rows, :]
            ms = jnp.mean(x * x, axis=-1, keepdims=True)
            y = x * lax.rsqrt(ms + EPS) * nw
            h_ref[rows, :] = (y * gain + shift).astype(bf16)
            return carry

        lax.fori_loop(0, h_ref.shape[0] // _NORM_ROWS, norm_rows, 0, unroll=4)

    acc = jnp.dot(h_ref[...], w_ref[...], preferred_element_type=f32)
    n_qk_tiles = 2 * ATT_W // _TN_IN

    @pl.when(j < n_qk_tiles)
    def _():
        sq = acc * acc
        hi = sq.astype(bf16)
        lo = (sq - hi.astype(f32)).astype(bf16)
        g = g_ref[...]
        ms = (jnp.dot(hi, g, preferred_element_type=f32)
              + jnp.dot(lo, g, preferred_element_type=f32))
        nw = jnp.where(j < n_qk_tiles // 2, qn_ref[...], kn_ref[...])
        o_ref[...] = acc * lax.rsqrt(ms + EPS) * nw

    @pl.when(j >= n_qk_tiles)
    def _():
        o_ref[...] = acc


def _inproj(x2d, mods, norm_w, w_in, qn_t, kn_t, gmat, per_tile_rows):
    n = x2d.shape[0]
    tm = 1024
    row = (lambda i: 1 + i) if per_tile_rows else (lambda i: 0)
    return pl.pallas_call(
        _inproj_body,
        grid=(n // tm, IN_COLS // _TN_IN),
        in_specs=[pl.BlockSpec((tm, D), lambda i, j: (i, 0)),
                  pl.BlockSpec((1, 6, D), lambda i, j: (row(i), 0, 0)),
                  pl.BlockSpec((1, D), lambda i, j: (0, 0)),
                  pl.BlockSpec((D, _TN_IN), lambda i, j: (0, j)),
                  pl.BlockSpec((1, _TN_IN), lambda i, j: (0, 0)),
                  pl.BlockSpec((1, _TN_IN), lambda i, j: (0, 0)),
                  pl.BlockSpec((_TN_IN, _TN_IN), lambda i, j: (0, 0))],
        out_specs=pl.BlockSpec((tm, _TN_IN), lambda i, j: (i, j)),
        out_shape=jax.ShapeDtypeStruct((n, IN_COLS), f32),
        scratch_shapes=[pltpu.VMEM((tm, D), bf16)],
        compiler_params=_cparams(("arbitrary", "arbitrary"), 52),
        name="in_projection",
    )(x2d, mods, norm_w, w_in, qn_t, kn_t, gmat)


def _softmax_pv(s_list, v_list):
    m = s_list[0].max(axis=-1, keepdims=True)
    for s in s_list[1:]:
        m = jnp.maximum(m, s.max(axis=-1, keepdims=True))
    l = None
    o = None
    for s, v in zip(s_list, v_list):
        e = jnp.exp(s - m)
        ls = e.sum(axis=-1, keepdims=True)
        os_ = jnp.dot(e.astype(bf16), v, preferred_element_type=f32)
        l = ls if l is None else l + ls
        o = os_ if o is None else o + os_
    return o, l


def _nt_dot(a, b):
    return lax.dot_general(a, b, (((1,), (1,)), ((), ())), preferred_element_type=f32)


_QK_SCALE = HEAD_DIM ** -0.5


def _ctx_attn_body(q_ref, k_ref, v_ref, o_ref, nk_ref, nv_ref):
    lane = lax.broadcasted_iota(i32, (1, LANES), 1)
    for p in range(ATT_W // LANES):
        cols = slice(p * LANES, (p + 1) * LANES)
        q = q_ref[:, cols] * _QK_SCALE
        kf = k_ref[:, cols]
        k = kf.astype(bf16)
        v = v_ref[:, cols]
        nk_ref[0, 0, 2 * p] = kf[:, :HEAD_DIM]
        nk_ref[0, 0, 2 * p + 1] = pltpu.roll(kf, HEAD_DIM, 1)[:, :HEAD_DIM]
        nv_ref[0, 0, 2 * p] = v[:, :HEAD_DIM]
        nv_ref[0, 0, 2 * p + 1] = pltpu.roll(v, HEAD_DIM, 1)[:, :HEAD_DIM]
        out = None
        for hh in range(2):
            hm = (lane >= hh * HEAD_DIM) & (lane < (hh + 1) * HEAD_DIM)
            qh = jnp.where(hm, q, 0.0).astype(bf16)
            vh = jnp.where(hm, v, 0.0).astype(bf16)
            s = _nt_dot(qh, k)
            o, l = _softmax_pv([s], [vh])
            o = o / l
            out = o if out is None else out + o
        o_ref[:, cols] = out.astype(bf16)


def _ctx_attention(proj, batch, seq):
    n = proj.shape[0]
    cache = pl.BlockSpec((1, 1, N_HEADS, seq, HEAD_DIM), lambda b: (b, 0, 0, 0, 0))
    cache_shape = jax.ShapeDtypeStruct((batch, 1, N_HEADS, seq, HEAD_DIM), f32)
    return pl.pallas_call(
        _ctx_attn_body,
        grid=(batch,),
        in_specs=[pl.BlockSpec((seq, ATT_W), lambda b: (b, 0)),
                  pl.BlockSpec((seq, ATT_W), lambda b: (b, 1)),
                  pl.BlockSpec((seq, ATT_W), lambda b: (b, 2))],
        out_specs=[pl.BlockSpec((seq, ATT_W), lambda b: (b, 0)), cache, cache],
        out_shape=[jax.ShapeDtypeStruct((n, ATT_W), bf16), cache_shape, cache_shape],
        compiler_params=_cparams(("arbitrary",), 32),
        name="context_attention",
    )(proj, proj, proj)


_NA_ROWS = 16
_NA_QROWS = 8
_NA_KROWS = 12


def _pad_rpb(rpb):
    nd, nc = 2 * WIN_H - 1, 2 * WIN_W - 1
    return jnp.pad(rpb.astype(f32), ((0, 0), (0, 2 * SUBLANES - nd), (0, LANES - nc)))


def _build_na_bias(rpb_ref, bias_ref):
    qc = lax.broadcasted_iota(i32, (GRID_W, LANES), 0)
    lane = lax.broadcasted_iota(i32, (GRID_W, LANES), 1)
    kc = lane & (GRID_W - 1)
    cs = jnp.clip(qc - WIN_W // 2, 0, GRID_W - WIN_W)
    col_ok = (kc >= cs) & (kc < cs + WIN_W)
    left = lane < GRID_W
    neg = jnp.full((GRID_W, LANES), NEG_INF, f32)
    for hh in range(2):
        tbl = rpb_ref[hh]
        tiles = {}

        def toeplitz(d, odd):
            if (d, odd) not in tiles:
                row = jnp.broadcast_to(tbl[d:d + 1, :], (GRID_W, LANES))
                shift = (LANES - (WIN_W - 1) + odd * GRID_W) % LANES
                t = pltpu.roll(row, shift, 1, stride=1, stride_axis=0)
                tiles[(d, odd)] = jnp.where(col_ok, t, NEG_INF)
            return tiles[(d, odd)]

        for j in range(2):
            for rq in range(_NA_QROWS):
                rq_abs = j * _NA_QROWS + rq
                rs = min(max(rq_abs - WIN_H // 2, 0), _NA_ROWS - WIN_H)
                for kp in range(_NA_KROWS // 2):
                    parts = []
                    for odd in range(2):
                        rk_abs = j * (_NA_ROWS - _NA_KROWS) + 2 * kp + odd
                        if rs <= rk_abs < rs + WIN_H:
                            parts.append(toeplitz(rk_abs - rq_abs + WIN_H - 1, odd))
                        else:
                            parts.append(neg)
                    bias_ref[hh, j, rq * GRID_W:(rq + 1) * GRID_W,
                             kp * LANES:(kp + 1) * LANES] = jnp.where(left, parts[0], parts[1])


def _lat_attn_body(q_ref, k_ref, v_ref, ck_ref, cv_ref, rpb_ref, o_ref, bias_ref):
    @pl.when(pl.program_id(1) == 0)
    def _():
        _build_na_bias(rpb_ref, bias_ref)

    lane = lax.broadcasted_iota(i32, (1, LANES), 1)
    nq = _NA_QROWS * GRID_W
    nk = _NA_KROWS * GRID_W
    kstep = (_NA_ROWS - _NA_KROWS) * GRID_W
    src = lax.broadcasted_iota(i32, (HEAD_DIM, LANES), 0)
    dst = lax.broadcasted_iota(i32, (HEAD_DIM, LANES), 1)
    ck = None
    cv_h = []
    for hh in range(2):
        place = jnp.where(dst == src + hh * HEAD_DIM, 1.0, 0.0).astype(bf16)
        part = jnp.dot(ck_ref[0, 0, hh].astype(bf16), place, preferred_element_type=f32)
        ck = part if ck is None else ck + part
        cv_h.append(jnp.dot(cv_ref[0, 0, hh].astype(bf16), place,
                            preferred_element_type=f32).astype(bf16))
    ck = ck.astype(bf16)
    for j in range(2):
        q = q_ref[j * nq:(j + 1) * nq, :] * _QK_SCALE
        k = k_ref[j * kstep:j * kstep + nk, :].astype(bf16)
        v = v_ref[j * kstep:j * kstep + nk, :]
        out = None
        for hh in range(2):
            hm = (lane >= hh * HEAD_DIM) & (lane < (hh + 1) * HEAD_DIM)
            qh = jnp.where(hm, q, 0.0).astype(bf16)
            s_win = _nt_dot(qh, k) + bias_ref[hh, j]
            s_ctx = _nt_dot(qh, ck)
            vh = jnp.where(hm, v, 0.0).astype(bf16)
            cvh = cv_h[hh]
            o, l = _softmax_pv([s_win, s_ctx], [vh, cvh])
            o = o / l
            out = o if out is None else out + o
        o_ref[j * nq:(j + 1) * nq, :] = out.astype(bf16)


def _lat_attention(proj, ck, cv, rpb_p, layer, batch, seq):
    n = proj.shape[0]
    past = ck.shape[3]
    cache = pl.BlockSpec((1, 1, 2, past, HEAD_DIM), lambda p, b: (b, layer, p, 0, 0))
    npair = ATT_W // LANES
    nq = _NA_QROWS * GRID_W
    nk = _NA_KROWS * GRID_W
    return pl.pallas_call(
        _lat_attn_body,
        grid=(npair, batch),
        in_specs=[pl.BlockSpec((seq, LANES), lambda p, b: (b, p)),
                  pl.BlockSpec((seq, LANES), lambda p, b: (b, npair + p)),
                  pl.BlockSpec((seq, LANES), lambda p, b: (b, 2 * npair + p)),
                  cache, cache,
                  pl.BlockSpec((2, 2 * SUBLANES, LANES), lambda p, b: (p, 0, 0))],
        out_specs=pl.BlockSpec((seq, LANES), lambda p, b: (b, p)),
        out_shape=jax.ShapeDtypeStruct((n, ATT_W), bf16),
        scratch_shapes=[pltpu.VMEM((2, 2, nq, nk), f32)],
        compiler_params=_cparams(("arbitrary", "arbitrary"), 48),
        name="neighbourhood_attention",
    )(proj, proj, proj, ck, cv, rpb_p)


_LRU_TC = 256
_GATE_ROWS = 256
_LRU_HALVES = _LRU_TC // LANES


def _gelu_tanh(x):
    c = 0.7978845608028654
    return x * (0.5 * (1.0 + jnp.tanh(c * (x + 0.044715 * (x * x * x)))))


def _lru_body(xr_ref, gr_ref, w_ref, cw_ref, cb_ref, ba_ref, bx_ref, lam_ref, h0f_ref, h0b_ref,
              y_ref, htf_ref, htb_ref, xc_ref, af_ref, uf_ref, ab_ref, ub_ref, *, T, nb):
    nslab = _LRU_HALVES * nb // SUBLANES
    R = nb * T
    x = xr_ref[...]
    t = lax.broadcasted_iota(i32, (R, 1), 0) & (T - 1)
    cw = cw_ref[...]
    xc = jnp.where(t >= 2, pltpu.roll(x, 2, 0), 0.0) * cw[0:1]
    xc = xc + jnp.where(t >= 1, pltpu.roll(x, 1, 0), 0.0) * cw[1:2]
    xc = xc + x * cw[2:3]
    xc = xc + jnp.where(t < T - 1, pltpu.roll(x, R - 1, 0), 0.0) * cw[3:4]
    xc_ref[...] = xc + cb_ref[...]

    nl = -lam_ref[...]
    sp = jnp.maximum(nl, 0.0) + jnp.log1p(jnp.exp(-jnp.abs(nl)))
    half_c_sp = (0.5 * LRU_C) * sp
    w = w_ref[0]
    dirs = ((af_ref, uf_ref), (ab_ref, ub_ref))
    for b in range(nb):
        for ci in range(T // _GATE_ROWS):
            r0 = b * T + ci * _GATE_ROWS
            xcc = xc_ref[r0:r0 + _GATE_ROWS, :]
            half_x = 0.5 * xcc
            z = jnp.dot(xcc.astype(bf16), w, preferred_element_type=f32)
            for d, (a_ref, u_ref) in enumerate(dirs):
                c0 = d * 2 * _LRU_TC
                tr = jnp.tanh(0.5 * (z[:, c0:c0 + _LRU_TC] + ba_ref[d:d + 1]))
                tg = jnp.tanh(0.5 * (z[:, c0 + _LRU_TC:c0 + 2 * _LRU_TC] + bx_ref[d:d + 1]))
                neg_log_a = (1.0 + tr) * half_c_sp[d:d + 1]
                a = jnp.exp(-neg_log_a)
                u = jnp.sqrt(jnp.tanh(neg_log_a) * (1.0 + a * a)) * ((1.0 + tg) * half_x)
                for hf in range(_LRU_HALVES):
                    s, sub = divmod(hf * nb + b, SUBLANES)
                    rows = pl.ds(ci * _GATE_ROWS * SUBLANES + sub, _GATE_ROWS, stride=SUBLANES)
                    a_ref[s, rows, :] = a[:, hf * LANES:(hf + 1) * LANES]
                    u_ref[s, rows, :] = u[:, hf * LANES:(hf + 1) * LANES]

    def step(i, carry):
        rf = pl.ds(pl.multiple_of(i * SUBLANES, SUBLANES), SUBLANES)
        rb = pl.ds(pl.multiple_of((T - 1 - i) * SUBLANES, SUBLANES), SUBLANES)
        new = []
        for s in range(nslab):
            hf_ = af_ref[s, rf, :] * carry[2 * s] + uf_ref[s, rf, :]
            uf_ref[s, rf, :] = hf_
            hb_ = ab_ref[s, rb, :] * carry[2 * s + 1] + ub_ref[s, rb, :]
            ub_ref[s, rb, :] = hb_
            new += [hf_, hb_]
        return tuple(new)

    init = []
    for s in range(nslab):
        init += [h0f_ref[:, s * LANES:(s + 1) * LANES], h0b_ref[:, s * LANES:(s + 1) * LANES]]
    fin = lax.fori_loop(0, T, step, tuple(init), unroll=8)
    for s in range(nslab):
        htf_ref[:, s * LANES:(s + 1) * LANES] = fin[2 * s]
        htb_ref[:, s * LANES:(s + 1) * LANES] = fin[2 * s + 1]

    for b in range(nb):
        for hf in range(_LRU_HALVES):
            s, sub = divmod(hf * nb + b, SUBLANES)
            rows = pl.ds(sub, T, stride=SUBLANES)
            hs = uf_ref[s, rows, :] + ub_ref[s, rows, :]
            g = gr_ref[b * T:(b + 1) * T, hf * LANES:(hf + 1) * LANES]
            y_ref[b * T:(b + 1) * T, hf * LANES:(hf + 1) * LANES] = (
                hs * _gelu_tanh(g)).astype(bf16)


def _rglru(proj, wcat, conv_w, conv_b, b_a, b_x, lam, h0f, h0b, T, nb):
    n = proj.shape[0]
    nslab = _LRU_HALVES * nb // SUBLANES
    W = nslab * LANES
    R = nb * T
    nct = LRU_W // _LRU_TC
    xr0 = 3 * ATT_W // _LRU_TC
    gr0 = xr0 + nct
    chan = lambda k: pl.BlockSpec((k, _LRU_TC), lambda g, j: (0, j))
    slab = pl.BlockSpec((SUBLANES, W), lambda g, j: (g, j))
    body = functools.partial(_lru_body, T=T, nb=nb)
    return pl.pallas_call(
        body,
        grid=(n // R, nct),
        in_specs=[pl.BlockSpec((R, _LRU_TC), lambda g, j: (g, xr0 + j)),
                  pl.BlockSpec((R, _LRU_TC), lambda g, j: (g, gr0 + j)),
                  pl.BlockSpec((1, _LRU_TC, 4 * _LRU_TC), lambda g, j: (j, 0, 0)),
                  chan(4), chan(1), chan(2), chan(2), chan(2), slab, slab],
        out_specs=[pl.BlockSpec((R, _LRU_TC), lambda g, j: (g, j)), slab, slab],
        out_shape=[jax.ShapeDtypeStruct((n, LRU_W), bf16),
                   jax.ShapeDtypeStruct(h0f.shape, f32),
                   jax.ShapeDtypeStruct(h0b.shape, f32)],
        scratch_shapes=[pltpu.VMEM((R, _LRU_TC), f32)]
        + [pltpu.VMEM((nslab, T * SUBLANES, LANES), f32) for _ in range(4)],
        compiler_params=_cparams(("arbitrary", "arbitrary"), 56),
        name="rg_lru",
    )(proj, proj, wcat, conv_w, conv_b, b_a, b_x, lam, h0f, h0b)


def _lru_gate_weights(w_a, w_x):
    per_tile = _LRU_TC // LRU_BLOCK
    nct = LRU_W // _LRU_TC
    eye = jnp.eye(per_tile, dtype=f32)

    def bd(w):
        w = w.reshape(nct, per_tile, LRU_BLOCK, LRU_BLOCK)
        m = w[:, :, :, None, :] * eye[None, :, None, :, None]
        return m.reshape(nct, _LRU_TC, _LRU_TC)

    return jnp.concatenate([bd(w_a[0]), bd(w_x[0]), bd(w_a[1]), bd(w_x[1])], axis=-1).astype(bf16)


def _outproj_body(att_ref, lru_ref, w_ref, x_ref, mods_ref, nw_ref, wr_ref,
                  x1_ref, h2_ref, aff_ref):
    mix = (jnp.dot(att_ref[...], w_ref[0:ATT_W, :], preferred_element_type=f32)
           + jnp.dot(lru_ref[...], w_ref[ATT_W:D, :], preferred_element_type=f32))
    m = mods_ref[0]
    x1 = x_ref[...] + m[2:3] * mix
    x1_ref[...] = x1
    ms = jnp.mean(x1 * x1, axis=-1, keepdims=True)
    h2 = (x1 * lax.rsqrt(ms + EPS) * nw_ref[...]) * (1.0 + m[4:5]) + m[3:4]
    h2_ref[...] = h2
    logits = _nt_dot(wr_ref[...], h2.astype(bf16))
    e = jnp.exp(logits - logits.max(axis=0, keepdims=True))
    aff_ref[...] = e / e.sum(axis=0, keepdims=True)


def _outproj(att, lru, w_out_b, x2d, mods, norm_w, w_router_t, tiles_per_row):
    n = x2d.shape[0]
    tm = 512
    row = (lambda i: 1 + i // tiles_per_row) if tiles_per_row else (lambda i: 0)
    return pl.pallas_call(
        _outproj_body,
        grid=(n // tm,),
        in_specs=[pl.BlockSpec((tm, ATT_W), lambda i: (i, 0)),
                  pl.BlockSpec((tm, LRU_W), lambda i: (i, 0)),
                  pl.BlockSpec((D, D), lambda i: (0, 0)),
                  pl.BlockSpec((tm, D), lambda i: (i, 0)),
                  pl.BlockSpec((1, 6, D), lambda i: (row(i), 0, 0)),
                  pl.BlockSpec((1, D), lambda i: (0, 0)),
                  pl.BlockSpec((N_EXPERTS, D), lambda i: (0, 0))],
        out_specs=[pl.BlockSpec((tm, D), lambda i: (i, 0)),
                   pl.BlockSpec((tm, D), lambda i: (i, 0)),
                   pl.BlockSpec((N_EXPERTS, tm), lambda i: (0, i))],
        out_shape=[jax.ShapeDtypeStruct((n, D), f32),
                   jax.ShapeDtypeStruct((n, D), f32),
                   jax.ShapeDtypeStruct((N_EXPERTS, n), f32)],
        compiler_params=_cparams(("arbitrary",), 52),
        name="out_projection_router",
    )(att, lru, w_out_b, x2d, mods, norm_w, w_router_t)


def _prefix_sum_lanes(x, lane, n):
    s = 1
    while s < n:
        x = x + jnp.where(lane >= s, pltpu.roll(x, s, 1), 0)
        s *= 2
    return x


def _route_body(aff_ref, idx_ref, g_ref, pos_ref, sp_ref, sel_ref, rem_ref, *, n, cap):
    aff = aff_ref[...]
    lane = lax.broadcasted_iota(i32, (N_EXPERTS, n), 1)

    def count(mask):
        return jnp.sum(jnp.where(mask, 1.0, 0.0), axis=1, keepdims=True).astype(i32)

    def as_value(bits):
        return pltpu.bitcast(jnp.broadcast_to(bits, (N_EXPERTS, LANES)), f32)[:, :1]

    def search(i, thr):
        cand = thr | jnp.left_shift(jnp.int32(1), 30 - i)
        return jnp.where(count(aff >= as_value(cand)) >= cap, cand, thr)

    thr = lax.fori_loop(0, 31, search, jnp.zeros((N_EXPERTS, 1), i32))
    sure = aff >= as_value(thr + 1)
    sel_ref[...] = sure.astype(i32)
    rem_ref[...] = ((aff >= as_value(thr)) & jnp.logical_not(sure)).astype(i32)

    def more(need):
        return jnp.max(need) > 0

    def take_largest(need):
        rem = rem_ref[...] > 0
        top = jnp.max(jnp.where(rem, aff, -1.0), axis=1, keepdims=True)
        is_top = rem & (aff == top) & (need > 0)
        rank = _prefix_sum_lanes(is_top.astype(i32), lane, n)
        pick = is_top & ((count(is_top) <= need) | (rank <= need))
        sel_ref[...] = sel_ref[...] | pick.astype(i32)
        rem_ref[...] = (rem & jnp.logical_not(is_top)).astype(i32)
        return need - count(pick)

    lax.while_loop(more, take_largest, cap - count(sure))
    sel_i = sel_ref[...]
    sel = sel_i > 0
    pos = _prefix_sum_lanes(sel_i, lane, n) - sel_i
    pos_ref[...] = pos
    sp_ref[...] = jnp.where(sel, pos, UNSELECTED)

    key = jnp.where(sel, lane - pos, -1)
    idx = lane
    g = aff
    b = 0
    while (1 << b) < n:
        s = 1 << b
        k_sh = pltpu.roll(key, n - s, 1)
        take = (k_sh >= 0) & ((jnp.right_shift(k_sh, b) & 1) == 1)
        stay = (key >= 0) & ((jnp.right_shift(key, b) & 1) == 0)
        idx = jnp.where(take, pltpu.roll(idx, n - s, 1), idx)
        g = jnp.where(take, pltpu.roll(g, n - s, 1), g)
        key = jnp.where(take, k_sh, jnp.where(stay, key, -1))
        b += 1
    idx_ref[...] = idx[:, :cap]
    g_ref[...] = g[:, :cap]


def _route(aff_t, cap):
    n = aff_t.shape[1]
    body = functools.partial(_route_body, n=n, cap=cap)
    full = lambda shape: pl.BlockSpec(shape, lambda: (0, 0))
    return pl.pallas_call(
        body,
        in_specs=[full((N_EXPERTS, n))],
        out_specs=[full((N_EXPERTS, cap)), full((N_EXPERTS, cap)),
                   full((N_EXPERTS, n)), full((N_EXPERTS, n))],
        out_shape=[jax.ShapeDtypeStruct((N_EXPERTS, cap), i32),
                   jax.ShapeDtypeStruct((N_EXPERTS, cap), f32),
                   jax.ShapeDtypeStruct((N_EXPERTS, n), i32),
                   jax.ShapeDtypeStruct((N_EXPERTS, n), i32)],
        scratch_shapes=[pltpu.VMEM((N_EXPERTS, n), i32), pltpu.VMEM((N_EXPERTS, n), i32)],
        compiler_params=pltpu.CompilerParams(vmem_limit_bytes=32 * _MIB),
        name="expert_choice_routing",
    )(aff_t)


_TF = 256
_TN_DOWN = 512


def _expert_body(idx_ref, h2a_hbm, h2b_hbm, g_ref, wg_ref, wu_ref, wd_ref, oa_ref, ob_ref,
                 xe_ref, xb_ref, he_ref, sem, *, caps, na, nd):
    e = pl.program_id(0)
    s = pl.program_id(1)
    srcs = (h2a_hbm, h2b_hbm)
    starts = (0, caps[0])
    rows_total = caps[0] + caps[1]

    def row_copy(grp, tok, r):
        return pltpu.make_async_copy(srcs[grp].at[pl.ds(tok, 1), :], xe_ref.at[pl.ds(r, 1), :], sem)

    def wait_all_rows():
        for grp in range(2):
            pltpu.make_async_copy(srcs[grp].at[pl.ds(0, caps[grp]), :],
                                  xe_ref.at[pl.ds(starts[grp], caps[grp]), :], sem).wait()

    @pl.when((e == 0) & (s == 0))
    def _():
        for grp in range(2):
            def issue(r, c, grp=grp):
                row_copy(grp, idx_ref[starts[grp] + r], starts[grp] + r).start()
                return c

            lax.fori_loop(0, caps[grp], issue, 0, unroll=8)

    @pl.when(s == 0)
    def _():
        wait_all_rows()
        xb_ref[...] = xe_ref[...].astype(bf16)

    @pl.when(s < na)
    def _():
        xb = xb_ref[...]
        gp = jnp.dot(xb, wg_ref[0].astype(bf16), preferred_element_type=f32)
        up = jnp.dot(xb, wu_ref[0].astype(bf16), preferred_element_type=f32)
        he = ((gp * jax.nn.sigmoid(gp)) * up).astype(bf16)
        for k in range(na):
            @pl.when(s == k)
            def _(k=k):
                he_ref[:, k * _TF:(k + 1) * _TF] = he

    @pl.when(s >= na)
    def _():
        base = jnp.minimum(e + 1, N_EXPERTS - 1) * rows_total
        for k in range(nd):
            @pl.when(s - na == k)
            def _(k=k):
                for grp in range(2):
                    per = caps[grp] // nd
                    for r in range(starts[grp] + k * per, starts[grp] + (k + 1) * per):
                        row_copy(grp, idx_ref[base + r], r).start()

        acc = jnp.dot(he_ref[...], wd_ref[0].astype(bf16), preferred_element_type=f32)
        scaled = acc * g_ref[0]
        oa_ref[0] = scaled[:caps[0]]
        ob_ref[0] = scaled[caps[0]:]

    @pl.when((e == N_EXPERTS - 1) & (s == na + nd - 1))
    def _():
        wait_all_rows()


def _expert_ffn(idx_a, idx_b, h2a, h2b, g_a, g_b, w_gate, w_up, w_down):
    caps = (idx_a.shape[1], idx_b.shape[1])
    rows = caps[0] + caps[1]
    na = D_FF // _TF
    nd = D // _TN_DOWN
    idx = jnp.concatenate([idx_a, idx_b], axis=1).reshape(N_EXPERTS * rows)
    g3 = jnp.concatenate([g_a, g_b], axis=1)[:, :, None]
    body = functools.partial(_expert_body, caps=caps, na=na, nd=nd)
    up_tile = lambda e, s, idx: (e, 0, jnp.minimum(s, na - 1))
    down_tile = lambda e, s, idx: (e, 0, jnp.maximum(s - na, 0))
    grid_spec = pltpu.PrefetchScalarGridSpec(
        num_scalar_prefetch=1,
        grid=(N_EXPERTS, na + nd),
        in_specs=[pl.BlockSpec(memory_space=pl.ANY),
                  pl.BlockSpec(memory_space=pl.ANY),
                  pl.BlockSpec((1, rows, 1), lambda e, s, idx: (e, 0, 0)),
                  pl.BlockSpec((1, D, _TF), up_tile),
                  pl.BlockSpec((1, D, _TF), up_tile),
                  pl.BlockSpec((1, D_FF, _TN_DOWN), down_tile)],
        out_specs=[pl.BlockSpec((1, caps[0], _TN_DOWN), down_tile),
                   pl.BlockSpec((1, caps[1], _TN_DOWN), down_tile)],
        scratch_shapes=[pltpu.VMEM((rows, D), f32), pltpu.VMEM((rows, D), bf16),
                        pltpu.VMEM((rows, D_FF), bf16), pltpu.SemaphoreType.DMA(())],
    )
    return pl.pallas_call(
        body,
        grid_spec=grid_spec,
        out_shape=[jax.ShapeDtypeStruct((N_EXPERTS, caps[0], D), f32),
                   jax.ShapeDtypeStruct((N_EXPERTS, caps[1], D), f32)],
        compiler_params=_cparams(("arbitrary", "arbitrary"), 60),
        name="expert_ffn",
    )(idx, h2a, h2b, g3, w_gate, w_up, w_down)


_TT = 128
_KB = 256
_RUN_MAX = _TT + 2 * SUBLANES
_RUN_PIECES = tuple(SUBLANES << i for i in reversed(range((_RUN_MAX // SUBLANES).bit_length())))
_STAGE_ROWS = N_EXPERTS * _RUN_MAX
_STAGE_PIECES = tuple(SUBLANES << i
                      for i in reversed(range((_STAGE_ROWS // SUBLANES).bit_length())))


def _combine_body(offs_ref, ye_hbm, sp_ref, x1_ref, mods_ref, o_ref, stage_ref, acc_ref, sem,
                  *, cap):
    j = pl.program_id(0)
    slot = j & 1

    def wait_rows(total, sl):
        for size in _STAGE_PIECES:
            @pl.when((total & size) != 0)
            def _(size=size):
                pltpu.make_async_copy(ye_hbm.at[pl.ds(0, size), :],
                                      stage_ref.at[sl, pl.ds(0, size), :], sem.at[sl]).wait()

    def tile_copies(jj, sl, start):
        bases, offs = [], []
        base = jnp.int32(0)
        for e in range(N_EXPERTS):
            off = offs_ref[e, jj]
            end = offs_ref[e, jj + 1]
            off8 = (off // SUBLANES) * SUBLANES
            length = jnp.where(end > off, ((end + SUBLANES - 1) // SUBLANES) * SUBLANES - off8, 0)
            bases.append(base)
            offs.append(off8)
            src = e * cap + off8
            dst = base
            for size in _RUN_PIECES if start else ():
                piece = length & size

                @pl.when(piece != 0)
                def _(src=src, dst=dst, size=size):
                    pltpu.make_async_copy(
                        ye_hbm.at[pl.ds(pl.multiple_of(src, SUBLANES), size), :],
                        stage_ref.at[sl, pl.ds(pl.multiple_of(dst, SUBLANES), size), :],
                        sem.at[sl]).start()

                src = src + piece
                dst = dst + piece
            base = base + length
        return bases, offs, base

    @pl.when(j == 0)
    def _():
        stage_ref[...] = jnp.zeros_like(stage_ref)
        tile_copies(0, 0, True)

    @pl.when(j + 1 < pl.num_programs(0))
    def _():
        tile_copies(j + 1, 1 - slot, True)

    bases, offs, total = tile_copies(j, slot, False)
    wait_rows(total, slot)

    sp = sp_ref[...]
    kiota = lax.broadcasted_iota(i32, (1, _KB), 1)
    acc_ref[...] = jnp.zeros_like(acc_ref)

    def kblock(kb, c):
        hit = None
        for e in range(N_EXPERTS):
            local = sp[:, e:e + 1] + (bases[e] - offs[e] - kb * _KB)
            h = local == kiota
            hit = h if hit is None else (hit | h)
        s = jnp.where(hit, 1.0, 0.0).astype(bf16)
        rows = stage_ref[slot, pl.ds(pl.multiple_of(kb * _KB, _KB), _KB), :]
        hi = rows.astype(bf16)
        lo = (rows - hi.astype(f32)).astype(bf16)
        acc_ref[...] += (jnp.dot(s, hi, preferred_element_type=f32)
                         + jnp.dot(s, lo, preferred_element_type=f32))
        return c

    lax.fori_loop(0, (total + _KB - 1) // _KB, kblock, 0)
    o_ref[...] = x1_ref[...] + mods_ref[0][5:6] * acc_ref[...]


def _combine(offs, ye_flat, sp_n, x1, mods, cap, tiles_per_row):
    n = x1.shape[0]
    row = (lambda j: 1 + j // tiles_per_row) if tiles_per_row else (lambda j: 0)
    body = functools.partial(_combine_body, cap=cap)
    grid_spec = pltpu.PrefetchScalarGridSpec(
        num_scalar_prefetch=1,
        grid=(n // _TT,),
        in_specs=[pl.BlockSpec(memory_space=pl.ANY),
                  pl.BlockSpec((_TT, N_EXPERTS), lambda j, offs: (j, 0)),
                  pl.BlockSpec((_TT, D), lambda j, offs: (j, 0)),
                  pl.BlockSpec((1, 6, D), lambda j, offs: (row(j), 0, 0))],
        out_specs=pl.BlockSpec((_TT, D), lambda j, offs: (j, 0)),
        scratch_shapes=[pltpu.VMEM((2, N_EXPERTS * _RUN_MAX, D), f32), pltpu.VMEM((_TT, D), f32),
                        pltpu.SemaphoreType.DMA((2,))],
    )
    return pl.pallas_call(
        body,
        grid_spec=grid_spec,
        out_shape=jax.ShapeDtypeStruct((n, D), f32),
        compiler_params=_cparams(("arbitrary",), 48),
        name="expert_combine",
    )(offs, ye_flat, sp_n, x1, mods)


def _moe(x1s, h2s, affs, mods, w_gate, w_up, w_down, tiles_per_row):
    routes = [_route(aff_t, CAPACITY_FACTOR * aff_t.shape[1] // N_EXPERTS) for aff_t in affs]
    (idx_a, g_a, _, _), (idx_b, g_b, _, _) = routes
    yes = _expert_ffn(idx_a, idx_b, h2s[0], h2s[1], g_a, g_b, w_gate, w_up, w_down)
    outs = []
    for ye, (idx, _, pos, sp), x1, tpr in zip(yes, routes, x1s, tiles_per_row):
        cap = idx.shape[1]
        offs = jnp.concatenate([pos[:, ::_TT], jnp.full((N_EXPERTS, 1), cap, i32)], axis=1)
        outs.append(_combine(offs, ye.reshape(N_EXPERTS * cap, D), sp.T, x1, mods, cap, tpr))
    return outs


def kernel(x_prompt, x_sample, cache_k, cache_v, state_fwd, state_bwd, c, c_ctx, w_mod, b_mod,
           norm_mix, w_in, q_norm, k_norm, rpb, conv_w, conv_b, lru_w_a, lru_b_a, lru_w_x,
           lru_b_x, lru_lambda, w_out, norm_ffn, w_router, w_gate, w_up, w_down):
    B, S, _ = x_prompt.shape
    DB, T, _ = x_sample.shape
    l = 0

    cvecs = jnp.concatenate([c_ctx[None, :], c, jnp.zeros((SUBLANES - 1 - DB, D), f32)], axis=0)
    mods = _adaln_mods(cvecs, w_mod[l], b_mod[l][None, :]).reshape(SUBLANES, 6, D)

    heads_per_tile = _TN_IN // HEAD_DIM
    qn_t = jnp.tile(q_norm[l], heads_per_tile)[None, :]
    kn_t = jnp.tile(k_norm[l], heads_per_tile)[None, :]
    gmat = (jnp.kron(jnp.eye(heads_per_tile, dtype=f32), jnp.ones((HEAD_DIM, HEAD_DIM), f32))
            / HEAD_DIM).astype(bf16)
    nmix = norm_mix[l][None, :]
    nffn = norm_ffn[l][None, :]
    w_in_b = w_in[l].astype(bf16)
    w_out_b = w_out[l].astype(bf16)
    w_router_t = w_router[l].T.astype(bf16)
    wcat = _lru_gate_weights(lru_w_a[l], lru_w_x[l])
    cb = conv_b[l][None, :]

    xp = x_prompt.reshape(B * S, D)
    xs = x_sample.reshape(DB * T, D)

    proj_c = _inproj(xp, mods, nmix, w_in_b, qn_t, kn_t, gmat, per_tile_rows=False)
    att_c, new_k, new_v = _ctx_attention(proj_c, B, S)
    zeros_c = jnp.zeros((B, LRU_W), f32)
    lru_c, htf, htb = _rglru(proj_c, wcat, conv_w[l], cb, lru_b_a[l], lru_b_x[l], lru_lambda[l],
                             zeros_c, zeros_c, T=S, nb=SUBLANES)
    x1_c, h2_c, aff_c = _outproj(att_c, lru_c, w_out_b, xp, mods, nffn, w_router_t, 0)

    nct = LRU_W // _LRU_TC

    def slabs(h0):
        return (h0.reshape(DB, nct, _LRU_HALVES, LANES).transpose(2, 0, 1, 3)
                .reshape(_LRU_HALVES * DB, nct * LANES))

    proj_l = _inproj(xs, mods, nmix, w_in_b, qn_t, kn_t, gmat, per_tile_rows=True)
    att_l = _lat_attention(proj_l, cache_k, cache_v, _pad_rpb(rpb[l]), l, DB, T)
    h0f_l = slabs(state_fwd[:, l])
    h0b_l = slabs(state_bwd[:, l])
    lru_l, _, _ = _rglru(proj_l, wcat, conv_w[l], cb, lru_b_a[l], lru_b_x[l], lru_lambda[l],
                         h0f_l, h0b_l, T=T, nb=DB)
    x1_l, h2_l, aff_l = _outproj(att_l, lru_l, w_out_b, xs, mods, nffn, w_router_t, T // 512)

    y_c, y_l = _moe((x1_c, x1_l), (h2_c, h2_l), (aff_c, aff_l), mods,
                    w_gate[l], w_up[l], w_down[l], (0, T // _TT))

    return (y_c.reshape(B, S, D), y_l.reshape(DB, T, D), new_k, new_v,
            htf[:, None, :], htb[:, None, :])
```

```python
import functools

import jax
import jax.numpy as jnp
from jax import lax
from jax.experimental import pallas as pl
from jax.experimental.pallas import tpu as pltpu

f32 = jnp.float32
bf16 = jnp.bfloat16
i32 = jnp.int32

D = 2048
N_HEADS = 16
HEAD_DIM = 64
ATT_W = N_HEADS * HEAD_DIM
LRU_W = D - ATT_W
LRU_BLOCK = 64
LRU_C = 8.0
GRID_W = 64
WIN_H = 8
WIN_W = 16
N_EXPERTS = 16
CAPACITY_FACTOR = 2
D_FF = 2048
EPS = 1e-6
NEG_INF = -1e30
IN_COLS = 3 * ATT_W + 2 * LRU_W

LANES = 128
SUBLANES = 8
MXU_DIM = 256
UNSELECTED = -(1 << 30)

_MIB = 1024 * 1024


def _cparams(sem, vmem_mib):
    return pltpu.CompilerParams(dimension_semantics=sem, vmem_limit_bytes=vmem_mib * _MIB)


def _mods_body(c_ref, w_ref, b_ref, o_ref):
    c = c_ref[...]
    s = (c * jax.nn.sigmoid(c)).astype(bf16)
    o_ref[...] = jnp.dot(s, w_ref[...].astype(bf16), preferred_element_type=f32) + b_ref[...]


def _adaln_mods(cvecs, w_mod, b_mod):
    tn = 1024
    return pl.pallas_call(
        _mods_body,
        grid=(6 * D // tn,),
        in_specs=[pl.BlockSpec((SUBLANES, D), lambda j: (0, 0)),
                  pl.BlockSpec((D, tn), lambda j: (0, j)),
                  pl.BlockSpec((1, tn), lambda j: (0, j))],
        out_specs=pl.BlockSpec((SUBLANES, tn), lambda j: (0, j)),
        out_shape=jax.ShapeDtypeStruct((SUBLANES, 6 * D), f32),
        compiler_params=_cparams(("arbitrary",), 40),
        name="adaln_mods",
    )(cvecs, w_mod, b_mod)


_TN_IN = 512
_NORM_ROWS = 16
_NORM_SLICES = 8


def _inproj_body(x0_ref, xn_ref, mods0_ref, modsn_ref, nw_ref, w_ref, qn_ref, kn_ref, g_ref,
                 o_ref, ha_ref, hb_ref):
    i = pl.program_id(0)
    j = pl.program_id(1)
    tm = ha_ref.shape[0]
    nw = nw_ref[...]
    n_qk_tiles = 2 * ATT_W // _TN_IN

    def norm_rows(x_ref, mods_ref, h_ref, r0):
        m = mods_ref[0]
        rows = pl.ds(pl.multiple_of(r0, _NORM_ROWS), _NORM_ROWS)
        x = x_ref[rows, :]
        ms = jnp.mean(x * x, axis=-1, keepdims=True)
        y = x * lax.rsqrt(ms + EPS) * nw
        h_ref[rows, :] = (y * (1.0 + m[1:2]) + m[0:1]).astype(bf16)

    @pl.when((i == 0) & (j == 0))
    def _():
        def first(c, carry):
            norm_rows(x0_ref, mods0_ref, ha_ref, c * _NORM_ROWS)
            return carry

        lax.fori_loop(0, tm // _NORM_ROWS, first, 0, unroll=4)

    def column_step(cur_ref, nxt_ref):
        slice_rows = tm // _NORM_SLICES
        r_slice = jnp.minimum(j, _NORM_SLICES - 1) * slice_rows
        for c in range(slice_rows // _NORM_ROWS):
            norm_rows(xn_ref, modsn_ref, nxt_ref, r_slice + c * _NORM_ROWS)

        acc = jnp.dot(cur_ref[...], w_ref[...], preferred_element_type=f32)

        @pl.when(j < n_qk_tiles)
        def _():
            sq = acc * acc
            hi = sq.astype(bf16)
            lo = (sq - hi.astype(f32)).astype(bf16)
            g = g_ref[...]
            ms = (jnp.dot(hi, g, preferred_element_type=f32)
                  + jnp.dot(lo, g, preferred_element_type=f32))
            head_w = jnp.where(j < n_qk_tiles // 2, qn_ref[...], kn_ref[...])
            o_ref[...] = acc * lax.rsqrt(ms + EPS) * head_w

        @pl.when(j >= n_qk_tiles)
        def _():
            o_ref[...] = acc

    @pl.when((i & 1) == 0)
    def _():
        column_step(ha_ref, hb_ref)

    @pl.when((i & 1) == 1)
    def _():
        column_step(hb_ref, ha_ref)


def _inproj(x2d, mods, norm_w, w_in, qn_t, kn_t, gmat, per_tile_rows):
    n = x2d.shape[0]
    tm = 1024
    last = n // tm - 1
    row = (lambda i: 1 + i) if per_tile_rows else (lambda i: 0)
    nxt = lambda i: jnp.minimum(i + 1, last)
    return pl.pallas_call(
        _inproj_body,
        grid=(n // tm, IN_COLS // _TN_IN),
        in_specs=[pl.BlockSpec((tm, D), lambda i, j: (0, 0)),
                  pl.BlockSpec((tm, D), lambda i, j: (nxt(i), 0)),
                  pl.BlockSpec((1, 6, D), lambda i, j: (row(0), 0, 0)),
                  pl.BlockSpec((1, 6, D), lambda i, j: (row(nxt(i)), 0, 0)),
                  pl.BlockSpec((1, D), lambda i, j: (0, 0)),
                  pl.BlockSpec((D, _TN_IN), lambda i, j: (0, j)),
                  pl.BlockSpec((1, _TN_IN), lambda i, j: (0, 0)),
                  pl.BlockSpec((1, _TN_IN), lambda i, j: (0, 0)),
                  pl.BlockSpec((_TN_IN, _TN_IN), lambda i, j: (0, 0))],
        out_specs=pl.BlockSpec((tm, _TN_IN), lambda i, j: (i, j)),
        out_shape=jax.ShapeDtypeStruct((n, IN_COLS), f32),
        scratch_shapes=[pltpu.VMEM((tm, D), bf16), pltpu.VMEM((tm, D), bf16)],
        compiler_params=_cparams(("arbitrary", "arbitrary"), 56),
        name="in_projection",
    )(x2d, x2d, mods, mods, norm_w, w_in, qn_t, kn_t, gmat)


def _softmax_pv(s_list, v_list):
    m = s_list[0].max(axis=-1, keepdims=True)
    for s in s_list[1:]:
        m = jnp.maximum(m, s.max(axis=-1, keepdims=True))
    o = None
    for s, v in zip(s_list, v_list):
        os_ = jnp.dot(jnp.exp(s - m).astype(bf16), v, preferred_element_type=f32)
        o = os_ if o is None else o + os_
    return o


def _normalise_pair(o0, o1, lane):
    left = lane < HEAD_DIM
    den = pltpu.roll(jnp.where(left, o1, o0), HEAD_DIM, 1)
    return jnp.where(left, o0, o1) / den


def _nt_dot(a, b):
    return lax.dot_general(a, b, (((1,), (1,)), ((), ())), preferred_element_type=f32)


_QK_SCALE = HEAD_DIM ** -0.5


def _ctx_attn_body(q_ref, k_ref, v_ref, o_ref, nk_ref, nv_ref):
    lane = lax.broadcasted_iota(i32, (1, LANES), 1)
    for p in range(ATT_W // LANES):
        cols = slice(p * LANES, (p + 1) * LANES)
        q = q_ref[:, cols] * _QK_SCALE
        kf = k_ref[:, cols]
        k = kf.astype(bf16)
        v = v_ref[:, cols]
        nk_ref[0, 0, 2 * p] = kf[:, :HEAD_DIM]
        nk_ref[0, 0, 2 * p + 1] = pltpu.roll(kf, HEAD_DIM, 1)[:, :HEAD_DIM]
        nv_ref[0, 0, 2 * p] = v[:, :HEAD_DIM]
        nv_ref[0, 0, 2 * p + 1] = pltpu.roll(v, HEAD_DIM, 1)[:, :HEAD_DIM]
        heads = []
        for hh in range(2):
            hm = (lane >= hh * HEAD_DIM) & (lane < (hh + 1) * HEAD_DIM)
            qh = jnp.where(hm, q, 0.0).astype(bf16)
            vh = jnp.where(hm, v, 1.0).astype(bf16)
            heads.append(_softmax_pv([_nt_dot(qh, k)], [vh]))
        o_ref[:, cols] = _normalise_pair(heads[0], heads[1], lane).astype(bf16)


def _ctx_attention(proj, batch, seq):
    n = proj.shape[0]
    cache = pl.BlockSpec((1, 1, N_HEADS, seq, HEAD_DIM), lambda b: (b, 0, 0, 0, 0))
    cache_shape = jax.ShapeDtypeStruct((batch, 1, N_HEADS, seq, HEAD_DIM), f32)
    return pl.pallas_call(
        _ctx_attn_body,
        grid=(batch,),
        in_specs=[pl.BlockSpec((seq, ATT_W), lambda b: (b, 0)),
                  pl.BlockSpec((seq, ATT_W), lambda b: (b, 1)),
                  pl.BlockSpec((seq, ATT_W), lambda b: (b, 2))],
        out_specs=[pl.BlockSpec((seq, ATT_W), lambda b: (b, 0)), cache, cache],
        out_shape=[jax.ShapeDtypeStruct((n, ATT_W), bf16), cache_shape, cache_shape],
        compiler_params=_cparams(("arbitrary",), 32),
        name="context_attention",
    )(proj, proj, proj)


_NA_ROWS = 16
_NA_QROWS = 8
_NA_KROWS = 12


def _pad_rpb(rpb):
    nd, nc = 2 * WIN_H - 1, 2 * WIN_W - 1
    return jnp.pad(rpb.astype(f32), ((0, 0), (0, 2 * SUBLANES - nd), (0, LANES - nc)))


def _build_na_bias(rpb_ref, bias_ref):
    qc = lax.broadcasted_iota(i32, (GRID_W, LANES), 0)
    lane = lax.broadcasted_iota(i32, (GRID_W, LANES), 1)
    kc = lane & (GRID_W - 1)
    cs = jnp.clip(qc - WIN_W // 2, 0, GRID_W - WIN_W)
    col_ok = (kc >= cs) & (kc < cs + WIN_W)
    left = lane < GRID_W
    neg = jnp.full((GRID_W, LANES), NEG_INF, f32)
    for hh in range(2):
        tbl = rpb_ref[hh]
        tiles = {}

        def toeplitz(d, odd):
            if (d, odd) not in tiles:
                row = jnp.broadcast_to(tbl[d:d + 1, :], (GRID_W, LANES))
                shift = (LANES - (WIN_W - 1) + odd * GRID_W) % LANES
                t = pltpu.roll(row, shift, 1, stride=1, stride_axis=0)
                tiles[(d, odd)] = jnp.where(col_ok, t, NEG_INF)
            return tiles[(d, odd)]

        for j in range(2):
            for rq in range(_NA_QROWS):
                rq_abs = j * _NA_QROWS + rq
                rs = min(max(rq_abs - WIN_H // 2, 0), _NA_ROWS - WIN_H)
                for kp in range(_NA_KROWS // 2):
                    parts = []
                    for odd in range(2):
                        rk_abs = j * (_NA_ROWS - _NA_KROWS) + 2 * kp + odd
                        if rs <= rk_abs < rs + WIN_H:
                            parts.append(toeplitz(rk_abs - rq_abs + WIN_H - 1, odd))
                        else:
                            parts.append(neg)
                    bias_ref[hh, j, rq * GRID_W:(rq + 1) * GRID_W,
                             kp * LANES:(kp + 1) * LANES] = jnp.where(left, parts[0], parts[1])


def _lat_attn_body(q_ref, k_ref, v_ref, ck_ref, cv_ref, rpb_ref, o_ref, bias_ref):
    @pl.when(pl.program_id(1) == 0)
    def _():
        _build_na_bias(rpb_ref, bias_ref)

    lane = lax.broadcasted_iota(i32, (1, LANES), 1)
    nq = _NA_QROWS * GRID_W
    nk = _NA_KROWS * GRID_W
    kstep = (_NA_ROWS - _NA_KROWS) * GRID_W
    src = lax.broadcasted_iota(i32, (HEAD_DIM, LANES), 0)
    dst = lax.broadcasted_iota(i32, (HEAD_DIM, LANES), 1)
    ck = None
    cv_h = []
    for hh in range(2):
        place = jnp.where(dst == src + hh * HEAD_DIM, 1.0, 0.0).astype(bf16)
        part = jnp.dot(ck_ref[0, 0, hh].astype(bf16), place, preferred_element_type=f32)
        ck = part if ck is None else ck + part
        placed = jnp.dot(cv_ref[0, 0, hh].astype(bf16), place, preferred_element_type=f32)
        own = (lane >= hh * HEAD_DIM) & (lane < (hh + 1) * HEAD_DIM)
        cv_h.append(jnp.where(own, placed, 1.0).astype(bf16))
    ck = ck.astype(bf16)
    for j in range(2):
        q = q_ref[j * nq:(j + 1) * nq, :] * _QK_SCALE
        k = k_ref[j * kstep:j * kstep + nk, :].astype(bf16)
        v = v_ref[j * kstep:j * kstep + nk, :]
        heads = []
        for hh in range(2):
            hm = (lane >= hh * HEAD_DIM) & (lane < (hh + 1) * HEAD_DIM)
            qh = jnp.where(hm, q, 0.0).astype(bf16)
            s_win = _nt_dot(qh, k) + bias_ref[hh, j]
            s_ctx = _nt_dot(qh, ck)
            vh = jnp.where(hm, v, 1.0).astype(bf16)
            heads.append(_softmax_pv([s_win, s_ctx], [vh, cv_h[hh]]))
        o_ref[j * nq:(j + 1) * nq, :] = _normalise_pair(heads[0], heads[1], lane).astype(bf16)


def _lat_attention(proj, ck, cv, rpb_p, layer, batch, seq):
    n = proj.shape[0]
    past = ck.shape[3]
    cache = pl.BlockSpec((1, 1, 2, past, HEAD_DIM), lambda p, b: (b, layer, p, 0, 0))
    npair = ATT_W // LANES
    nq = _NA_QROWS * GRID_W
    nk = _NA_KROWS * GRID_W
    return pl.pallas_call(
        _lat_attn_body,
        grid=(npair, batch),
        in_specs=[pl.BlockSpec((seq, LANES), lambda p, b: (b, p)),
                  pl.BlockSpec((seq, LANES), lambda p, b: (b, npair + p)),
                  pl.BlockSpec((seq, LANES), lambda p, b: (b, 2 * npair + p)),
                  cache, cache,
                  pl.BlockSpec((2, 2 * SUBLANES, LANES), lambda p, b: (p, 0, 0))],
        out_specs=pl.BlockSpec((seq, LANES), lambda p, b: (b, p)),
        out_shape=jax.ShapeDtypeStruct((n, ATT_W), bf16),
        scratch_shapes=[pltpu.VMEM((2, 2, nq, nk), f32)],
        compiler_params=_cparams(("arbitrary", "arbitrary"), 48),
        name="neighbourhood_attention",
    )(proj, proj, proj, ck, cv, rpb_p)


_LRU_TC = 256
_GATE_ROWS = 256
_LRU_HALVES = _LRU_TC // LANES


def _gelu_tanh(x):
    c = 0.7978845608028654
    return x * (0.5 * (1.0 + jnp.tanh(c * (x + 0.044715 * (x * x * x)))))


def _lru_body(xr_ref, gr_ref, w_ref, cw_ref, cb_ref, ba_ref, bx_ref, lam_ref, h0f_ref, h0b_ref,
              y_ref, htf_ref, htb_ref, xc_ref, af_ref, uf_ref, ab_ref, ub_ref, *, T, nb):
    nslab = _LRU_HALVES * nb // SUBLANES
    R = nb * T
    x = xr_ref[...]
    t = lax.broadcasted_iota(i32, (R, 1), 0) & (T - 1)
    cw = cw_ref[...]
    xc = jnp.where(t >= 2, pltpu.roll(x, 2, 0), 0.0) * cw[0:1]
    xc = xc + jnp.where(t >= 1, pltpu.roll(x, 1, 0), 0.0) * cw[1:2]
    xc = xc + x * cw[2:3]
    xc = xc + jnp.where(t < T - 1, pltpu.roll(x, R - 1, 0), 0.0) * cw[3:4]
    xc_ref[...] = xc + cb_ref[...]

    nl = -lam_ref[...]
    sp = jnp.maximum(nl, 0.0) + jnp.log1p(jnp.exp(-jnp.abs(nl)))
    half_c_sp = (0.5 * LRU_C) * sp
    w = w_ref[0]
    dirs = ((af_ref, uf_ref), (ab_ref, ub_ref))
    for b in range(nb):
        for ci in range(T // _GATE_ROWS):
            r0 = b * T + ci * _GATE_ROWS
            xcc = xc_ref[r0:r0 + _GATE_ROWS, :]
            half_x = 0.5 * xcc
            z = jnp.dot(xcc.astype(bf16), w, preferred_element_type=f32)
            for d, (a_ref, u_ref) in enumerate(dirs):
                c0 = d * 2 * _LRU_TC
                tr = jnp.tanh(0.5 * (z[:, c0:c0 + _LRU_TC] + ba_ref[d:d + 1]))
                tg = jnp.tanh(0.5 * (z[:, c0 + _LRU_TC:c0 + 2 * _LRU_TC] + bx_ref[d:d + 1]))
                neg_log_a = (1.0 + tr) * half_c_sp[d:d + 1]
                a = jnp.exp(-neg_log_a)
                one_m_a2 = jnp.tanh(neg_log_a) * (1.0 + a * a)
                root = jnp.where(one_m_a2 > 0.0, one_m_a2 * lax.rsqrt(one_m_a2), 0.0)
                u = root * ((1.0 + tg) * half_x)
                for hf in range(_LRU_HALVES):
                    s, sub = divmod(hf * nb + b, SUBLANES)
                    rows = pl.ds(ci * _GATE_ROWS * SUBLANES + sub, _GATE_ROWS, stride=SUBLANES)
                    a_ref[s, rows, :] = a[:, hf * LANES:(hf + 1) * LANES]
                    u_ref[s, rows, :] = u[:, hf * LANES:(hf + 1) * LANES]

    def step(i, carry):
        rf = pl.ds(pl.multiple_of(i * SUBLANES, SUBLANES), SUBLANES)
        rb = pl.ds(pl.multiple_of((T - 1 - i) * SUBLANES, SUBLANES), SUBLANES)
        new = []
        for s in range(nslab):
            hf_ = af_ref[s, rf, :] * carry[2 * s] + uf_ref[s, rf, :]
            uf_ref[s, rf, :] = hf_
            hb_ = ab_ref[s, rb, :] * carry[2 * s + 1] + ub_ref[s, rb, :]
            ub_ref[s, rb, :] = hb_
            new += [hf_, hb_]
        return tuple(new)

    init = []
    for s in range(nslab):
        init += [h0f_ref[:, s * LANES:(s + 1) * LANES], h0b_ref[:, s * LANES:(s + 1) * LANES]]
    fin = lax.fori_loop(0, T, step, tuple(init), unroll=8)
    for s in range(nslab):
        htf_ref[:, s * LANES:(s + 1) * LANES] = fin[2 * s]
        htb_ref[:, s * LANES:(s + 1) * LANES] = fin[2 * s + 1]

    for b in range(nb):
        for hf in range(_LRU_HALVES):
            s, sub = divmod(hf * nb + b, SUBLANES)
            rows = pl.ds(sub, T, stride=SUBLANES)
            hs = uf_ref[s, rows, :] + ub_ref[s, rows, :]
            g = gr_ref[b * T:(b + 1) * T, hf * LANES:(hf + 1) * LANES]
            y_ref[b * T:(b + 1) * T, hf * LANES:(hf + 1) * LANES] = (
                hs * _gelu_tanh(g)).astype(bf16)


def _rglru(proj, wcat, conv_w, conv_b, b_a, b_x, lam, h0f, h0b, T, nb):
    n = proj.shape[0]
    nslab = _LRU_HALVES * nb // SUBLANES
    W = nslab * LANES
    R = nb * T
    nct = LRU_W // _LRU_TC
    xr0 = 3 * ATT_W // _LRU_TC
    gr0 = xr0 + nct
    chan = lambda k: pl.BlockSpec((k, _LRU_TC), lambda g, j: (0, j))
    slab = pl.BlockSpec((SUBLANES, W), lambda g, j: (g, j))
    body = functools.partial(_lru_body, T=T, nb=nb)
    return pl.pallas_call(
        body,
        grid=(n // R, nct),
        in_specs=[pl.BlockSpec((R, _LRU_TC), lambda g, j: (g, xr0 + j)),
                  pl.BlockSpec((R, _LRU_TC), lambda g, j: (g, gr0 + j)),
                  pl.BlockSpec((1, _LRU_TC, 4 * _LRU_TC), lambda g, j: (j, 0, 0)),
                  chan(4), chan(1), chan(2), chan(2), chan(2), slab, slab],
        out_specs=[pl.BlockSpec((R, _LRU_TC), lambda g, j: (g, j)), slab, slab],
        out_shape=[jax.ShapeDtypeStruct((n, LRU_W), bf16),
                   jax.ShapeDtypeStruct(h0f.shape, f32),
                   jax.ShapeDtypeStruct(h0b.shape, f32)],
        scratch_shapes=[pltpu.VMEM((R, _LRU_TC), f32)]
        + [pltpu.VMEM((nslab, T * SUBLANES, LANES), f32) for _ in range(4)],
        compiler_params=_cparams(("arbitrary", "arbitrary"), 56),
        name="rg_lru",
    )(proj, proj, wcat, conv_w, conv_b, b_a, b_x, lam, h0f, h0b)


def _lru_gate_weights(w_a, w_x):
    per_tile = _LRU_TC // LRU_BLOCK
    nct = LRU_W // _LRU_TC
    eye = jnp.eye(per_tile, dtype=f32)

    def bd(w):
        w = w.reshape(nct, per_tile, LRU_BLOCK, LRU_BLOCK)
        m = w[:, :, :, None, :] * eye[None, :, None, :, None]
        return m.reshape(nct, _LRU_TC, _LRU_TC)

    return jnp.concatenate([bd(w_a[0]), bd(w_x[0]), bd(w_a[1]), bd(w_x[1])], axis=-1).astype(bf16)


def _outproj_body(att_ref, lru_ref, w_ref, x_ref, mods_ref, nw_ref, wr_ref,
                  x1_ref, h2_ref, aff_ref):
    mix = (jnp.dot(att_ref[...], w_ref[0:ATT_W, :], preferred_element_type=f32)
           + jnp.dot(lru_ref[...], w_ref[ATT_W:D, :], preferred_element_type=f32))
    m = mods_ref[0]
    x1 = x_ref[...] + m[2:3] * mix
    x1_ref[...] = x1
    ms = jnp.mean(x1 * x1, axis=-1, keepdims=True)
    h2 = (x1 * lax.rsqrt(ms + EPS) * nw_ref[...]) * (1.0 + m[4:5]) + m[3:4]
    h2_ref[...] = h2
    logits = _nt_dot(wr_ref[...], h2.astype(bf16))
    e = jnp.exp(logits - logits.max(axis=0, keepdims=True))
    aff_ref[...] = e / e.sum(axis=0, keepdims=True)


def _outproj(att, lru, w_out_b, x2d, mods, norm_w, w_router_t, tiles_per_row):
    n = x2d.shape[0]
    tm = 512
    row = (lambda i: 1 + i // tiles_per_row) if tiles_per_row else (lambda i: 0)
    return pl.pallas_call(
        _outproj_body,
        grid=(n // tm,),
        in_specs=[pl.BlockSpec((tm, ATT_W), lambda i: (i, 0)),
                  pl.BlockSpec((tm, LRU_W), lambda i: (i, 0)),
                  pl.BlockSpec((D, D), lambda i: (0, 0)),
                  pl.BlockSpec((tm, D), lambda i: (i, 0)),
                  pl.BlockSpec((1, 6, D), lambda i: (row(i), 0, 0)),
                  pl.BlockSpec((1, D), lambda i: (0, 0)),
                  pl.BlockSpec((N_EXPERTS, D), lambda i: (0, 0))],
        out_specs=[pl.BlockSpec((tm, D), lambda i: (i, 0)),
                   pl.BlockSpec((tm, D), lambda i: (i, 0)),
                   pl.BlockSpec((N_EXPERTS, tm), lambda i: (0, i))],
        out_shape=[jax.ShapeDtypeStruct((n, D), f32),
                   jax.ShapeDtypeStruct((n, D), f32),
                   jax.ShapeDtypeStruct((N_EXPERTS, n), f32)],
        compiler_params=_cparams(("arbitrary",), 52),
        name="out_projection_router",
    )(att, lru, w_out_b, x2d, mods, norm_w, w_router_t)


def _prefix_sum_lanes(x, lane, n):
    s = 1
    while s < n:
        x = x + jnp.where(lane >= s, pltpu.roll(x, s, 1), 0)
        s *= 2
    return x


def _route_body(aff_ref, idx_ref, g_ref, pos_ref, sp_ref, sel_ref, rem_ref, *, n, cap):
    aff = aff_ref[...]
    lane = lax.broadcasted_iota(i32, (N_EXPERTS, n), 1)

    def count(mask):
        return jnp.sum(jnp.where(mask, 1.0, 0.0), axis=1, keepdims=True).astype(i32)

    def as_value(bits):
        return pltpu.bitcast(jnp.broadcast_to(bits, (N_EXPERTS, LANES)), f32)[:, :1]

    def search(i, thr):
        cand = thr | jnp.left_shift(jnp.int32(1), 30 - i)
        return jnp.where(count(aff >= as_value(cand)) >= cap, cand, thr)

    thr = lax.fori_loop(0, 31, search, jnp.zeros((N_EXPERTS, 1), i32))
    sure = aff >= as_value(thr + 1)
    sel_ref[...] = sure.astype(i32)
    rem_ref[...] = ((aff >= as_value(thr)) & jnp.logical_not(sure)).astype(i32)

    def more(need):
        return jnp.max(need) > 0

    def take_largest(need):
        rem = rem_ref[...] > 0
        top = jnp.max(jnp.where(rem, aff, -1.0), axis=1, keepdims=True)
        is_top = rem & (aff == top) & (need > 0)
        rank = _prefix_sum_lanes(is_top.astype(i32), lane, n)
        pick = is_top & ((count(is_top) <= need) | (rank <= need))
        sel_ref[...] = sel_ref[...] | pick.astype(i32)
        rem_ref[...] = (rem & jnp.logical_not(is_top)).astype(i32)
        return need - count(pick)

    lax.while_loop(more, take_largest, cap - count(sure))
    sel_i = sel_ref[...]
    sel = sel_i > 0
    pos = _prefix_sum_lanes(sel_i, lane, n) - sel_i
    pos_ref[...] = pos
    sp_ref[...] = jnp.where(sel, pos, UNSELECTED)

    key = jnp.where(sel, lane - pos, -1)
    idx = lane
    g = aff
    b = 0
    while (1 << b) < n:
        s = 1 << b
        k_sh = pltpu.roll(key, n - s, 1)
        take = (k_sh >= 0) & ((jnp.right_shift(k_sh, b) & 1) == 1)
        stay = (key >= 0) & ((jnp.right_shift(key, b) & 1) == 0)
        idx = jnp.where(take, pltpu.roll(idx, n - s, 1), idx)
        g = jnp.where(take, pltpu.roll(g, n - s, 1), g)
        key = jnp.where(take, k_sh, jnp.where(stay, key, -1))
        b += 1
    idx_ref[...] = idx[:, :cap]
    g_ref[...] = g[:, :cap]


def _route(aff_t, cap):
    n = aff_t.shape[1]
    body = functools.partial(_route_body, n=n, cap=cap)
    full = lambda shape: pl.BlockSpec(shape, lambda: (0, 0))
    return pl.pallas_call(
        body,
        in_specs=[full((N_EXPERTS, n))],
        out_specs=[full((N_EXPERTS, cap)), full((N_EXPERTS, cap)),
                   full((N_EXPERTS, n)), full((N_EXPERTS, n))],
        out_shape=[jax.ShapeDtypeStruct((N_EXPERTS, cap), i32),
                   jax.ShapeDtypeStruct((N_EXPERTS, cap), f32),
                   jax.ShapeDtypeStruct((N_EXPERTS, n), i32),
                   jax.ShapeDtypeStruct((N_EXPERTS, n), i32)],
        scratch_shapes=[pltpu.VMEM((N_EXPERTS, n), i32), pltpu.VMEM((N_EXPERTS, n), i32)],
        compiler_params=pltpu.CompilerParams(vmem_limit_bytes=32 * _MIB),
        name="expert_choice_routing",
    )(aff_t)


_TF = 256
_TN_DOWN = 512


def _expert_body(idx_ref, h2a_hbm, h2b_hbm, g_ref, wg_ref, wu_ref, wd_ref, oa_ref, ob_ref,
                 xe_ref, xb_ref, he_ref, sem, *, caps, na, nd):
    e = pl.program_id(0)
    s = pl.program_id(1)
    srcs = (h2a_hbm, h2b_hbm)
    starts = (0, caps[0])
    rows_total = caps[0] + caps[1]

    def row_copy(grp, tok, r):
        return pltpu.make_async_copy(srcs[grp].at[pl.ds(tok, 1), :], xe_ref.at[pl.ds(r, 1), :], sem)

    def wait_all_rows():
        for grp in range(2):
            pltpu.make_async_copy(srcs[grp].at[pl.ds(0, caps[grp]), :],
                                  xe_ref.at[pl.ds(starts[grp], caps[grp]), :], sem).wait()

    @pl.when((e == 0) & (s == 0))
    def _():
        for grp in range(2):
            def issue(r, c, grp=grp):
                row_copy(grp, idx_ref[starts[grp] + r], starts[grp] + r).start()
                return c

            lax.fori_loop(0, caps[grp], issue, 0, unroll=8)

    @pl.when(s == 0)
    def _():
        wait_all_rows()
        xb_ref[...] = xe_ref[...].astype(bf16)

    @pl.when(s < na)
    def _():
        xb = xb_ref[...]
        gp = jnp.dot(xb, wg_ref[0].astype(bf16), preferred_element_type=f32)
        up = jnp.dot(xb, wu_ref[0].astype(bf16), preferred_element_type=f32)
        he = ((gp * jax.nn.sigmoid(gp)) * up).astype(bf16)
        for k in range(na):
            @pl.when(s == k)
            def _(k=k):
                he_ref[:, k * _TF:(k + 1) * _TF] = he

    @pl.when(s >= na)
    def _():
        base = jnp.minimum(e + 1, N_EXPERTS - 1) * rows_total
        for k in range(nd):
            @pl.when(s - na == k)
            def _(k=k):
                for grp in range(2):
                    per = caps[grp] // nd
                    for r in range(starts[grp] + k * per, starts[grp] + (k + 1) * per):
                        row_copy(grp, idx_ref[base + r], r).start()

        acc = jnp.dot(he_ref[...], wd_ref[0].astype(bf16), preferred_element_type=f32)
        scaled = acc * g_ref[0]
        oa_ref[0] = scaled[:caps[0]]
        ob_ref[0] = scaled[caps[0]:]

    @pl.when((e == N_EXPERTS - 1) & (s == na + nd - 1))
    def _():
        wait_all_rows()


def _expert_ffn(idx_a, idx_b, h2a, h2b, g_a, g_b, w_gate, w_up, w_down):
    caps = (idx_a.shape[1], idx_b.shape[1])
    rows = caps[0] + caps[1]
    na = D_FF // _TF
    nd = D // _TN_DOWN
    idx = jnp.concatenate([idx_a, idx_b], axis=1).reshape(N_EXPERTS * rows)
    g3 = jnp.concatenate([g_a, g_b], axis=1)[:, :, None]
    body = functools.partial(_expert_body, caps=caps, na=na, nd=nd)
    up_tile = lambda e, s, idx: (e, 0, jnp.minimum(s, na - 1))
    down_tile = lambda e, s, idx: (e, 0, jnp.maximum(s - na, 0))
    grid_spec = pltpu.PrefetchScalarGridSpec(
        num_scalar_prefetch=1,
        grid=(N_EXPERTS, na + nd),
        in_specs=[pl.BlockSpec(memory_space=pl.ANY),
                  pl.BlockSpec(memory_space=pl.ANY),
                  pl.BlockSpec((1, rows, 1), lambda e, s, idx: (e, 0, 0)),
                  pl.BlockSpec((1, D, _TF), up_tile),
                  pl.BlockSpec((1, D, _TF), up_tile),
                  pl.BlockSpec((1, D_FF, _TN_DOWN), down_tile)],
        out_specs=[pl.BlockSpec((1, caps[0], _TN_DOWN), down_tile),
                   pl.BlockSpec((1, caps[1], _TN_DOWN), down_tile)],
        scratch_shapes=[pltpu.VMEM((rows, D), f32), pltpu.VMEM((rows, D), bf16),
                        pltpu.VMEM((rows, D_FF), bf16), pltpu.SemaphoreType.DMA(())],
    )
    return pl.pallas_call(
        body,
        grid_spec=grid_spec,
        out_shape=[jax.ShapeDtypeStruct((N_EXPERTS, caps[0], D), f32),
                   jax.ShapeDtypeStruct((N_EXPERTS, caps[1], D), f32)],
        compiler_params=_cparams(("arbitrary", "arbitrary"), 60),
        name="expert_ffn",
    )(idx, h2a, h2b, g3, w_gate, w_up, w_down)


_TT = 128
_KB = 256
_RUN_MAX = _TT + 2 * SUBLANES
_RUN_PIECES = tuple(SUBLANES << i for i in reversed(range((_RUN_MAX // SUBLANES).bit_length())))
_STAGE_ROWS = N_EXPERTS * _RUN_MAX
_STAGE_PIECES = tuple(SUBLANES << i
                      for i in reversed(range((_STAGE_ROWS // SUBLANES).bit_length())))


def _combine_body(offs_ref, ye_hbm, sp_ref, x1_ref, mods_ref, o_ref, stage_ref, acc_ref, sem,
                  *, cap):
    j = pl.program_id(0)
    slot = j & 1

    def wait_rows(total, sl):
        for size in _STAGE_PIECES:
            @pl.when((total & size) != 0)
            def _(size=size):
                pltpu.make_async_copy(ye_hbm.at[pl.ds(0, size), :],
                                      stage_ref.at[sl, pl.ds(0, size), :], sem.at[sl]).wait()

    def tile_copies(jj, sl, start):
        bases, offs = [], []
        base = jnp.int32(0)
        for e in range(N_EXPERTS):
            off = offs_ref[e, jj]
            end = offs_ref[e, jj + 1]
            off8 = (off // SUBLANES) * SUBLANES
            length = jnp.where(end > off, ((end + SUBLANES - 1) // SUBLANES) * SUBLANES - off8, 0)
            bases.append(base)
            offs.append(off8)
            src = e * cap + off8
            dst = base
            for size in _RUN_PIECES if start else ():
                piece = length & size

                @pl.when(piece != 0)
                def _(src=src, dst=dst, size=size):
                    pltpu.make_async_copy(
                        ye_hbm.at[pl.ds(pl.multiple_of(src, SUBLANES), size), :],
                        stage_ref.at[sl, pl.ds(pl.multiple_of(dst, SUBLANES), size), :],
                        sem.at[sl]).start()

                src = src + piece
                dst = dst + piece
            base = base + length
        return bases, offs, base

    @pl.when(j == 0)
    def _():
        stage_ref[...] = jnp.zeros_like(stage_ref)
        tile_copies(0, 0, True)

    @pl.when(j + 1 < pl.num_programs(0))
    def _():
        tile_copies(j + 1, 1 - slot, True)

    bases, offs, total = tile_copies(j, slot, False)
    wait_rows(total, slot)

    sp = sp_ref[...]
    kiota = lax.broadcasted_iota(i32, (1, _KB), 1)
    acc_ref[...] = jnp.zeros_like(acc_ref)

    def kblock(kb, c):
        hit = None
        for e in range(N_EXPERTS):
            local = sp[:, e:e + 1] + (bases[e] - offs[e] - kb * _KB)
            h = local == kiota
            hit = h if hit is None else (hit | h)
        s = jnp.where(hit, 1.0, 0.0).astype(bf16)
        rows = stage_ref[slot, pl.ds(pl.multiple_of(kb * _KB, _KB), _KB), :]
        hi = rows.astype(bf16)
        lo = (rows - hi.astype(f32)).astype(bf16)
        acc_ref[...] += (jnp.dot(s, hi, preferred_element_type=f32)
                         + jnp.dot(s, lo, preferred_element_type=f32))
        return c

    lax.fori_loop(0, (total + _KB - 1) // _KB, kblock, 0)
    o_ref[...] = x1_ref[...] + mods_ref[0][5:6] * acc_ref[...]


def _combine(offs, ye_flat, sp_n, x1, mods, cap, tiles_per_row):
    n = x1.shape[0]
    row = (lambda j: 1 + j // tiles_per_row) if tiles_per_row else (lambda j: 0)
    body = functools.partial(_combine_body, cap=cap)
    grid_spec = pltpu.PrefetchScalarGridSpec(
        num_scalar_prefetch=1,
        grid=(n // _TT,),
        in_specs=[pl.BlockSpec(memory_space=pl.ANY),
                  pl.BlockSpec((_TT, N_EXPERTS), lambda j, offs: (j, 0)),
                  pl.BlockSpec((_TT, D), lambda j, offs: (j, 0)),
                  pl.BlockSpec((1, 6, D), lambda j, offs: (row(j), 0, 0))],
        out_specs=pl.BlockSpec((_TT, D), lambda j, offs: (j, 0)),
        scratch_shapes=[pltpu.VMEM((2, N_EXPERTS * _RUN_MAX, D), f32), pltpu.VMEM((_TT, D), f32),
                        pltpu.SemaphoreType.DMA((2,))],
    )
    return pl.pallas_call(
        body,
        grid_spec=grid_spec,
        out_shape=jax.ShapeDtypeStruct((n, D), f32),
        compiler_params=_cparams(("arbitrary",), 48),
        name="expert_combine",
    )(offs, ye_flat, sp_n, x1, mods)


def _moe(x1s, h2s, affs, mods, w_gate, w_up, w_down, tiles_per_row):
    routes = [_route(aff_t, CAPACITY_FACTOR * aff_t.shape[1] // N_EXPERTS) for aff_t in affs]
    (idx_a, g_a, _, _), (idx_b, g_b, _, _) = routes
    yes = _expert_ffn(idx_a, idx_b, h2s[0], h2s[1], g_a, g_b, w_gate, w_up, w_down)
    outs = []
    for ye, (idx, _, pos, sp), x1, tpr in zip(yes, routes, x1s, tiles_per_row):
        cap = idx.shape[1]
        offs = jnp.concatenate([pos[:, ::_TT], jnp.full((N_EXPERTS, 1), cap, i32)], axis=1)
        outs.append(_combine(offs, ye.reshape(N_EXPERTS * cap, D), sp.T, x1, mods, cap, tpr))
    return outs


def kernel(x_prompt, x_sample, cache_k, cache_v, state_fwd, state_bwd, c, c_ctx, w_mod, b_mod,
           norm_mix, w_in, q_norm, k_norm, rpb, conv_w, conv_b, lru_w_a, lru_b_a, lru_w_x,
           lru_b_x, lru_lambda, w_out, norm_ffn, w_router, w_gate, w_up, w_down):
    B, S, _ = x_prompt.shape
    DB, T, _ = x_sample.shape
    l = 0

    cvecs = jnp.concatenate([c_ctx[None, :], c, jnp.zeros((SUBLANES - 1 - DB, D), f32)], axis=0)
    mods = _adaln_mods(cvecs, w_mod[l], b_mod[l][None, :]).reshape(SUBLANES, 6, D)

    heads_per_tile = _TN_IN // HEAD_DIM
    qn_t = jnp.tile(q_norm[l], heads_per_tile)[None, :]
    kn_t = jnp.tile(k_norm[l], heads_per_tile)[None, :]
    gmat = (jnp.kron(jnp.eye(heads_per_tile, dtype=f32), jnp.ones((HEAD_DIM, HEAD_DIM), f32))
            / HEAD_DIM).astype(bf16)
    nmix = norm_mix[l][None, :]
    nffn = norm_ffn[l][None, :]
    w_in_b = w_in[l].astype(bf16)
    w_out_b = w_out[l].astype(bf16)
    w_router_t = w_router[l].T.astype(bf16)
    wcat = _lru_gate_weights(lru_w_a[l], lru_w_x[l])
    cb = conv_b[l][None, :]

    xp = x_prompt.reshape(B * S, D)
    xs = x_sample.reshape(DB * T, D)

    proj_c = _inproj(xp, mods, nmix, w_in_b, qn_t, kn_t, gmat, per_tile_rows=False)
    att_c, new_k, new_v = _ctx_attention(proj_c, B, S)
    zeros_c = jnp.zeros((B, LRU_W), f32)
    lru_c, htf, htb = _rglru(proj_c, wcat, conv_w[l], cb, lru_b_a[l], lru_b_x[l], lru_lambda[l],
                             zeros_c, zeros_c, T=S, nb=SUBLANES)
    x1_c, h2_c, aff_c = _outproj(att_c, lru_c, w_out_b, xp, mods, nffn, w_router_t, 0)

    nct = LRU_W // _LRU_TC

    def slabs(h0):
        return (h0.reshape(DB, nct, _LRU_HALVES, LANES).transpose(2, 0, 1, 3)
                .reshape(_LRU_HALVES * DB, nct * LANES))

    proj_l = _inproj(xs, mods, nmix, w_in_b, qn_t, kn_t, gmat, per_tile_rows=True)
    att_l = _lat_attention(proj_l, cache_k, cache_v, _pad_rpb(rpb[l]), l, DB, T)
    h0f_l = slabs(state_fwd[:, l])
    h0b_l = slabs(state_bwd[:, l])
    lru_l, _, _ = _rglru(proj_l, wcat, conv_w[l], cb, lru_b_a[l], lru_b_x[l], lru_lambda[l],
                         h0f_l, h0b_l, T=T, nb=DB)
    x1_l, h2_l, aff_l = _outproj(att_l, lru_l, w_out_b, xs, mods, nffn, w_router_t, T // 512)

    y_c, y_l = _moe((x1_c, x1_l), (h2_c, h2_l), (aff_c, aff_l), mods,
                    w_gate[l], w_up[l], w_down[l], (0, T // _TT))

    return (y_c.reshape(B, S, D), y_l.reshape(DB, T, D), new_k, new_v,
            htf[:, None, :], htb[:, None, :])
```

```python
import functools

import jax
import jax.numpy as jnp
from jax import lax
from jax.experimental import pallas as pl
from jax.experimental.pallas import tpu as pltpu

f32 = jnp.float32
bf16 = jnp.bfloat16
i32 = jnp.int32

D = 2048
N_HEADS = 16
HEAD_DIM = 64
ATT_W = N_HEADS * HEAD_DIM
LRU_W = D - ATT_W
LRU_BLOCK = 64
LRU_C = 8.0
GRID_W = 64
WIN_H = 8
WIN_W = 16
N_EXPERTS = 16
CAPACITY_FACTOR = 2
D_FF = 2048
EPS = 1e-6
NEG_INF = -1e30
IN_COLS = 3 * ATT_W + 2 * LRU_W

LANES = 128
SUBLANES = 8
MXU_DIM = 256
UNSELECTED = -(1 << 30)

_MIB = 1024 * 1024


def _cparams(sem, vmem_mib):
    return pltpu.CompilerParams(dimension_semantics=sem, vmem_limit_bytes=vmem_mib * _MIB)


def _mods_body(c_ref, w_ref, b_ref, o_ref):
    c = c_ref[...]
    s = (c * jax.nn.sigmoid(c)).astype(bf16)
    o_ref[...] = jnp.dot(s, w_ref[...].astype(bf16), preferred_element_type=f32) + b_ref[...]


def _adaln_mods(cvecs, w_mod, b_mod):
    tn = 1024
    return pl.pallas_call(
        _mods_body,
        grid=(6 * D // tn,),
        in_specs=[pl.BlockSpec((SUBLANES, D), lambda j: (0, 0)),
                  pl.BlockSpec((D, tn), lambda j: (0, j)),
                  pl.BlockSpec((1, tn), lambda j: (0, j))],
        out_specs=pl.BlockSpec((SUBLANES, tn), lambda j: (0, j)),
        out_shape=jax.ShapeDtypeStruct((SUBLANES, 6 * D), f32),
        compiler_params=_cparams(("arbitrary",), 40),
        name="adaln_mods",
    )(cvecs, w_mod, b_mod)


_TN_IN = 512
_NORM_ROWS = 16
_NORM_SLICES = 8


def _inproj_body(x0_ref, xn_ref, mods0_ref, modsn_ref, nw_ref, w_ref, qn_ref, kn_ref, g_ref,
                 o_ref, ha_ref, hb_ref):
    i = pl.program_id(0)
    j = pl.program_id(1)
    tm = ha_ref.shape[0]
    nw = nw_ref[...]
    n_qk_tiles = 2 * ATT_W // _TN_IN

    def norm_rows(x_ref, mods_ref, h_ref, r0):
        m = mods_ref[0]
        rows = pl.ds(pl.multiple_of(r0, _NORM_ROWS), _NORM_ROWS)
        x = x_ref[rows, :]
        ms = jnp.mean(x * x, axis=-1, keepdims=True)
        y = x * lax.rsqrt(ms + EPS) * nw
        h_ref[rows, :] = (y * (1.0 + m[1:2]) + m[0:1]).astype(bf16)

    @pl.when((i == 0) & (j == 0))
    def _():
        def first(c, carry):
            norm_rows(x0_ref, mods0_ref, ha_ref, c * _NORM_ROWS)
            return carry

        lax.fori_loop(0, tm // _NORM_ROWS, first, 0, unroll=4)

    def column_step(cur_ref, nxt_ref):
        slice_rows = tm // _NORM_SLICES
        r_slice = jnp.minimum(j, _NORM_SLICES - 1) * slice_rows
        for c in range(slice_rows // _NORM_ROWS):
            norm_rows(xn_ref, modsn_ref, nxt_ref, r_slice + c * _NORM_ROWS)

        acc = jnp.dot(cur_ref[...], w_ref[...], preferred_element_type=f32)

        @pl.when(j < n_qk_tiles)
        def _():
            sq = acc * acc
            hi = sq.astype(bf16)
            lo = (sq - hi.astype(f32)).astype(bf16)
            g = g_ref[...]
            ms = (jnp.dot(hi, g, preferred_element_type=f32)
                  + jnp.dot(lo, g, preferred_element_type=f32))
            head_w = jnp.where(j < n_qk_tiles // 2, qn_ref[...], kn_ref[...])
            o_ref[...] = acc * lax.rsqrt(ms + EPS) * head_w

        @pl.when(j >= n_qk_tiles)
        def _():
            o_ref[...] = acc

    @pl.when((i & 1) == 0)
    def _():
        column_step(ha_ref, hb_ref)

    @pl.when((i & 1) == 1)
    def _():
        column_step(hb_ref, ha_ref)


def _inproj(x2d, mods, norm_w, w_in, qn_t, kn_t, gmat, per_tile_rows):
    n = x2d.shape[0]
    tm = 1024
    last = n // tm - 1
    row = (lambda i: 1 + i) if per_tile_rows else (lambda i: 0)
    nxt = lambda i: jnp.minimum(i + 1, last)
    return pl.pallas_call(
        _inproj_body,
        grid=(n // tm, IN_COLS // _TN_IN),
        in_specs=[pl.BlockSpec((tm, D), lambda i, j: (0, 0)),
                  pl.BlockSpec((tm, D), lambda i, j: (nxt(i), 0)),
                  pl.BlockSpec((1, 6, D), lambda i, j: (row(0), 0, 0)),
                  pl.BlockSpec((1, 6, D), lambda i, j: (row(nxt(i)), 0, 0)),
                  pl.BlockSpec((1, D), lambda i, j: (0, 0)),
                  pl.BlockSpec((D, _TN_IN), lambda i, j: (0, j)),
                  pl.BlockSpec((1, _TN_IN), lambda i, j: (0, 0)),
                  pl.BlockSpec((1, _TN_IN), lambda i, j: (0, 0)),
                  pl.BlockSpec((_TN_IN, _TN_IN), lambda i, j: (0, 0))],
        out_specs=pl.BlockSpec((tm, _TN_IN), lambda i, j: (i, j)),
        out_shape=jax.ShapeDtypeStruct((n, IN_COLS), f32),
        scratch_shapes=[pltpu.VMEM((tm, D), bf16), pltpu.VMEM((tm, D), bf16)],
        compiler_params=_cparams(("arbitrary", "arbitrary"), 56),
        name="in_projection",
    )(x2d, x2d, mods, mods, norm_w, w_in, qn_t, kn_t, gmat)


def _softmax_pv(s_list, v_list):
    m = s_list[0].max(axis=-1, keepdims=True)
    for s in s_list[1:]:
        m = jnp.maximum(m, s.max(axis=-1, keepdims=True))
    o = None
    for s, v in zip(s_list, v_list):
        os_ = jnp.dot(jnp.exp(s - m).astype(bf16), v, preferred_element_type=f32)
        o = os_ if o is None else o + os_
    return o


def _normalise_pair(o0, o1, lane):
    left = lane < HEAD_DIM
    den = pltpu.roll(jnp.where(left, o1, o0), HEAD_DIM, 1)
    return jnp.where(left, o0, o1) / den


def _nt_dot(a, b):
    return lax.dot_general(a, b, (((1,), (1,)), ((), ())), preferred_element_type=f32)


_QK_SCALE = HEAD_DIM ** -0.5


def _ctx_attn_body(q_ref, k_ref, v_ref, o_ref, nk_ref, nv_ref):
    lane = lax.broadcasted_iota(i32, (1, LANES), 1)
    for p in range(ATT_W // LANES):
        cols = slice(p * LANES, (p + 1) * LANES)
        q = q_ref[:, cols] * _QK_SCALE
        kf = k_ref[:, cols]
        k = kf.astype(bf16)
        v = v_ref[:, cols]
        nk_ref[0, 0, 2 * p] = kf[:, :HEAD_DIM]
        nk_ref[0, 0, 2 * p + 1] = pltpu.roll(kf, HEAD_DIM, 1)[:, :HEAD_DIM]
        nv_ref[0, 0, 2 * p] = v[:, :HEAD_DIM]
        nv_ref[0, 0, 2 * p + 1] = pltpu.roll(v, HEAD_DIM, 1)[:, :HEAD_DIM]
        heads = []
        for hh in range(2):
            hm = (lane >= hh * HEAD_DIM) & (lane < (hh + 1) * HEAD_DIM)
            qh = jnp.where(hm, q, 0.0).astype(bf16)
            vh = jnp.where(hm, v, 1.0).astype(bf16)
            heads.append(_softmax_pv([_nt_dot(qh, k)], [vh]))
        o_ref[:, cols] = _normalise_pair(heads[0], heads[1], lane).astype(bf16)


def _ctx_attention(proj, batch, seq):
    n = proj.shape[0]
    cache = pl.BlockSpec((1, 1, N_HEADS, seq, HEAD_DIM), lambda b: (b, 0, 0, 0, 0))
    cache_shape = jax.ShapeDtypeStruct((batch, 1, N_HEADS, seq, HEAD_DIM), f32)
    return pl.pallas_call(
        _ctx_attn_body,
        grid=(batch,),
        in_specs=[pl.BlockSpec((seq, ATT_W), lambda b: (b, 0)),
                  pl.BlockSpec((seq, ATT_W), lambda b: (b, 1)),
                  pl.BlockSpec((seq, ATT_W), lambda b: (b, 2))],
        out_specs=[pl.BlockSpec((seq, ATT_W), lambda b: (b, 0)), cache, cache],
        out_shape=[jax.ShapeDtypeStruct((n, ATT_W), bf16), cache_shape, cache_shape],
        compiler_params=_cparams(("arbitrary",), 32),
        name="context_attention",
    )(proj, proj, proj)


_NA_ROWS = 16
_NA_QROWS = 8
_NA_KROWS = 12


def _pad_rpb(rpb):
    nd, nc = 2 * WIN_H - 1, 2 * WIN_W - 1
    return jnp.pad(rpb.astype(f32), ((0, 0), (0, 2 * SUBLANES - nd), (0, LANES - nc)))


def _build_na_bias(rpb_ref, bias_ref):
    qc = lax.broadcasted_iota(i32, (GRID_W, LANES), 0)
    lane = lax.broadcasted_iota(i32, (GRID_W, LANES), 1)
    kc = lane & (GRID_W - 1)
    cs = jnp.clip(qc - WIN_W // 2, 0, GRID_W - WIN_W)
    col_ok = (kc >= cs) & (kc < cs + WIN_W)
    left = lane < GRID_W
    neg = jnp.full((GRID_W, LANES), NEG_INF, f32)
    for hh in range(2):
        tbl = rpb_ref[hh]
        tiles = {}

        def toeplitz(d, odd):
            if (d, odd) not in tiles:
                row = jnp.broadcast_to(tbl[d:d + 1, :], (GRID_W, LANES))
                shift = (LANES - (WIN_W - 1) + odd * GRID_W) % LANES
                t = pltpu.roll(row, shift, 1, stride=1, stride_axis=0)
                tiles[(d, odd)] = jnp.where(col_ok, t, NEG_INF)
            return tiles[(d, odd)]

        for j in range(2):
            for rq in range(_NA_QROWS):
                rq_abs = j * _NA_QROWS + rq
                rs = min(max(rq_abs - WIN_H // 2, 0), _NA_ROWS - WIN_H)
                for kp in range(_NA_KROWS // 2):
                    parts = []
                    for odd in range(2):
                        rk_abs = j * (_NA_ROWS - _NA_KROWS) + 2 * kp + odd
                        if rs <= rk_abs < rs + WIN_H:
                            parts.append(toeplitz(rk_abs - rq_abs + WIN_H - 1, odd))
                        else:
                            parts.append(neg)
                    bias_ref[hh, j, rq * GRID_W:(rq + 1) * GRID_W,
                             kp * LANES:(kp + 1) * LANES] = jnp.where(left, parts[0], parts[1])


def _lat_attn_body(q_ref, k_ref, v_ref, ck_ref, cv_ref, rpb_ref, o_ref, bias_ref):
    @pl.when(pl.program_id(1) == 0)
    def _():
        _build_na_bias(rpb_ref, bias_ref)

    lane = lax.broadcasted_iota(i32, (1, LANES), 1)
    nq = _NA_QROWS * GRID_W
    nk = _NA_KROWS * GRID_W
    kstep = (_NA_ROWS - _NA_KROWS) * GRID_W
    ck = [ck_ref[0, 0, hh].astype(bf16) for hh in range(2)]
    cv = []
    for hh in range(2):
        own = (lane >= hh * HEAD_DIM) & (lane < (hh + 1) * HEAD_DIM)
        vv = cv_ref[0, 0, hh]
        swapped = pltpu.roll(vv, HEAD_DIM, 1)
        cv.append((jnp.where(own, vv if hh == 0 else swapped, 1.0).astype(bf16),
                   jnp.where(own, swapped if hh == 0 else vv, 1.0).astype(bf16)))
    for j in range(2):
        q = q_ref[j * nq:(j + 1) * nq, :] * _QK_SCALE
        k = k_ref[j * kstep:j * kstep + nk, :].astype(bf16)
        v = v_ref[j * kstep:j * kstep + nk, :]
        heads = []
        for hh in range(2):
            hm = (lane >= hh * HEAD_DIM) & (lane < (hh + 1) * HEAD_DIM)
            q_own = jnp.where(hm, q, 0.0)
            q_other = pltpu.roll(q_own, HEAD_DIM, 1)
            q_lo, q_hi = (q_own, q_other) if hh == 0 else (q_other, q_own)
            qh = q_own.astype(bf16)
            s_win = _nt_dot(qh, k) + bias_ref[hh, j]
            s_even = _nt_dot(q_lo.astype(bf16), ck[hh])
            s_odd = _nt_dot(q_hi.astype(bf16), ck[hh])
            vh = jnp.where(hm, v, 1.0).astype(bf16)
            heads.append(_softmax_pv([s_win, s_even, s_odd], [vh, cv[hh][0], cv[hh][1]]))
        o_ref[j * nq:(j + 1) * nq, :] = _normalise_pair(heads[0], heads[1], lane).astype(bf16)


def _lat_attention(proj, ck, cv, rpb_p, layer, batch, seq):
    n = proj.shape[0]
    depth, past = ck.shape[1], ck.shape[3]
    pairs = past * HEAD_DIM // LANES
    ck = ck.reshape(batch, depth, N_HEADS, pairs, LANES)
    cv = cv.reshape(batch, depth, N_HEADS, pairs, LANES)
    cache = pl.BlockSpec((1, 1, 2, pairs, LANES), lambda p, b: (b, layer, p, 0, 0))
    npair = ATT_W // LANES
    nq = _NA_QROWS * GRID_W
    nk = _NA_KROWS * GRID_W
    return pl.pallas_call(
        _lat_attn_body,
        grid=(npair, batch),
        in_specs=[pl.BlockSpec((seq, LANES), lambda p, b: (b, p)),
                  pl.BlockSpec((seq, LANES), lambda p, b: (b, npair + p)),
                  pl.BlockSpec((seq, LANES), lambda p, b: (b, 2 * npair + p)),
                  cache, cache,
                  pl.BlockSpec((2, 2 * SUBLANES, LANES), lambda p, b: (p, 0, 0))],
        out_specs=pl.BlockSpec((seq, LANES), lambda p, b: (b, p)),
        out_shape=jax.ShapeDtypeStruct((n, ATT_W), bf16),
        scratch_shapes=[pltpu.VMEM((2, 2, nq, nk), f32)],
        compiler_params=_cparams(("arbitrary", "arbitrary"), 48),
        name="neighbourhood_attention",
    )(proj, proj, proj, ck, cv, rpb_p)


_LRU_TC = 256
_GATE_ROWS = 256
_LRU_HALVES = _LRU_TC // LANES


def _gelu_tanh(x):
    c = 0.7978845608028654
    return x * (0.5 * (1.0 + jnp.tanh(c * (x + 0.044715 * (x * x * x)))))


def _lru_body(xr_ref, gr_ref, w_ref, cw_ref, cb_ref, ba_ref, bx_ref, lam_ref, h0f_ref, h0b_ref,
              y_ref, htf_ref, htb_ref, xc_ref, af_ref, uf_ref, ab_ref, ub_ref, *, T, nb):
    nslab = _LRU_HALVES * nb // SUBLANES
    R = nb * T
    x = xr_ref[...]
    t = lax.broadcasted_iota(i32, (R, 1), 0) & (T - 1)
    cw = cw_ref[...]
    xc = jnp.where(t >= 2, pltpu.roll(x, 2, 0), 0.0) * cw[0:1]
    xc = xc + jnp.where(t >= 1, pltpu.roll(x, 1, 0), 0.0) * cw[1:2]
    xc = xc + x * cw[2:3]
    xc = xc + jnp.where(t < T - 1, pltpu.roll(x, R - 1, 0), 0.0) * cw[3:4]
    xc_ref[...] = xc + cb_ref[...]

    nl = -lam_ref[...]
    sp = jnp.maximum(nl, 0.0) + jnp.log1p(jnp.exp(-jnp.abs(nl)))
    half_c_sp = (0.5 * LRU_C) * sp
    w = w_ref[0]
    dirs = ((af_ref, uf_ref), (ab_ref, ub_ref))
    for b in range(nb):
        for ci in range(T // _GATE_ROWS):
            r0 = b * T + ci * _GATE_ROWS
            xcc = xc_ref[r0:r0 + _GATE_ROWS, :]
            half_x = 0.5 * xcc
            z = jnp.dot(xcc.astype(bf16), w, preferred_element_type=f32)
            for d, (a_ref, u_ref) in enumerate(dirs):
                c0 = d * 2 * _LRU_TC
                tr = jnp.tanh(0.5 * (z[:, c0:c0 + _LRU_TC] + ba_ref[d:d + 1]))
                tg = jnp.tanh(0.5 * (z[:, c0 + _LRU_TC:c0 + 2 * _LRU_TC] + bx_ref[d:d + 1]))
                neg_log_a = (1.0 + tr) * half_c_sp[d:d + 1]
                a = jnp.exp(-neg_log_a)
                one_m_a2 = jnp.tanh(neg_log_a) * (1.0 + a * a)
                root = jnp.where(one_m_a2 > 0.0, one_m_a2 * lax.rsqrt(one_m_a2), 0.0)
                u = root * ((1.0 + tg) * half_x)
                for hf in range(_LRU_HALVES):
                    s, sub = divmod(hf * nb + b, SUBLANES)
                    rows = pl.ds(ci * _GATE_ROWS * SUBLANES + sub, _GATE_ROWS, stride=SUBLANES)
                    a_ref[s, rows, :] = a[:, hf * LANES:(hf + 1) * LANES]
                    u_ref[s, rows, :] = u[:, hf * LANES:(hf + 1) * LANES]

    def step(i, carry):
        rf = pl.ds(pl.multiple_of(i * SUBLANES, SUBLANES), SUBLANES)
        rb = pl.ds(pl.multiple_of((T - 1 - i) * SUBLANES, SUBLANES), SUBLANES)
        new = []
        for s in range(nslab):
            hf_ = af_ref[s, rf, :] * carry[2 * s] + uf_ref[s, rf, :]
            uf_ref[s, rf, :] = hf_
            hb_ = ab_ref[s, rb, :] * carry[2 * s + 1] + ub_ref[s, rb, :]
            ub_ref[s, rb, :] = hb_
            new += [hf_, hb_]
        return tuple(new)

    init = []
    for s in range(nslab):
        init += [h0f_ref[:, s * LANES:(s + 1) * LANES], h0b_ref[:, s * LANES:(s + 1) * LANES]]
    fin = lax.fori_loop(0, T, step, tuple(init), unroll=8)
    for s in range(nslab):
        htf_ref[:, s * LANES:(s + 1) * LANES] = fin[2 * s]
        htb_ref[:, s * LANES:(s + 1) * LANES] = fin[2 * s + 1]

    for b in range(nb):
        for hf in range(_LRU_HALVES):
            s, sub = divmod(hf * nb + b, SUBLANES)
            rows = pl.ds(sub, T, stride=SUBLANES)
            hs = uf_ref[s, rows, :] + ub_ref[s, rows, :]
            g = gr_ref[b * T:(b + 1) * T, hf * LANES:(hf + 1) * LANES]
            y_ref[b * T:(b + 1) * T, hf * LANES:(hf + 1) * LANES] = (
                hs * _gelu_tanh(g)).astype(bf16)


def _rglru(proj, wcat, conv_w, conv_b, b_a, b_x, lam, h0f, h0b, T, nb):
    n = proj.shape[0]
    nslab = _LRU_HALVES * nb // SUBLANES
    W = nslab * LANES
    R = nb * T
    nct = LRU_W // _LRU_TC
    xr0 = 3 * ATT_W // _LRU_TC
    gr0 = xr0 + nct
    chan = lambda k: pl.BlockSpec((k, _LRU_TC), lambda g, j: (0, j))
    slab = pl.BlockSpec((SUBLANES, W), lambda g, j: (g, j))
    body = functools.partial(_lru_body, T=T, nb=nb)
    return pl.pallas_call(
        body,
        grid=(n // R, nct),
        in_specs=[pl.BlockSpec((R, _LRU_TC), lambda g, j: (g, xr0 + j)),
                  pl.BlockSpec((R, _LRU_TC), lambda g, j: (g, gr0 + j)),
                  pl.BlockSpec((1, _LRU_TC, 4 * _LRU_TC), lambda g, j: (j, 0, 0)),
                  chan(4), chan(1), chan(2), chan(2), chan(2), slab, slab],
        out_specs=[pl.BlockSpec((R, _LRU_TC), lambda g, j: (g, j)), slab, slab],
        out_shape=[jax.ShapeDtypeStruct((n, LRU_W), bf16),
                   jax.ShapeDtypeStruct(h0f.shape, f32),
                   jax.ShapeDtypeStruct(h0b.shape, f32)],
        scratch_shapes=[pltpu.VMEM((R, _LRU_TC), f32)]
        + [pltpu.VMEM((nslab, T * SUBLANES, LANES), f32) for _ in range(4)],
        compiler_params=_cparams(("arbitrary", "arbitrary"), 56),
        name="rg_lru",
    )(proj, proj, wcat, conv_w, conv_b, b_a, b_x, lam, h0f, h0b)


def _lru_gate_weights(w_a, w_x):
    per_tile = _LRU_TC // LRU_BLOCK
    nct = LRU_W // _LRU_TC
    eye = jnp.eye(per_tile, dtype=f32)

    def bd(w):
        w = w.reshape(nct, per_tile, LRU_BLOCK, LRU_BLOCK)
        m = w[:, :, :, None, :] * eye[None, :, None, :, None]
        return m.reshape(nct, _LRU_TC, _LRU_TC)

    return jnp.concatenate([bd(w_a[0]), bd(w_x[0]), bd(w_a[1]), bd(w_x[1])], axis=-1).astype(bf16)


def _outproj_body(att_ref, lru_ref, w_ref, x_ref, mods_ref, nw_ref, wr_ref,
                  x1_ref, h2_ref, aff_ref):
    mix = (jnp.dot(att_ref[...], w_ref[0:ATT_W, :], preferred_element_type=f32)
           + jnp.dot(lru_ref[...], w_ref[ATT_W:D, :], preferred_element_type=f32))
    m = mods_ref[0]
    x1 = x_ref[...] + m[2:3] * mix
    x1_ref[...] = x1
    ms = jnp.mean(x1 * x1, axis=-1, keepdims=True)
    h2 = (x1 * lax.rsqrt(ms + EPS) * nw_ref[...]) * (1.0 + m[4:5]) + m[3:4]
    h2_ref[...] = h2
    logits = _nt_dot(wr_ref[...], h2.astype(bf16))
    e = jnp.exp(logits - logits.max(axis=0, keepdims=True))
    aff_ref[...] = e / e.sum(axis=0, keepdims=True)


def _outproj(att, lru, w_out_b, x2d, mods, norm_w, w_router_t, tiles_per_row):
    n = x2d.shape[0]
    tm = 512
    row = (lambda i: 1 + i // tiles_per_row) if tiles_per_row else (lambda i: 0)
    return pl.pallas_call(
        _outproj_body,
        grid=(n // tm,),
        in_specs=[pl.BlockSpec((tm, ATT_W), lambda i: (i, 0)),
                  pl.BlockSpec((tm, LRU_W), lambda i: (i, 0)),
                  pl.BlockSpec((D, D), lambda i: (0, 0)),
                  pl.BlockSpec((tm, D), lambda i: (i, 0)),
                  pl.BlockSpec((1, 6, D), lambda i: (row(i), 0, 0)),
                  pl.BlockSpec((1, D), lambda i: (0, 0)),
                  pl.BlockSpec((N_EXPERTS, D), lambda i: (0, 0))],
        out_specs=[pl.BlockSpec((tm, D), lambda i: (i, 0)),
                   pl.BlockSpec((tm, D), lambda i: (i, 0)),
                   pl.BlockSpec((N_EXPERTS, tm), lambda i: (0, i))],
        out_shape=[jax.ShapeDtypeStruct((n, D), f32),
                   jax.ShapeDtypeStruct((n, D), f32),
                   jax.ShapeDtypeStruct((N_EXPERTS, n), f32)],
        compiler_params=_cparams(("arbitrary",), 52),
        name="out_projection_router",
    )(att, lru, w_out_b, x2d, mods, norm_w, w_router_t)


def _prefix_sum_lanes(x, lane, n):
    s = 1
    while s < n:
        x = x + jnp.where(lane >= s, pltpu.roll(x, s, 1), 0)
        s *= 2
    return x


def _route_body(aff_ref, idx_ref, g_ref, pos_ref, sp_ref, sel_ref, rem_ref, *, n, cap):
    aff = aff_ref[...]
    lane = lax.broadcasted_iota(i32, (N_EXPERTS, n), 1)

    def count(mask):
        return jnp.sum(jnp.where(mask, 1.0, 0.0), axis=1, keepdims=True).astype(i32)

    def as_value(bits):
        return pltpu.bitcast(jnp.broadcast_to(bits, (N_EXPERTS, LANES)), f32)[:, :1]

    def search(i, thr):
        cand = thr | jnp.left_shift(jnp.int32(1), 30 - i)
        return jnp.where(count(aff >= as_value(cand)) >= cap, cand, thr)

    thr = lax.fori_loop(0, 31, search, jnp.zeros((N_EXPERTS, 1), i32))
    sure = aff >= as_value(thr + 1)
    sel_ref[...] = sure.astype(i32)
    rem_ref[...] = ((aff >= as_value(thr)) & jnp.logical_not(sure)).astype(i32)

    def more(need):
        return jnp.max(need) > 0

    def take_largest(need):
        rem = rem_ref[...] > 0
        top = jnp.max(jnp.where(rem, aff, -1.0), axis=1, keepdims=True)
        is_top = rem & (aff == top) & (need > 0)
        rank = _prefix_sum_lanes(is_top.astype(i32), lane, n)
        pick = is_top & ((count(is_top) <= need) | (rank <= need))
        sel_ref[...] = sel_ref[...] | pick.astype(i32)
        rem_ref[...] = (rem & jnp.logical_not(is_top)).astype(i32)
        return need - count(pick)

    lax.while_loop(more, take_largest, cap - count(sure))
    sel_i = sel_ref[...]
    sel = sel_i > 0
    pos = _prefix_sum_lanes(sel_i, lane, n) - sel_i
    pos_ref[...] = pos
    sp_ref[...] = jnp.where(sel, pos, UNSELECTED)

    key = jnp.where(sel, lane - pos, -1)
    idx = lane
    g = aff
    b = 0
    while (1 << b) < n:
        s = 1 << b
        k_sh = pltpu.roll(key, n - s, 1)
        take = (k_sh >= 0) & ((jnp.right_shift(k_sh, b) & 1) == 1)
        stay = (key >= 0) & ((jnp.right_shift(key, b) & 1) == 0)
        idx = jnp.where(take, pltpu.roll(idx, n - s, 1), idx)
        g = jnp.where(take, pltpu.roll(g, n - s, 1), g)
        key = jnp.where(take, k_sh, jnp.where(stay, key, -1))
        b += 1
    idx_ref[...] = idx[:, :cap]
    g_ref[...] = g[:, :cap]


def _route(aff_t, cap):
    n = aff_t.shape[1]
    body = functools.partial(_route_body, n=n, cap=cap)
    full = lambda shape: pl.BlockSpec(shape, lambda: (0, 0))
    return pl.pallas_call(
        body,
        in_specs=[full((N_EXPERTS, n))],
        out_specs=[full((N_EXPERTS, cap)), full((N_EXPERTS, cap)),
                   full((N_EXPERTS, n)), full((N_EXPERTS, n))],
        out_shape=[jax.ShapeDtypeStruct((N_EXPERTS, cap), i32),
                   jax.ShapeDtypeStruct((N_EXPERTS, cap), f32),
                   jax.ShapeDtypeStruct((N_EXPERTS, n), i32),
                   jax.ShapeDtypeStruct((N_EXPERTS, n), i32)],
        scratch_shapes=[pltpu.VMEM((N_EXPERTS, n), i32), pltpu.VMEM((N_EXPERTS, n), i32)],
        compiler_params=pltpu.CompilerParams(vmem_limit_bytes=32 * _MIB),
        name="expert_choice_routing",
    )(aff_t)


_TF = 256
_TN_DOWN = 512


def _expert_body(idx_ref, h2a_hbm, h2b_hbm, g_ref, wg_ref, wu_ref, wd_ref, oa_ref, ob_ref,
                 xe_ref, xb_ref, he_ref, sem, *, caps, na, nd):
    e = pl.program_id(0)
    s = pl.program_id(1)
    srcs = (h2a_hbm, h2b_hbm)
    starts = (0, caps[0])
    rows_total = caps[0] + caps[1]

    def row_copy(grp, tok, r):
        return pltpu.make_async_copy(srcs[grp].at[pl.ds(tok, 1), :], xe_ref.at[pl.ds(r, 1), :], sem)

    def wait_all_rows():
        for grp in range(2):
            pltpu.make_async_copy(srcs[grp].at[pl.ds(0, caps[grp]), :],
                                  xe_ref.at[pl.ds(starts[grp], caps[grp]), :], sem).wait()

    @pl.when((e == 0) & (s == 0))
    def _():
        for grp in range(2):
            def issue(r, c, grp=grp):
                row_copy(grp, idx_ref[starts[grp] + r], starts[grp] + r).start()
                return c

            lax.fori_loop(0, caps[grp], issue, 0, unroll=8)

    @pl.when(s == 0)
    def _():
        wait_all_rows()
        xb_ref[...] = xe_ref[...].astype(bf16)

    @pl.when(s < na)
    def _():
        wg = wg_ref[0].astype(bf16)
        wu = wu_ref[0].astype(bf16)
        half = rows_total // 2
        for rows in (slice(0, half), slice(half, rows_total)):
            xb = xb_ref[rows, :]
            gp = jnp.dot(xb, wg, preferred_element_type=f32)
            up = jnp.dot(xb, wu, preferred_element_type=f32)
            he_ref[s, rows, :] = ((gp * jax.nn.sigmoid(gp)) * up).astype(bf16)

    @pl.when(s >= na)
    def _():
        base = jnp.minimum(e + 1, N_EXPERTS - 1) * rows_total
        for k in range(nd):
            @pl.when(s - na == k)
            def _(k=k):
                for grp in range(2):
                    per = caps[grp] // nd
                    for r in range(starts[grp] + k * per, starts[grp] + (k + 1) * per):
                        row_copy(grp, idx_ref[base + r], r).start()

        wd = wd_ref[0].astype(bf16)
        acc = None
        for k in range(na):
            part = jnp.dot(he_ref[k], wd[k * _TF:(k + 1) * _TF, :], preferred_element_type=f32)
            acc = part if acc is None else acc + part
        scaled = acc * g_ref[0]
        oa_ref[0] = scaled[:caps[0]]
        ob_ref[0] = scaled[caps[0]:]

    @pl.when((e == N_EXPERTS - 1) & (s == na + nd - 1))
    def _():
        wait_all_rows()


def _expert_ffn(idx_a, idx_b, h2a, h2b, g_a, g_b, w_gate, w_up, w_down):
    caps = (idx_a.shape[1], idx_b.shape[1])
    rows = caps[0] + caps[1]
    na = D_FF // _TF
    nd = D // _TN_DOWN
    idx = jnp.concatenate([idx_a, idx_b], axis=1).reshape(N_EXPERTS * rows)
    g3 = jnp.concatenate([g_a, g_b], axis=1)[:, :, None]
    body = functools.partial(_expert_body, caps=caps, na=na, nd=nd)
    up_tile = lambda e, s, idx: (e, 0, jnp.minimum(s, na - 1))
    down_tile = lambda e, s, idx: (e, 0, jnp.maximum(s - na, 0))
    grid_spec = pltpu.PrefetchScalarGridSpec(
        num_scalar_prefetch=1,
        grid=(N_EXPERTS, na + nd),
        in_specs=[pl.BlockSpec(memory_space=pl.ANY),
                  pl.BlockSpec(memory_space=pl.ANY),
                  pl.BlockSpec((1, rows, 1), lambda e, s, idx: (e, 0, 0)),
                  pl.BlockSpec((1, D, _TF), up_tile),
                  pl.BlockSpec((1, D, _TF), up_tile),
                  pl.BlockSpec((1, D_FF, _TN_DOWN), down_tile)],
        out_specs=[pl.BlockSpec((1, caps[0], _TN_DOWN), down_tile),
                   pl.BlockSpec((1, caps[1], _TN_DOWN), down_tile)],
        scratch_shapes=[pltpu.VMEM((rows, D), f32), pltpu.VMEM((rows, D), bf16),
                        pltpu.VMEM((na, rows, _TF), bf16), pltpu.SemaphoreType.DMA(())],
    )
    return pl.pallas_call(
        body,
        grid_spec=grid_spec,
        out_shape=[jax.ShapeDtypeStruct((N_EXPERTS, caps[0], D), f32),
                   jax.ShapeDtypeStruct((N_EXPERTS, caps[1], D), f32)],
        compiler_params=_cparams(("arbitrary", "arbitrary"), 60),
        name="expert_ffn",
    )(idx, h2a, h2b, g3, w_gate, w_up, w_down)


_TT = 128
_KB = 256
_RUN_MAX = _TT + 2 * SUBLANES
_RUN_PIECES = tuple(SUBLANES << i for i in reversed(range((_RUN_MAX // SUBLANES).bit_length())))
_STAGE_ROWS = N_EXPERTS * _RUN_MAX
_STAGE_PIECES = tuple(SUBLANES << i
                      for i in reversed(range((_STAGE_ROWS // SUBLANES).bit_length())))


def _combine_body(offs_ref, ye_hbm, sp_ref, x1_ref, mods_ref, o_ref, stage_ref, acc_ref, sem,
                  *, cap):
    j = pl.program_id(0)
    slot = j & 1

    def wait_rows(total, sl):
        for size in _STAGE_PIECES:
            @pl.when((total & size) != 0)
            def _(size=size):
                pltpu.make_async_copy(ye_hbm.at[pl.ds(0, size), :],
                                      stage_ref.at[sl, pl.ds(0, size), :], sem.at[sl]).wait()

    def tile_copies(jj, sl, start):
        bases, offs = [], []
        base = jnp.int32(0)
        for e in range(N_EXPERTS):
            off = offs_ref[e, jj]
            end = offs_ref[e, jj + 1]
            off8 = (off // SUBLANES) * SUBLANES
            length = jnp.where(end > off, ((end + SUBLANES - 1) // SUBLANES) * SUBLANES - off8, 0)
            bases.append(base)
            offs.append(off8)
            src = e * cap + off8
            dst = base
            for size in _RUN_PIECES if start else ():
                piece = length & size

                @pl.when(piece != 0)
                def _(src=src, dst=dst, size=size):
                    pltpu.make_async_copy(
                        ye_hbm.at[pl.ds(pl.multiple_of(src, SUBLANES), size), :],
                        stage_ref.at[sl, pl.ds(pl.multiple_of(dst, SUBLANES), size), :],
                        sem.at[sl]).start()

                src = src + piece
                dst = dst + piece
            base = base + length
        return bases, offs, base

    @pl.when(j == 0)
    def _():
        stage_ref[...] = jnp.zeros_like(stage_ref)
        tile_copies(0, 0, True)

    @pl.when(j + 1 < pl.num_programs(0))
    def _():
        tile_copies(j + 1, 1 - slot, True)

    bases, offs, total = tile_copies(j, slot, False)
    wait_rows(total, slot)

    sp = sp_ref[...]
    kiota = lax.broadcasted_iota(i32, (1, _KB), 1)
    acc_ref[...] = jnp.zeros_like(acc_ref)

    def kblock(kb, c):
        hit = None
        for e in range(N_EXPERTS):
            local = sp[:, e:e + 1] + (bases[e] - offs[e] - kb * _KB)
            h = local == kiota
            hit = h if hit is None else (hit | h)
        s = jnp.where(hit, 1.0, 0.0).astype(bf16)
        rows = stage_ref[slot, pl.ds(pl.multiple_of(kb * _KB, _KB), _KB), :]
        hi = rows.astype(bf16)
        lo = (rows - hi.astype(f32)).astype(bf16)
        acc_ref[...] += (jnp.dot(s, hi, preferred_element_type=f32)
                         + jnp.dot(s, lo, preferred_element_type=f32))
        return c

    lax.fori_loop(0, (total + _KB - 1) // _KB, kblock, 0)
    o_ref[...] = x1_ref[...] + mods_ref[0][5:6] * acc_ref[...]


def _combine(offs, ye_flat, sp_n, x1, mods, cap, tiles_per_row):
    n = x1.shape[0]
    row = (lambda j: 1 + j // tiles_per_row) if tiles_per_row else (lambda j: 0)
    body = functools.partial(_combine_body, cap=cap)
    grid_spec = pltpu.PrefetchScalarGridSpec(
        num_scalar_prefetch=1,
        grid=(n // _TT,),
        in_specs=[pl.BlockSpec(memory_space=pl.ANY),
                  pl.BlockSpec((_TT, N_EXPERTS), lambda j, offs: (j, 0)),
                  pl.BlockSpec((_TT, D), lambda j, offs: (j, 0)),
                  pl.BlockSpec((1, 6, D), lambda j, offs: (row(j), 0, 0))],
        out_specs=pl.BlockSpec((_TT, D), lambda j, offs: (j, 0)),
        scratch_shapes=[pltpu.VMEM((2, N_EXPERTS * _RUN_MAX, D), f32), pltpu.VMEM((_TT, D), f32),
                        pltpu.SemaphoreType.DMA((2,))],
    )
    return pl.pallas_call(
        body,
        grid_spec=grid_spec,
        out_shape=jax.ShapeDtypeStruct((n, D), f32),
        compiler_params=_cparams(("arbitrary",), 48),
        name="expert_combine",
    )(offs, ye_flat, sp_n, x1, mods)


def _moe(x1s, h2s, affs, mods, w_gate, w_up, w_down, tiles_per_row):
    routes = [_route(aff_t, CAPACITY_FACTOR * aff_t.shape[1] // N_EXPERTS) for aff_t in affs]
    (idx_a, g_a, _, _), (idx_b, g_b, _, _) = routes
    yes = _expert_ffn(idx_a, idx_b, h2s[0], h2s[1], g_a, g_b, w_gate, w_up, w_down)
    outs = []
    for ye, (idx, _, pos, sp), x1, tpr in zip(yes, routes, x1s, tiles_per_row):
        cap = idx.shape[1]
        offs = jnp.concatenate([pos[:, ::_TT], jnp.full((N_EXPERTS, 1), cap, i32)], axis=1)
        outs.append(_combine(offs, ye.reshape(N_EXPERTS * cap, D), sp.T, x1, mods, cap, tpr))
    return outs


def kernel(x_prompt, x_sample, cache_k, cache_v, state_fwd, state_bwd, c, c_ctx, w_mod, b_mod,
           norm_mix, w_in, q_norm, k_norm, rpb, conv_w, conv_b, lru_w_a, lru_b_a, lru_w_x,
           lru_b_x, lru_lambda, w_out, norm_ffn, w_router, w_gate, w_up, w_down):
    B, S, _ = x_prompt.shape
    DB, T, _ = x_sample.shape
    l = 0

    cvecs = jnp.concatenate([c_ctx[None, :], c, jnp.zeros((SUBLANES - 1 - DB, D), f32)], axis=0)
    mods = _adaln_mods(cvecs, w_mod[l], b_mod[l][None, :]).reshape(SUBLANES, 6, D)

    heads_per_tile = _TN_IN // HEAD_DIM
    qn_t = jnp.tile(q_norm[l], heads_per_tile)[None, :]
    kn_t = jnp.tile(k_norm[l], heads_per_tile)[None, :]
    gmat = (jnp.kron(jnp.eye(heads_per_tile, dtype=f32), jnp.ones((HEAD_DIM, HEAD_DIM), f32))
            / HEAD_DIM).astype(bf16)
    nmix = norm_mix[l][None, :]
    nffn = norm_ffn[l][None, :]
    w_in_b = w_in[l].astype(bf16)
    w_out_b = w_out[l].astype(bf16)
    w_router_t = w_router[l].T.astype(bf16)
    wcat = _lru_gate_weights(lru_w_a[l], lru_w_x[l])
    cb = conv_b[l][None, :]

    xp = x_prompt.reshape(B * S, D)
    xs = x_sample.reshape(DB * T, D)

    proj_c = _inproj(xp, mods, nmix, w_in_b, qn_t, kn_t, gmat, per_tile_rows=False)
    att_c, new_k, new_v = _ctx_attention(proj_c, B, S)
    zeros_c = jnp.zeros((B, LRU_W), f32)
    lru_c, htf, htb = _rglru(proj_c, wcat, conv_w[l], cb, lru_b_a[l], lru_b_x[l], lru_lambda[l],
                             zeros_c, zeros_c, T=S, nb=SUBLANES)
    x1_c, h2_c, aff_c = _outproj(att_c, lru_c, w_out_b, xp, mods, nffn, w_router_t, 0)

    nct = LRU_W // _LRU_TC

    def slabs(h0):
        return (h0.reshape(DB, nct, _LRU_HALVES, LANES).transpose(2, 0, 1, 3)
                .reshape(_LRU_HALVES * DB, nct * LANES))

    proj_l = _inproj(xs, mods, nmix, w_in_b, qn_t, kn_t, gmat, per_tile_rows=True)
    att_l = _lat_attention(proj_l, cache_k, cache_v, _pad_rpb(rpb[l]), l, DB, T)
    h0f_l = slabs(state_fwd[:, l])
    h0b_l = slabs(state_bwd[:, l])
    lru_l, _, _ = _rglru(proj_l, wcat, conv_w[l], cb, lru_b_a[l], lru_b_x[l], lru_lambda[l],
                         h0f_l, h0b_l, T=T, nb=DB)
    x1_l, h2_l, aff_l = _outproj(att_l, lru_l, w_out_b, xs, mods, nffn, w_router_t, T // 512)

    y_c, y_l = _moe((x1_c, x1_l), (h2_c, h2_l), (aff_c, aff_l), mods,
                    w_gate[l], w_up[l], w_down[l], (0, T // _TT))

    return (y_c.reshape(B, S, D), y_l.reshape(DB, T, D), new_k, new_v,
            htf[:, None, :], htb[:, None, :])
```

```python
import functools

import jax
import jax.numpy as jnp
from jax import lax
from jax.experimental import pallas as pl
from jax.experimental.pallas import tpu as pltpu

f32 = jnp.float32
bf16 = jnp.bfloat16
i32 = jnp.int32

D = 2048
N_HEADS = 16
HEAD_DIM = 64
ATT_W = N_HEADS * HEAD_DIM
LRU_W = D - ATT_W
LRU_BLOCK = 64
LRU_C = 8.0
GRID_W = 64
WIN_H = 8
WIN_W = 16
N_EXPERTS = 16
CAPACITY_FACTOR = 2
D_FF = 2048
EPS = 1e-6
NEG_INF = -1e30
IN_COLS = 3 * ATT_W + 2 * LRU_W

LANES = 128
SUBLANES = 8
MXU_DIM = 256
UNSELECTED = -(1 << 30)

_MIB = 1024 * 1024


def _cparams(sem, vmem_mib):
    return pltpu.CompilerParams(dimension_semantics=sem, vmem_limit_bytes=vmem_mib * _MIB)


def _mods_body(c_ref, w_ref, b_ref, o_ref):
    c = c_ref[...]
    s = (c * jax.nn.sigmoid(c)).astype(bf16)
    o_ref[...] = jnp.dot(s, w_ref[...].astype(bf16), preferred_element_type=f32) + b_ref[...]


def _adaln_mods(cvecs, w_mod, b_mod):
    tn = 1024
    return pl.pallas_call(
        _mods_body,
        grid=(6 * D // tn,),
        in_specs=[pl.BlockSpec((SUBLANES, D), lambda j: (0, 0)),
                  pl.BlockSpec((D, tn), lambda j: (0, j)),
                  pl.BlockSpec((1, tn), lambda j: (0, j))],
        out_specs=pl.BlockSpec((SUBLANES, tn), lambda j: (0, j)),
        out_shape=jax.ShapeDtypeStruct((SUBLANES, 6 * D), f32),
        compiler_params=_cparams(("arbitrary",), 40),
        name="adaln_mods",
    )(cvecs, w_mod, b_mod)


_TN_IN = 512
_NORM_ROWS = 16
_NORM_SLICES = 8


def _inproj_body(x0_ref, xn_ref, mods0_ref, modsn_ref, nw_ref, w_ref, qn_ref, kn_ref, g_ref,
                 o_ref, ha_ref, hb_ref):
    i = pl.program_id(0)
    j = pl.program_id(1)
    tm = ha_ref.shape[0]
    nw = nw_ref[...]
    n_qk_tiles = 2 * ATT_W // _TN_IN

    def norm_rows(x_ref, mods_ref, h_ref, r0):
        m = mods_ref[0]
        rows = pl.ds(pl.multiple_of(r0, _NORM_ROWS), _NORM_ROWS)
        x = x_ref[rows, :]
        ms = jnp.mean(x * x, axis=-1, keepdims=True)
        y = x * lax.rsqrt(ms + EPS) * nw
        h_ref[rows, :] = (y * (1.0 + m[1:2]) + m[0:1]).astype(bf16)

    @pl.when((i == 0) & (j == 0))
    def _():
        def first(c, carry):
            norm_rows(x0_ref, mods0_ref, ha_ref, c * _NORM_ROWS)
            return carry

        lax.fori_loop(0, tm // _NORM_ROWS, first, 0, unroll=4)

    def column_step(cur_ref, nxt_ref):
        slice_rows = tm // _NORM_SLICES
        r_slice = jnp.minimum(j, _NORM_SLICES - 1) * slice_rows
        for c in range(slice_rows // _NORM_ROWS):
            norm_rows(xn_ref, modsn_ref, nxt_ref, r_slice + c * _NORM_ROWS)

        acc = jnp.dot(cur_ref[...], w_ref[...].astype(bf16), preferred_element_type=f32)

        @pl.when(j < n_qk_tiles)
        def _():
            sq = acc * acc
            hi = sq.astype(bf16)
            lo = (sq - hi.astype(f32)).astype(bf16)
            g = g_ref[...]
            ms = (jnp.dot(hi, g, preferred_element_type=f32)
                  + jnp.dot(lo, g, preferred_element_type=f32))
            head_w = jnp.where(j < n_qk_tiles // 2, qn_ref[...], kn_ref[...])
            o_ref[...] = acc * lax.rsqrt(ms + EPS) * head_w

        @pl.when(j >= n_qk_tiles)
        def _():
            o_ref[...] = acc

    @pl.when((i & 1) == 0)
    def _():
        column_step(ha_ref, hb_ref)

    @pl.when((i & 1) == 1)
    def _():
        column_step(hb_ref, ha_ref)


def _inproj(x2d, mods, norm_w, w_in, qn_t, kn_t, gmat, per_tile_rows):
    n = x2d.shape[0]
    tm = 1024
    last = n // tm - 1
    row = (lambda i: 1 + i) if per_tile_rows else (lambda i: 0)
    nxt = lambda i: jnp.minimum(i + 1, last)
    return pl.pallas_call(
        _inproj_body,
        grid=(n // tm, IN_COLS // _TN_IN),
        in_specs=[pl.BlockSpec((tm, D), lambda i, j: (0, 0)),
                  pl.BlockSpec((tm, D), lambda i, j: (nxt(i), 0)),
                  pl.BlockSpec((1, 6, D), lambda i, j: (row(0), 0, 0)),
                  pl.BlockSpec((1, 6, D), lambda i, j: (row(nxt(i)), 0, 0)),
                  pl.BlockSpec((1, D), lambda i, j: (0, 0)),
                  pl.BlockSpec((D, _TN_IN), lambda i, j: (0, j)),
                  pl.BlockSpec((1, _TN_IN), lambda i, j: (0, 0)),
                  pl.BlockSpec((1, _TN_IN), lambda i, j: (0, 0)),
                  pl.BlockSpec((_TN_IN, _TN_IN), lambda i, j: (0, 0))],
        out_specs=pl.BlockSpec((tm, _TN_IN), lambda i, j: (i, j)),
        out_shape=jax.ShapeDtypeStruct((n, IN_COLS), f32),
        scratch_shapes=[pltpu.VMEM((tm, D), bf16), pltpu.VMEM((tm, D), bf16)],
        compiler_params=_cparams(("arbitrary", "arbitrary"), 56),
        name="in_projection",
    )(x2d, x2d, mods, mods, norm_w, w_in, qn_t, kn_t, gmat)


def _softmax_pv(s_list, v_list):
    m = s_list[0].max(axis=-1, keepdims=True)
    for s in s_list[1:]:
        m = jnp.maximum(m, s.max(axis=-1, keepdims=True))
    o = None
    for s, v in zip(s_list, v_list):
        os_ = jnp.dot(jnp.exp(s - m).astype(bf16), v, preferred_element_type=f32)
        o = os_ if o is None else o + os_
    return o


def _normalise_pair(o0, o1, lane):
    left = lane < HEAD_DIM
    den = pltpu.roll(jnp.where(left, o1, o0), HEAD_DIM, 1)
    return jnp.where(left, o0, o1) / den


def _nt_dot(a, b):
    return lax.dot_general(a, b, (((1,), (1,)), ((), ())), preferred_element_type=f32)


_QK_SCALE = HEAD_DIM ** -0.5


def _ctx_attn_body(q_ref, k_ref, v_ref, o_ref, nk_ref, nv_ref):
    lane = lax.broadcasted_iota(i32, (1, LANES), 1)
    for p in range(ATT_W // LANES):
        cols = slice(p * LANES, (p + 1) * LANES)
        q = q_ref[:, cols] * _QK_SCALE
        kf = k_ref[:, cols]
        k = kf.astype(bf16)
        v = v_ref[:, cols]
        nk_ref[0, 0, 2 * p] = kf[:, :HEAD_DIM]
        nk_ref[0, 0, 2 * p + 1] = pltpu.roll(kf, HEAD_DIM, 1)[:, :HEAD_DIM]
        nv_ref[0, 0, 2 * p] = v[:, :HEAD_DIM]
        nv_ref[0, 0, 2 * p + 1] = pltpu.roll(v, HEAD_DIM, 1)[:, :HEAD_DIM]
        heads = []
        for hh in range(2):
            hm = (lane >= hh * HEAD_DIM) & (lane < (hh + 1) * HEAD_DIM)
            qh = jnp.where(hm, q, 0.0).astype(bf16)
            vh = jnp.where(hm, v, 1.0).astype(bf16)
            heads.append(_softmax_pv([_nt_dot(qh, k)], [vh]))
        o_ref[:, cols] = _normalise_pair(heads[0], heads[1], lane).astype(bf16)


def _ctx_attention(proj, batch, seq):
    n = proj.shape[0]
    cache = pl.BlockSpec((1, 1, N_HEADS, seq, HEAD_DIM), lambda b: (b, 0, 0, 0, 0))
    cache_shape = jax.ShapeDtypeStruct((batch, 1, N_HEADS, seq, HEAD_DIM), f32)
    return pl.pallas_call(
        _ctx_attn_body,
        grid=(batch,),
        in_specs=[pl.BlockSpec((seq, ATT_W), lambda b: (b, 0)),
                  pl.BlockSpec((seq, ATT_W), lambda b: (b, 1)),
                  pl.BlockSpec((seq, ATT_W), lambda b: (b, 2))],
        out_specs=[pl.BlockSpec((seq, ATT_W), lambda b: (b, 0)), cache, cache],
        out_shape=[jax.ShapeDtypeStruct((n, ATT_W), bf16), cache_shape, cache_shape],
        compiler_params=_cparams(("arbitrary",), 32),
        name="context_attention",
    )(proj, proj, proj)


_NA_ROWS = 16
_NA_QROWS = 8
_NA_KROWS = 12


def _pad_rpb(rpb):
    nd, nc = 2 * WIN_H - 1, 2 * WIN_W - 1
    return jnp.pad(rpb.astype(f32), ((0, 0), (0, 2 * SUBLANES - nd), (0, LANES - nc)))


def _build_na_bias(rpb_ref, bias_ref):
    qc = lax.broadcasted_iota(i32, (GRID_W, LANES), 0)
    lane = lax.broadcasted_iota(i32, (GRID_W, LANES), 1)
    kc = lane & (GRID_W - 1)
    cs = jnp.clip(qc - WIN_W // 2, 0, GRID_W - WIN_W)
    col_ok = (kc >= cs) & (kc < cs + WIN_W)
    left = lane < GRID_W
    neg = jnp.full((GRID_W, LANES), NEG_INF, f32)
    for hh in range(2):
        tbl = rpb_ref[hh]
        tiles = {}

        def toeplitz(d, odd):
            if (d, odd) not in tiles:
                row = jnp.broadcast_to(tbl[d:d + 1, :], (GRID_W, LANES))
                shift = (LANES - (WIN_W - 1) + odd * GRID_W) % LANES
                t = pltpu.roll(row, shift, 1, stride=1, stride_axis=0)
                tiles[(d, odd)] = jnp.where(col_ok, t, NEG_INF)
            return tiles[(d, odd)]

        for j in range(2):
            for rq in range(_NA_QROWS):
                rq_abs = j * _NA_QROWS + rq
                rs = min(max(rq_abs - WIN_H // 2, 0), _NA_ROWS - WIN_H)
                for kp in range(_NA_KROWS // 2):
                    parts = []
                    for odd in range(2):
                        rk_abs = j * (_NA_ROWS - _NA_KROWS) + 2 * kp + odd
                        if rs <= rk_abs < rs + WIN_H:
                            parts.append(toeplitz(rk_abs - rq_abs + WIN_H - 1, odd))
                        else:
                            parts.append(neg)
                    bias_ref[hh, j, rq * GRID_W:(rq + 1) * GRID_W,
                             kp * LANES:(kp + 1) * LANES] = jnp.where(left, parts[0], parts[1])


def _lat_attn_body(q_ref, k_ref, v_ref, ck_ref, cv_ref, rpb_ref, o_ref, bias_ref):
    @pl.when(pl.program_id(1) == 0)
    def _():
        _build_na_bias(rpb_ref, bias_ref)

    lane = lax.broadcasted_iota(i32, (1, LANES), 1)
    nq = _NA_QROWS * GRID_W
    nk = _NA_KROWS * GRID_W
    kstep = (_NA_ROWS - _NA_KROWS) * GRID_W
    src = lax.broadcasted_iota(i32, (HEAD_DIM, LANES), 0)
    dst = lax.broadcasted_iota(i32, (HEAD_DIM, LANES), 1)
    ck = None
    cv_h = []
    for hh in range(2):
        place = jnp.where(dst == src + hh * HEAD_DIM, 1.0, 0.0).astype(bf16)
        part = jnp.dot(ck_ref[0, 0, hh].astype(bf16), place, preferred_element_type=f32)
        ck = part if ck is None else ck + part
        placed = jnp.dot(cv_ref[0, 0, hh].astype(bf16), place, preferred_element_type=f32)
        own = (lane >= hh * HEAD_DIM) & (lane < (hh + 1) * HEAD_DIM)
        cv_h.append(jnp.where(own, placed, 1.0).astype(bf16))
    ck = ck.astype(bf16)
    for j in range(2):
        q = q_ref[j * nq:(j + 1) * nq, :] * _QK_SCALE
        k = k_ref[j * kstep:j * kstep + nk, :].astype(bf16)
        v = v_ref[j * kstep:j * kstep + nk, :]
        heads = []
        for hh in range(2):
            hm = (lane >= hh * HEAD_DIM) & (lane < (hh + 1) * HEAD_DIM)
            qh = jnp.where(hm, q, 0.0).astype(bf16)
            s_win = _nt_dot(qh, k) + bias_ref[hh, j]
            s_ctx = _nt_dot(qh, ck)
            vh = jnp.where(hm, v, 1.0).astype(bf16)
            heads.append(_softmax_pv([s_win, s_ctx], [vh, cv_h[hh]]))
        o_ref[j * nq:(j + 1) * nq, :] = _normalise_pair(heads[0], heads[1], lane).astype(bf16)


def _lat_attention(proj, ck, cv, rpb_p, layer, batch, seq):
    n = proj.shape[0]
    past = ck.shape[3]
    cache = pl.BlockSpec((1, 1, 2, past, HEAD_DIM), lambda p, b: (b, layer, p, 0, 0))
    npair = ATT_W // LANES
    nq = _NA_QROWS * GRID_W
    nk = _NA_KROWS * GRID_W
    return pl.pallas_call(
        _lat_attn_body,
        grid=(npair, batch),
        in_specs=[pl.BlockSpec((seq, LANES), lambda p, b: (b, p)),
                  pl.BlockSpec((seq, LANES), lambda p, b: (b, npair + p)),
                  pl.BlockSpec((seq, LANES), lambda p, b: (b, 2 * npair + p)),
                  cache, cache,
                  pl.BlockSpec((2, 2 * SUBLANES, LANES), lambda p, b: (p, 0, 0))],
        out_specs=pl.BlockSpec((seq, LANES), lambda p, b: (b, p)),
        out_shape=jax.ShapeDtypeStruct((n, ATT_W), bf16),
        scratch_shapes=[pltpu.VMEM((2, 2, nq, nk), f32)],
        compiler_params=_cparams(("arbitrary", "arbitrary"), 48),
        name="neighbourhood_attention",
    )(proj, proj, proj, ck, cv, rpb_p)


_LRU_TC = 256
_GATE_ROWS = 256
_LRU_HALVES = _LRU_TC // LANES


def _gelu_tanh(x):
    c = 0.7978845608028654
    return x * (0.5 * (1.0 + jnp.tanh(c * (x + 0.044715 * (x * x * x)))))


def _lru_body(xr_ref, gr_ref, w_ref, cw_ref, cb_ref, ba_ref, bx_ref, lam_ref, h0f_ref, h0b_ref,
              y_ref, htf_ref, htb_ref, xc_ref, af_ref, uf_ref, ab_ref, ub_ref, *, T, nb):
    nslab = _LRU_HALVES * nb // SUBLANES
    R = nb * T
    x = xr_ref[...]
    t = lax.broadcasted_iota(i32, (R, 1), 0) & (T - 1)
    cw = cw_ref[...]
    xc = jnp.where(t >= 2, pltpu.roll(x, 2, 0), 0.0) * cw[0:1]
    xc = xc + jnp.where(t >= 1, pltpu.roll(x, 1, 0), 0.0) * cw[1:2]
    xc = xc + x * cw[2:3]
    xc = xc + jnp.where(t < T - 1, pltpu.roll(x, R - 1, 0), 0.0) * cw[3:4]
    xc_ref[...] = xc + cb_ref[...]

    nl = -lam_ref[...]
    sp = jnp.maximum(nl, 0.0) + jnp.log1p(jnp.exp(-jnp.abs(nl)))
    half_c_sp = (0.5 * LRU_C) * sp
    w = w_ref[0]
    dirs = ((af_ref, uf_ref), (ab_ref, ub_ref))
    for b in range(nb):
        for ci in range(T // _GATE_ROWS):
            r0 = b * T + ci * _GATE_ROWS
            xcc = xc_ref[r0:r0 + _GATE_ROWS, :]
            half_x = 0.5 * xcc
            z = jnp.dot(xcc.astype(bf16), w, preferred_element_type=f32)
            for d, (a_ref, u_ref) in enumerate(dirs):
                c0 = d * 2 * _LRU_TC
                tr = jnp.tanh(0.5 * (z[:, c0:c0 + _LRU_TC] + ba_ref[d:d + 1]))
                tg = jnp.tanh(0.5 * (z[:, c0 + _LRU_TC:c0 + 2 * _LRU_TC] + bx_ref[d:d + 1]))
                neg_log_a = (1.0 + tr) * half_c_sp[d:d + 1]
                a = jnp.exp(-neg_log_a)
                one_m_a2 = jnp.tanh(neg_log_a) * (1.0 + a * a)
                root = jnp.where(one_m_a2 > 0.0, one_m_a2 * lax.rsqrt(one_m_a2), 0.0)
                u = root * ((1.0 + tg) * half_x)
                for hf in range(_LRU_HALVES):
                    s, sub = divmod(hf * nb + b, SUBLANES)
                    rows = pl.ds(ci * _GATE_ROWS * SUBLANES + sub, _GATE_ROWS, stride=SUBLANES)
                    a_ref[s, rows, :] = a[:, hf * LANES:(hf + 1) * LANES]
                    u_ref[s, rows, :] = u[:, hf * LANES:(hf + 1) * LANES]

    def step(i, carry):
        rf = pl.ds(pl.multiple_of(i * SUBLANES, SUBLANES), SUBLANES)
        rb = pl.ds(pl.multiple_of((T - 1 - i) * SUBLANES, SUBLANES), SUBLANES)
        new = []
        for s in range(nslab):
            hf_ = af_ref[s, rf, :] * carry[2 * s] + uf_ref[s, rf, :]
            uf_ref[s, rf, :] = hf_
            hb_ = ab_ref[s, rb, :] * carry[2 * s + 1] + ub_ref[s, rb, :]
            ub_ref[s, rb, :] = hb_
            new += [hf_, hb_]
        return tuple(new)

    init = []
    for s in range(nslab):
        init += [h0f_ref[:, s * LANES:(s + 1) * LANES], h0b_ref[:, s * LANES:(s + 1) * LANES]]
    fin = lax.fori_loop(0, T, step, tuple(init), unroll=8)
    for s in range(nslab):
        htf_ref[:, s * LANES:(s + 1) * LANES] = fin[2 * s]
        htb_ref[:, s * LANES:(s + 1) * LANES] = fin[2 * s + 1]

    for b in range(nb):
        for hf in range(_LRU_HALVES):
            s, sub = divmod(hf * nb + b, SUBLANES)
            rows = pl.ds(sub, T, stride=SUBLANES)
            hs = uf_ref[s, rows, :] + ub_ref[s, rows, :]
            g = gr_ref[b * T:(b + 1) * T, hf * LANES:(hf + 1) * LANES]
            y_ref[b * T:(b + 1) * T, hf * LANES:(hf + 1) * LANES] = (
                hs * _gelu_tanh(g)).astype(bf16)


def _rglru(proj, wcat, conv_w, conv_b, b_a, b_x, lam, h0f, h0b, T, nb):
    n = proj.shape[0]
    nslab = _LRU_HALVES * nb // SUBLANES
    W = nslab * LANES
    R = nb * T
    nct = LRU_W // _LRU_TC
    xr0 = 3 * ATT_W // _LRU_TC
    gr0 = xr0 + nct
    chan = lambda k: pl.BlockSpec((k, _LRU_TC), lambda g, j: (0, j))
    slab = pl.BlockSpec((SUBLANES, W), lambda g, j: (g, j))
    body = functools.partial(_lru_body, T=T, nb=nb)
    return pl.pallas_call(
        body,
        grid=(n // R, nct),
        in_specs=[pl.BlockSpec((R, _LRU_TC), lambda g, j: (g, xr0 + j)),
                  pl.BlockSpec((R, _LRU_TC), lambda g, j: (g, gr0 + j)),
                  pl.BlockSpec((1, _LRU_TC, 4 * _LRU_TC), lambda g, j: (j, 0, 0)),
                  chan(4), chan(1), chan(2), chan(2), chan(2), slab, slab],
        out_specs=[pl.BlockSpec((R, _LRU_TC), lambda g, j: (g, j)), slab, slab],
        out_shape=[jax.ShapeDtypeStruct((n, LRU_W), bf16),
                   jax.ShapeDtypeStruct(h0f.shape, f32),
                   jax.ShapeDtypeStruct(h0b.shape, f32)],
        scratch_shapes=[pltpu.VMEM((R, _LRU_TC), f32)]
        + [pltpu.VMEM((nslab, T * SUBLANES, LANES), f32) for _ in range(4)],
        compiler_params=_cparams(("arbitrary", "arbitrary"), 56),
        name="rg_lru",
    )(proj, proj, wcat, conv_w, conv_b, b_a, b_x, lam, h0f, h0b)


def _lru_gate_weights(w_a, w_x):
    per_tile = _LRU_TC // LRU_BLOCK
    nct = LRU_W // _LRU_TC
    eye = jnp.eye(per_tile, dtype=f32)

    def bd(w):
        w = w.reshape(nct, per_tile, LRU_BLOCK, LRU_BLOCK)
        m = w[:, :, :, None, :] * eye[None, :, None, :, None]
        return m.reshape(nct, _LRU_TC, _LRU_TC)

    return jnp.concatenate([bd(w_a[0]), bd(w_x[0]), bd(w_a[1]), bd(w_x[1])], axis=-1).astype(bf16)


def _outproj_body(att_ref, lru_ref, w_ref, x_ref, mods_ref, nw_ref, wr_ref,
                  x1_ref, h2_ref, aff_ref):
    mix = (jnp.dot(att_ref[...], w_ref[0:ATT_W, :], preferred_element_type=f32)
           + jnp.dot(lru_ref[...], w_ref[ATT_W:D, :], preferred_element_type=f32))
    m = mods_ref[0]
    x1 = x_ref[...] + m[2:3] * mix
    x1_ref[...] = x1
    ms = jnp.mean(x1 * x1, axis=-1, keepdims=True)
    h2 = (x1 * lax.rsqrt(ms + EPS) * nw_ref[...]) * (1.0 + m[4:5]) + m[3:4]
    h2_ref[...] = h2
    logits = _nt_dot(wr_ref[...], h2.astype(bf16))
    e = jnp.exp(logits - logits.max(axis=0, keepdims=True))
    aff_ref[...] = e / e.sum(axis=0, keepdims=True)


def _outproj(att, lru, w_out_b, x2d, mods, norm_w, w_router_t, tiles_per_row):
    n = x2d.shape[0]
    tm = 512
    row = (lambda i: 1 + i // tiles_per_row) if tiles_per_row else (lambda i: 0)
    return pl.pallas_call(
        _outproj_body,
        grid=(n // tm,),
        in_specs=[pl.BlockSpec((tm, ATT_W), lambda i: (i, 0)),
                  pl.BlockSpec((tm, LRU_W), lambda i: (i, 0)),
                  pl.BlockSpec((D, D), lambda i: (0, 0)),
                  pl.BlockSpec((tm, D), lambda i: (i, 0)),
                  pl.BlockSpec((1, 6, D), lambda i: (row(i), 0, 0)),
                  pl.BlockSpec((1, D), lambda i: (0, 0)),
                  pl.BlockSpec((N_EXPERTS, D), lambda i: (0, 0))],
        out_specs=[pl.BlockSpec((tm, D), lambda i: (i, 0)),
                   pl.BlockSpec((tm, D), lambda i: (i, 0)),
                   pl.BlockSpec((N_EXPERTS, tm), lambda i: (0, i))],
        out_shape=[jax.ShapeDtypeStruct((n, D), f32),
                   jax.ShapeDtypeStruct((n, D), f32),
                   jax.ShapeDtypeStruct((N_EXPERTS, n), f32)],
        compiler_params=_cparams(("arbitrary",), 52),
        name="out_projection_router",
    )(att, lru, w_out_b, x2d, mods, norm_w, w_router_t)


def _prefix_sum_lanes(x, lane, n):
    s = 1
    while s < n:
        x = x + jnp.where(lane >= s, pltpu.roll(x, s, 1), 0)
        s *= 2
    return x


def _route_body(aff_ref, idx_ref, g_ref, pos_ref, sp_ref, sel_ref, rem_ref, *, n, cap):
    aff = aff_ref[...]
    lane = lax.broadcasted_iota(i32, (N_EXPERTS, n), 1)

    def count(mask):
        return jnp.sum(jnp.where(mask, 1.0, 0.0), axis=1, keepdims=True).astype(i32)

    def as_value(bits):
        return pltpu.bitcast(jnp.broadcast_to(bits, (N_EXPERTS, LANES)), f32)[:, :1]

    def search(i, thr):
        cand = thr | jnp.left_shift(jnp.int32(1), 30 - i)
        return jnp.where(count(aff >= as_value(cand)) >= cap, cand, thr)

    thr = lax.fori_loop(0, 31, search, jnp.zeros((N_EXPERTS, 1), i32))
    sure = aff >= as_value(thr + 1)
    sel_ref[...] = sure.astype(i32)
    rem_ref[...] = ((aff >= as_value(thr)) & jnp.logical_not(sure)).astype(i32)

    def more(need):
        return jnp.max(need) > 0

    def take_largest(need):
        rem = rem_ref[...] > 0
        top = jnp.max(jnp.where(rem, aff, -1.0), axis=1, keepdims=True)
        is_top = rem & (aff == top) & (need > 0)
        rank = _prefix_sum_lanes(is_top.astype(i32), lane, n)
        pick = is_top & ((count(is_top) <= need) | (rank <= need))
        sel_ref[...] = sel_ref[...] | pick.astype(i32)
        rem_ref[...] = (rem & jnp.logical_not(is_top)).astype(i32)
        return need - count(pick)

    lax.while_loop(more, take_largest, cap - count(sure))
    sel_i = sel_ref[...]
    sel = sel_i > 0
    pos = _prefix_sum_lanes(sel_i, lane, n) - sel_i
    pos_ref[...] = pos
    sp_ref[...] = jnp.where(sel, pos, UNSELECTED)

    key = jnp.where(sel, lane - pos, -1)
    idx = lane
    g = aff
    b = 0
    while (1 << b) < n:
        s = 1 << b
        k_sh = pltpu.roll(key, n - s, 1)
        take = (k_sh >= 0) & ((jnp.right_shift(k_sh, b) & 1) == 1)
        stay = (key >= 0) & ((jnp.right_shift(key, b) & 1) == 0)
        idx = jnp.where(take, pltpu.roll(idx, n - s, 1), idx)
        g = jnp.where(take, pltpu.roll(g, n - s, 1), g)
        key = jnp.where(take, k_sh, jnp.where(stay, key, -1))
        b += 1
    idx_ref[...] = idx[:, :cap]
    g_ref[...] = g[:, :cap]


def _route(aff_t, cap):
    n = aff_t.shape[1]
    body = functools.partial(_route_body, n=n, cap=cap)
    full = lambda shape: pl.BlockSpec(shape, lambda: (0, 0))
    return pl.pallas_call(
        body,
        in_specs=[full((N_EXPERTS, n))],
        out_specs=[full((N_EXPERTS, cap)), full((N_EXPERTS, cap)),
                   full((N_EXPERTS, n)), full((N_EXPERTS, n))],
        out_shape=[jax.ShapeDtypeStruct((N_EXPERTS, cap), i32),
                   jax.ShapeDtypeStruct((N_EXPERTS, cap), f32),
                   jax.ShapeDtypeStruct((N_EXPERTS, n), i32),
                   jax.ShapeDtypeStruct((N_EXPERTS, n), i32)],
        scratch_shapes=[pltpu.VMEM((N_EXPERTS, n), i32), pltpu.VMEM((N_EXPERTS, n), i32)],
        compiler_params=pltpu.CompilerParams(vmem_limit_bytes=32 * _MIB),
        name="expert_choice_routing",
    )(aff_t)


_TF = 256
_TN_DOWN = 512


def _expert_body(idx_ref, h2a_hbm, h2b_hbm, g_ref, wg_ref, wu_ref, wd_ref, oa_ref, ob_ref,
                 xe_ref, xb_ref, he_ref, sem, *, caps, na, nd):
    e = pl.program_id(0)
    s = pl.program_id(1)
    srcs = (h2a_hbm, h2b_hbm)
    starts = (0, caps[0])
    rows_total = caps[0] + caps[1]

    def row_copy(grp, tok, r):
        return pltpu.make_async_copy(srcs[grp].at[pl.ds(tok, 1), :], xe_ref.at[pl.ds(r, 1), :], sem)

    def wait_all_rows():
        for grp in range(2):
            pltpu.make_async_copy(srcs[grp].at[pl.ds(0, caps[grp]), :],
                                  xe_ref.at[pl.ds(starts[grp], caps[grp]), :], sem).wait()

    @pl.when((e == 0) & (s == 0))
    def _():
        for grp in range(2):
            def issue(r, c, grp=grp):
                row_copy(grp, idx_ref[starts[grp] + r], starts[grp] + r).start()
                return c

            lax.fori_loop(0, caps[grp], issue, 0, unroll=8)

    @pl.when(s == 0)
    def _():
        wait_all_rows()
        xb_ref[...] = xe_ref[...].astype(bf16)

    @pl.when(s < na)
    def _():
        wg = wg_ref[0].astype(bf16)
        wu = wu_ref[0].astype(bf16)
        half = rows_total // 2
        for rows in (slice(0, half), slice(half, rows_total)):
            xb = xb_ref[rows, :]
            gp = jnp.dot(xb, wg, preferred_element_type=f32)
            up = jnp.dot(xb, wu, preferred_element_type=f32)
            he_ref[s, rows, :] = ((gp * jax.nn.sigmoid(gp)) * up).astype(bf16)

    @pl.when(s >= na)
    def _():
        base = jnp.minimum(e + 1, N_EXPERTS - 1) * rows_total
        for k in range(nd):
            @pl.when(s - na == k)
            def _(k=k):
                for grp in range(2):
                    per = caps[grp] // nd
                    for r in range(starts[grp] + k * per, starts[grp] + (k + 1) * per):
                        row_copy(grp, idx_ref[base + r], r).start()

        wd = wd_ref[0].astype(bf16)
        acc = None
        for k in range(na):
            part = jnp.dot(he_ref[k], wd[k * _TF:(k + 1) * _TF, :], preferred_element_type=f32)
            acc = part if acc is None else acc + part
        scaled = acc * g_ref[0]
        oa_ref[0] = scaled[:caps[0]]
        ob_ref[0] = scaled[caps[0]:]

    @pl.when((e == N_EXPERTS - 1) & (s == na + nd - 1))
    def _():
        wait_all_rows()


def _expert_ffn(idx_a, idx_b, h2a, h2b, g_a, g_b, w_gate, w_up, w_down):
    caps = (idx_a.shape[1], idx_b.shape[1])
    rows = caps[0] + caps[1]
    na = D_FF // _TF
    nd = D // _TN_DOWN
    idx = jnp.concatenate([idx_a, idx_b], axis=1).reshape(N_EXPERTS * rows)
    g3 = jnp.concatenate([g_a, g_b], axis=1)[:, :, None]
    body = functools.partial(_expert_body, caps=caps, na=na, nd=nd)
    up_tile = lambda e, s, idx: (e, 0, jnp.minimum(s, na - 1))
    down_tile = lambda e, s, idx: (e, 0, jnp.maximum(s - na, 0))
    grid_spec = pltpu.PrefetchScalarGridSpec(
        num_scalar_prefetch=1,
        grid=(N_EXPERTS, na + nd),
        in_specs=[pl.BlockSpec(memory_space=pl.ANY),
                  pl.BlockSpec(memory_space=pl.ANY),
                  pl.BlockSpec((1, rows, 1), lambda e, s, idx: (e, 0, 0)),
                  pl.BlockSpec((1, D, _TF), up_tile),
                  pl.BlockSpec((1, D, _TF), up_tile),
                  pl.BlockSpec((1, D_FF, _TN_DOWN), down_tile)],
        out_specs=[pl.BlockSpec((1, caps[0], _TN_DOWN), down_tile),
                   pl.BlockSpec((1, caps[1], _TN_DOWN), down_tile)],
        scratch_shapes=[pltpu.VMEM((rows, D), f32), pltpu.VMEM((rows, D), bf16),
                        pltpu.VMEM((na, rows, _TF), bf16), pltpu.SemaphoreType.DMA(())],
    )
    return pl.pallas_call(
        body,
        grid_spec=grid_spec,
        out_shape=[jax.ShapeDtypeStruct((N_EXPERTS, caps[0], D), f32),
                   jax.ShapeDtypeStruct((N_EXPERTS, caps[1], D), f32)],
        compiler_params=_cparams(("arbitrary", "arbitrary"), 60),
        name="expert_ffn",
    )(idx, h2a, h2b, g3, w_gate, w_up, w_down)


_TT = 128
_KB = 256
_RUN_MAX = _TT + 2 * SUBLANES
_RUN_PIECES = tuple(SUBLANES << i for i in reversed(range((_RUN_MAX // SUBLANES).bit_length())))
_STAGE_ROWS = N_EXPERTS * _RUN_MAX
_STAGE_PIECES = tuple(SUBLANES << i
                      for i in reversed(range((_STAGE_ROWS // SUBLANES).bit_length())))


def _combine_body(offs_ref, ye_hbm, sp_ref, x1_ref, mods_ref, o_ref, stage_ref, acc_ref, sem,
                  *, cap):
    j = pl.program_id(0)
    slot = j & 1

    def wait_rows(total, sl):
        for size in _STAGE_PIECES:
            @pl.when((total & size) != 0)
            def _(size=size):
                pltpu.make_async_copy(ye_hbm.at[pl.ds(0, size), :],
                                      stage_ref.at[sl, pl.ds(0, size), :], sem.at[sl]).wait()

    def tile_copies(jj, sl, start):
        bases, offs = [], []
        base = jnp.int32(0)
        for e in range(N_EXPERTS):
            off = offs_ref[e, jj]
            end = offs_ref[e, jj + 1]
            off8 = (off // SUBLANES) * SUBLANES
            length = jnp.where(end > off, ((end + SUBLANES - 1) // SUBLANES) * SUBLANES - off8, 0)
            bases.append(base)
            offs.append(off8)
            src = e * cap + off8
            dst = base
            for size in _RUN_PIECES if start else ():
                piece = length & size

                @pl.when(piece != 0)
                def _(src=src, dst=dst, size=size):
                    pltpu.make_async_copy(
                        ye_hbm.at[pl.ds(pl.multiple_of(src, SUBLANES), size), :],
                        stage_ref.at[sl, pl.ds(pl.multiple_of(dst, SUBLANES), size), :],
                        sem.at[sl]).start()

                src = src + piece
                dst = dst + piece
            base = base + length
        return bases, offs, base

    @pl.when(j == 0)
    def _():
        stage_ref[...] = jnp.zeros_like(stage_ref)
        tile_copies(0, 0, True)

    @pl.when(j + 1 < pl.num_programs(0))
    def _():
        tile_copies(j + 1, 1 - slot, True)

    bases, offs, total = tile_copies(j, slot, False)
    wait_rows(total, slot)

    sp = sp_ref[...]
    kiota = lax.broadcasted_iota(i32, (1, _KB), 1)
    acc_ref[...] = jnp.zeros_like(acc_ref)

    def kblock(kb, c):
        hit = None
        for e in range(N_EXPERTS):
            local = sp[:, e:e + 1] + (bases[e] - offs[e] - kb * _KB)
            h = local == kiota
            hit = h if hit is None else (hit | h)
        s = jnp.where(hit, 1.0, 0.0).astype(bf16)
        rows = stage_ref[slot, pl.ds(pl.multiple_of(kb * _KB, _KB), _KB), :]
        hi = rows.astype(bf16)
        lo = (rows - hi.astype(f32)).astype(bf16)
        acc_ref[...] += (jnp.dot(s, hi, preferred_element_type=f32)
                         + jnp.dot(s, lo, preferred_element_type=f32))
        return c

    lax.fori_loop(0, (total + _KB - 1) // _KB, kblock, 0)
    o_ref[...] = x1_ref[...] + mods_ref[0][5:6] * acc_ref[...]


def _combine(offs, ye_flat, sp_n, x1, mods, cap, tiles_per_row):
    n = x1.shape[0]
    row = (lambda j: 1 + j // tiles_per_row) if tiles_per_row else (lambda j: 0)
    body = functools.partial(_combine_body, cap=cap)
    grid_spec = pltpu.PrefetchScalarGridSpec(
        num_scalar_prefetch=1,
        grid=(n // _TT,),
        in_specs=[pl.BlockSpec(memory_space=pl.ANY),
                  pl.BlockSpec((_TT, N_EXPERTS), lambda j, offs: (j, 0)),
                  pl.BlockSpec((_TT, D), lambda j, offs: (j, 0)),
                  pl.BlockSpec((1, 6, D), lambda j, offs: (row(j), 0, 0))],
        out_specs=pl.BlockSpec((_TT, D), lambda j, offs: (j, 0)),
        scratch_shapes=[pltpu.VMEM((2, N_EXPERTS * _RUN_MAX, D), f32), pltpu.VMEM((_TT, D), f32),
                        pltpu.SemaphoreType.DMA((2,))],
    )
    return pl.pallas_call(
        body,
        grid_spec=grid_spec,
        out_shape=jax.ShapeDtypeStruct((n, D), f32),
        compiler_params=_cparams(("arbitrary",), 48),
        name="expert_combine",
    )(offs, ye_flat, sp_n, x1, mods)


def _moe(x1s, h2s, affs, mods, w_gate, w_up, w_down, tiles_per_row):
    routes = [_route(aff_t, CAPACITY_FACTOR * aff_t.shape[1] // N_EXPERTS) for aff_t in affs]
    (idx_a, g_a, _, _), (idx_b, g_b, _, _) = routes
    yes = _expert_ffn(idx_a, idx_b, h2s[0], h2s[1], g_a, g_b, w_gate, w_up, w_down)
    outs = []
    for ye, (idx, _, pos, sp), x1, tpr in zip(yes, routes, x1s, tiles_per_row):
        cap = idx.shape[1]
        offs = jnp.concatenate([pos[:, ::_TT], jnp.full((N_EXPERTS, 1), cap, i32)], axis=1)
        outs.append(_combine(offs, ye.reshape(N_EXPERTS * cap, D), sp.T, x1, mods, cap, tpr))
    return outs


def kernel(x_prompt, x_sample, cache_k, cache_v, state_fwd, state_bwd, c, c_ctx, w_mod, b_mod,
           norm_mix, w_in, q_norm, k_norm, rpb, conv_w, conv_b, lru_w_a, lru_b_a, lru_w_x,
           lru_b_x, lru_lambda, w_out, norm_ffn, w_router, w_gate, w_up, w_down):
    B, S, _ = x_prompt.shape
    DB, T, _ = x_sample.shape
    l = 0

    cvecs = jnp.concatenate([c_ctx[None, :], c, jnp.zeros((SUBLANES - 1 - DB, D), f32)], axis=0)
    mods = _adaln_mods(cvecs, w_mod[l], b_mod[l][None, :]).reshape(SUBLANES, 6, D)

    heads_per_tile = _TN_IN // HEAD_DIM
    qn_t = jnp.tile(q_norm[l], heads_per_tile)[None, :]
    kn_t = jnp.tile(k_norm[l], heads_per_tile)[None, :]
    gmat = (jnp.kron(jnp.eye(heads_per_tile, dtype=f32), jnp.ones((HEAD_DIM, HEAD_DIM), f32))
            / HEAD_DIM).astype(bf16)
    nmix = norm_mix[l][None, :]
    nffn = norm_ffn[l][None, :]
    w_in_b = w_in[l]
    w_out_b = w_out[l].astype(bf16)
    w_router_t = w_router[l].T.astype(bf16)
    wcat = _lru_gate_weights(lru_w_a[l], lru_w_x[l])
    cb = conv_b[l][None, :]

    xp = x_prompt.reshape(B * S, D)
    xs = x_sample.reshape(DB * T, D)

    proj_c = _inproj(xp, mods, nmix, w_in_b, qn_t, kn_t, gmat, per_tile_rows=False)
    att_c, new_k, new_v = _ctx_attention(proj_c, B, S)
    zeros_c = jnp.zeros((B, LRU_W), f32)
    lru_c, htf, htb = _rglru(proj_c, wcat, conv_w[l], cb, lru_b_a[l], lru_b_x[l], lru_lambda[l],
                             zeros_c, zeros_c, T=S, nb=SUBLANES)
    x1_c, h2_c, aff_c = _outproj(att_c, lru_c, w_out_b, xp, mods, nffn, w_router_t, 0)

    nct = LRU_W // _LRU_TC

    def slabs(h0):
        return (h0.reshape(DB, nct, _LRU_HALVES, LANES).transpose(2, 0, 1, 3)
                .reshape(_LRU_HALVES * DB, nct * LANES))

    proj_l = _inproj(xs, mods, nmix, w_in_b, qn_t, kn_t, gmat, per_tile_rows=True)
    att_l = _lat_attention(proj_l, cache_k, cache_v, _pad_rpb(rpb[l]), l, DB, T)
    h0f_l = slabs(state_fwd[:, l])
    h0b_l = slabs(state_bwd[:, l])
    lru_l, _, _ = _rglru(proj_l, wcat, conv_w[l], cb, lru_b_a[l], lru_b_x[l], lru_lambda[l],
                         h0f_l, h0b_l, T=T, nb=DB)
    x1_l, h2_l, aff_l = _outproj(att_l, lru_l, w_out_b, xs, mods, nffn, w_router_t, T // 512)

    y_c, y_l = _moe((x1_c, x1_l), (h2_c, h2_l), (aff_c, aff_l), mods,
                    w_gate[l], w_up[l], w_down[l], (0, T // _TT))

    return (y_c.reshape(B, S, D), y_l.reshape(DB, T, D), new_k, new_v,
            htf[:, None, :], htb[:, None, :])
```

```python
import functools

import jax
import jax.numpy as jnp
from jax import lax
from jax.experimental import pallas as pl
from jax.experimental.pallas import tpu as pltpu

f32 = jnp.float32
bf16 = jnp.bfloat16
i32 = jnp.int32

D = 2048
N_HEADS = 16
HEAD_DIM = 64
ATT_W = N_HEADS * HEAD_DIM
LRU_W = D - ATT_W
LRU_BLOCK = 64
LRU_C = 8.0
GRID_W = 64
WIN_H = 8
WIN_W = 16
N_EXPERTS = 16
CAPACITY_FACTOR = 2
D_FF = 2048
EPS = 1e-6
NEG_INF = -1e30
IN_COLS = 3 * ATT_W + 2 * LRU_W

LANES = 128
SUBLANES = 8
MXU_DIM = 256
UNSELECTED = -(1 << 30)

_MIB = 1024 * 1024


def _cparams(sem, vmem_mib):
    return pltpu.CompilerParams(dimension_semantics=sem, vmem_limit_bytes=vmem_mib * _MIB)


def _mods_body(c_ref, w_ref, b_ref, o_ref):
    c = c_ref[...]
    s = (c * jax.nn.sigmoid(c)).astype(bf16)
    o_ref[...] = jnp.dot(s, w_ref[...].astype(bf16), preferred_element_type=f32) + b_ref[...]


def _adaln_mods(cvecs, w_mod, b_mod):
    tn = 1024
    return pl.pallas_call(
        _mods_body,
        grid=(6 * D // tn,),
        in_specs=[pl.BlockSpec((SUBLANES, D), lambda j: (0, 0)),
                  pl.BlockSpec((D, tn), lambda j: (0, j)),
                  pl.BlockSpec((1, tn), lambda j: (0, j))],
        out_specs=pl.BlockSpec((SUBLANES, tn), lambda j: (0, j)),
        out_shape=jax.ShapeDtypeStruct((SUBLANES, 6 * D), f32),
        compiler_params=_cparams(("arbitrary",), 40),
        name="adaln_mods",
    )(cvecs, w_mod, b_mod)


_TN_IN = 512
_NORM_ROWS = 16
_NORM_SLICES = 8


def _inproj_body(x0_ref, xn_ref, mods0_ref, modsn_ref, nw_ref, w_ref, qn_ref, kn_ref, g_ref,
                 o_ref, ha_ref, hb_ref):
    i = pl.program_id(0)
    j = pl.program_id(1)
    tm = ha_ref.shape[0]
    nw = nw_ref[...]
    n_qk_tiles = 2 * ATT_W // _TN_IN

    def norm_rows(x_ref, mods_ref, h_ref, r0):
        m = mods_ref[0]
        rows = pl.ds(pl.multiple_of(r0, _NORM_ROWS), _NORM_ROWS)
        x = x_ref[rows, :]
        ms = jnp.mean(x * x, axis=-1, keepdims=True)
        y = x * lax.rsqrt(ms + EPS) * nw
        h_ref[rows, :] = (y * (1.0 + m[1:2]) + m[0:1]).astype(bf16)

    @pl.when((i == 0) & (j == 0))
    def _():
        def first(c, carry):
            norm_rows(x0_ref, mods0_ref, ha_ref, c * _NORM_ROWS)
            return carry

        lax.fori_loop(0, tm // _NORM_ROWS, first, 0, unroll=4)

    def column_step(cur_ref, nxt_ref):
        slice_rows = tm // _NORM_SLICES
        r_slice = jnp.minimum(j, _NORM_SLICES - 1) * slice_rows
        for c in range(slice_rows // _NORM_ROWS):
            norm_rows(xn_ref, modsn_ref, nxt_ref, r_slice + c * _NORM_ROWS)

        acc = jnp.dot(cur_ref[...], w_ref[...].astype(bf16), preferred_element_type=f32)

        @pl.when(j < n_qk_tiles)
        def _():
            sq = acc * acc
            hi = sq.astype(bf16)
            lo = (sq - hi.astype(f32)).astype(bf16)
            g = g_ref[...]
            ms = (jnp.dot(hi, g, preferred_element_type=f32)
                  + jnp.dot(lo, g, preferred_element_type=f32))
            head_w = jnp.where(j < n_qk_tiles // 2, qn_ref[...], kn_ref[...])
            o_ref[...] = acc * lax.rsqrt(ms + EPS) * head_w

        @pl.when(j >= n_qk_tiles)
        def _():
            o_ref[...] = acc

    @pl.when((i & 1) == 0)
    def _():
        column_step(ha_ref, hb_ref)

    @pl.when((i & 1) == 1)
    def _():
        column_step(hb_ref, ha_ref)


def _inproj(x2d, mods, norm_w, w_in, qn_t, kn_t, gmat, per_tile_rows):
    n = x2d.shape[0]
    tm = 1024
    last = n // tm - 1
    row = (lambda i: 1 + i) if per_tile_rows else (lambda i: 0)
    nxt = lambda i: jnp.minimum(i + 1, last)
    return pl.pallas_call(
        _inproj_body,
        grid=(n // tm, IN_COLS // _TN_IN),
        in_specs=[pl.BlockSpec((tm, D), lambda i, j: (0, 0)),
                  pl.BlockSpec((tm, D), lambda i, j: (nxt(i), 0)),
                  pl.BlockSpec((1, 6, D), lambda i, j: (row(0), 0, 0)),
                  pl.BlockSpec((1, 6, D), lambda i, j: (row(nxt(i)), 0, 0)),
                  pl.BlockSpec((1, D), lambda i, j: (0, 0)),
                  pl.BlockSpec((D, _TN_IN), lambda i, j: (0, j)),
                  pl.BlockSpec((1, _TN_IN), lambda i, j: (0, 0)),
                  pl.BlockSpec((1, _TN_IN), lambda i, j: (0, 0)),
                  pl.BlockSpec((_TN_IN, _TN_IN), lambda i, j: (0, 0))],
        out_specs=pl.BlockSpec((tm, _TN_IN), lambda i, j: (i, j)),
        out_shape=jax.ShapeDtypeStruct((n, IN_COLS), f32),
        scratch_shapes=[pltpu.VMEM((tm, D), bf16), pltpu.VMEM((tm, D), bf16)],
        compiler_params=_cparams(("arbitrary", "arbitrary"), 56),
        name="in_projection",
    )(x2d, x2d, mods, mods, norm_w, w_in, qn_t, kn_t, gmat)


def _softmax_pv(s_list, v_list):
    m = s_list[0].max(axis=-1, keepdims=True)
    for s in s_list[1:]:
        m = jnp.maximum(m, s.max(axis=-1, keepdims=True))
    o = None
    for s, v in zip(s_list, v_list):
        os_ = jnp.dot(jnp.exp(s - m).astype(bf16), v, preferred_element_type=f32)
        o = os_ if o is None else o + os_
    return o


def _normalise_pair(o0, o1, lane):
    left = lane < HEAD_DIM
    den = pltpu.roll(jnp.where(left, o1, o0), HEAD_DIM, 1)
    return jnp.where(left, o0, o1) / den


def _nt_dot(a, b):
    return lax.dot_general(a, b, (((1,), (1,)), ((), ())), preferred_element_type=f32)


_QK_SCALE = HEAD_DIM ** -0.5


def _ctx_attn_body(q_ref, k_ref, v_ref, o_ref, nk_ref, nv_ref):
    lane = lax.broadcasted_iota(i32, (1, LANES), 1)
    for p in range(ATT_W // LANES):
        cols = slice(p * LANES, (p + 1) * LANES)
        q = q_ref[:, cols] * _QK_SCALE
        kf = k_ref[:, cols]
        k = kf.astype(bf16)
        v = v_ref[:, cols]
        nk_ref[0, 0, 2 * p] = kf[:, :HEAD_DIM]
        nk_ref[0, 0, 2 * p + 1] = pltpu.roll(kf, HEAD_DIM, 1)[:, :HEAD_DIM]
        nv_ref[0, 0, 2 * p] = v[:, :HEAD_DIM]
        nv_ref[0, 0, 2 * p + 1] = pltpu.roll(v, HEAD_DIM, 1)[:, :HEAD_DIM]
        heads = []
        for hh in range(2):
            hm = (lane >= hh * HEAD_DIM) & (lane < (hh + 1) * HEAD_DIM)
            qh = jnp.where(hm, q, 0.0).astype(bf16)
            vh = jnp.where(hm, v, 1.0).astype(bf16)
            heads.append(_softmax_pv([_nt_dot(qh, k)], [vh]))
        o_ref[:, cols] = _normalise_pair(heads[0], heads[1], lane).astype(bf16)


def _ctx_attention(proj, batch, seq):
    n = proj.shape[0]
    cache = pl.BlockSpec((1, 1, N_HEADS, seq, HEAD_DIM), lambda b: (b, 0, 0, 0, 0))
    cache_shape = jax.ShapeDtypeStruct((batch, 1, N_HEADS, seq, HEAD_DIM), f32)
    return pl.pallas_call(
        _ctx_attn_body,
        grid=(batch,),
        in_specs=[pl.BlockSpec((seq, ATT_W), lambda b: (b, 0)),
                  pl.BlockSpec((seq, ATT_W), lambda b: (b, 1)),
                  pl.BlockSpec((seq, ATT_W), lambda b: (b, 2))],
        out_specs=[pl.BlockSpec((seq, ATT_W), lambda b: (b, 0)), cache, cache],
        out_shape=[jax.ShapeDtypeStruct((n, ATT_W), bf16), cache_shape, cache_shape],
        compiler_params=_cparams(("arbitrary",), 32),
        name="context_attention",
    )(proj, proj, proj)


_NA_ROWS = 16
_NA_QROWS = 8
_NA_KROWS = 12


def _pad_rpb(rpb):
    nd, nc = 2 * WIN_H - 1, 2 * WIN_W - 1
    return jnp.pad(rpb.astype(f32), ((0, 0), (0, 2 * SUBLANES - nd), (0, LANES - nc)))


def _build_na_bias(rpb_ref, bias_ref):
    qc = lax.broadcasted_iota(i32, (GRID_W, LANES), 0)
    lane = lax.broadcasted_iota(i32, (GRID_W, LANES), 1)
    kc = lane & (GRID_W - 1)
    cs = jnp.clip(qc - WIN_W // 2, 0, GRID_W - WIN_W)
    col_ok = (kc >= cs) & (kc < cs + WIN_W)
    left = lane < GRID_W
    neg = jnp.full((GRID_W, LANES), NEG_INF, f32)
    for hh in range(2):
        tbl = rpb_ref[hh]
        tiles = {}

        def toeplitz(d, odd):
            if (d, odd) not in tiles:
                row = jnp.broadcast_to(tbl[d:d + 1, :], (GRID_W, LANES))
                shift = (LANES - (WIN_W - 1) + odd * GRID_W) % LANES
                t = pltpu.roll(row, shift, 1, stride=1, stride_axis=0)
                tiles[(d, odd)] = jnp.where(col_ok, t, NEG_INF)
            return tiles[(d, odd)]

        for j in range(2):
            for rq in range(_NA_QROWS):
                rq_abs = j * _NA_QROWS + rq
                rs = min(max(rq_abs - WIN_H // 2, 0), _NA_ROWS - WIN_H)
                for kp in range(_NA_KROWS // 2):
                    parts = []
                    for odd in range(2):
                        rk_abs = j * (_NA_ROWS - _NA_KROWS) + 2 * kp + odd
                        if rs <= rk_abs < rs + WIN_H:
                            parts.append(toeplitz(rk_abs - rq_abs + WIN_H - 1, odd))
                        else:
                            parts.append(neg)
                    bias_ref[hh, j, rq * GRID_W:(rq + 1) * GRID_W,
                             kp * LANES:(kp + 1) * LANES] = jnp.where(left, parts[0], parts[1])


def _lat_attn_body(q_ref, k_ref, v_ref, ck_ref, cv_ref, rpb_ref, o_ref, bias_ref):
    @pl.when(pl.program_id(1) == 0)
    def _():
        _build_na_bias(rpb_ref, bias_ref)

    lane = lax.broadcasted_iota(i32, (1, LANES), 1)
    nq = _NA_QROWS * GRID_W
    nk = _NA_KROWS * GRID_W
    kstep = (_NA_ROWS - _NA_KROWS) * GRID_W
    src = lax.broadcasted_iota(i32, (HEAD_DIM, LANES), 0)
    dst = lax.broadcasted_iota(i32, (HEAD_DIM, LANES), 1)
    ck = None
    cv_h = []
    for hh in range(2):
        place = jnp.where(dst == src + hh * HEAD_DIM, 1.0, 0.0).astype(bf16)
        part = jnp.dot(ck_ref[0, 0, hh].astype(bf16), place, preferred_element_type=f32)
        ck = part if ck is None else ck + part
        placed = jnp.dot(cv_ref[0, 0, hh].astype(bf16), place, preferred_element_type=f32)
        own = (lane >= hh * HEAD_DIM) & (lane < (hh + 1) * HEAD_DIM)
        cv_h.append(jnp.where(own, placed, 1.0).astype(bf16))
    ck = ck.astype(bf16)
    for j in range(2):
        q = q_ref[j * nq:(j + 1) * nq, :] * _QK_SCALE
        k = k_ref[j * kstep:j * kstep + nk, :].astype(bf16)
        v = v_ref[j * kstep:j * kstep + nk, :]
        heads = []
        for hh in range(2):
            hm = (lane >= hh * HEAD_DIM) & (lane < (hh + 1) * HEAD_DIM)
            qh = jnp.where(hm, q, 0.0).astype(bf16)
            s_win = _nt_dot(qh, k) + bias_ref[hh, j]
            s_ctx = _nt_dot(qh, ck)
            vh = jnp.where(hm, v, 1.0).astype(bf16)
            heads.append(_softmax_pv([s_win, s_ctx], [vh, cv_h[hh]]))
        o_ref[j * nq:(j + 1) * nq, :] = _normalise_pair(heads[0], heads[1], lane).astype(bf16)


def _lat_attention(proj, ck, cv, rpb_p, layer, batch, seq):
    n = proj.shape[0]
    past = ck.shape[3]
    cache = pl.BlockSpec((1, 1, 2, past, HEAD_DIM), lambda p, b: (b, layer, p, 0, 0))
    npair = ATT_W // LANES
    nq = _NA_QROWS * GRID_W
    nk = _NA_KROWS * GRID_W
    return pl.pallas_call(
        _lat_attn_body,
        grid=(npair, batch),
        in_specs=[pl.BlockSpec((seq, LANES), lambda p, b: (b, p)),
                  pl.BlockSpec((seq, LANES), lambda p, b: (b, npair + p)),
                  pl.BlockSpec((seq, LANES), lambda p, b: (b, 2 * npair + p)),
                  cache, cache,
                  pl.BlockSpec((2, 2 * SUBLANES, LANES), lambda p, b: (p, 0, 0))],
        out_specs=pl.BlockSpec((seq, LANES), lambda p, b: (b, p)),
        out_shape=jax.ShapeDtypeStruct((n, ATT_W), bf16),
        scratch_shapes=[pltpu.VMEM((2, 2, nq, nk), f32)],
        compiler_params=_cparams(("arbitrary", "arbitrary"), 48),
        name="neighbourhood_attention",
    )(proj, proj, proj, ck, cv, rpb_p)


_LRU_TC = 256
_GATE_ROWS = 256
_LRU_HALVES = _LRU_TC // LANES


def _gelu_tanh(x):
    c = 0.7978845608028654
    return x * (0.5 * (1.0 + jnp.tanh(c * (x + 0.044715 * (x * x * x)))))


def _lru_body(xr_ref, gr_ref, w_ref, cw_ref, cb_ref, ba_ref, bx_ref, lam_ref, h0f_ref, h0b_ref,
              y_ref, htf_ref, htb_ref, xc_ref, af_ref, uf_ref, ab_ref, ub_ref, *, T, nb):
    nslab = _LRU_HALVES * nb // SUBLANES
    R = nb * T
    x = xr_ref[...]
    t = lax.broadcasted_iota(i32, (R, 1), 0) & (T - 1)
    cw = cw_ref[...]
    xc = jnp.where(t >= 2, pltpu.roll(x, 2, 0), 0.0) * cw[0:1]
    xc = xc + jnp.where(t >= 1, pltpu.roll(x, 1, 0), 0.0) * cw[1:2]
    xc = xc + x * cw[2:3]
    xc = xc + jnp.where(t < T - 1, pltpu.roll(x, R - 1, 0), 0.0) * cw[3:4]
    xc_ref[...] = xc + cb_ref[...]

    nl = -lam_ref[...]
    sp = jnp.maximum(nl, 0.0) + jnp.log1p(jnp.exp(-jnp.abs(nl)))
    half_c_sp = (0.5 * LRU_C) * sp
    w = w_ref[0]
    dirs = ((af_ref, uf_ref), (ab_ref, ub_ref))
    for b in range(nb):
        for ci in range(T // _GATE_ROWS):
            r0 = b * T + ci * _GATE_ROWS
            xcc = xc_ref[r0:r0 + _GATE_ROWS, :]
            half_x = 0.5 * xcc
            z = jnp.dot(xcc.astype(bf16), w, preferred_element_type=f32)
            for d, (a_ref, u_ref) in enumerate(dirs):
                c0 = d * 2 * _LRU_TC
                tr = jnp.tanh(0.5 * (z[:, c0:c0 + _LRU_TC] + ba_ref[d:d + 1]))
                tg = jnp.tanh(0.5 * (z[:, c0 + _LRU_TC:c0 + 2 * _LRU_TC] + bx_ref[d:d + 1]))
                neg_log_a = (1.0 + tr) * half_c_sp[d:d + 1]
                a = jnp.exp(-neg_log_a)
                one_m_a2 = jnp.tanh(neg_log_a) * (1.0 + a * a)
                root = jnp.where(one_m_a2 > 0.0, one_m_a2 * lax.rsqrt(one_m_a2), 0.0)
                u = root * ((1.0 + tg) * half_x)
                for hf in range(_LRU_HALVES):
                    s, sub = divmod(hf * nb + b, SUBLANES)
                    rows = pl.ds(ci * _GATE_ROWS * SUBLANES + sub, _GATE_ROWS, stride=SUBLANES)
                    a_ref[s, rows, :] = a[:, hf * LANES:(hf + 1) * LANES]
                    u_ref[s, rows, :] = u[:, hf * LANES:(hf + 1) * LANES]

    def step(i, carry):
        rf = pl.ds(pl.multiple_of(i * SUBLANES, SUBLANES), SUBLANES)
        rb = pl.ds(pl.multiple_of((T - 1 - i) * SUBLANES, SUBLANES), SUBLANES)
        new = []
        for s in range(nslab):
            hf_ = af_ref[s, rf, :] * carry[2 * s] + uf_ref[s, rf, :]
            uf_ref[s, rf, :] = hf_
            hb_ = ab_ref[s, rb, :] * carry[2 * s + 1] + ub_ref[s, rb, :]
            ub_ref[s, rb, :] = hb_
            new += [hf_, hb_]
        return tuple(new)

    init = []
    for s in range(nslab):
        init += [h0f_ref[:, s * LANES:(s + 1) * LANES], h0b_ref[:, s * LANES:(s + 1) * LANES]]
    fin = lax.fori_loop(0, T, step, tuple(init), unroll=8)
    for s in range(nslab):
        htf_ref[:, s * LANES:(s + 1) * LANES] = fin[2 * s]
        htb_ref[:, s * LANES:(s + 1) * LANES] = fin[2 * s + 1]

    for b in range(nb):
        for hf in range(_LRU_HALVES):
            s, sub = divmod(hf * nb + b, SUBLANES)
            rows = pl.ds(sub, T, stride=SUBLANES)
            hs = uf_ref[s, rows, :] + ub_ref[s, rows, :]
            g = gr_ref[b * T:(b + 1) * T, hf * LANES:(hf + 1) * LANES]
            y_ref[b * T:(b + 1) * T, hf * LANES:(hf + 1) * LANES] = (
                hs * _gelu_tanh(g)).astype(bf16)


def _rglru(proj, wcat, conv_w, conv_b, b_a, b_x, lam, h0f, h0b, T, nb):
    n = proj.shape[0]
    nslab = _LRU_HALVES * nb // SUBLANES
    W = nslab * LANES
    R = nb * T
    nct = LRU_W // _LRU_TC
    xr0 = 3 * ATT_W // _LRU_TC
    gr0 = xr0 + nct
    chan = lambda k: pl.BlockSpec((k, _LRU_TC), lambda g, j: (0, j))
    slab = pl.BlockSpec((SUBLANES, W), lambda g, j: (g, j))
    body = functools.partial(_lru_body, T=T, nb=nb)
    return pl.pallas_call(
        body,
        grid=(n // R, nct),
        in_specs=[pl.BlockSpec((R, _LRU_TC), lambda g, j: (g, xr0 + j)),
                  pl.BlockSpec((R, _LRU_TC), lambda g, j: (g, gr0 + j)),
                  pl.BlockSpec((1, _LRU_TC, 4 * _LRU_TC), lambda g, j: (j, 0, 0)),
                  chan(4), chan(1), chan(2), chan(2), chan(2), slab, slab],
        out_specs=[pl.BlockSpec((R, _LRU_TC), lambda g, j: (g, j)), slab, slab],
        out_shape=[jax.ShapeDtypeStruct((n, LRU_W), bf16),
                   jax.ShapeDtypeStruct(h0f.shape, f32),
                   jax.ShapeDtypeStruct(h0b.shape, f32)],
        scratch_shapes=[pltpu.VMEM((R, _LRU_TC), f32)]
        + [pltpu.VMEM((nslab, T * SUBLANES, LANES), f32) for _ in range(4)],
        compiler_params=_cparams(("arbitrary", "arbitrary"), 56),
        name="rg_lru",
    )(proj, proj, wcat, conv_w, conv_b, b_a, b_x, lam, h0f, h0b)


def _lru_gate_weights(w_a, w_x):
    per_tile = _LRU_TC // LRU_BLOCK
    nct = LRU_W // _LRU_TC
    eye = jnp.eye(per_tile, dtype=f32)

    def bd(w):
        w = w.reshape(nct, per_tile, LRU_BLOCK, LRU_BLOCK)
        m = w[:, :, :, None, :] * eye[None, :, None, :, None]
        return m.reshape(nct, _LRU_TC, _LRU_TC)

    return jnp.concatenate([bd(w_a[0]), bd(w_x[0]), bd(w_a[1]), bd(w_x[1])], axis=-1).astype(bf16)


def _outproj_body(att_ref, lru_ref, w_ref, x_ref, mods_ref, nw_ref, wr_ref,
                  x1_ref, h2_ref, aff_ref):
    mix = (jnp.dot(att_ref[...], w_ref[0:ATT_W, :], preferred_element_type=f32)
           + jnp.dot(lru_ref[...], w_ref[ATT_W:D, :], preferred_element_type=f32))
    m = mods_ref[0]
    x1 = x_ref[...] + m[2:3] * mix
    x1_ref[...] = x1
    ms = jnp.mean(x1 * x1, axis=-1, keepdims=True)
    h2 = (x1 * lax.rsqrt(ms + EPS) * nw_ref[...]) * (1.0 + m[4:5]) + m[3:4]
    h2_ref[...] = pltpu.pack_elementwise([h2[:, :D // 2], h2[:, D // 2:]], packed_dtype=bf16)
    logits = _nt_dot(wr_ref[...], h2.astype(bf16))
    e = jnp.exp(logits - logits.max(axis=0, keepdims=True))
    aff_ref[...] = e / e.sum(axis=0, keepdims=True)


def _outproj(att, lru, w_out_b, x2d, mods, norm_w, w_router_t, tiles_per_row):
    n = x2d.shape[0]
    tm = 512
    row = (lambda i: 1 + i // tiles_per_row) if tiles_per_row else (lambda i: 0)
    return pl.pallas_call(
        _outproj_body,
        grid=(n // tm,),
        in_specs=[pl.BlockSpec((tm, ATT_W), lambda i: (i, 0)),
                  pl.BlockSpec((tm, LRU_W), lambda i: (i, 0)),
                  pl.BlockSpec((D, D), lambda i: (0, 0)),
                  pl.BlockSpec((tm, D), lambda i: (i, 0)),
                  pl.BlockSpec((1, 6, D), lambda i: (row(i), 0, 0)),
                  pl.BlockSpec((1, D), lambda i: (0, 0)),
                  pl.BlockSpec((N_EXPERTS, D), lambda i: (0, 0))],
        out_specs=[pl.BlockSpec((tm, D), lambda i: (i, 0)),
                   pl.BlockSpec((tm, D // 2), lambda i: (i, 0)),
                   pl.BlockSpec((N_EXPERTS, tm), lambda i: (0, i))],
        out_shape=[jax.ShapeDtypeStruct((n, D), f32),
                   jax.ShapeDtypeStruct((n, D // 2), jnp.uint32),
                   jax.ShapeDtypeStruct((N_EXPERTS, n), f32)],
        compiler_params=_cparams(("arbitrary",), 52),
        name="out_projection_router",
    )(att, lru, w_out_b, x2d, mods, norm_w, w_router_t)


def _prefix_sum_lanes(x, lane, n):
    s = 1
    while s < n:
        x = x + jnp.where(lane >= s, pltpu.roll(x, s, 1), 0)
        s *= 2
    return x


def _route_body(aff_ref, idx_ref, g_ref, pos_ref, sp_ref, sel_ref, rem_ref, *, n, cap):
    aff = aff_ref[...]
    lane = lax.broadcasted_iota(i32, (N_EXPERTS, n), 1)

    def count(mask):
        return jnp.sum(jnp.where(mask, 1.0, 0.0), axis=1, keepdims=True).astype(i32)

    def as_value(bits):
        return pltpu.bitcast(jnp.broadcast_to(bits, (N_EXPERTS, LANES)), f32)[:, :1]

    def search(i, thr):
        cand = thr | jnp.left_shift(jnp.int32(1), 30 - i)
        return jnp.where(count(aff >= as_value(cand)) >= cap, cand, thr)

    thr = lax.fori_loop(0, 31, search, jnp.zeros((N_EXPERTS, 1), i32))
    sure = aff >= as_value(thr + 1)
    sel_ref[...] = sure.astype(i32)
    rem_ref[...] = ((aff >= as_value(thr)) & jnp.logical_not(sure)).astype(i32)

    def more(need):
        return jnp.max(need) > 0

    def take_largest(need):
        rem = rem_ref[...] > 0
        top = jnp.max(jnp.where(rem, aff, -1.0), axis=1, keepdims=True)
        is_top = rem & (aff == top) & (need > 0)
        rank = _prefix_sum_lanes(is_top.astype(i32), lane, n)
        pick = is_top & ((count(is_top) <= need) | (rank <= need))
        sel_ref[...] = sel_ref[...] | pick.astype(i32)
        rem_ref[...] = (rem & jnp.logical_not(is_top)).astype(i32)
        return need - count(pick)

    lax.while_loop(more, take_largest, cap - count(sure))
    sel_i = sel_ref[...]
    sel = sel_i > 0
    pos = _prefix_sum_lanes(sel_i, lane, n) - sel_i
    pos_ref[...] = pos
    sp_ref[...] = jnp.where(sel, pos, UNSELECTED)

    key = jnp.where(sel, lane - pos, -1)
    idx = lane
    g = aff
    b = 0
    while (1 << b) < n:
        s = 1 << b
        k_sh = pltpu.roll(key, n - s, 1)
        take = (k_sh >= 0) & ((jnp.right_shift(k_sh, b) & 1) == 1)
        stay = (key >= 0) & ((jnp.right_shift(key, b) & 1) == 0)
        idx = jnp.where(take, pltpu.roll(idx, n - s, 1), idx)
        g = jnp.where(take, pltpu.roll(g, n - s, 1), g)
        key = jnp.where(take, k_sh, jnp.where(stay, key, -1))
        b += 1
    idx_ref[...] = idx[:, :cap]
    g_ref[...] = g[:, :cap]


def _route(aff_t, cap):
    n = aff_t.shape[1]
    body = functools.partial(_route_body, n=n, cap=cap)
    full = lambda shape: pl.BlockSpec(shape, lambda: (0, 0))
    return pl.pallas_call(
        body,
        in_specs=[full((N_EXPERTS, n))],
        out_specs=[full((N_EXPERTS, cap)), full((N_EXPERTS, cap)),
                   full((N_EXPERTS, n)), full((N_EXPERTS, n))],
        out_shape=[jax.ShapeDtypeStruct((N_EXPERTS, cap), i32),
                   jax.ShapeDtypeStruct((N_EXPERTS, cap), f32),
                   jax.ShapeDtypeStruct((N_EXPERTS, n), i32),
                   jax.ShapeDtypeStruct((N_EXPERTS, n), i32)],
        scratch_shapes=[pltpu.VMEM((N_EXPERTS, n), i32), pltpu.VMEM((N_EXPERTS, n), i32)],
        compiler_params=pltpu.CompilerParams(vmem_limit_bytes=32 * _MIB),
        name="expert_choice_routing",
    )(aff_t)


_TF = 512
_TN_DOWN = 512


def _expert_body(idx_ref, h2a_hbm, h2b_hbm, g_ref, wg_ref, wu_ref, wd_ref, oa_ref, ob_ref,
                 xe_ref, xb_ref, he_ref, sem, *, caps, na, nd):
    e = pl.program_id(0)
    s = pl.program_id(1)
    srcs = (h2a_hbm, h2b_hbm)
    starts = (0, caps[0])
    rows_total = caps[0] + caps[1]

    def row_copy(grp, tok, r):
        return pltpu.make_async_copy(srcs[grp].at[pl.ds(tok, 1), :], xe_ref.at[pl.ds(r, 1), :], sem)

    def wait_all_rows():
        for grp in range(2):
            pltpu.make_async_copy(srcs[grp].at[pl.ds(0, caps[grp]), :],
                                  xe_ref.at[pl.ds(starts[grp], caps[grp]), :], sem).wait()

    @pl.when((e == 0) & (s == 0))
    def _():
        for grp in range(2):
            def issue(r, c, grp=grp):
                row_copy(grp, idx_ref[starts[grp] + r], starts[grp] + r).start()
                return c

            lax.fori_loop(0, caps[grp], issue, 0, unroll=8)

    @pl.when(s == 0)
    def _():
        wait_all_rows()
        packed = xe_ref[...]
        for part in range(2):
            xb_ref[:, part * (D // 2):(part + 1) * (D // 2)] = pltpu.unpack_elementwise(
                packed, index=part, packed_dtype=bf16, unpacked_dtype=f32).astype(bf16)

    @pl.when(s < na)
    def _():
        wg = wg_ref[0].astype(bf16)
        wu = wu_ref[0].astype(bf16)
        half = rows_total // 2
        for rows in (slice(0, half), slice(half, rows_total)):
            xb = xb_ref[rows, :]
            gp = jnp.dot(xb, wg, preferred_element_type=f32)
            up = jnp.dot(xb, wu, preferred_element_type=f32)
            he_ref[s, rows, :] = ((gp * jax.nn.sigmoid(gp)) * up).astype(bf16)

    @pl.when(s >= na)
    def _():
        base = jnp.minimum(e + 1, N_EXPERTS - 1) * rows_total
        for k in range(nd):
            @pl.when(s - na == k)
            def _(k=k):
                for grp in range(2):
                    per = caps[grp] // nd
                    for r in range(starts[grp] + k * per, starts[grp] + (k + 1) * per):
                        row_copy(grp, idx_ref[base + r], r).start()

        wd = wd_ref[0].astype(bf16)
        acc = None
        for k in range(na):
            part = jnp.dot(he_ref[k], wd[k * _TF:(k + 1) * _TF, :], preferred_element_type=f32)
            acc = part if acc is None else acc + part
        scaled = acc * g_ref[0]
        oa_ref[0] = scaled[:caps[0]]
        ob_ref[0] = scaled[caps[0]:]

    @pl.when((e == N_EXPERTS - 1) & (s == na + nd - 1))
    def _():
        wait_all_rows()


def _expert_ffn(idx_a, idx_b, h2a, h2b, g_a, g_b, w_gate, w_up, w_down):
    caps = (idx_a.shape[1], idx_b.shape[1])
    rows = caps[0] + caps[1]
    na = D_FF // _TF
    nd = D // _TN_DOWN
    idx = jnp.concatenate([idx_a, idx_b], axis=1).reshape(N_EXPERTS * rows)
    g3 = jnp.concatenate([g_a, g_b], axis=1)[:, :, None]
    body = functools.partial(_expert_body, caps=caps, na=na, nd=nd)
    up_tile = lambda e, s, idx: (e, 0, jnp.minimum(s, na - 1))
    down_tile = lambda e, s, idx: (e, 0, jnp.maximum(s - na, 0))
    grid_spec = pltpu.PrefetchScalarGridSpec(
        num_scalar_prefetch=1,
        grid=(N_EXPERTS, na + nd),
        in_specs=[pl.BlockSpec(memory_space=pl.ANY),
                  pl.BlockSpec(memory_space=pl.ANY),
                  pl.BlockSpec((1, rows, 1), lambda e, s, idx: (e, 0, 0)),
                  pl.BlockSpec((1, D, _TF), up_tile),
                  pl.BlockSpec((1, D, _TF), up_tile),
                  pl.BlockSpec((1, D_FF, _TN_DOWN), down_tile)],
        out_specs=[pl.BlockSpec((1, caps[0], _TN_DOWN), down_tile),
                   pl.BlockSpec((1, caps[1], _TN_DOWN), down_tile)],
        scratch_shapes=[pltpu.VMEM((rows, D // 2), jnp.uint32), pltpu.VMEM((rows, D), bf16),
                        pltpu.VMEM((na, rows, _TF), bf16), pltpu.SemaphoreType.DMA(())],
    )
    return pl.pallas_call(
        body,
        grid_spec=grid_spec,
        out_shape=[jax.ShapeDtypeStruct((N_EXPERTS, caps[0], D), f32),
                   jax.ShapeDtypeStruct((N_EXPERTS, caps[1], D), f32)],
        compiler_params=_cparams(("arbitrary", "arbitrary"), 60),
        name="expert_ffn",
    )(idx, h2a, h2b, g3, w_gate, w_up, w_down)


_TT = 128
_KB = 512
_RUN_MAX = _TT + 2 * SUBLANES
_RUN_PIECES = tuple(SUBLANES << i for i in reversed(range((_RUN_MAX // SUBLANES).bit_length())))
_STAGE_ROWS = pl.cdiv(N_EXPERTS * _RUN_MAX, _KB) * _KB
_STAGE_PIECES = tuple(SUBLANES << i
                      for i in reversed(range((_STAGE_ROWS // SUBLANES).bit_length())))


def _combine_body(offs_ref, ye_hbm, sp_ref, x1_ref, mods_ref, o_ref, stage_ref, acc_ref, sem,
                  *, cap):
    j = pl.program_id(0)
    slot = j & 1

    def wait_rows(total, sl):
        for size in _STAGE_PIECES:
            @pl.when((total & size) != 0)
            def _(size=size):
                pltpu.make_async_copy(ye_hbm.at[pl.ds(0, size), :],
                                      stage_ref.at[sl, pl.ds(0, size), :], sem.at[sl]).wait()

    def tile_copies(jj, sl, start):
        bases, offs = [], []
        base = jnp.int32(0)
        for e in range(N_EXPERTS):
            off = offs_ref[e, jj]
            end = offs_ref[e, jj + 1]
            off8 = (off // SUBLANES) * SUBLANES
            length = jnp.where(end > off, ((end + SUBLANES - 1) // SUBLANES) * SUBLANES - off8, 0)
            bases.append(base)
            offs.append(off8)
            src = e * cap + off8
            dst = base
            for size in _RUN_PIECES if start else ():
                piece = length & size

                @pl.when(piece != 0)
                def _(src=src, dst=dst, size=size):
                    pltpu.make_async_copy(
                        ye_hbm.at[pl.ds(pl.multiple_of(src, SUBLANES), size), :],
                        stage_ref.at[sl, pl.ds(pl.multiple_of(dst, SUBLANES), size), :],
                        sem.at[sl]).start()

                src = src + piece
                dst = dst + piece
            base = base + length
        return bases, offs, base

    @pl.when(j == 0)
    def _():
        stage_ref[...] = jnp.zeros_like(stage_ref)
        tile_copies(0, 0, True)

    @pl.when(j + 1 < pl.num_programs(0))
    def _():
        tile_copies(j + 1, 1 - slot, True)

    bases, offs, total = tile_copies(j, slot, False)
    wait_rows(total, slot)

    sp = sp_ref[...]
    kiota = lax.broadcasted_iota(i32, (1, _KB), 1)
    acc_ref[...] = jnp.zeros_like(acc_ref)

    def kblock(kb, c):
        hit = None
        for e in range(N_EXPERTS):
            local = sp[:, e:e + 1] + (bases[e] - offs[e] - kb * _KB)
            h = local == kiota
            hit = h if hit is None else (hit | h)
        s = jnp.where(hit, 1.0, 0.0).astype(bf16)
        rows = stage_ref[slot, pl.ds(pl.multiple_of(kb * _KB, _KB), _KB), :]
        hi = rows.astype(bf16)
        lo = (rows - hi.astype(f32)).astype(bf16)
        acc_ref[...] += (jnp.dot(s, hi, preferred_element_type=f32)
                         + jnp.dot(s, lo, preferred_element_type=f32))
        return c

    lax.fori_loop(0, (total + _KB - 1) // _KB, kblock, 0)
    o_ref[...] = x1_ref[...] + mods_ref[0][5:6] * acc_ref[...]


def _combine(offs, ye_flat, sp_n, x1, mods, cap, tiles_per_row):
    n = x1.shape[0]
    row = (lambda j: 1 + j // tiles_per_row) if tiles_per_row else (lambda j: 0)
    body = functools.partial(_combine_body, cap=cap)
    grid_spec = pltpu.PrefetchScalarGridSpec(
        num_scalar_prefetch=1,
        grid=(n // _TT,),
        in_specs=[pl.BlockSpec(memory_space=pl.ANY),
                  pl.BlockSpec((_TT, N_EXPERTS), lambda j, offs: (j, 0)),
                  pl.BlockSpec((_TT, D), lambda j, offs: (j, 0)),
                  pl.BlockSpec((1, 6, D), lambda j, offs: (row(j), 0, 0))],
        out_specs=pl.BlockSpec((_TT, D), lambda j, offs: (j, 0)),
        scratch_shapes=[pltpu.VMEM((2, _STAGE_ROWS, D), f32), pltpu.VMEM((_TT, D), f32),
                        pltpu.SemaphoreType.DMA((2,))],
    )
    return pl.pallas_call(
        body,
        grid_spec=grid_spec,
        out_shape=jax.ShapeDtypeStruct((n, D), f32),
        compiler_params=_cparams(("arbitrary",), 52),
        name="expert_combine",
    )(offs, ye_flat, sp_n, x1, mods)


def _moe(x1s, h2s, affs, mods, w_gate, w_up, w_down, tiles_per_row):
    routes = [_route(aff_t, CAPACITY_FACTOR * aff_t.shape[1] // N_EXPERTS) for aff_t in affs]
    (idx_a, g_a, _, _), (idx_b, g_b, _, _) = routes
    yes = _expert_ffn(idx_a, idx_b, h2s[0], h2s[1], g_a, g_b, w_gate, w_up, w_down)
    outs = []
    for ye, (idx, _, pos, sp), x1, tpr in zip(yes, routes, x1s, tiles_per_row):
        cap = idx.shape[1]
        offs = jnp.concatenate([pos[:, ::_TT], jnp.full((N_EXPERTS, 1), cap, i32)], axis=1)
        outs.append(_combine(offs, ye.reshape(N_EXPERTS * cap, D), sp.T, x1, mods, cap, tpr))
    return outs


def kernel(x_prompt, x_sample, cache_k, cache_v, state_fwd, state_bwd, c, c_ctx, w_mod, b_mod,
           norm_mix, w_in, q_norm, k_norm, rpb, conv_w, conv_b, lru_w_a, lru_b_a, lru_w_x,
           lru_b_x, lru_lambda, w_out, norm_ffn, w_router, w_gate, w_up, w_down):
    B, S, _ = x_prompt.shape
    DB, T, _ = x_sample.shape
    l = 0

    cvecs = jnp.concatenate([c_ctx[None, :], c, jnp.zeros((SUBLANES - 1 - DB, D), f32)], axis=0)
    mods = _adaln_mods(cvecs, w_mod[l], b_mod[l][None, :]).reshape(SUBLANES, 6, D)

    heads_per_tile = _TN_IN // HEAD_DIM
    qn_t = jnp.tile(q_norm[l], heads_per_tile)[None, :]
    kn_t = jnp.tile(k_norm[l], heads_per_tile)[None, :]
    gmat = (jnp.kron(jnp.eye(heads_per_tile, dtype=f32), jnp.ones((HEAD_DIM, HEAD_DIM), f32))
            / HEAD_DIM).astype(bf16)
    nmix = norm_mix[l][None, :]
    nffn = norm_ffn[l][None, :]
    w_in_b = w_in[l]
    w_out_b = w_out[l].astype(bf16)
    w_router_t = w_router[l].T.astype(bf16)
    wcat = _lru_gate_weights(lru_w_a[l], lru_w_x[l])
    cb = conv_b[l][None, :]

    xp = x_prompt.reshape(B * S, D)
    xs = x_sample.reshape(DB * T, D)

    proj_c = _inproj(xp, mods, nmix, w_in_b, qn_t, kn_t, gmat, per_tile_rows=False)
    att_c, new_k, new_v = _ctx_attention(proj_c, B, S)
    zeros_c = jnp.zeros((B, LRU_W), f32)
    lru_c, htf, htb = _rglru(proj_c, wcat, conv_w[l], cb, lru_b_a[l], lru_b_x[l], lru_lambda[l],
                             zeros_c, zeros_c, T=S, nb=SUBLANES)
    x1_c, h2_c, aff_c = _outproj(att_c, lru_c, w_out_b, xp, mods, nffn, w_router_t, 0)

    nct = LRU_W // _LRU_TC

    def slabs(h0):
        return (h0.reshape(DB, nct, _LRU_HALVES, LANES).transpose(2, 0, 1, 3)
                .reshape(_LRU_HALVES * DB, nct * LANES))

    proj_l = _inproj(xs, mods, nmix, w_in_b, qn_t, kn_t, gmat, per_tile_rows=True)
    att_l = _lat_attention(proj_l, cache_k, cache_v, _pad_rpb(rpb[l]), l, DB, T)
    h0f_l = slabs(state_fwd[:, l])
    h0b_l = slabs(state_bwd[:, l])
    lru_l, _, _ = _rglru(proj_l, wcat, conv_w[l], cb, lru_b_a[l], lru_b_x[l], lru_lambda[l],
                         h0f_l, h0b_l, T=T, nb=DB)
    x1_l, h2_l, aff_l = _outproj(att_l, lru_l, w_out_b, xs, mods, nffn, w_router_t, T // 512)

    y_c, y_l = _moe((x1_c, x1_l), (h2_c, h2_l), (aff_c, aff_l), mods,
                    w_gate[l], w_up[l], w_down[l], (0, T // _TT))

    return (y_c.reshape(B, S, D), y_l.reshape(DB, T, D), new_k, new_v,
            htf[:, None, :], htb[:, None, :])
```

```python
import functools

import jax
import jax.numpy as jnp
from jax import lax
from jax.experimental import pallas as pl
from jax.experimental.pallas import tpu as pltpu

f32 = jnp.float32
bf16 = jnp.bfloat16
i32 = jnp.int32

D = 2048
N_HEADS = 16
HEAD_DIM = 64
ATT_W = N_HEADS * HEAD_DIM
LRU_W = D - ATT_W
LRU_BLOCK = 64
LRU_C = 8.0
GRID_W = 64
WIN_H = 8
WIN_W = 16
N_EXPERTS = 16
CAPACITY_FACTOR = 2
D_FF = 2048
EPS = 1e-6
NEG_INF = -1e30
IN_COLS = 3 * ATT_W + 2 * LRU_W

LANES = 128
SUBLANES = 8
MXU_DIM = 256
UNSELECTED = -(1 << 30)

_MIB = 1024 * 1024


def _cparams(sem, vmem_mib):
    return pltpu.CompilerParams(dimension_semantics=sem, vmem_limit_bytes=vmem_mib * _MIB)


def _mods_body(c_ref, w_ref, b_ref, o_ref):
    c = c_ref[...]
    s = (c * jax.nn.sigmoid(c)).astype(bf16)
    o_ref[...] = jnp.dot(s, w_ref[...].astype(bf16), preferred_element_type=f32) + b_ref[...]


def _adaln_mods(cvecs, w_mod, b_mod):
    tn = 1024
    return pl.pallas_call(
        _mods_body,
        grid=(6 * D // tn,),
        in_specs=[pl.BlockSpec((SUBLANES, D), lambda j: (0, 0)),
                  pl.BlockSpec((D, tn), lambda j: (0, j)),
                  pl.BlockSpec((1, tn), lambda j: (0, j))],
        out_specs=pl.BlockSpec((SUBLANES, tn), lambda j: (0, j)),
        out_shape=jax.ShapeDtypeStruct((SUBLANES, 6 * D), f32),
        compiler_params=_cparams(("arbitrary",), 40),
        name="adaln_mods",
    )(cvecs, w_mod, b_mod)


_TN_IN = 512
_NORM_ROWS = 16
_NORM_SLICES = 8


def _inproj_body(x0_ref, xn_ref, mods0_ref, modsn_ref, nw_ref, w_ref, qn_ref, kn_ref, g_ref,
                 o_ref, ha_ref, hb_ref):
    i = pl.program_id(0)
    j = pl.program_id(1)
    tm = ha_ref.shape[0]
    nw = nw_ref[...]
    n_qk_tiles = 2 * ATT_W // _TN_IN

    def norm_rows(x_ref, mods_ref, h_ref, r0):
        m = mods_ref[0]
        rows = pl.ds(pl.multiple_of(r0, _NORM_ROWS), _NORM_ROWS)
        x = x_ref[rows, :]
        ms = jnp.mean(x * x, axis=-1, keepdims=True)
        y = x * lax.rsqrt(ms + EPS) * nw
        h_ref[rows, :] = (y * (1.0 + m[1:2]) + m[0:1]).astype(bf16)

    @pl.when((i == 0) & (j == 0))
    def _():
        def first(c, carry):
            norm_rows(x0_ref, mods0_ref, ha_ref, c * _NORM_ROWS)
            return carry

        lax.fori_loop(0, tm // _NORM_ROWS, first, 0, unroll=4)

    def column_step(cur_ref, nxt_ref):
        slice_rows = tm // _NORM_SLICES
        r_slice = jnp.minimum(j, _NORM_SLICES - 1) * slice_rows
        for c in range(slice_rows // _NORM_ROWS):
            norm_rows(xn_ref, modsn_ref, nxt_ref, r_slice + c * _NORM_ROWS)

        acc = jnp.dot(cur_ref[...], w_ref[...].astype(bf16), preferred_element_type=f32)

        @pl.when(j < n_qk_tiles)
        def _():
            sq = acc * acc
            hi = sq.astype(bf16)
            lo = (sq - hi.astype(f32)).astype(bf16)
            g = g_ref[...]
            ms = (jnp.dot(hi, g, preferred_element_type=f32)
                  + jnp.dot(lo, g, preferred_element_type=f32))
            head_w = jnp.where(j < n_qk_tiles // 2, qn_ref[...], kn_ref[...])
            o_ref[...] = acc * lax.rsqrt(ms + EPS) * head_w

        @pl.when(j >= n_qk_tiles)
        def _():
            o_ref[...] = acc

    @pl.when((i & 1) == 0)
    def _():
        column_step(ha_ref, hb_ref)

    @pl.when((i & 1) == 1)
    def _():
        column_step(hb_ref, ha_ref)


def _inproj(x2d, mods, norm_w, w_in, qn_t, kn_t, gmat, per_tile_rows):
    n = x2d.shape[0]
    tm = 1024
    last = n // tm - 1
    row = (lambda i: 1 + i) if per_tile_rows else (lambda i: 0)
    nxt = lambda i: jnp.minimum(i + 1, last)
    return pl.pallas_call(
        _inproj_body,
        grid=(n // tm, IN_COLS // _TN_IN),
        in_specs=[pl.BlockSpec((tm, D), lambda i, j: (0, 0)),
                  pl.BlockSpec((tm, D), lambda i, j: (nxt(i), 0)),
                  pl.BlockSpec((1, 6, D), lambda i, j: (row(0), 0, 0)),
                  pl.BlockSpec((1, 6, D), lambda i, j: (row(nxt(i)), 0, 0)),
                  pl.BlockSpec((1, D), lambda i, j: (0, 0)),
                  pl.BlockSpec((D, _TN_IN), lambda i, j: (0, j)),
                  pl.BlockSpec((1, _TN_IN), lambda i, j: (0, 0)),
                  pl.BlockSpec((1, _TN_IN), lambda i, j: (0, 0)),
                  pl.BlockSpec((_TN_IN, _TN_IN), lambda i, j: (0, 0))],
        out_specs=pl.BlockSpec((tm, _TN_IN), lambda i, j: (i, j)),
        out_shape=jax.ShapeDtypeStruct((n, IN_COLS), f32),
        scratch_shapes=[pltpu.VMEM((tm, D), bf16), pltpu.VMEM((tm, D), bf16)],
        compiler_params=_cparams(("arbitrary", "arbitrary"), 56),
        name="in_projection",
    )(x2d, x2d, mods, mods, norm_w, w_in, qn_t, kn_t, gmat)


def _softmax_pv(s_list, v_list):
    m = s_list[0].max(axis=-1, keepdims=True)
    for s in s_list[1:]:
        m = jnp.maximum(m, s.max(axis=-1, keepdims=True))
    o = None
    for s, v in zip(s_list, v_list):
        os_ = jnp.dot(jnp.exp(s - m).astype(bf16), v, preferred_element_type=f32)
        o = os_ if o is None else o + os_
    return o


def _normalise_pair(o0, o1, lane):
    left = lane < HEAD_DIM
    den = pltpu.roll(jnp.where(left, o1, o0), HEAD_DIM, 1)
    return jnp.where(left, o0, o1) / den


def _nt_dot(a, b):
    return lax.dot_general(a, b, (((1,), (1,)), ((), ())), preferred_element_type=f32)


_QK_SCALE = HEAD_DIM ** -0.5


def _ctx_attn_body(q_ref, k_ref, v_ref, o_ref, nk_ref, nv_ref):
    lane = lax.broadcasted_iota(i32, (1, LANES), 1)
    for p in range(ATT_W // LANES):
        cols = slice(p * LANES, (p + 1) * LANES)
        q = q_ref[:, cols] * _QK_SCALE
        kf = k_ref[:, cols]
        k = kf.astype(bf16)
        v = v_ref[:, cols]
        nk_ref[0, 0, 2 * p] = kf[:, :HEAD_DIM]
        nk_ref[0, 0, 2 * p + 1] = pltpu.roll(kf, HEAD_DIM, 1)[:, :HEAD_DIM]
        nv_ref[0, 0, 2 * p] = v[:, :HEAD_DIM]
        nv_ref[0, 0, 2 * p + 1] = pltpu.roll(v, HEAD_DIM, 1)[:, :HEAD_DIM]
        heads = []
        for hh in range(2):
            hm = (lane >= hh * HEAD_DIM) & (lane < (hh + 1) * HEAD_DIM)
            qh = jnp.where(hm, q, 0.0).astype(bf16)
            vh = jnp.where(hm, v, 1.0).astype(bf16)
            heads.append(_softmax_pv([_nt_dot(qh, k)], [vh]))
        o_ref[:, cols] = _normalise_pair(heads[0], heads[1], lane).astype(bf16)


def _ctx_attention(proj, batch, seq):
    n = proj.shape[0]
    cache = pl.BlockSpec((1, 1, N_HEADS, seq, HEAD_DIM), lambda b: (b, 0, 0, 0, 0))
    cache_shape = jax.ShapeDtypeStruct((batch, 1, N_HEADS, seq, HEAD_DIM), f32)
    return pl.pallas_call(
        _ctx_attn_body,
        grid=(batch,),
        in_specs=[pl.BlockSpec((seq, ATT_W), lambda b: (b, 0)),
                  pl.BlockSpec((seq, ATT_W), lambda b: (b, 1)),
                  pl.BlockSpec((seq, ATT_W), lambda b: (b, 2))],
        out_specs=[pl.BlockSpec((seq, ATT_W), lambda b: (b, 0)), cache, cache],
        out_shape=[jax.ShapeDtypeStruct((n, ATT_W), bf16), cache_shape, cache_shape],
        compiler_params=_cparams(("arbitrary",), 32),
        name="context_attention",
    )(proj, proj, proj)


_NA_ROWS = 16
_NA_QROWS = 8
_NA_KROWS = 12


def _pad_rpb(rpb):
    nd, nc = 2 * WIN_H - 1, 2 * WIN_W - 1
    return jnp.pad(rpb.astype(f32), ((0, 0), (0, 2 * SUBLANES - nd), (0, LANES - nc)))


def _build_na_bias(rpb_ref, bias_ref):
    qc = lax.broadcasted_iota(i32, (GRID_W, LANES), 0)
    lane = lax.broadcasted_iota(i32, (GRID_W, LANES), 1)
    kc = lane & (GRID_W - 1)
    cs = jnp.clip(qc - WIN_W // 2, 0, GRID_W - WIN_W)
    col_ok = (kc >= cs) & (kc < cs + WIN_W)
    left = lane < GRID_W
    neg = jnp.full((GRID_W, LANES), NEG_INF, f32)
    for hh in range(2):
        tbl = rpb_ref[hh]
        tiles = {}

        def toeplitz(d, odd):
            if (d, odd) not in tiles:
                row = jnp.broadcast_to(tbl[d:d + 1, :], (GRID_W, LANES))
                shift = (LANES - (WIN_W - 1) + odd * GRID_W) % LANES
                t = pltpu.roll(row, shift, 1, stride=1, stride_axis=0)
                tiles[(d, odd)] = jnp.where(col_ok, t, NEG_INF)
            return tiles[(d, odd)]

        for j in range(2):
            for rq in range(_NA_QROWS):
                rq_abs = j * _NA_QROWS + rq
                rs = min(max(rq_abs - WIN_H // 2, 0), _NA_ROWS - WIN_H)
                for kp in range(_NA_KROWS // 2):
                    parts = []
                    for odd in range(2):
                        rk_abs = j * (_NA_ROWS - _NA_KROWS) + 2 * kp + odd
                        if rs <= rk_abs < rs + WIN_H:
                            parts.append(toeplitz(rk_abs - rq_abs + WIN_H - 1, odd))
                        else:
                            parts.append(neg)
                    bias_ref[hh, j, rq * GRID_W:(rq + 1) * GRID_W,
                             kp * LANES:(kp + 1) * LANES] = jnp.where(left, parts[0], parts[1])


def _lat_attn_body(q_ref, k_ref, v_ref, ck_ref, cv_ref, rpb_ref, o_ref, bias_ref):
    @pl.when(pl.program_id(1) == 0)
    def _():
        _build_na_bias(rpb_ref, bias_ref)

    lane = lax.broadcasted_iota(i32, (1, LANES), 1)
    nq = _NA_QROWS * GRID_W
    nk = _NA_KROWS * GRID_W
    kstep = (_NA_ROWS - _NA_KROWS) * GRID_W
    src = lax.broadcasted_iota(i32, (HEAD_DIM, LANES), 0)
    dst = lax.broadcasted_iota(i32, (HEAD_DIM, LANES), 1)
    ck = None
    cv_h = []
    for hh in range(2):
        place = jnp.where(dst == src + hh * HEAD_DIM, 1.0, 0.0).astype(bf16)
        part = jnp.dot(ck_ref[0, 0, hh].astype(bf16), place, preferred_element_type=f32)
        ck = part if ck is None else ck + part
        placed = jnp.dot(cv_ref[0, 0, hh].astype(bf16), place, preferred_element_type=f32)
        own = (lane >= hh * HEAD_DIM) & (lane < (hh + 1) * HEAD_DIM)
        cv_h.append(jnp.where(own, placed, 1.0).astype(bf16))
    ck = ck.astype(bf16)
    for j in range(2):
        q = q_ref[j * nq:(j + 1) * nq, :] * _QK_SCALE
        k = k_ref[j * kstep:j * kstep + nk, :].astype(bf16)
        v = v_ref[j * kstep:j * kstep + nk, :]
        heads = []
        for hh in range(2):
            hm = (lane >= hh * HEAD_DIM) & (lane < (hh + 1) * HEAD_DIM)
            qh = jnp.where(hm, q, 0.0).astype(bf16)
            s_win = _nt_dot(qh, k) + bias_ref[hh, j]
            s_ctx = _nt_dot(qh, ck)
            vh = jnp.where(hm, v, 1.0).astype(bf16)
            heads.append(_softmax_pv([s_win, s_ctx], [vh, cv_h[hh]]))
        o_ref[j * nq:(j + 1) * nq, :] = _normalise_pair(heads[0], heads[1], lane).astype(bf16)


def _lat_attention(proj, ck, cv, rpb_p, layer, batch, seq):
    n = proj.shape[0]
    past = ck.shape[3]
    cache = pl.BlockSpec((1, 1, 2, past, HEAD_DIM), lambda p, b: (b, layer, p, 0, 0))
    npair = ATT_W // LANES
    nq = _NA_QROWS * GRID_W
    nk = _NA_KROWS * GRID_W
    return pl.pallas_call(
        _lat_attn_body,
        grid=(npair, batch),
        in_specs=[pl.BlockSpec((seq, LANES), lambda p, b: (b, p)),
                  pl.BlockSpec((seq, LANES), lambda p, b: (b, npair + p)),
                  pl.BlockSpec((seq, LANES), lambda p, b: (b, 2 * npair + p)),
                  cache, cache,
                  pl.BlockSpec((2, 2 * SUBLANES, LANES), lambda p, b: (p, 0, 0))],
        out_specs=pl.BlockSpec((seq, LANES), lambda p, b: (b, p)),
        out_shape=jax.ShapeDtypeStruct((n, ATT_W), bf16),
        scratch_shapes=[pltpu.VMEM((2, 2, nq, nk), f32)],
        compiler_params=_cparams(("arbitrary", "arbitrary"), 48),
        name="neighbourhood_attention",
    )(proj, proj, proj, ck, cv, rpb_p)


_LRU_TC = 256
_GATE_ROWS = 256
_LRU_HALVES = _LRU_TC // LANES


def _gelu_tanh(x):
    c = 0.7978845608028654
    return x * (0.5 * (1.0 + jnp.tanh(c * (x + 0.044715 * (x * x * x)))))


def _lru_body(xr_ref, gr_ref, w_ref, cw_ref, cb_ref, ba_ref, bx_ref, lam_ref, h0f_ref, h0b_ref,
              y_ref, htf_ref, htb_ref, xc_ref, af_ref, uf_ref, ab_ref, ub_ref, *, T, nb):
    nslab = _LRU_HALVES * nb // SUBLANES
    R = nb * T
    x = xr_ref[...]
    t = lax.broadcasted_iota(i32, (R, 1), 0) & (T - 1)
    cw = cw_ref[...]
    xc = jnp.where(t >= 2, pltpu.roll(x, 2, 0), 0.0) * cw[0:1]
    xc = xc + jnp.where(t >= 1, pltpu.roll(x, 1, 0), 0.0) * cw[1:2]
    xc = xc + x * cw[2:3]
    xc = xc + jnp.where(t < T - 1, pltpu.roll(x, R - 1, 0), 0.0) * cw[3:4]
    xc_ref[...] = xc + cb_ref[...]

    nl = -lam_ref[...]
    sp = jnp.maximum(nl, 0.0) + jnp.log1p(jnp.exp(-jnp.abs(nl)))
    half_c_sp = (0.5 * LRU_C) * sp
    w = w_ref[0]
    dirs = ((af_ref, uf_ref), (ab_ref, ub_ref))
    for b in range(nb):
        for ci in range(T // _GATE_ROWS):
            r0 = b * T + ci * _GATE_ROWS
            xcc = xc_ref[r0:r0 + _GATE_ROWS, :]
            half_x = 0.5 * xcc
            z = jnp.dot(xcc.astype(bf16), w, preferred_element_type=f32)
            for d, (a_ref, u_ref) in enumerate(dirs):
                c0 = d * 2 * _LRU_TC
                tr = jnp.tanh(0.5 * (z[:, c0:c0 + _LRU_TC] + ba_ref[d:d + 1]))
                tg = jnp.tanh(0.5 * (z[:, c0 + _LRU_TC:c0 + 2 * _LRU_TC] + bx_ref[d:d + 1]))
                neg_log_a = (1.0 + tr) * half_c_sp[d:d + 1]
                a = jnp.exp(-neg_log_a)
                one_m_a2 = jnp.tanh(neg_log_a) * (1.0 + a * a)
                root = jnp.where(one_m_a2 > 0.0, one_m_a2 * lax.rsqrt(one_m_a2), 0.0)
                u = root * ((1.0 + tg) * half_x)
                for hf in range(_LRU_HALVES):
                    s, sub = divmod(hf * nb + b, SUBLANES)
                    rows = pl.ds(ci * _GATE_ROWS * SUBLANES + sub, _GATE_ROWS, stride=SUBLANES)
                    a_ref[s, rows, :] = a[:, hf * LANES:(hf + 1) * LANES]
                    u_ref[s, rows, :] = u[:, hf * LANES:(hf + 1) * LANES]

    def step(i, carry):
        rf = pl.ds(pl.multiple_of(i * SUBLANES, SUBLANES), SUBLANES)
        rb = pl.ds(pl.multiple_of((T - 1 - i) * SUBLANES, SUBLANES), SUBLANES)
        new = []
        for s in range(nslab):
            hf_ = af_ref[s, rf, :] * carry[2 * s] + uf_ref[s, rf, :]
            uf_ref[s, rf, :] = hf_
            hb_ = ab_ref[s, rb, :] * carry[2 * s + 1] + ub_ref[s, rb, :]
            ub_ref[s, rb, :] = hb_
            new += [hf_, hb_]
        return tuple(new)

    init = []
    for s in range(nslab):
        init += [h0f_ref[:, s * LANES:(s + 1) * LANES], h0b_ref[:, s * LANES:(s + 1) * LANES]]
    fin = lax.fori_loop(0, T, step, tuple(init), unroll=8)
    for s in range(nslab):
        htf_ref[:, s * LANES:(s + 1) * LANES] = fin[2 * s]
        htb_ref[:, s * LANES:(s + 1) * LANES] = fin[2 * s + 1]

    for b in range(nb):
        for hf in range(_LRU_HALVES):
            s, sub = divmod(hf * nb + b, SUBLANES)
            rows = pl.ds(sub, T, stride=SUBLANES)
            hs = uf_ref[s, rows, :] + ub_ref[s, rows, :]
            g = gr_ref[b * T:(b + 1) * T, hf * LANES:(hf + 1) * LANES]
            y_ref[b * T:(b + 1) * T, hf * LANES:(hf + 1) * LANES] = (
                hs * _gelu_tanh(g)).astype(bf16)


def _rglru(proj, wcat, conv_w, conv_b, b_a, b_x, lam, h0f, h0b, T, nb):
    n = proj.shape[0]
    nslab = _LRU_HALVES * nb // SUBLANES
    W = nslab * LANES
    R = nb * T
    nct = LRU_W // _LRU_TC
    xr0 = 3 * ATT_W // _LRU_TC
    gr0 = xr0 + nct
    chan = lambda k: pl.BlockSpec((k, _LRU_TC), lambda g, j: (0, j))
    slab = pl.BlockSpec((SUBLANES, W), lambda g, j: (g, j))
    body = functools.partial(_lru_body, T=T, nb=nb)
    return pl.pallas_call(
        body,
        grid=(n // R, nct),
        in_specs=[pl.BlockSpec((R, _LRU_TC), lambda g, j: (g, xr0 + j)),
                  pl.BlockSpec((R, _LRU_TC), lambda g, j: (g, gr0 + j)),
                  pl.BlockSpec((1, _LRU_TC, 4 * _LRU_TC), lambda g, j: (j, 0, 0)),
                  chan(4), chan(1), chan(2), chan(2), chan(2), slab, slab],
        out_specs=[pl.BlockSpec((R, _LRU_TC), lambda g, j: (g, j)), slab, slab],
        out_shape=[jax.ShapeDtypeStruct((n, LRU_W), bf16),
                   jax.ShapeDtypeStruct(h0f.shape, f32),
                   jax.ShapeDtypeStruct(h0b.shape, f32)],
        scratch_shapes=[pltpu.VMEM((R, _LRU_TC), f32)]
        + [pltpu.VMEM((nslab, T * SUBLANES, LANES), f32) for _ in range(4)],
        compiler_params=_cparams(("arbitrary", "arbitrary"), 56),
        name="rg_lru",
    )(proj, proj, wcat, conv_w, conv_b, b_a, b_x, lam, h0f, h0b)


def _lru_gate_weights(w_a, w_x):
    per_tile = _LRU_TC // LRU_BLOCK
    nct = LRU_W // _LRU_TC
    eye = jnp.eye(per_tile, dtype=f32)

    def bd(w):
        w = w.reshape(nct, per_tile, LRU_BLOCK, LRU_BLOCK)
        m = w[:, :, :, None, :] * eye[None, :, None, :, None]
        return m.reshape(nct, _LRU_TC, _LRU_TC)

    return jnp.concatenate([bd(w_a[0]), bd(w_x[0]), bd(w_a[1]), bd(w_x[1])], axis=-1).astype(bf16)


def _outproj_body(att_ref, lru_ref, w_ref, x_ref, mods_ref, nw_ref, wr_ref,
                  x1_ref, h2_ref, aff_ref):
    mix = (jnp.dot(att_ref[...], w_ref[0:ATT_W, :], preferred_element_type=f32)
           + jnp.dot(lru_ref[...], w_ref[ATT_W:D, :], preferred_element_type=f32))
    m = mods_ref[0]
    x1 = x_ref[...] + m[2:3] * mix
    x1_ref[...] = x1
    ms = jnp.mean(x1 * x1, axis=-1, keepdims=True)
    h2 = (x1 * lax.rsqrt(ms + EPS) * nw_ref[...]) * (1.0 + m[4:5]) + m[3:4]
    h2_ref[...] = pltpu.pack_elementwise([h2[:, :D // 2], h2[:, D // 2:]], packed_dtype=bf16)
    logits = _nt_dot(wr_ref[...], h2.astype(bf16))
    e = jnp.exp(logits - logits.max(axis=0, keepdims=True))
    aff_ref[...] = e / e.sum(axis=0, keepdims=True)


def _outproj(att, lru, w_out_b, x2d, mods, norm_w, w_router_t, tiles_per_row):
    n = x2d.shape[0]
    tm = 512
    row = (lambda i: 1 + i // tiles_per_row) if tiles_per_row else (lambda i: 0)
    return pl.pallas_call(
        _outproj_body,
        grid=(n // tm,),
        in_specs=[pl.BlockSpec((tm, ATT_W), lambda i: (i, 0)),
                  pl.BlockSpec((tm, LRU_W), lambda i: (i, 0)),
                  pl.BlockSpec((D, D), lambda i: (0, 0)),
                  pl.BlockSpec((tm, D), lambda i: (i, 0)),
                  pl.BlockSpec((1, 6, D), lambda i: (row(i), 0, 0)),
                  pl.BlockSpec((1, D), lambda i: (0, 0)),
                  pl.BlockSpec((N_EXPERTS, D), lambda i: (0, 0))],
        out_specs=[pl.BlockSpec((tm, D), lambda i: (i, 0)),
                   pl.BlockSpec((tm, D // 2), lambda i: (i, 0)),
                   pl.BlockSpec((N_EXPERTS, tm), lambda i: (0, i))],
        out_shape=[jax.ShapeDtypeStruct((n, D), f32),
                   jax.ShapeDtypeStruct((n, D // 2), jnp.uint32),
                   jax.ShapeDtypeStruct((N_EXPERTS, n), f32)],
        compiler_params=_cparams(("arbitrary",), 52),
        name="out_projection_router",
    )(att, lru, w_out_b, x2d, mods, norm_w, w_router_t)


def _prefix_sum_lanes(x, lane, n):
    s = 1
    while s < n:
        x = x + jnp.where(lane >= s, pltpu.roll(x, s, 1), 0)
        s *= 2
    return x


def _route_body(aff_ref, idx_ref, g_ref, pos_ref, sp_ref, sel_ref, rem_ref, *, n, cap):
    aff = aff_ref[...]
    lane = lax.broadcasted_iota(i32, (N_EXPERTS, n), 1)

    def count(mask):
        return jnp.sum(jnp.where(mask, 1.0, 0.0), axis=1, keepdims=True).astype(i32)

    def as_value(bits):
        return pltpu.bitcast(jnp.broadcast_to(bits, (N_EXPERTS, LANES)), f32)[:, :1]

    def search(i, thr):
        cand = thr | jnp.left_shift(jnp.int32(1), 30 - i)
        return jnp.where(count(aff >= as_value(cand)) >= cap, cand, thr)

    thr = lax.fori_loop(0, 31, search, jnp.zeros((N_EXPERTS, 1), i32))
    sure = aff >= as_value(thr + 1)
    sel_ref[...] = sure.astype(i32)
    rem_ref[...] = ((aff >= as_value(thr)) & jnp.logical_not(sure)).astype(i32)

    def more(need):
        return jnp.max(need) > 0

    def take_largest(need):
        rem = rem_ref[...] > 0
        top = jnp.max(jnp.where(rem, aff, -1.0), axis=1, keepdims=True)
        is_top = rem & (aff == top) & (need > 0)
        rank = _prefix_sum_lanes(is_top.astype(i32), lane, n)
        pick = is_top & ((count(is_top) <= need) | (rank <= need))
        sel_ref[...] = sel_ref[...] | pick.astype(i32)
        rem_ref[...] = (rem & jnp.logical_not(is_top)).astype(i32)
        return need - count(pick)

    lax.while_loop(more, take_largest, cap - count(sure))
    sel_i = sel_ref[...]
    sel = sel_i > 0
    pos = _prefix_sum_lanes(sel_i, lane, n) - sel_i
    pos_ref[...] = pos
    sp_ref[...] = jnp.where(sel, pos, UNSELECTED)

    key = jnp.where(sel, lane - pos, -1)
    idx = lane
    g = aff
    b = 0
    while (1 << b) < n:
        s = 1 << b
        k_sh = pltpu.roll(key, n - s, 1)
        take = (k_sh >= 0) & ((jnp.right_shift(k_sh, b) & 1) == 1)
        stay = (key >= 0) & ((jnp.right_shift(key, b) & 1) == 0)
        idx = jnp.where(take, pltpu.roll(idx, n - s, 1), idx)
        g = jnp.where(take, pltpu.roll(g, n - s, 1), g)
        key = jnp.where(take, k_sh, jnp.where(stay, key, -1))
        b += 1
    idx_ref[...] = idx[:, :cap]
    g_ref[...] = g[:, :cap]


def _route(aff_t, cap):
    n = aff_t.shape[1]
    body = functools.partial(_route_body, n=n, cap=cap)
    full = lambda shape: pl.BlockSpec(shape, lambda: (0, 0))
    return pl.pallas_call(
        body,
        in_specs=[full((N_EXPERTS, n))],
        out_specs=[full((N_EXPERTS, cap)), full((N_EXPERTS, cap)),
                   full((N_EXPERTS, n)), full((N_EXPERTS, n))],
        out_shape=[jax.ShapeDtypeStruct((N_EXPERTS, cap), i32),
                   jax.ShapeDtypeStruct((N_EXPERTS, cap), f32),
                   jax.ShapeDtypeStruct((N_EXPERTS, n), i32),
                   jax.ShapeDtypeStruct((N_EXPERTS, n), i32)],
        scratch_shapes=[pltpu.VMEM((N_EXPERTS, n), i32), pltpu.VMEM((N_EXPERTS, n), i32)],
        compiler_params=pltpu.CompilerParams(vmem_limit_bytes=32 * _MIB),
        name="expert_choice_routing",
    )(aff_t)


_TF = 512
_TN_DOWN = 512


def _expert_body(idx_ref, h2a_hbm, h2b_hbm, g_ref, wg_ref, wu_ref, wd_ref, oa_ref, ob_ref,
                 xe_ref, xb_ref, he_ref, sem, *, caps, na, nd):
    e = pl.program_id(0)
    s = pl.program_id(1)
    srcs = (h2a_hbm, h2b_hbm)
    starts = (0, caps[0])
    rows_total = caps[0] + caps[1]

    def row_copy(grp, tok, r):
        return pltpu.make_async_copy(srcs[grp].at[pl.ds(tok, 1), :], xe_ref.at[pl.ds(r, 1), :], sem)

    def wait_all_rows():
        for grp in range(2):
            pltpu.make_async_copy(srcs[grp].at[pl.ds(0, caps[grp]), :],
                                  xe_ref.at[pl.ds(starts[grp], caps[grp]), :], sem).wait()

    @pl.when((e == 0) & (s == 0))
    def _():
        for grp in range(2):
            def issue(r, c, grp=grp):
                row_copy(grp, idx_ref[starts[grp] + r], starts[grp] + r).start()
                return c

            lax.fori_loop(0, caps[grp], issue, 0, unroll=8)

    @pl.when(s == 0)
    def _():
        wait_all_rows()
        packed = xe_ref[...]
        for part in range(2):
            xb_ref[:, part * (D // 2):(part + 1) * (D // 2)] = pltpu.unpack_elementwise(
                packed, index=part, packed_dtype=bf16, unpacked_dtype=f32).astype(bf16)

    @pl.when(s < na)
    def _():
        wg = wg_ref[0].astype(bf16)
        wu = wu_ref[0].astype(bf16)
        half = rows_total // 2
        for rows in (slice(0, half), slice(half, rows_total)):
            xb = xb_ref[rows, :]
            gp = jnp.dot(xb, wg, preferred_element_type=f32)
            up = jnp.dot(xb, wu, preferred_element_type=f32)
            he_ref[s, rows, :] = ((gp * jax.nn.sigmoid(gp)) * up).astype(bf16)

    @pl.when(s >= na)
    def _():
        base = jnp.minimum(e + 1, N_EXPERTS - 1) * rows_total
        for k in range(nd):
            @pl.when(s - na == k)
            def _(k=k):
                for grp in range(2):
                    per = caps[grp] // nd
                    for r in range(starts[grp] + k * per, starts[grp] + (k + 1) * per):
                        row_copy(grp, idx_ref[base + r], r).start(priority=r % 2)

        wd = wd_ref[0].astype(bf16)
        acc = None
        for k in range(na):
            part = jnp.dot(he_ref[k], wd[k * _TF:(k + 1) * _TF, :], preferred_element_type=f32)
            acc = part if acc is None else acc + part
        scaled = acc * g_ref[0]
        oa_ref[0] = scaled[:caps[0]]
        ob_ref[0] = scaled[caps[0]:]

    @pl.when((e == N_EXPERTS - 1) & (s == na + nd - 1))
    def _():
        wait_all_rows()


def _expert_ffn(idx_a, idx_b, h2a, h2b, g_a, g_b, w_gate, w_up, w_down):
    caps = (idx_a.shape[1], idx_b.shape[1])
    rows = caps[0] + caps[1]
    na = D_FF // _TF
    nd = D // _TN_DOWN
    idx = jnp.concatenate([idx_a, idx_b], axis=1).reshape(N_EXPERTS * rows)
    g3 = jnp.concatenate([g_a, g_b], axis=1)[:, :, None]
    body = functools.partial(_expert_body, caps=caps, na=na, nd=nd)
    up_tile = lambda e, s, idx: (e, 0, jnp.minimum(s, na - 1))
    down_tile = lambda e, s, idx: (e, 0, jnp.maximum(s - na, 0))
    grid_spec = pltpu.PrefetchScalarGridSpec(
        num_scalar_prefetch=1,
        grid=(N_EXPERTS, na + nd),
        in_specs=[pl.BlockSpec(memory_space=pl.ANY),
                  pl.BlockSpec(memory_space=pl.ANY),
                  pl.BlockSpec((1, rows, 1), lambda e, s, idx: (e, 0, 0)),
                  pl.BlockSpec((1, D, _TF), up_tile),
                  pl.BlockSpec((1, D, _TF), up_tile),
                  pl.BlockSpec((1, D_FF, _TN_DOWN), down_tile)],
        out_specs=[pl.BlockSpec((1, caps[0], _TN_DOWN), down_tile),
                   pl.BlockSpec((1, caps[1], _TN_DOWN), down_tile)],
        scratch_shapes=[pltpu.VMEM((rows, D // 2), jnp.uint32), pltpu.VMEM((rows, D), bf16),
                        pltpu.VMEM((na, rows, _TF), bf16), pltpu.SemaphoreType.DMA(())],
    )
    return pl.pallas_call(
        body,
        grid_spec=grid_spec,
        out_shape=[jax.ShapeDtypeStruct((N_EXPERTS, caps[0], D), f32),
                   jax.ShapeDtypeStruct((N_EXPERTS, caps[1], D), f32)],
        compiler_params=_cparams(("arbitrary", "arbitrary"), 60),
        name="expert_ffn",
    )(idx, h2a, h2b, g3, w_gate, w_up, w_down)


_TT = 128
_KB = 512
_RUN_MAX = _TT + 2 * SUBLANES
_RUN_PIECES = tuple(SUBLANES << i for i in reversed(range((_RUN_MAX // SUBLANES).bit_length())))
_STAGE_ROWS = pl.cdiv(N_EXPERTS * _RUN_MAX, _KB) * _KB
_STAGE_PIECES = tuple(SUBLANES << i
                      for i in reversed(range((_STAGE_ROWS // SUBLANES).bit_length())))


def _combine_body(offs_ref, ye_hbm, sp_ref, x1_ref, mods_ref, o_ref, stage_ref, acc_ref, sem,
                  *, cap):
    j = pl.program_id(0)
    slot = j & 1

    def wait_rows(total, sl):
        for size in _STAGE_PIECES:
            @pl.when((total & size) != 0)
            def _(size=size):
                pltpu.make_async_copy(ye_hbm.at[pl.ds(0, size), :],
                                      stage_ref.at[sl, pl.ds(0, size), :], sem.at[sl]).wait()

    def tile_copies(jj, sl, start):
        bases, offs = [], []
        base = jnp.int32(0)
        for e in range(N_EXPERTS):
            off = offs_ref[e, jj]
            end = offs_ref[e, jj + 1]
            off8 = (off // SUBLANES) * SUBLANES
            length = jnp.where(end > off, ((end + SUBLANES - 1) // SUBLANES) * SUBLANES - off8, 0)
            bases.append(base)
            offs.append(off8)
            src = e * cap + off8
            dst = base
            for size in _RUN_PIECES if start else ():
                piece = length & size

                @pl.when(piece != 0)
                def _(src=src, dst=dst, size=size):
                    pltpu.make_async_copy(
                        ye_hbm.at[pl.ds(pl.multiple_of(src, SUBLANES), size), :],
                        stage_ref.at[sl, pl.ds(pl.multiple_of(dst, SUBLANES), size), :],
                        sem.at[sl]).start()

                src = src + piece
                dst = dst + piece
            base = base + length
        return bases, offs, base

    @pl.when(j == 0)
    def _():
        stage_ref[...] = jnp.zeros_like(stage_ref)
        tile_copies(0, 0, True)

    @pl.when(j + 1 < pl.num_programs(0))
    def _():
        tile_copies(j + 1, 1 - slot, True)

    bases, offs, total = tile_copies(j, slot, False)
    wait_rows(total, slot)

    sp = sp_ref[...]
    kiota = lax.broadcasted_iota(i32, (1, _KB), 1)
    acc_ref[...] = jnp.zeros_like(acc_ref)

    def kblock(kb, c):
        hit = None
        for e in range(N_EXPERTS):
            local = sp[:, e:e + 1] + (bases[e] - offs[e] - kb * _KB)
            h = local == kiota
            hit = h if hit is None else (hit | h)
        s = jnp.where(hit, 1.0, 0.0).astype(bf16)
        rows = stage_ref[slot, pl.ds(pl.multiple_of(kb * _KB, _KB), _KB), :]
        hi = rows.astype(bf16)
        lo = (rows - hi.astype(f32)).astype(bf16)
        acc_ref[...] += (jnp.dot(s, hi, preferred_element_type=f32)
                         + jnp.dot(s, lo, preferred_element_type=f32))
        return c

    lax.fori_loop(0, (total + _KB - 1) // _KB, kblock, 0)
    o_ref[...] = x1_ref[...] + mods_ref[0][5:6] * acc_ref[...]


def _combine(offs, ye_flat, sp_n, x1, mods, cap, tiles_per_row):
    n = x1.shape[0]
    row = (lambda j: 1 + j // tiles_per_row) if tiles_per_row else (lambda j: 0)
    body = functools.partial(_combine_body, cap=cap)
    grid_spec = pltpu.PrefetchScalarGridSpec(
        num_scalar_prefetch=1,
        grid=(n // _TT,),
        in_specs=[pl.BlockSpec(memory_space=pl.ANY),
                  pl.BlockSpec((_TT, N_EXPERTS), lambda j, offs: (j, 0)),
                  pl.BlockSpec((_TT, D), lambda j, offs: (j, 0)),
                  pl.BlockSpec((1, 6, D), lambda j, offs: (row(j), 0, 0))],
        out_specs=pl.BlockSpec((_TT, D), lambda j, offs: (j, 0)),
        scratch_shapes=[pltpu.VMEM((2, _STAGE_ROWS, D), f32), pltpu.VMEM((_TT, D), f32),
                        pltpu.SemaphoreType.DMA((2,))],
    )
    return pl.pallas_call(
        body,
        grid_spec=grid_spec,
        out_shape=jax.ShapeDtypeStruct((n, D), f32),
        compiler_params=_cparams(("arbitrary",), 52),
        name="expert_combine",
    )(offs, ye_flat, sp_n, x1, mods)


def _moe(x1s, h2s, affs, mods, w_gate, w_up, w_down, tiles_per_row):
    routes = [_route(aff_t, CAPACITY_FACTOR * aff_t.shape[1] // N_EXPERTS) for aff_t in affs]
    (idx_a, g_a, _, _), (idx_b, g_b, _, _) = routes
    yes = _expert_ffn(idx_a, idx_b, h2s[0], h2s[1], g_a, g_b, w_gate, w_up, w_down)
    outs = []
    for ye, (idx, _, pos, sp), x1, tpr in zip(yes, routes, x1s, tiles_per_row):
        cap = idx.shape[1]
        offs = jnp.concatenate([pos[:, ::_TT], jnp.full((N_EXPERTS, 1), cap, i32)], axis=1)
        outs.append(_combine(offs, ye.reshape(N_EXPERTS * cap, D), sp.T, x1, mods, cap, tpr))
    return outs


def kernel(x_prompt, x_sample, cache_k, cache_v, state_fwd, state_bwd, c, c_ctx, w_mod, b_mod,
           norm_mix, w_in, q_norm, k_norm, rpb, conv_w, conv_b, lru_w_a, lru_b_a, lru_w_x,
           lru_b_x, lru_lambda, w_out, norm_ffn, w_router, w_gate, w_up, w_down):
    B, S, _ = x_prompt.shape
    DB, T, _ = x_sample.shape
    l = 0

    cvecs = jnp.concatenate([c_ctx[None, :], c, jnp.zeros((SUBLANES - 1 - DB, D), f32)], axis=0)
    mods = _adaln_mods(cvecs, w_mod[l], b_mod[l][None, :]).reshape(SUBLANES, 6, D)

    heads_per_tile = _TN_IN // HEAD_DIM
    qn_t = jnp.tile(q_norm[l], heads_per_tile)[None, :]
    kn_t = jnp.tile(k_norm[l], heads_per_tile)[None, :]
    gmat = (jnp.kron(jnp.eye(heads_per_tile, dtype=f32), jnp.ones((HEAD_DIM, HEAD_DIM), f32))
            / HEAD_DIM).astype(bf16)
    nmix = norm_mix[l][None, :]
    nffn = norm_ffn[l][None, :]
    w_in_b = w_in[l]
    w_out_b = w_out[l].astype(bf16)
    w_router_t = w_router[l].T.astype(bf16)
    wcat = _lru_gate_weights(lru_w_a[l], lru_w_x[l])
    cb = conv_b[l][None, :]

    xp = x_prompt.reshape(B * S, D)
    xs = x_sample.reshape(DB * T, D)

    proj_c = _inproj(xp, mods, nmix, w_in_b, qn_t, kn_t, gmat, per_tile_rows=False)
    att_c, new_k, new_v = _ctx_attention(proj_c, B, S)
    zeros_c = jnp.zeros((B, LRU_W), f32)
    lru_c, htf, htb = _rglru(proj_c, wcat, conv_w[l], cb, lru_b_a[l], lru_b_x[l], lru_lambda[l],
                             zeros_c, zeros_c, T=S, nb=SUBLANES)
    x1_c, h2_c, aff_c = _outproj(att_c, lru_c, w_out_b, xp, mods, nffn, w_router_t, 0)

    nct = LRU_W // _LRU_TC

    def slabs(h0):
        return (h0.reshape(DB, nct, _LRU_HALVES, LANES).transpose(2, 0, 1, 3)
                .reshape(_LRU_HALVES * DB, nct * LANES))

    proj_l = _inproj(xs, mods, nmix, w_in_b, qn_t, kn_t, gmat, per_tile_rows=True)
    att_l = _lat_attention(proj_l, cache_k, cache_v, _pad_rpb(rpb[l]), l, DB, T)
    h0f_l = slabs(state_fwd[:, l])
    h0b_l = slabs(state_bwd[:, l])
    lru_l, _, _ = _rglru(proj_l, wcat, conv_w[l], cb, lru_b_a[l], lru_b_x[l], lru_lambda[l],
                         h0f_l, h0b_l, T=T, nb=DB)
    x1_l, h2_l, aff_l = _outproj(att_l, lru_l, w_out_b, xs, mods, nffn, w_router_t, T // 512)

    y_c, y_l = _moe((x1_c, x1_l), (h2_c, h2_l), (aff_c, aff_l), mods,
                    w_gate[l], w_up[l], w_down[l], (0, T // _TT))

    return (y_c.reshape(B, S, D), y_l.reshape(DB, T, D), new_k, new_v,
            htf[:, None, :], htb[:, None, :])
```

```python
import functools

import jax
import jax.numpy as jnp
from jax import lax
from jax.experimental import pallas as pl
from jax.experimental.pallas import tpu as pltpu

f32 = jnp.float32
bf16 = jnp.bfloat16
i32 = jnp.int32

D = 2048
N_HEADS = 16
HEAD_DIM = 64
ATT_W = N_HEADS * HEAD_DIM
LRU_W = D - ATT_W
LRU_BLOCK = 64
LRU_C = 8.0
GRID_W = 64
WIN_H = 8
WIN_W = 16
N_EXPERTS = 16
CAPACITY_FACTOR = 2
D_FF = 2048
EPS = 1e-6
NEG_INF = -1e30
IN_COLS = 3 * ATT_W + 2 * LRU_W

LANES = 128
SUBLANES = 8
MXU_DIM = 256
UNSELECTED = -(1 << 30)

_MIB = 1024 * 1024


def _cparams(sem, vmem_mib):
    return pltpu.CompilerParams(dimension_semantics=sem, vmem_limit_bytes=vmem_mib * _MIB)


def _mods_body(c_ref, w_ref, b_ref, o_ref):
    c = c_ref[...]
    s = (c * jax.nn.sigmoid(c)).astype(bf16)
    o_ref[...] = jnp.dot(s, w_ref[...].astype(bf16), preferred_element_type=f32) + b_ref[...]


def _adaln_mods(cvecs, w_mod, b_mod):
    tn = 1024
    return pl.pallas_call(
        _mods_body,
        grid=(6 * D // tn,),
        in_specs=[pl.BlockSpec((SUBLANES, D), lambda j: (0, 0)),
                  pl.BlockSpec((D, tn), lambda j: (0, j)),
                  pl.BlockSpec((1, tn), lambda j: (0, j))],
        out_specs=pl.BlockSpec((SUBLANES, tn), lambda j: (0, j)),
        out_shape=jax.ShapeDtypeStruct((SUBLANES, 6 * D), f32),
        compiler_params=_cparams(("arbitrary",), 40),
        name="adaln_mods",
    )(cvecs, w_mod, b_mod)


_TN_IN = 512
_NORM_ROWS = 16
_NORM_SLICES = 8


def _inproj_body(x0_ref, xn_ref, mods0_ref, modsn_ref, nw_ref, w_ref, qn_ref, kn_ref, g_ref,
                 o_ref, ha_ref, hb_ref):
    i = pl.program_id(0)
    j = pl.program_id(1)
    tm = ha_ref.shape[0]
    nw = nw_ref[...]
    n_qk_tiles = 2 * ATT_W // _TN_IN

    def norm_rows(x_ref, mods_ref, h_ref, r0):
        m = mods_ref[0]
        rows = pl.ds(pl.multiple_of(r0, _NORM_ROWS), _NORM_ROWS)
        x = x_ref[rows, :]
        ms = jnp.mean(x * x, axis=-1, keepdims=True)
        y = x * lax.rsqrt(ms + EPS) * nw
        h_ref[rows, :] = (y * (1.0 + m[1:2]) + m[0:1]).astype(bf16)

    @pl.when((i == 0) & (j == 0))
    def _():
        def first(c, carry):
            norm_rows(x0_ref, mods0_ref, ha_ref, c * _NORM_ROWS)
            return carry

        lax.fori_loop(0, tm // _NORM_ROWS, first, 0, unroll=4)

    def column_step(cur_ref, nxt_ref):
        slice_rows = tm // _NORM_SLICES
        r_slice = jnp.minimum(j, _NORM_SLICES - 1) * slice_rows
        for c in range(slice_rows // _NORM_ROWS):
            norm_rows(xn_ref, modsn_ref, nxt_ref, r_slice + c * _NORM_ROWS)

        acc = jnp.dot(cur_ref[...], w_ref[...].astype(bf16), preferred_element_type=f32)

        @pl.when(j < n_qk_tiles)
        def _():
            sq = acc * acc
            hi = sq.astype(bf16)
            lo = (sq - hi.astype(f32)).astype(bf16)
            g = g_ref[...]
            ms = (jnp.dot(hi, g, preferred_element_type=f32)
                  + jnp.dot(lo, g, preferred_element_type=f32))
            head_w = jnp.where(j < n_qk_tiles // 2, qn_ref[...], kn_ref[...])
            o_ref[...] = acc * lax.rsqrt(ms + EPS) * head_w

        @pl.when(j >= n_qk_tiles)
        def _():
            o_ref[...] = acc

    @pl.when((i & 1) == 0)
    def _():
        column_step(ha_ref, hb_ref)

    @pl.when((i & 1) == 1)
    def _():
        column_step(hb_ref, ha_ref)


def _inproj(x2d, mods, norm_w, w_in, qn_t, kn_t, gmat, per_tile_rows):
    n = x2d.shape[0]
    tm = 1024
    last = n // tm - 1
    row = (lambda i: 1 + i) if per_tile_rows else (lambda i: 0)
    nxt = lambda i: jnp.minimum(i + 1, last)
    return pl.pallas_call(
        _inproj_body,
        grid=(n // tm, IN_COLS // _TN_IN),
        in_specs=[pl.BlockSpec((tm, D), lambda i, j: (0, 0)),
                  pl.BlockSpec((tm, D), lambda i, j: (nxt(i), 0)),
                  pl.BlockSpec((1, 6, D), lambda i, j: (row(0), 0, 0)),
                  pl.BlockSpec((1, 6, D), lambda i, j: (row(nxt(i)), 0, 0)),
                  pl.BlockSpec((1, D), lambda i, j: (0, 0)),
                  pl.BlockSpec((D, _TN_IN), lambda i, j: (0, j)),
                  pl.BlockSpec((1, _TN_IN), lambda i, j: (0, 0)),
                  pl.BlockSpec((1, _TN_IN), lambda i, j: (0, 0)),
                  pl.BlockSpec((_TN_IN, _TN_IN), lambda i, j: (0, 0))],
        out_specs=pl.BlockSpec((tm, _TN_IN), lambda i, j: (i, j)),
        out_shape=jax.ShapeDtypeStruct((n, IN_COLS), f32),
        scratch_shapes=[pltpu.VMEM((tm, D), bf16), pltpu.VMEM((tm, D), bf16)],
        compiler_params=_cparams(("arbitrary", "arbitrary"), 56),
        name="in_projection",
    )(x2d, x2d, mods, mods, norm_w, w_in, qn_t, kn_t, gmat)


def _softmax_pv(s_list, v_list):
    m = s_list[0].max(axis=-1, keepdims=True)
    for s in s_list[1:]:
        m = jnp.maximum(m, s.max(axis=-1, keepdims=True))
    o = None
    for s, v in zip(s_list, v_list):
        os_ = jnp.dot(jnp.exp(s - m).astype(bf16), v, preferred_element_type=f32)
        o = os_ if o is None else o + os_
    return o


def _normalise_pair(o0, o1, lane):
    left = lane < HEAD_DIM
    den = pltpu.roll(jnp.where(left, o1, o0), HEAD_DIM, 1)
    return jnp.where(left, o0, o1) / den


def _nt_dot(a, b):
    return lax.dot_general(a, b, (((1,), (1,)), ((), ())), preferred_element_type=f32)


_QK_SCALE = HEAD_DIM ** -0.5


def _ctx_attn_body(q_ref, k_ref, v_ref, o_ref, nk_ref, nv_ref):
    lane = lax.broadcasted_iota(i32, (1, LANES), 1)
    for p in range(ATT_W // LANES):
        cols = slice(p * LANES, (p + 1) * LANES)
        q = q_ref[:, cols] * _QK_SCALE
        kf = k_ref[:, cols]
        k = kf.astype(bf16)
        v = v_ref[:, cols]
        nk_ref[0, 0, 2 * p] = kf[:, :HEAD_DIM]
        nk_ref[0, 0, 2 * p + 1] = pltpu.roll(kf, HEAD_DIM, 1)[:, :HEAD_DIM]
        nv_ref[0, 0, 2 * p] = v[:, :HEAD_DIM]
        nv_ref[0, 0, 2 * p + 1] = pltpu.roll(v, HEAD_DIM, 1)[:, :HEAD_DIM]
        heads = []
        for hh in range(2):
            hm = (lane >= hh * HEAD_DIM) & (lane < (hh + 1) * HEAD_DIM)
            qh = jnp.where(hm, q, 0.0).astype(bf16)
            vh = jnp.where(hm, v, 1.0).astype(bf16)
            heads.append(_softmax_pv([_nt_dot(qh, k)], [vh]))
        o_ref[:, cols] = _normalise_pair(heads[0], heads[1], lane).astype(bf16)


def _ctx_attention(proj, batch, seq):
    n = proj.shape[0]
    cache = pl.BlockSpec((1, 1, N_HEADS, seq, HEAD_DIM), lambda b: (b, 0, 0, 0, 0))
    cache_shape = jax.ShapeDtypeStruct((batch, 1, N_HEADS, seq, HEAD_DIM), f32)
    return pl.pallas_call(
        _ctx_attn_body,
        grid=(batch,),
        in_specs=[pl.BlockSpec((seq, ATT_W), lambda b: (b, 0)),
                  pl.BlockSpec((seq, ATT_W), lambda b: (b, 1)),
                  pl.BlockSpec((seq, ATT_W), lambda b: (b, 2))],
        out_specs=[pl.BlockSpec((seq, ATT_W), lambda b: (b, 0)), cache, cache],
        out_shape=[jax.ShapeDtypeStruct((n, ATT_W), bf16), cache_shape, cache_shape],
        compiler_params=_cparams(("arbitrary",), 32),
        name="context_attention",
    )(proj, proj, proj)


_NA_ROWS = 16
_NA_QROWS = 8
_NA_KROWS = 12


def _pad_rpb(rpb):
    nd, nc = 2 * WIN_H - 1, 2 * WIN_W - 1
    return jnp.pad(rpb.astype(f32), ((0, 0), (0, 2 * SUBLANES - nd), (0, LANES - nc)))


def _build_na_bias(rpb_ref, bias_ref):
    qc = lax.broadcasted_iota(i32, (GRID_W, LANES), 0)
    lane = lax.broadcasted_iota(i32, (GRID_W, LANES), 1)
    kc = lane & (GRID_W - 1)
    cs = jnp.clip(qc - WIN_W // 2, 0, GRID_W - WIN_W)
    col_ok = (kc >= cs) & (kc < cs + WIN_W)
    left = lane < GRID_W
    neg = jnp.full((GRID_W, LANES), NEG_INF, f32)
    for hh in range(2):
        tbl = rpb_ref[hh]
        tiles = {}

        def toeplitz(d, odd):
            if (d, odd) not in tiles:
                row = jnp.broadcast_to(tbl[d:d + 1, :], (GRID_W, LANES))
                shift = (LANES - (WIN_W - 1) + odd * GRID_W) % LANES
                t = pltpu.roll(row, shift, 1, stride=1, stride_axis=0)
                tiles[(d, odd)] = jnp.where(col_ok, t, NEG_INF)
            return tiles[(d, odd)]

        for j in range(2):
            for rq in range(_NA_QROWS):
                rq_abs = j * _NA_QROWS + rq
                rs = min(max(rq_abs - WIN_H // 2, 0), _NA_ROWS - WIN_H)
                for kp in range(_NA_KROWS // 2):
                    parts = []
                    for odd in range(2):
                        rk_abs = j * (_NA_ROWS - _NA_KROWS) + 2 * kp + odd
                        if rs <= rk_abs < rs + WIN_H:
                            parts.append(toeplitz(rk_abs - rq_abs + WIN_H - 1, odd))
                        else:
                            parts.append(neg)
                    bias_ref[hh, j, rq * GRID_W:(rq + 1) * GRID_W,
                             kp * LANES:(kp + 1) * LANES] = jnp.where(left, parts[0], parts[1])


def _lat_attn_body(q_ref, k_ref, v_ref, ck_ref, cv_ref, rpb_ref, o_ref, bias_ref):
    @pl.when(pl.program_id(1) == 0)
    def _():
        _build_na_bias(rpb_ref, bias_ref)

    lane = lax.broadcasted_iota(i32, (1, LANES), 1)
    nq = _NA_QROWS * GRID_W
    nk = _NA_KROWS * GRID_W
    kstep = (_NA_ROWS - _NA_KROWS) * GRID_W
    src = lax.broadcasted_iota(i32, (HEAD_DIM, LANES), 0)
    dst = lax.broadcasted_iota(i32, (HEAD_DIM, LANES), 1)
    ck = None
    cv_h = []
    for hh in range(2):
        place = jnp.where(dst == src + hh * HEAD_DIM, 1.0, 0.0).astype(bf16)
        part = jnp.dot(ck_ref[0, 0, hh].astype(bf16), place, preferred_element_type=f32)
        ck = part if ck is None else ck + part
        placed = jnp.dot(cv_ref[0, 0, hh].astype(bf16), place, preferred_element_type=f32)
        own = (lane >= hh * HEAD_DIM) & (lane < (hh + 1) * HEAD_DIM)
        cv_h.append(jnp.where(own, placed, 1.0).astype(bf16))
    ck = ck.astype(bf16)
    for j in range(2):
        q = q_ref[j * nq:(j + 1) * nq, :] * _QK_SCALE
        k = k_ref[j * kstep:j * kstep + nk, :].astype(bf16)
        v = v_ref[j * kstep:j * kstep + nk, :]
        heads = []
        for hh in range(2):
            hm = (lane >= hh * HEAD_DIM) & (lane < (hh + 1) * HEAD_DIM)
            qh = jnp.where(hm, q, 0.0).astype(bf16)
            s_win = _nt_dot(qh, k) + bias_ref[hh, j]
            s_ctx = _nt_dot(qh, ck)
            vh = jnp.where(hm, v, 1.0).astype(bf16)
            heads.append(_softmax_pv([s_win, s_ctx], [vh, cv_h[hh]]))
        o_ref[j * nq:(j + 1) * nq, :] = _normalise_pair(heads[0], heads[1], lane).astype(bf16)


def _lat_attention(proj, ck, cv, rpb_p, layer, batch, seq):
    n = proj.shape[0]
    past = ck.shape[3]
    cache = pl.BlockSpec((1, 1, 2, past, HEAD_DIM), lambda p, b: (b, layer, p, 0, 0))
    npair = ATT_W // LANES
    nq = _NA_QROWS * GRID_W
    nk = _NA_KROWS * GRID_W
    return pl.pallas_call(
        _lat_attn_body,
        grid=(npair, batch),
        in_specs=[pl.BlockSpec((seq, LANES), lambda p, b: (b, p)),
                  pl.BlockSpec((seq, LANES), lambda p, b: (b, npair + p)),
                  pl.BlockSpec((seq, LANES), lambda p, b: (b, 2 * npair + p)),
                  cache, cache,
                  pl.BlockSpec((2, 2 * SUBLANES, LANES), lambda p, b: (p, 0, 0))],
        out_specs=pl.BlockSpec((seq, LANES), lambda p, b: (b, p)),
        out_shape=jax.ShapeDtypeStruct((n, ATT_W), bf16),
        scratch_shapes=[pltpu.VMEM((2, 2, nq, nk), f32)],
        compiler_params=_cparams(("arbitrary", "arbitrary"), 48),
        name="neighbourhood_attention",
    )(proj, proj, proj, ck, cv, rpb_p)


_LRU_TC = 256
_GATE_ROWS = 256
_LRU_HALVES = _LRU_TC // LANES


def _gelu_tanh(x):
    c = 0.7978845608028654
    return x * (0.5 * (1.0 + jnp.tanh(c * (x + 0.044715 * (x * x * x)))))


def _lru_body(xr_ref, gr_ref, w_ref, cw_ref, cb_ref, ba_ref, bx_ref, lam_ref, h0f_ref, h0b_ref,
              y_ref, htf_ref, htb_ref, xc_ref, af_ref, uf_ref, ab_ref, ub_ref, *, T, nb):
    nslab = _LRU_HALVES * nb // SUBLANES
    R = nb * T
    x = xr_ref[...]
    t = lax.broadcasted_iota(i32, (R, 1), 0) & (T - 1)
    cw = cw_ref[...]
    xc = jnp.where(t >= 2, pltpu.roll(x, 2, 0), 0.0) * cw[0:1]
    xc = xc + jnp.where(t >= 1, pltpu.roll(x, 1, 0), 0.0) * cw[1:2]
    xc = xc + x * cw[2:3]
    xc = xc + jnp.where(t < T - 1, pltpu.roll(x, R - 1, 0), 0.0) * cw[3:4]
    xc_ref[...] = xc + cb_ref[...]

    nl = -lam_ref[...]
    sp = jnp.maximum(nl, 0.0) + jnp.log1p(jnp.exp(-jnp.abs(nl)))
    half_c_sp = (0.5 * LRU_C) * sp
    w = w_ref[0]
    dirs = ((af_ref, uf_ref), (ab_ref, ub_ref))
    for b in range(nb):
        for ci in range(T // _GATE_ROWS):
            r0 = b * T + ci * _GATE_ROWS
            xcc = xc_ref[r0:r0 + _GATE_ROWS, :]
            half_x = 0.5 * xcc
            z = jnp.dot(xcc.astype(bf16), w, preferred_element_type=f32)
            for d, (a_ref, u_ref) in enumerate(dirs):
                c0 = d * 2 * _LRU_TC
                tr = jnp.tanh(0.5 * (z[:, c0:c0 + _LRU_TC] + ba_ref[d:d + 1]))
                tg = jnp.tanh(0.5 * (z[:, c0 + _LRU_TC:c0 + 2 * _LRU_TC] + bx_ref[d:d + 1]))
                neg_log_a = (1.0 + tr) * half_c_sp[d:d + 1]
                a = jnp.exp(-neg_log_a)
                one_m_a2 = jnp.tanh(neg_log_a) * (1.0 + a * a)
                root = jnp.where(one_m_a2 > 0.0, one_m_a2 * lax.rsqrt(one_m_a2), 0.0)
                u = root * ((1.0 + tg) * half_x)
                for hf in range(_LRU_HALVES):
                    s, sub = divmod(hf * nb + b, SUBLANES)
                    rows = pl.ds(ci * _GATE_ROWS * SUBLANES + sub, _GATE_ROWS, stride=SUBLANES)
                    a_ref[s, rows, :] = a[:, hf * LANES:(hf + 1) * LANES]
                    u_ref[s, rows, :] = u[:, hf * LANES:(hf + 1) * LANES]

    def step(i, carry):
        rf = pl.ds(pl.multiple_of(i * SUBLANES, SUBLANES), SUBLANES)
        rb = pl.ds(pl.multiple_of((T - 1 - i) * SUBLANES, SUBLANES), SUBLANES)
        new = []
        for s in range(nslab):
            hf_ = af_ref[s, rf, :] * carry[2 * s] + uf_ref[s, rf, :]
            uf_ref[s, rf, :] = hf_
            hb_ = ab_ref[s, rb, :] * carry[2 * s + 1] + ub_ref[s, rb, :]
            ub_ref[s, rb, :] = hb_
            new += [hf_, hb_]
        return tuple(new)

    init = []
    for s in range(nslab):
        init += [h0f_ref[:, s * LANES:(s + 1) * LANES], h0b_ref[:, s * LANES:(s + 1) * LANES]]
    fin = lax.fori_loop(0, T, step, tuple(init), unroll=8)
    for s in range(nslab):
        htf_ref[:, s * LANES:(s + 1) * LANES] = fin[2 * s]
        htb_ref[:, s * LANES:(s + 1) * LANES] = fin[2 * s + 1]

    for b in range(nb):
        for hf in range(_LRU_HALVES):
            s, sub = divmod(hf * nb + b, SUBLANES)
            rows = pl.ds(sub, T, stride=SUBLANES)
            hs = uf_ref[s, rows, :] + ub_ref[s, rows, :]
            g = gr_ref[b * T:(b + 1) * T, hf * LANES:(hf + 1) * LANES]
            y_ref[b * T:(b + 1) * T, hf * LANES:(hf + 1) * LANES] = (
                hs * _gelu_tanh(g)).astype(bf16)


def _rglru(proj, wcat, conv_w, conv_b, b_a, b_x, lam, h0f, h0b, T, nb):
    n = proj.shape[0]
    nslab = _LRU_HALVES * nb // SUBLANES
    W = nslab * LANES
    R = nb * T
    nct = LRU_W // _LRU_TC
    xr0 = 3 * ATT_W // _LRU_TC
    gr0 = xr0 + nct
    chan = lambda k: pl.BlockSpec((k, _LRU_TC), lambda g, j: (0, j))
    slab = pl.BlockSpec((SUBLANES, W), lambda g, j: (g, j))
    body = functools.partial(_lru_body, T=T, nb=nb)
    return pl.pallas_call(
        body,
        grid=(n // R, nct),
        in_specs=[pl.BlockSpec((R, _LRU_TC), lambda g, j: (g, xr0 + j)),
                  pl.BlockSpec((R, _LRU_TC), lambda g, j: (g, gr0 + j)),
                  pl.BlockSpec((1, _LRU_TC, 4 * _LRU_TC), lambda g, j: (j, 0, 0)),
                  chan(4), chan(1), chan(2), chan(2), chan(2), slab, slab],
        out_specs=[pl.BlockSpec((R, _LRU_TC), lambda g, j: (g, j)), slab, slab],
        out_shape=[jax.ShapeDtypeStruct((n, LRU_W), bf16),
                   jax.ShapeDtypeStruct(h0f.shape, f32),
                   jax.ShapeDtypeStruct(h0b.shape, f32)],
        scratch_shapes=[pltpu.VMEM((R, _LRU_TC), f32)]
        + [pltpu.VMEM((nslab, T * SUBLANES, LANES), f32) for _ in range(4)],
        compiler_params=_cparams(("arbitrary", "arbitrary"), 56),
        name="rg_lru",
    )(proj, proj, wcat, conv_w, conv_b, b_a, b_x, lam, h0f, h0b)


def _lru_gate_weights(w_a, w_x):
    per_tile = _LRU_TC // LRU_BLOCK
    nct = LRU_W // _LRU_TC
    eye = jnp.eye(per_tile, dtype=f32)

    def bd(w):
        w = w.reshape(nct, per_tile, LRU_BLOCK, LRU_BLOCK)
        m = w[:, :, :, None, :] * eye[None, :, None, :, None]
        return m.reshape(nct, _LRU_TC, _LRU_TC)

    return jnp.concatenate([bd(w_a[0]), bd(w_x[0]), bd(w_a[1]), bd(w_x[1])], axis=-1).astype(bf16)


def _outproj_body(att_ref, lru_ref, w_ref, x_ref, mods_ref, nw_ref, wr_ref,
                  x1_ref, h2_ref, aff_ref):
    mix = (jnp.dot(att_ref[...], w_ref[0:ATT_W, :], preferred_element_type=f32)
           + jnp.dot(lru_ref[...], w_ref[ATT_W:D, :], preferred_element_type=f32))
    m = mods_ref[0]
    x1 = x_ref[...] + m[2:3] * mix
    x1_ref[...] = x1
    ms = jnp.mean(x1 * x1, axis=-1, keepdims=True)
    h2 = (x1 * lax.rsqrt(ms + EPS) * nw_ref[...]) * (1.0 + m[4:5]) + m[3:4]
    h2_ref[...] = pltpu.pack_elementwise([h2[:, :D // 2], h2[:, D // 2:]], packed_dtype=bf16)
    logits = _nt_dot(wr_ref[...], h2.astype(bf16))
    e = jnp.exp(logits - logits.max(axis=0, keepdims=True))
    aff_ref[...] = e / e.sum(axis=0, keepdims=True)


def _outproj(att, lru, w_out_b, x2d, mods, norm_w, w_router_t, tiles_per_row):
    n = x2d.shape[0]
    tm = 512
    row = (lambda i: 1 + i // tiles_per_row) if tiles_per_row else (lambda i: 0)
    return pl.pallas_call(
        _outproj_body,
        grid=(n // tm,),
        in_specs=[pl.BlockSpec((tm, ATT_W), lambda i: (i, 0)),
                  pl.BlockSpec((tm, LRU_W), lambda i: (i, 0)),
                  pl.BlockSpec((D, D), lambda i: (0, 0)),
                  pl.BlockSpec((tm, D), lambda i: (i, 0)),
                  pl.BlockSpec((1, 6, D), lambda i: (row(i), 0, 0)),
                  pl.BlockSpec((1, D), lambda i: (0, 0)),
                  pl.BlockSpec((N_EXPERTS, D), lambda i: (0, 0))],
        out_specs=[pl.BlockSpec((tm, D), lambda i: (i, 0)),
                   pl.BlockSpec((tm, D // 2), lambda i: (i, 0)),
                   pl.BlockSpec((N_EXPERTS, tm), lambda i: (0, i))],
        out_shape=[jax.ShapeDtypeStruct((n, D), f32),
                   jax.ShapeDtypeStruct((n, D // 2), jnp.uint32),
                   jax.ShapeDtypeStruct((N_EXPERTS, n), f32)],
        compiler_params=_cparams(("arbitrary",), 52),
        name="out_projection_router",
    )(att, lru, w_out_b, x2d, mods, norm_w, w_router_t)


def _prefix_sum_lanes(x, lane, n):
    s = 1
    while s < n:
        x = x + jnp.where(lane >= s, pltpu.roll(x, s, 1), 0)
        s *= 2
    return x


def _route_body(aff_ref, idx_ref, g_ref, pos_ref, sp_ref, sel_ref, rem_ref, *, n, cap):
    aff = aff_ref[...]
    lane = lax.broadcasted_iota(i32, (N_EXPERTS, n), 1)

    def count(mask):
        return jnp.sum(jnp.where(mask, 1.0, 0.0), axis=1, keepdims=True).astype(i32)

    def as_value(bits):
        return pltpu.bitcast(jnp.broadcast_to(bits, (N_EXPERTS, LANES)), f32)[:, :1]

    def search(i, thr):
        cand = thr | jnp.left_shift(jnp.int32(1), 30 - i)
        return jnp.where(count(aff >= as_value(cand)) >= cap, cand, thr)

    thr = lax.fori_loop(0, 31, search, jnp.zeros((N_EXPERTS, 1), i32))
    sure = aff >= as_value(thr + 1)
    sel_ref[...] = sure.astype(i32)
    rem_ref[...] = ((aff >= as_value(thr)) & jnp.logical_not(sure)).astype(i32)

    def more(need):
        return jnp.max(need) > 0

    def take_largest(need):
        rem = rem_ref[...] > 0
        top = jnp.max(jnp.where(rem, aff, -1.0), axis=1, keepdims=True)
        is_top = rem & (aff == top) & (need > 0)
        rank = _prefix_sum_lanes(is_top.astype(i32), lane, n)
        pick = is_top & ((count(is_top) <= need) | (rank <= need))
        sel_ref[...] = sel_ref[...] | pick.astype(i32)
        rem_ref[...] = (rem & jnp.logical_not(is_top)).astype(i32)
        return need - count(pick)

    lax.while_loop(more, take_largest, cap - count(sure))
    sel_i = sel_ref[...]
    sel = sel_i > 0
    pos = _prefix_sum_lanes(sel_i, lane, n) - sel_i
    pos_ref[...] = pos
    sp_ref[...] = jnp.where(sel, pos, UNSELECTED)

    key = jnp.where(sel, lane - pos, -1)
    idx = lane
    g = aff
    b = 0
    while (1 << b) < n:
        s = 1 << b
        k_sh = pltpu.roll(key, n - s, 1)
        take = (k_sh >= 0) & ((jnp.right_shift(k_sh, b) & 1) == 1)
        stay = (key >= 0) & ((jnp.right_shift(key, b) & 1) == 0)
        idx = jnp.where(take, pltpu.roll(idx, n - s, 1), idx)
        g = jnp.where(take, pltpu.roll(g, n - s, 1), g)
        key = jnp.where(take, k_sh, jnp.where(stay, key, -1))
        b += 1
    idx_ref[...] = idx[:, :cap]
    g_ref[...] = g[:, :cap]


def _route(aff_t, cap):
    n = aff_t.shape[1]
    body = functools.partial(_route_body, n=n, cap=cap)
    full = lambda shape: pl.BlockSpec(shape, lambda: (0, 0))
    return pl.pallas_call(
        body,
        in_specs=[full((N_EXPERTS, n))],
        out_specs=[full((N_EXPERTS, cap)), full((N_EXPERTS, cap)),
                   full((N_EXPERTS, n)), full((N_EXPERTS, n))],
        out_shape=[jax.ShapeDtypeStruct((N_EXPERTS, cap), i32),
                   jax.ShapeDtypeStruct((N_EXPERTS, cap), f32),
                   jax.ShapeDtypeStruct((N_EXPERTS, n), i32),
                   jax.ShapeDtypeStruct((N_EXPERTS, n), i32)],
        scratch_shapes=[pltpu.VMEM((N_EXPERTS, n), i32), pltpu.VMEM((N_EXPERTS, n), i32)],
        compiler_params=pltpu.CompilerParams(vmem_limit_bytes=32 * _MIB),
        name="expert_choice_routing",
    )(aff_t)


_TF = 512
_TN_DOWN = 512


def _expert_body(idx_ref, h2a_hbm, h2b_hbm, g_ref, wg_ref, wu_ref, wd_ref, oa_ref, ob_ref,
                 xe_ref, xb_ref, he_ref, sem, *, caps, na, nd):
    e = pl.program_id(0)
    s = pl.program_id(1)
    srcs = (h2a_hbm, h2b_hbm)
    starts = (0, caps[0])
    rows_total = caps[0] + caps[1]

    def row_copy(grp, tok, r):
        return pltpu.make_async_copy(srcs[grp].at[pl.ds(tok, 1), :], xe_ref.at[pl.ds(r, 1), :], sem)

    def wait_all_rows():
        for grp in range(2):
            pltpu.make_async_copy(srcs[grp].at[pl.ds(0, caps[grp]), :],
                                  xe_ref.at[pl.ds(starts[grp], caps[grp]), :], sem).wait()

    @pl.when((e == 0) & (s == 0))
    def _():
        for grp in range(2):
            def issue(r, c, grp=grp):
                row_copy(grp, idx_ref[starts[grp] + r], starts[grp] + r).start()
                return c

            lax.fori_loop(0, caps[grp], issue, 0, unroll=8)

    @pl.when(s == 0)
    def _():
        wait_all_rows()
        packed = xe_ref[...]
        for part in range(2):
            xb_ref[:, part * (D // 2):(part + 1) * (D // 2)] = pltpu.unpack_elementwise(
                packed, index=part, packed_dtype=bf16, unpacked_dtype=f32).astype(bf16)

    @pl.when(s < na)
    def _():
        wg = wg_ref[0].astype(bf16)
        wu = wu_ref[0].astype(bf16)
        half = rows_total // 2
        for rows in (slice(0, half), slice(half, rows_total)):
            xb = xb_ref[rows, :]
            gp = jnp.dot(xb, wg, preferred_element_type=f32)
            up = jnp.dot(xb, wu, preferred_element_type=f32)
            he_ref[s, rows, :] = ((gp * jax.nn.sigmoid(gp)) * up).astype(bf16)

    @pl.when(s >= na)
    def _():
        base = jnp.minimum(e + 1, N_EXPERTS - 1) * rows_total
        for k in range(nd):
            @pl.when(s - na == k)
            def _(k=k):
                for grp in range(2):
                    per = caps[grp] // nd
                    for r in range(starts[grp] + k * per, starts[grp] + (k + 1) * per):
                        row_copy(grp, idx_ref[base + r], r).start()

        wd = wd_ref[0].astype(bf16)
        acc = None
        for k in range(na):
            part = jnp.dot(he_ref[k], wd[k * _TF:(k + 1) * _TF, :], preferred_element_type=f32)
            acc = part if acc is None else acc + part
        scaled = acc * g_ref[0]
        oa_ref[0] = scaled[:caps[0]]
        ob_ref[0] = scaled[caps[0]:]

    @pl.when((e == N_EXPERTS - 1) & (s == na + nd - 1))
    def _():
        wait_all_rows()


def _expert_ffn(idx_a, idx_b, h2a, h2b, g_a, g_b, w_gate, w_up, w_down):
    caps = (idx_a.shape[1], idx_b.shape[1])
    rows = caps[0] + caps[1]
    na = D_FF // _TF
    nd = D // _TN_DOWN
    idx = jnp.concatenate([idx_a, idx_b], axis=1).reshape(N_EXPERTS * rows)
    g3 = jnp.concatenate([g_a, g_b], axis=1)[:, :, None]
    body = functools.partial(_expert_body, caps=caps, na=na, nd=nd)
    up_tile = lambda e, s, idx: (e, 0, jnp.minimum(s, na - 1))
    down_tile = lambda e, s, idx: (e, 0, jnp.maximum(s - na, 0))
    grid_spec = pltpu.PrefetchScalarGridSpec(
        num_scalar_prefetch=1,
        grid=(N_EXPERTS, na + nd),
        in_specs=[pl.BlockSpec(memory_space=pl.ANY),
                  pl.BlockSpec(memory_space=pl.ANY),
                  pl.BlockSpec((1, rows, 1), lambda e, s, idx: (e, 0, 0)),
                  pl.BlockSpec((1, D, _TF), up_tile),
                  pl.BlockSpec((1, D, _TF), up_tile),
                  pl.BlockSpec((1, D_FF, _TN_DOWN), down_tile)],
        out_specs=[pl.BlockSpec((1, caps[0], _TN_DOWN), down_tile),
                   pl.BlockSpec((1, caps[1], _TN_DOWN), down_tile)],
        scratch_shapes=[pltpu.VMEM((rows, D // 2), jnp.uint32), pltpu.VMEM((rows, D), bf16),
                        pltpu.VMEM((na, rows, _TF), bf16), pltpu.SemaphoreType.DMA(())],
    )
    return pl.pallas_call(
        body,
        grid_spec=grid_spec,
        out_shape=[jax.ShapeDtypeStruct((N_EXPERTS, caps[0], D), f32),
                   jax.ShapeDtypeStruct((N_EXPERTS, caps[1], D), f32)],
        compiler_params=_cparams(("arbitrary", "arbitrary"), 60),
        name="expert_ffn",
    )(idx, h2a, h2b, g3, w_gate, w_up, w_down)


_TT = 128
_KB = 512
_RUN_FIXED = 32
_RUN_MAX = _RUN_FIXED + _TT
_RUN_PIECES = tuple(SUBLANES << i for i in reversed(range((_TT // SUBLANES).bit_length())))
_STAGE_ROWS = pl.cdiv(N_EXPERTS * _RUN_MAX, _KB) * _KB
_STAGE_PIECES = tuple(SUBLANES << i
                      for i in reversed(range((_STAGE_ROWS // SUBLANES).bit_length())))


def _combine_body(offs_ref, ye_hbm, sp_ref, x1_ref, mods_ref, o_ref, stage_ref, acc_ref, sem,
                  *, cap):
    j = pl.program_id(0)
    slot = j & 1

    def wait_rows(total, sl):
        for size in _STAGE_PIECES:
            @pl.when((total & size) != 0)
            def _(size=size):
                pltpu.make_async_copy(ye_hbm.at[pl.ds(0, size), :],
                                      stage_ref.at[sl, pl.ds(0, size), :], sem.at[sl]).wait()

    def tile_copies(jj, sl, start):
        def copy(src, dst, size):
            return pltpu.make_async_copy(
                ye_hbm.at[pl.ds(pl.multiple_of(src, SUBLANES), size), :],
                stage_ref.at[sl, pl.ds(pl.multiple_of(dst, SUBLANES), size), :], sem.at[sl])

        bases, offs = [], []
        base = jnp.int32(0)
        for e in range(N_EXPERTS):
            off = offs_ref[e, jj]
            end = offs_ref[e, jj + 1]
            first = jnp.minimum((off // SUBLANES) * SUBLANES, cap - _RUN_FIXED)
            rest = jnp.maximum(
                ((end - first - _RUN_FIXED + SUBLANES - 1) // SUBLANES) * SUBLANES, 0)
            bases.append(base)
            offs.append(first)
            if start:
                src = e * cap + first
                copy(src, base, _RUN_FIXED).start()

                @pl.when(rest > 0)
                def _(src=src, base=base, rest=rest):
                    s2 = src + _RUN_FIXED
                    d2 = base + _RUN_FIXED
                    for size in _RUN_PIECES:
                        piece = rest & size

                        @pl.when(piece != 0)
                        def _(s2=s2, d2=d2, size=size):
                            copy(s2, d2, size).start()

                        s2 = s2 + piece
                        d2 = d2 + piece

            base = base + _RUN_FIXED + rest
        return bases, offs, base

    @pl.when(j == 0)
    def _():
        stage_ref[...] = jnp.zeros_like(stage_ref)
        tile_copies(0, 0, True)

    @pl.when(j + 1 < pl.num_programs(0))
    def _():
        tile_copies(j + 1, 1 - slot, True)

    bases, offs, total = tile_copies(j, slot, False)
    wait_rows(total, slot)

    sp = sp_ref[...]
    kiota = lax.broadcasted_iota(i32, (1, _KB), 1)
    acc_ref[...] = jnp.zeros_like(acc_ref)

    def kblock(kb, c):
        hit = None
        for e in range(N_EXPERTS):
            local = sp[:, e:e + 1] + (bases[e] - offs[e] - kb * _KB)
            h = local == kiota
            hit = h if hit is None else (hit | h)
        s = jnp.where(hit, 1.0, 0.0).astype(bf16)
        rows = stage_ref[slot, pl.ds(pl.multiple_of(kb * _KB, _KB), _KB), :]
        hi = rows.astype(bf16)
        lo = (rows - hi.astype(f32)).astype(bf16)
        acc_ref[...] += (jnp.dot(s, hi, preferred_element_type=f32)
                         + jnp.dot(s, lo, preferred_element_type=f32))
        return c

    lax.fori_loop(0, (total + _KB - 1) // _KB, kblock, 0)
    o_ref[...] = x1_ref[...] + mods_ref[0][5:6] * acc_ref[...]


def _combine(offs, ye_flat, sp_n, x1, mods, cap, tiles_per_row):
    n = x1.shape[0]
    row = (lambda j: 1 + j // tiles_per_row) if tiles_per_row else (lambda j: 0)
    body = functools.partial(_combine_body, cap=cap)
    grid_spec = pltpu.PrefetchScalarGridSpec(
        num_scalar_prefetch=1,
        grid=(n // _TT,),
        in_specs=[pl.BlockSpec(memory_space=pl.ANY),
                  pl.BlockSpec((_TT, N_EXPERTS), lambda j, offs: (j, 0)),
                  pl.BlockSpec((_TT, D), lambda j, offs: (j, 0)),
                  pl.BlockSpec((1, 6, D), lambda j, offs: (row(j), 0, 0))],
        out_specs=pl.BlockSpec((_TT, D), lambda j, offs: (j, 0)),
        scratch_shapes=[pltpu.VMEM((2, _STAGE_ROWS, D), f32), pltpu.VMEM((_TT, D), f32),
                        pltpu.SemaphoreType.DMA((2,))],
    )
    return pl.pallas_call(
        body,
        grid_spec=grid_spec,
        out_shape=jax.ShapeDtypeStruct((n, D), f32),
        compiler_params=_cparams(("arbitrary",), 52),
        name="expert_combine",
    )(offs, ye_flat, sp_n, x1, mods)


def _moe(x1s, h2s, affs, mods, w_gate, w_up, w_down, tiles_per_row):
    routes = [_route(aff_t, CAPACITY_FACTOR * aff_t.shape[1] // N_EXPERTS) for aff_t in affs]
    (idx_a, g_a, _, _), (idx_b, g_b, _, _) = routes
    yes = _expert_ffn(idx_a, idx_b, h2s[0], h2s[1], g_a, g_b, w_gate, w_up, w_down)
    outs = []
    for ye, (idx, _, pos, sp), x1, tpr in zip(yes, routes, x1s, tiles_per_row):
        cap = idx.shape[1]
        offs = jnp.concatenate([pos[:, ::_TT], jnp.full((N_EXPERTS, 1), cap, i32)], axis=1)
        outs.append(_combine(offs, ye.reshape(N_EXPERTS * cap, D), sp.T, x1, mods, cap, tpr))
    return outs


def kernel(x_prompt, x_sample, cache_k, cache_v, state_fwd, state_bwd, c, c_ctx, w_mod, b_mod,
           norm_mix, w_in, q_norm, k_norm, rpb, conv_w, conv_b, lru_w_a, lru_b_a, lru_w_x,
           lru_b_x, lru_lambda, w_out, norm_ffn, w_router, w_gate, w_up, w_down):
    B, S, _ = x_prompt.shape
    DB, T, _ = x_sample.shape
    l = 0

    cvecs = jnp.concatenate([c_ctx[None, :], c, jnp.zeros((SUBLANES - 1 - DB, D), f32)], axis=0)
    mods = _adaln_mods(cvecs, w_mod[l], b_mod[l][None, :]).reshape(SUBLANES, 6, D)

    heads_per_tile = _TN_IN // HEAD_DIM
    qn_t = jnp.tile(q_norm[l], heads_per_tile)[None, :]
    kn_t = jnp.tile(k_norm[l], heads_per_tile)[None, :]
    gmat = (jnp.kron(jnp.eye(heads_per_tile, dtype=f32), jnp.ones((HEAD_DIM, HEAD_DIM), f32))
            / HEAD_DIM).astype(bf16)
    nmix = norm_mix[l][None, :]
    nffn = norm_ffn[l][None, :]
    w_in_b = w_in[l]
    w_out_b = w_out[l].astype(bf16)
    w_router_t = w_router[l].T.astype(bf16)
    wcat = _lru_gate_weights(lru_w_a[l], lru_w_x[l])
    cb = conv_b[l][None, :]

    xp = x_prompt.reshape(B * S, D)
    xs = x_sample.reshape(DB * T, D)

    proj_c = _inproj(xp, mods, nmix, w_in_b, qn_t, kn_t, gmat, per_tile_rows=False)
    att_c, new_k, new_v = _ctx_attention(proj_c, B, S)
    zeros_c = jnp.zeros((B, LRU_W), f32)
    lru_c, htf, htb = _rglru(proj_c, wcat, conv_w[l], cb, lru_b_a[l], lru_b_x[l], lru_lambda[l],
                             zeros_c, zeros_c, T=S, nb=SUBLANES)
    x1_c, h2_c, aff_c = _outproj(att_c, lru_c, w_out_b, xp, mods, nffn, w_router_t, 0)

    nct = LRU_W // _LRU_TC

    def slabs(h0):
        return (h0.reshape(DB, nct, _LRU_HALVES, LANES).transpose(2, 0, 1, 3)
                .reshape(_LRU_HALVES * DB, nct * LANES))

    proj_l = _inproj(xs, mods, nmix, w_in_b, qn_t, kn_t, gmat, per_tile_rows=True)
    att_l = _lat_attention(proj_l, cache_k, cache_v, _pad_rpb(rpb[l]), l, DB, T)
    h0f_l = slabs(state_fwd[:, l])
    h0b_l = slabs(state_bwd[:, l])
    lru_l, _, _ = _rglru(proj_l, wcat, conv_w[l], cb, lru_b_a[l], lru_b_x[l], lru_lambda[l],
                         h0f_l, h0b_l, T=T, nb=DB)
    x1_l, h2_l, aff_l = _outproj(att_l, lru_l, w_out_b, xs, mods, nffn, w_router_t, T // 512)

    y_c, y_l = _moe((x1_c, x1_l), (h2_c, h2_l), (aff_c, aff_l), mods,
                    w_gate[l], w_up[l], w_down[l], (0, T // _TT))

    return (y_c.reshape(B, S, D), y_l.reshape(DB, T, D), new_k, new_v,
            htf[:, None, :], htb[:, None, :])
```

```python
import functools

import jax
import jax.numpy as jnp
from jax import lax
from jax.experimental import pallas as pl
from jax.experimental.pallas import tpu as pltpu

f32 = jnp.float32
bf16 = jnp.bfloat16
i32 = jnp.int32

D = 2048
N_HEADS = 16
HEAD_DIM = 64
ATT_W = N_HEADS * HEAD_DIM
LRU_W = D - ATT_W
LRU_BLOCK = 64
LRU_C = 8.0
GRID_W = 64
WIN_H = 8
WIN_W = 16
N_EXPERTS = 16
CAPACITY_FACTOR = 2
D_FF = 2048
EPS = 1e-6
NEG_INF = -1e30
IN_COLS = 3 * ATT_W + 2 * LRU_W

LANES = 128
SUBLANES = 8
MXU_DIM = 256
UNSELECTED = -(1 << 30)

_MIB = 1024 * 1024


def _cparams(sem, vmem_mib):
    return pltpu.CompilerParams(dimension_semantics=sem, vmem_limit_bytes=vmem_mib * _MIB)


_TN_MOD = 1024


def _mods_body(c_ref, w_ref, b_ref, o_ref):
    c = c_ref[...]
    s = (c * jax.nn.sigmoid(c)).astype(bf16)
    o_ref[...] = jnp.dot(s, w_ref[...].astype(bf16), preferred_element_type=f32) + b_ref[...]


def _adaln_mods(cvecs, w_mod, b_mod):
    tn = _TN_MOD
    return pl.pallas_call(
        _mods_body,
        grid=(6 * D // tn,),
        in_specs=[pl.BlockSpec((SUBLANES, D), lambda j: (0, 0)),
                  pl.BlockSpec((D, tn), lambda j: (0, j)),
                  pl.BlockSpec((1, tn), lambda j: (0, j))],
        out_specs=pl.BlockSpec((SUBLANES, tn), lambda j: (0, j)),
        out_shape=jax.ShapeDtypeStruct((SUBLANES, 6 * D), f32),
        compiler_params=_cparams(("arbitrary",), 40),
        name="adaln_mods",
    )(cvecs, w_mod, b_mod)


_TM_IN = 1024
_TN_IN = 512
_NORM_ROWS = 16
_NORM_SLICES = 8


def _inproj_body(x0_ref, xn_ref, mods0_ref, modsn_ref, nw_ref, w_ref, qn_ref, kn_ref, g_ref,
                 o_ref, ha_ref, hb_ref):
    i = pl.program_id(0)
    j = pl.program_id(1)
    tm = ha_ref.shape[0]
    nw = nw_ref[...]
    n_qk_tiles = 2 * ATT_W // _TN_IN

    def norm_rows(x_ref, mods_ref, h_ref, r0):
        m = mods_ref[0]
        rows = pl.ds(pl.multiple_of(r0, _NORM_ROWS), _NORM_ROWS)
        x = x_ref[rows, :]
        ms = jnp.mean(x * x, axis=-1, keepdims=True)
        y = x * lax.rsqrt(ms + EPS) * nw
        h_ref[rows, :] = (y * (1.0 + m[1:2]) + m[0:1]).astype(bf16)

    @pl.when((i == 0) & (j == 0))
    def _():
        def first(c, carry):
            norm_rows(x0_ref, mods0_ref, ha_ref, c * _NORM_ROWS)
            return carry

        lax.fori_loop(0, tm // _NORM_ROWS, first, 0, unroll=4)

    def column_step(cur_ref, nxt_ref):
        slice_rows = tm // _NORM_SLICES
        r_slice = jnp.minimum(j, _NORM_SLICES - 1) * slice_rows
        for c in range(slice_rows // _NORM_ROWS):
            norm_rows(xn_ref, modsn_ref, nxt_ref, r_slice + c * _NORM_ROWS)

        acc = jnp.dot(cur_ref[...], w_ref[...].astype(bf16), preferred_element_type=f32)

        @pl.when(j < n_qk_tiles)
        def _():
            sq = acc * acc
            hi = sq.astype(bf16)
            lo = (sq - hi.astype(f32)).astype(bf16)
            g = g_ref[...]
            ms = (jnp.dot(hi, g, preferred_element_type=f32)
                  + jnp.dot(lo, g, preferred_element_type=f32))
            head_w = jnp.where(j < n_qk_tiles // 2, qn_ref[...], kn_ref[...])
            o_ref[...] = acc * lax.rsqrt(ms + EPS) * head_w

        @pl.when(j >= n_qk_tiles)
        def _():
            o_ref[...] = acc

    @pl.when((i & 1) == 0)
    def _():
        column_step(ha_ref, hb_ref)

    @pl.when((i & 1) == 1)
    def _():
        column_step(hb_ref, ha_ref)


def _inproj(x2d, mods, norm_w, w_in, qn_t, kn_t, gmat, per_tile_rows):
    n = x2d.shape[0]
    tm = _TM_IN
    last = n // tm - 1
    row = (lambda i: 1 + i) if per_tile_rows else (lambda i: 0)
    nxt = lambda i: jnp.minimum(i + 1, last)
    return pl.pallas_call(
        _inproj_body,
        grid=(n // tm, IN_COLS // _TN_IN),
        in_specs=[pl.BlockSpec((tm, D), lambda i, j: (0, 0)),
                  pl.BlockSpec((tm, D), lambda i, j: (nxt(i), 0)),
                  pl.BlockSpec((1, 6, D), lambda i, j: (row(0), 0, 0)),
                  pl.BlockSpec((1, 6, D), lambda i, j: (row(nxt(i)), 0, 0)),
                  pl.BlockSpec((1, D), lambda i, j: (0, 0)),
                  pl.BlockSpec((D, _TN_IN), lambda i, j: (0, j)),
                  pl.BlockSpec((1, _TN_IN), lambda i, j: (0, 0)),
                  pl.BlockSpec((1, _TN_IN), lambda i, j: (0, 0)),
                  pl.BlockSpec((_TN_IN, _TN_IN), lambda i, j: (0, 0))],
        out_specs=pl.BlockSpec((tm, _TN_IN), lambda i, j: (i, j)),
        out_shape=jax.ShapeDtypeStruct((n, IN_COLS), f32),
        scratch_shapes=[pltpu.VMEM((tm, D), bf16), pltpu.VMEM((tm, D), bf16)],
        compiler_params=_cparams(("arbitrary", "arbitrary"), 56),
        name="in_projection",
    )(x2d, x2d, mods, mods, norm_w, w_in, qn_t, kn_t, gmat)


def _softmax_pv(s_list, v_list):
    m = s_list[0].max(axis=-1, keepdims=True)
    for s in s_list[1:]:
        m = jnp.maximum(m, s.max(axis=-1, keepdims=True))
    o = None
    for s, v in zip(s_list, v_list):
        os_ = jnp.dot(jnp.exp(s - m).astype(bf16), v, preferred_element_type=f32)
        o = os_ if o is None else o + os_
    return o


def _normalise_pair(o0, o1, lane):
    left = lane < HEAD_DIM
    den = pltpu.roll(jnp.where(left, o1, o0), HEAD_DIM, 1)
    return jnp.where(left, o0, o1) / den


def _nt_dot(a, b):
    return lax.dot_general(a, b, (((1,), (1,)), ((), ())), preferred_element_type=f32)


_QK_SCALE = HEAD_DIM ** -0.5


def _ctx_attn_body(q_ref, k_ref, v_ref, o_ref, nk_ref, nv_ref):
    lane = lax.broadcasted_iota(i32, (1, LANES), 1)
    for p in range(ATT_W // LANES):
        cols = slice(p * LANES, (p + 1) * LANES)
        q = q_ref[:, cols] * _QK_SCALE
        kf = k_ref[:, cols]
        k = kf.astype(bf16)
        v = v_ref[:, cols]
        nk_ref[0, 0, 2 * p] = kf[:, :HEAD_DIM]
        nk_ref[0, 0, 2 * p + 1] = pltpu.roll(kf, HEAD_DIM, 1)[:, :HEAD_DIM]
        nv_ref[0, 0, 2 * p] = v[:, :HEAD_DIM]
        nv_ref[0, 0, 2 * p + 1] = pltpu.roll(v, HEAD_DIM, 1)[:, :HEAD_DIM]
        heads = []
        for hh in range(2):
            hm = (lane >= hh * HEAD_DIM) & (lane < (hh + 1) * HEAD_DIM)
            qh = jnp.where(hm, q, 0.0).astype(bf16)
            vh = jnp.where(hm, v, 1.0).astype(bf16)
            heads.append(_softmax_pv([_nt_dot(qh, k)], [vh]))
        o_ref[:, cols] = _normalise_pair(heads[0], heads[1], lane).astype(bf16)


def _ctx_attention(proj, batch, seq):
    n = proj.shape[0]
    cache = pl.BlockSpec((1, 1, N_HEADS, seq, HEAD_DIM), lambda b: (b, 0, 0, 0, 0))
    cache_shape = jax.ShapeDtypeStruct((batch, 1, N_HEADS, seq, HEAD_DIM), f32)
    return pl.pallas_call(
        _ctx_attn_body,
        grid=(batch,),
        in_specs=[pl.BlockSpec((seq, ATT_W), lambda b: (b, 0)),
                  pl.BlockSpec((seq, ATT_W), lambda b: (b, 1)),
                  pl.BlockSpec((seq, ATT_W), lambda b: (b, 2))],
        out_specs=[pl.BlockSpec((seq, ATT_W), lambda b: (b, 0)), cache, cache],
        out_shape=[jax.ShapeDtypeStruct((n, ATT_W), bf16), cache_shape, cache_shape],
        compiler_params=_cparams(("arbitrary",), 32),
        name="context_attention",
    )(proj, proj, proj)


_NA_ROWS = 16
_NA_QROWS = 8
_NA_KROWS = 12


def _pad_rpb(rpb):
    nd, nc = 2 * WIN_H - 1, 2 * WIN_W - 1
    return jnp.pad(rpb.astype(f32), ((0, 0), (0, 2 * SUBLANES - nd), (0, LANES - nc)))


def _build_na_bias(rpb_ref, bias_ref):
    qc = lax.broadcasted_iota(i32, (GRID_W, LANES), 0)
    lane = lax.broadcasted_iota(i32, (GRID_W, LANES), 1)
    kc = lane & (GRID_W - 1)
    cs = jnp.clip(qc - WIN_W // 2, 0, GRID_W - WIN_W)
    col_ok = (kc >= cs) & (kc < cs + WIN_W)
    left = lane < GRID_W
    neg = jnp.full((GRID_W, LANES), NEG_INF, f32)
    for hh in range(2):
        tbl = rpb_ref[hh]
        tiles = {}

        def toeplitz(d, odd):
            if (d, odd) not in tiles:
                row = jnp.broadcast_to(tbl[d:d + 1, :], (GRID_W, LANES))
                shift = (LANES - (WIN_W - 1) + odd * GRID_W) % LANES
                t = pltpu.roll(row, shift, 1, stride=1, stride_axis=0)
                tiles[(d, odd)] = jnp.where(col_ok, t, NEG_INF)
            return tiles[(d, odd)]

        for j in range(2):
            for rq in range(_NA_QROWS):
                rq_abs = j * _NA_QROWS + rq
                rs = min(max(rq_abs - WIN_H // 2, 0), _NA_ROWS - WIN_H)
                for kp in range(_NA_KROWS // 2):
                    parts = []
                    for odd in range(2):
                        rk_abs = j * (_NA_ROWS - _NA_KROWS) + 2 * kp + odd
                        if rs <= rk_abs < rs + WIN_H:
                            parts.append(toeplitz(rk_abs - rq_abs + WIN_H - 1, odd))
                        else:
                            parts.append(neg)
                    bias_ref[hh, j, rq * GRID_W:(rq + 1) * GRID_W,
                             kp * LANES:(kp + 1) * LANES] = jnp.where(left, parts[0], parts[1])


def _lat_attn_body(q_ref, k_ref, v_ref, ck_ref, cv_ref, rpb_ref, o_ref, bias_ref):
    @pl.when(pl.program_id(1) == 0)
    def _():
        _build_na_bias(rpb_ref, bias_ref)

    lane = lax.broadcasted_iota(i32, (1, LANES), 1)
    nq = _NA_QROWS * GRID_W
    nk = _NA_KROWS * GRID_W
    kstep = (_NA_ROWS - _NA_KROWS) * GRID_W
    src = lax.broadcasted_iota(i32, (HEAD_DIM, LANES), 0)
    dst = lax.broadcasted_iota(i32, (HEAD_DIM, LANES), 1)
    ck = None
    cv_h = []
    for hh in range(2):
        place = jnp.where(dst == src + hh * HEAD_DIM, 1.0, 0.0).astype(bf16)
        part = jnp.dot(ck_ref[0, 0, hh].astype(bf16), place, preferred_element_type=f32)
        ck = part if ck is None else ck + part
        placed = jnp.dot(cv_ref[0, 0, hh].astype(bf16), place, preferred_element_type=f32)
        own = (lane >= hh * HEAD_DIM) & (lane < (hh + 1) * HEAD_DIM)
        cv_h.append(jnp.where(own, placed, 1.0).astype(bf16))
    ck = ck.astype(bf16)
    for j in range(2):
        q = q_ref[j * nq:(j + 1) * nq, :] * _QK_SCALE
        k = k_ref[j * kstep:j * kstep + nk, :].astype(bf16)
        v = v_ref[j * kstep:j * kstep + nk, :]
        heads = []
        for hh in range(2):
            hm = (lane >= hh * HEAD_DIM) & (lane < (hh + 1) * HEAD_DIM)
            qh = jnp.where(hm, q, 0.0).astype(bf16)
            s_win = _nt_dot(qh, k) + bias_ref[hh, j]
            s_ctx = _nt_dot(qh, ck)
            vh = jnp.where(hm, v, 1.0).astype(bf16)
            heads.append(_softmax_pv([s_win, s_ctx], [vh, cv_h[hh]]))
        o_ref[j * nq:(j + 1) * nq, :] = _normalise_pair(heads[0], heads[1], lane).astype(bf16)


def _lat_attention(proj, ck, cv, rpb_p, layer, batch, seq):
    n = proj.shape[0]
    past = ck.shape[3]
    cache = pl.BlockSpec((1, 1, 2, past, HEAD_DIM), lambda p, b: (b, layer, p, 0, 0))
    npair = ATT_W // LANES
    nq = _NA_QROWS * GRID_W
    nk = _NA_KROWS * GRID_W
    return pl.pallas_call(
        _lat_attn_body,
        grid=(npair, batch),
        in_specs=[pl.BlockSpec((seq, LANES), lambda p, b: (b, p)),
                  pl.BlockSpec((seq, LANES), lambda p, b: (b, npair + p)),
                  pl.BlockSpec((seq, LANES), lambda p, b: (b, 2 * npair + p)),
                  cache, cache,
                  pl.BlockSpec((2, 2 * SUBLANES, LANES), lambda p, b: (p, 0, 0))],
        out_specs=pl.BlockSpec((seq, LANES), lambda p, b: (b, p)),
        out_shape=jax.ShapeDtypeStruct((n, ATT_W), bf16),
        scratch_shapes=[pltpu.VMEM((2, 2, nq, nk), f32)],
        compiler_params=_cparams(("arbitrary", "arbitrary"), 48),
        name="neighbourhood_attention",
    )(proj, proj, proj, ck, cv, rpb_p)


_LRU_TC = 256
_GATE_ROWS = 256
_LRU_HALVES = _LRU_TC // LANES


def _gelu_tanh(x):
    c = 0.7978845608028654
    return x * (0.5 * (1.0 + jnp.tanh(c * (x + 0.044715 * (x * x * x)))))


def _lru_body(xr_ref, gr_ref, w_ref, cw_ref, cb_ref, ba_ref, bx_ref, lam_ref, h0f_ref, h0b_ref,
              y_ref, htf_ref, htb_ref, xc_ref, af_ref, uf_ref, ab_ref, ub_ref, *, T, nb):
    nslab = _LRU_HALVES * nb // SUBLANES
    R = nb * T
    x = xr_ref[...]
    t = lax.broadcasted_iota(i32, (R, 1), 0) & (T - 1)
    cw = cw_ref[...]
    xc = jnp.where(t >= 2, pltpu.roll(x, 2, 0), 0.0) * cw[0:1]
    xc = xc + jnp.where(t >= 1, pltpu.roll(x, 1, 0), 0.0) * cw[1:2]
    xc = xc + x * cw[2:3]
    xc = xc + jnp.where(t < T - 1, pltpu.roll(x, R - 1, 0), 0.0) * cw[3:4]
    xc_ref[...] = xc + cb_ref[...]

    nl = -lam_ref[...]
    sp = jnp.maximum(nl, 0.0) + jnp.log1p(jnp.exp(-jnp.abs(nl)))
    half_c_sp = (0.5 * LRU_C) * sp
    w = w_ref[0]
    dirs = ((af_ref, uf_ref), (ab_ref, ub_ref))
    for b in range(nb):
        for ci in range(T // _GATE_ROWS):
            r0 = b * T + ci * _GATE_ROWS
            xcc = xc_ref[r0:r0 + _GATE_ROWS, :]
            half_x = 0.5 * xcc
            z = jnp.dot(xcc.astype(bf16), w, preferred_element_type=f32)
            for d, (a_ref, u_ref) in enumerate(dirs):
                c0 = d * 2 * _LRU_TC
                tr = jnp.tanh(0.5 * (z[:, c0:c0 + _LRU_TC] + ba_ref[d:d + 1]))
                tg = jnp.tanh(0.5 * (z[:, c0 + _LRU_TC:c0 + 2 * _LRU_TC] + bx_ref[d:d + 1]))
                neg_log_a = (1.0 + tr) * half_c_sp[d:d + 1]
                a = jnp.exp(-neg_log_a)
                one_m_a2 = jnp.tanh(neg_log_a) * (1.0 + a * a)
                root = jnp.where(one_m_a2 > 0.0, one_m_a2 * lax.rsqrt(one_m_a2), 0.0)
                u = root * ((1.0 + tg) * half_x)
                for hf in range(_LRU_HALVES):
                    s, sub = divmod(hf * nb + b, SUBLANES)
                    rows = pl.ds(ci * _GATE_ROWS * SUBLANES + sub, _GATE_ROWS, stride=SUBLANES)
                    a_ref[s, rows, :] = a[:, hf * LANES:(hf + 1) * LANES]
                    u_ref[s, rows, :] = u[:, hf * LANES:(hf + 1) * LANES]

    def step(i, carry):
        rf = pl.ds(pl.multiple_of(i * SUBLANES, SUBLANES), SUBLANES)
        rb = pl.ds(pl.multiple_of((T - 1 - i) * SUBLANES, SUBLANES), SUBLANES)
        new = []
        for s in range(nslab):
            hf_ = af_ref[s, rf, :] * carry[2 * s] + uf_ref[s, rf, :]
            uf_ref[s, rf, :] = hf_
            hb_ = ab_ref[s, rb, :] * carry[2 * s + 1] + ub_ref[s, rb, :]
            ub_ref[s, rb, :] = hb_
            new += [hf_, hb_]
        return tuple(new)

    init = []
    for s in range(nslab):
        init += [h0f_ref[:, s * LANES:(s + 1) * LANES], h0b_ref[:, s * LANES:(s + 1) * LANES]]
    fin = lax.fori_loop(0, T, step, tuple(init), unroll=8)
    for s in range(nslab):
        htf_ref[:, s * LANES:(s + 1) * LANES] = fin[2 * s]
        htb_ref[:, s * LANES:(s + 1) * LANES] = fin[2 * s + 1]

    for b in range(nb):
        for hf in range(_LRU_HALVES):
            s, sub = divmod(hf * nb + b, SUBLANES)
            rows = pl.ds(sub, T, stride=SUBLANES)
            hs = uf_ref[s, rows, :] + ub_ref[s, rows, :]
            g = gr_ref[b * T:(b + 1) * T, hf * LANES:(hf + 1) * LANES]
            y_ref[b * T:(b + 1) * T, hf * LANES:(hf + 1) * LANES] = (
                hs * _gelu_tanh(g)).astype(bf16)


def _rglru(proj, wcat, conv_w, conv_b, b_a, b_x, lam, h0f, h0b, T, nb):
    n = proj.shape[0]
    nslab = _LRU_HALVES * nb // SUBLANES
    W = nslab * LANES
    R = nb * T
    nct = LRU_W // _LRU_TC
    xr0 = 3 * ATT_W // _LRU_TC
    gr0 = xr0 + nct
    chan = lambda k: pl.BlockSpec((k, _LRU_TC), lambda g, j: (0, j))
    slab = pl.BlockSpec((SUBLANES, W), lambda g, j: (g, j))
    body = functools.partial(_lru_body, T=T, nb=nb)
    return pl.pallas_call(
        body,
        grid=(n // R, nct),
        in_specs=[pl.BlockSpec((R, _LRU_TC), lambda g, j: (g, xr0 + j)),
                  pl.BlockSpec((R, _LRU_TC), lambda g, j: (g, gr0 + j)),
                  pl.BlockSpec((1, _LRU_TC, 4 * _LRU_TC), lambda g, j: (j, 0, 0)),
                  chan(4), chan(1), chan(2), chan(2), chan(2), slab, slab],
        out_specs=[pl.BlockSpec((R, _LRU_TC), lambda g, j: (g, j)), slab, slab],
        out_shape=[jax.ShapeDtypeStruct((n, LRU_W), bf16),
                   jax.ShapeDtypeStruct(h0f.shape, f32),
                   jax.ShapeDtypeStruct(h0b.shape, f32)],
        scratch_shapes=[pltpu.VMEM((R, _LRU_TC), f32)]
        + [pltpu.VMEM((nslab, T * SUBLANES, LANES), f32) for _ in range(4)],
        compiler_params=_cparams(("arbitrary", "arbitrary"), 56),
        name="rg_lru",
    )(proj, proj, wcat, conv_w, conv_b, b_a, b_x, lam, h0f, h0b)


def _lru_gate_weights(w_a, w_x):
    per_tile = _LRU_TC // LRU_BLOCK
    nct = LRU_W // _LRU_TC
    eye = jnp.eye(per_tile, dtype=f32)

    def bd(w):
        w = w.reshape(nct, per_tile, LRU_BLOCK, LRU_BLOCK)
        m = w[:, :, :, None, :] * eye[None, :, None, :, None]
        return m.reshape(nct, _LRU_TC, _LRU_TC)

    return jnp.concatenate([bd(w_a[0]), bd(w_x[0]), bd(w_a[1]), bd(w_x[1])], axis=-1).astype(bf16)


_TM_OUT = 512


def _outproj_body(att_ref, lru_ref, w_ref, x_ref, mods_ref, nw_ref, wr_ref,
                  x1_ref, h2_ref, aff_ref):
    mix = (jnp.dot(att_ref[...], w_ref[0:ATT_W, :], preferred_element_type=f32)
           + jnp.dot(lru_ref[...], w_ref[ATT_W:D, :], preferred_element_type=f32))
    m = mods_ref[0]
    x1 = x_ref[...] + m[2:3] * mix
    x1_ref[...] = x1
    ms = jnp.mean(x1 * x1, axis=-1, keepdims=True)
    h2 = (x1 * lax.rsqrt(ms + EPS) * nw_ref[...]) * (1.0 + m[4:5]) + m[3:4]
    h2_ref[...] = pltpu.pack_elementwise([h2[:, :D // 2], h2[:, D // 2:]], packed_dtype=bf16)
    logits = _nt_dot(wr_ref[...], h2.astype(bf16))
    e = jnp.exp(logits - logits.max(axis=0, keepdims=True))
    aff_ref[...] = e / e.sum(axis=0, keepdims=True)


def _outproj(att, lru, w_out_b, x2d, mods, norm_w, w_router_t, tiles_per_row):
    n = x2d.shape[0]
    tm = _TM_OUT
    row = (lambda i: 1 + i // tiles_per_row) if tiles_per_row else (lambda i: 0)
    return pl.pallas_call(
        _outproj_body,
        grid=(n // tm,),
        in_specs=[pl.BlockSpec((tm, ATT_W), lambda i: (i, 0)),
                  pl.BlockSpec((tm, LRU_W), lambda i: (i, 0)),
                  pl.BlockSpec((D, D), lambda i: (0, 0)),
                  pl.BlockSpec((tm, D), lambda i: (i, 0)),
                  pl.BlockSpec((1, 6, D), lambda i: (row(i), 0, 0)),
                  pl.BlockSpec((1, D), lambda i: (0, 0)),
                  pl.BlockSpec((N_EXPERTS, D), lambda i: (0, 0))],
        out_specs=[pl.BlockSpec((tm, D), lambda i: (i, 0)),
                   pl.BlockSpec((tm, D // 2), lambda i: (i, 0)),
                   pl.BlockSpec((N_EXPERTS, tm), lambda i: (0, i))],
        out_shape=[jax.ShapeDtypeStruct((n, D), f32),
                   jax.ShapeDtypeStruct((n, D // 2), jnp.uint32),
                   jax.ShapeDtypeStruct((N_EXPERTS, n), f32)],
        compiler_params=_cparams(("arbitrary",), 52),
        name="out_projection_router",
    )(att, lru, w_out_b, x2d, mods, norm_w, w_router_t)


def _prefix_sum_lanes(x, lane, n):
    s = 1
    while s < n:
        x = x + jnp.where(lane >= s, pltpu.roll(x, s, 1), 0)
        s *= 2
    return x


def _route_body(aff_ref, idx_ref, g_ref, pos_ref, sp_ref, sel_ref, rem_ref, *, n, cap):
    aff = aff_ref[...]
    lane = lax.broadcasted_iota(i32, (N_EXPERTS, n), 1)

    def count(mask):
        return jnp.sum(jnp.where(mask, 1.0, 0.0), axis=1, keepdims=True).astype(i32)

    def as_value(bits):
        return pltpu.bitcast(jnp.broadcast_to(bits, (N_EXPERTS, LANES)), f32)[:, :1]

    def search(i, thr):
        cand = thr | jnp.left_shift(jnp.int32(1), 30 - i)
        return jnp.where(count(aff >= as_value(cand)) >= cap, cand, thr)

    thr = lax.fori_loop(0, 31, search, jnp.zeros((N_EXPERTS, 1), i32))
    sure = aff >= as_value(thr + 1)
    sel_ref[...] = sure.astype(i32)
    rem_ref[...] = ((aff >= as_value(thr)) & jnp.logical_not(sure)).astype(i32)

    def more(need):
        return jnp.max(need) > 0

    def take_largest(need):
        rem = rem_ref[...] > 0
        top = jnp.max(jnp.where(rem, aff, -1.0), axis=1, keepdims=True)
        is_top = rem & (aff == top) & (need > 0)
        rank = _prefix_sum_lanes(is_top.astype(i32), lane, n)
        pick = is_top & ((count(is_top) <= need) | (rank <= need))
        sel_ref[...] = sel_ref[...] | pick.astype(i32)
        rem_ref[...] = (rem & jnp.logical_not(is_top)).astype(i32)
        return need - count(pick)

    lax.while_loop(more, take_largest, cap - count(sure))
    sel_i = sel_ref[...]
    sel = sel_i > 0
    pos = _prefix_sum_lanes(sel_i, lane, n) - sel_i
    pos_ref[...] = pos
    sp_ref[...] = jnp.where(sel, pos, UNSELECTED)

    key = jnp.where(sel, lane - pos, -1)
    idx = lane
    g = aff
    b = 0
    while (1 << b) < n:
        s = 1 << b
        k_sh = pltpu.roll(key, n - s, 1)
        take = (k_sh >= 0) & ((jnp.right_shift(k_sh, b) & 1) == 1)
        stay = (key >= 0) & ((jnp.right_shift(key, b) & 1) == 0)
        idx = jnp.where(take, pltpu.roll(idx, n - s, 1), idx)
        g = jnp.where(take, pltpu.roll(g, n - s, 1), g)
        key = jnp.where(take, k_sh, jnp.where(stay, key, -1))
        b += 1
    idx_ref[...] = idx[:, :cap]
    g_ref[...] = g[:, :cap]


def _route(aff_t, cap):
    n = aff_t.shape[1]
    body = functools.partial(_route_body, n=n, cap=cap)
    full = lambda shape: pl.BlockSpec(shape, lambda: (0, 0))
    return pl.pallas_call(
        body,
        in_specs=[full((N_EXPERTS, n))],
        out_specs=[full((N_EXPERTS, cap)), full((N_EXPERTS, cap)),
                   full((N_EXPERTS, n)), full((N_EXPERTS, n))],
        out_shape=[jax.ShapeDtypeStruct((N_EXPERTS, cap), i32),
                   jax.ShapeDtypeStruct((N_EXPERTS, cap), f32),
                   jax.ShapeDtypeStruct((N_EXPERTS, n), i32),
                   jax.ShapeDtypeStruct((N_EXPERTS, n), i32)],
        scratch_shapes=[pltpu.VMEM((N_EXPERTS, n), i32), pltpu.VMEM((N_EXPERTS, n), i32)],
        compiler_params=pltpu.CompilerParams(vmem_limit_bytes=32 * _MIB),
        name="expert_choice_routing",
    )(aff_t)


_TF = 512
_TN_DOWN = 512


def _expert_body(idx_ref, h2a_hbm, h2b_hbm, g_ref, wg_ref, wu_ref, wd_ref, oa_ref, ob_ref,
                 xe_ref, xb_ref, he_ref, sem, *, caps, na, nd):
    e = pl.program_id(0)
    s = pl.program_id(1)
    srcs = (h2a_hbm, h2b_hbm)
    starts = (0, caps[0])
    rows_total = caps[0] + caps[1]

    def row_copy(grp, tok, r):
        return pltpu.make_async_copy(srcs[grp].at[pl.ds(tok, 1), :], xe_ref.at[pl.ds(r, 1), :], sem)

    def wait_all_rows():
        for grp in range(2):
            pltpu.make_async_copy(srcs[grp].at[pl.ds(0, caps[grp]), :],
                                  xe_ref.at[pl.ds(starts[grp], caps[grp]), :], sem).wait()

    @pl.when((e == 0) & (s == 0))
    def _():
        for grp in range(2):
            def issue(r, c, grp=grp):
                row_copy(grp, idx_ref[starts[grp] + r], starts[grp] + r).start()
                return c

            lax.fori_loop(0, caps[grp], issue, 0, unroll=8)

    @pl.when(s == 0)
    def _():
        wait_all_rows()
        packed = xe_ref[...]
        for part in range(2):
            xb_ref[:, part * (D // 2):(part + 1) * (D // 2)] = pltpu.unpack_elementwise(
                packed, index=part, packed_dtype=bf16, unpacked_dtype=f32).astype(bf16)

    @pl.when(s < na)
    def _():
        wg = wg_ref[0].astype(bf16)
        wu = wu_ref[0].astype(bf16)
        half = rows_total // 2
        for rows in (slice(0, half), slice(half, rows_total)):
            xb = xb_ref[rows, :]
            gp = jnp.dot(xb, wg, preferred_element_type=f32)
            up = jnp.dot(xb, wu, preferred_element_type=f32)
            he_ref[s, rows, :] = ((gp * jax.nn.sigmoid(gp)) * up).astype(bf16)

    @pl.when(s >= na)
    def _():
        base = jnp.minimum(e + 1, N_EXPERTS - 1) * rows_total
        for k in range(nd):
            @pl.when(s - na == k)
            def _(k=k):
                for grp in range(2):
                    per = caps[grp] // nd
                    for r in range(starts[grp] + k * per, starts[grp] + (k + 1) * per):
                        row_copy(grp, idx_ref[base + r], r).start()

        wd = wd_ref[0].astype(bf16)
        acc = None
        for k in range(na):
            part = jnp.dot(he_ref[k], wd[k * _TF:(k + 1) * _TF, :], preferred_element_type=f32)
            acc = part if acc is None else acc + part
        scaled = acc * g_ref[0]
        oa_ref[0] = scaled[:caps[0]]
        ob_ref[0] = scaled[caps[0]:]

    @pl.when((e == N_EXPERTS - 1) & (s == na + nd - 1))
    def _():
        wait_all_rows()


def _expert_ffn(idx_a, idx_b, h2a, h2b, g_a, g_b, w_gate, w_up, w_down):
    caps = (idx_a.shape[1], idx_b.shape[1])
    rows = caps[0] + caps[1]
    na = D_FF // _TF
    nd = D // _TN_DOWN
    idx = jnp.concatenate([idx_a, idx_b], axis=1).reshape(N_EXPERTS * rows)
    g3 = jnp.concatenate([g_a, g_b], axis=1)[:, :, None]
    body = functools.partial(_expert_body, caps=caps, na=na, nd=nd)
    up_tile = lambda e, s, idx: (e, 0, jnp.minimum(s, na - 1))
    down_tile = lambda e, s, idx: (e, 0, jnp.maximum(s - na, 0))
    grid_spec = pltpu.PrefetchScalarGridSpec(
        num_scalar_prefetch=1,
        grid=(N_EXPERTS, na + nd),
        in_specs=[pl.BlockSpec(memory_space=pl.ANY),
                  pl.BlockSpec(memory_space=pl.ANY),
                  pl.BlockSpec((1, rows, 1), lambda e, s, idx: (e, 0, 0)),
                  pl.BlockSpec((1, D, _TF), up_tile),
                  pl.BlockSpec((1, D, _TF), up_tile),
                  pl.BlockSpec((1, D_FF, _TN_DOWN), down_tile)],
        out_specs=[pl.BlockSpec((1, caps[0], _TN_DOWN), down_tile),
                   pl.BlockSpec((1, caps[1], _TN_DOWN), down_tile)],
        scratch_shapes=[pltpu.VMEM((rows, D // 2), jnp.uint32), pltpu.VMEM((rows, D), bf16),
                        pltpu.VMEM((na, rows, _TF), bf16), pltpu.SemaphoreType.DMA(())],
    )
    return pl.pallas_call(
        body,
        grid_spec=grid_spec,
        out_shape=[jax.ShapeDtypeStruct((N_EXPERTS, caps[0], D), f32),
                   jax.ShapeDtypeStruct((N_EXPERTS, caps[1], D), f32)],
        compiler_params=_cparams(("arbitrary", "arbitrary"), 60),
        name="expert_ffn",
    )(idx, h2a, h2b, g3, w_gate, w_up, w_down)


_TT = 128
_KB = 512
_RUN_MAX = _TT + 2 * SUBLANES
_RUN_PIECES = tuple(SUBLANES << i for i in reversed(range((_RUN_MAX // SUBLANES).bit_length())))
_STAGE_ROWS = pl.cdiv(N_EXPERTS * _RUN_MAX, _KB) * _KB
_STAGE_PIECES = tuple(SUBLANES << i
                      for i in reversed(range((_STAGE_ROWS // SUBLANES).bit_length())))


def _combine_body(offs_ref, ye_hbm, sp_ref, x1_ref, mods_ref, o_ref, stage_ref, acc_ref, sem,
                  *, cap):
    j = pl.program_id(0)
    slot = j & 1

    def wait_rows(total, sl):
        for size in _STAGE_PIECES:
            @pl.when((total & size) != 0)
            def _(size=size):
                pltpu.make_async_copy(ye_hbm.at[pl.ds(0, size), :],
                                      stage_ref.at[sl, pl.ds(0, size), :], sem.at[sl]).wait()

    def tile_copies(jj, sl, start):
        bases, offs = [], []
        base = jnp.int32(0)
        for e in range(N_EXPERTS):
            off = offs_ref[e, jj]
            end = offs_ref[e, jj + 1]
            off8 = (off // SUBLANES) * SUBLANES
            length = jnp.where(end > off, ((end + SUBLANES - 1) // SUBLANES) * SUBLANES - off8, 0)
            bases.append(base)
            offs.append(off8)
            src = e * cap + off8
            dst = base
            for size in _RUN_PIECES if start else ():
                piece = length & size

                @pl.when(piece != 0)
                def _(src=src, dst=dst, size=size):
                    pltpu.make_async_copy(
                        ye_hbm.at[pl.ds(pl.multiple_of(src, SUBLANES), size), :],
                        stage_ref.at[sl, pl.ds(pl.multiple_of(dst, SUBLANES), size), :],
                        sem.at[sl]).start()

                src = src + piece
                dst = dst + piece
            base = base + length
        return bases, offs, base

    @pl.when(j == 0)
    def _():
        stage_ref[...] = jnp.zeros_like(stage_ref)
        tile_copies(0, 0, True)

    @pl.when(j + 1 < pl.num_programs(0))
    def _():
        tile_copies(j + 1, 1 - slot, True)

    bases, offs, total = tile_copies(j, slot, False)
    wait_rows(total, slot)

    sp = sp_ref[...]
    kiota = lax.broadcasted_iota(i32, (1, _KB), 1)
    acc_ref[...] = jnp.zeros_like(acc_ref)

    def kblock(kb, c):
        hit = None
        for e in range(N_EXPERTS):
            local = sp[:, e:e + 1] + (bases[e] - offs[e] - kb * _KB)
            h = local == kiota
            hit = h if hit is None else (hit | h)
        s = jnp.where(hit, 1.0, 0.0).astype(bf16)
        rows = stage_ref[slot, pl.ds(pl.multiple_of(kb * _KB, _KB), _KB), :]
        hi = rows.astype(bf16)
        lo = (rows - hi.astype(f32)).astype(bf16)
        acc_ref[...] += (jnp.dot(s, hi, preferred_element_type=f32)
                         + jnp.dot(s, lo, preferred_element_type=f32))
        return c

    lax.fori_loop(0, (total + _KB - 1) // _KB, kblock, 0)
    o_ref[...] = x1_ref[...] + mods_ref[0][5:6] * acc_ref[...]


def _combine(offs, ye_flat, sp_n, x1, mods, cap, tiles_per_row):
    n = x1.shape[0]
    row = (lambda j: 1 + j // tiles_per_row) if tiles_per_row else (lambda j: 0)
    body = functools.partial(_combine_body, cap=cap)
    grid_spec = pltpu.PrefetchScalarGridSpec(
        num_scalar_prefetch=1,
        grid=(n // _TT,),
        in_specs=[pl.BlockSpec(memory_space=pl.ANY),
                  pl.BlockSpec((_TT, N_EXPERTS), lambda j, offs: (j, 0)),
                  pl.BlockSpec((_TT, D), lambda j, offs: (j, 0)),
                  pl.BlockSpec((1, 6, D), lambda j, offs: (row(j), 0, 0))],
        out_specs=pl.BlockSpec((_TT, D), lambda j, offs: (j, 0)),
        scratch_shapes=[pltpu.VMEM((2, _STAGE_ROWS, D), f32), pltpu.VMEM((_TT, D), f32),
                        pltpu.SemaphoreType.DMA((2,))],
    )
    return pl.pallas_call(
        body,
        grid_spec=grid_spec,
        out_shape=jax.ShapeDtypeStruct((n, D), f32),
        compiler_params=_cparams(("arbitrary",), 52),
        name="expert_combine",
    )(offs, ye_flat, sp_n, x1, mods)


def _moe(x1s, h2s, affs, mods, w_gate, w_up, w_down, tiles_per_row):
    routes = [_route(aff_t, CAPACITY_FACTOR * aff_t.shape[1] // N_EXPERTS) for aff_t in affs]
    (idx_a, g_a, _, _), (idx_b, g_b, _, _) = routes
    yes = _expert_ffn(idx_a, idx_b, h2s[0], h2s[1], g_a, g_b, w_gate, w_up, w_down)
    outs = []
    for ye, (idx, _, pos, sp), x1, tpr in zip(yes, routes, x1s, tiles_per_row):
        cap = idx.shape[1]
        offs = jnp.concatenate([pos[:, ::_TT], jnp.full((N_EXPERTS, 1), cap, i32)], axis=1)
        outs.append(_combine(offs, ye.reshape(N_EXPERTS * cap, D), sp.T, x1, mods, cap, tpr))
    return outs


def kernel(x_prompt, x_sample, cache_k, cache_v, state_fwd, state_bwd, c, c_ctx, w_mod, b_mod,
           norm_mix, w_in, q_norm, k_norm, rpb, conv_w, conv_b, lru_w_a, lru_b_a, lru_w_x,
           lru_b_x, lru_lambda, w_out, norm_ffn, w_router, w_gate, w_up, w_down):
    B, S, _ = x_prompt.shape
    DB, T, _ = x_sample.shape
    assert x_prompt.shape[2] == D and w_mod.shape[0] == 1, "one layer of width D"
    assert B % SUBLANES == 0 and SUBLANES % DB == 0 and DB < SUBLANES
    assert T == _NA_ROWS * GRID_W == _TM_IN and (B * S) % _TM_IN == 0 and S % _GATE_ROWS == 0
    l = 0

    cvecs = jnp.concatenate([c_ctx[None, :], c, jnp.zeros((SUBLANES - 1 - DB, D), f32)], axis=0)
    mods = _adaln_mods(cvecs, w_mod[l], b_mod[l][None, :]).reshape(SUBLANES, 6, D)

    heads_per_tile = _TN_IN // HEAD_DIM
    qn_t = jnp.tile(q_norm[l], heads_per_tile)[None, :]
    kn_t = jnp.tile(k_norm[l], heads_per_tile)[None, :]
    gmat = (jnp.kron(jnp.eye(heads_per_tile, dtype=f32), jnp.ones((HEAD_DIM, HEAD_DIM), f32))
            / HEAD_DIM).astype(bf16)
    nmix = norm_mix[l][None, :]
    nffn = norm_ffn[l][None, :]
    w_in_l = w_in[l]
    w_out_b = w_out[l].astype(bf16)
    w_router_t = w_router[l].T.astype(bf16)
    wcat = _lru_gate_weights(lru_w_a[l], lru_w_x[l])
    cb = conv_b[l][None, :]

    xp = x_prompt.reshape(B * S, D)
    xs = x_sample.reshape(DB * T, D)

    proj_c = _inproj(xp, mods, nmix, w_in_l, qn_t, kn_t, gmat, per_tile_rows=False)
    att_c, new_k, new_v = _ctx_attention(proj_c, B, S)
    zeros_c = jnp.zeros((B, LRU_W), f32)
    lru_c, htf, htb = _rglru(proj_c, wcat, conv_w[l], cb, lru_b_a[l], lru_b_x[l], lru_lambda[l],
                             zeros_c, zeros_c, T=S, nb=SUBLANES)
    x1_c, h2_c, aff_c = _outproj(att_c, lru_c, w_out_b, xp, mods, nffn, w_router_t, 0)

    nct = LRU_W // _LRU_TC

    def slabs(h0):
        return (h0.reshape(DB, nct, _LRU_HALVES, LANES).transpose(2, 0, 1, 3)
                .reshape(_LRU_HALVES * DB, nct * LANES))

    proj_l = _inproj(xs, mods, nmix, w_in_l, qn_t, kn_t, gmat, per_tile_rows=True)
    att_l = _lat_attention(proj_l, cache_k, cache_v, _pad_rpb(rpb[l]), l, DB, T)
    h0f_l = slabs(state_fwd[:, l])
    h0b_l = slabs(state_bwd[:, l])
    lru_l, _, _ = _rglru(proj_l, wcat, conv_w[l], cb, lru_b_a[l], lru_b_x[l], lru_lambda[l],
                         h0f_l, h0b_l, T=T, nb=DB)
    x1_l, h2_l, aff_l = _outproj(att_l, lru_l, w_out_b, xs, mods, nffn, w_router_t, T // _TM_OUT)

    y_c, y_l = _moe((x1_c, x1_l), (h2_c, h2_l), (aff_c, aff_l), mods,
                    w_gate[l], w_up[l], w_down[l], (0, T // _TT))

    return (y_c.reshape(B, S, D), y_l.reshape(DB, T, D), new_k, new_v,
            htf[:, None, :], htb[:, None, :])
```

```python
import functools

import jax
import jax.numpy as jnp
from jax import lax
from jax.experimental import pallas as pl
from jax.experimental.pallas import tpu as pltpu

f32 = jnp.float32
bf16 = jnp.bfloat16
i32 = jnp.int32

D = 2048
N_HEADS = 16
HEAD_DIM = 64
ATT_W = N_HEADS * HEAD_DIM
LRU_W = D - ATT_W
LRU_BLOCK = 64
LRU_C = 8.0
GRID_W = 64
WIN_H = 8
WIN_W = 16
N_EXPERTS = 16
CAPACITY_FACTOR = 2
D_FF = 2048
EPS = 1e-6
NEG_INF = -1e30
IN_COLS = 3 * ATT_W + 2 * LRU_W

LANES = 128
SUBLANES = 8
MXU_DIM = 256
UNSELECTED = -(1 << 30)

_MIB = 1024 * 1024


def _cparams(sem, vmem_mib):
    return pltpu.CompilerParams(dimension_semantics=sem, vmem_limit_bytes=vmem_mib * _MIB)


_TN_MOD = 1024


def _mods_body(c_ref, w_ref, b_ref, o_ref):
    c = c_ref[...]
    s = (c * jax.nn.sigmoid(c)).astype(bf16)
    o_ref[...] = jnp.dot(s, w_ref[...].astype(bf16), preferred_element_type=f32) + b_ref[...]


def _adaln_mods(cvecs, w_mod, b_mod):
    tn = _TN_MOD
    return pl.pallas_call(
        _mods_body,
        grid=(6 * D // tn,),
        in_specs=[pl.BlockSpec((SUBLANES, D), lambda j: (0, 0)),
                  pl.BlockSpec((D, tn), lambda j: (0, j)),
                  pl.BlockSpec((1, tn), lambda j: (0, j))],
        out_specs=pl.BlockSpec((SUBLANES, tn), lambda j: (0, j)),
        out_shape=jax.ShapeDtypeStruct((SUBLANES, 6 * D), f32),
        compiler_params=_cparams(("arbitrary",), 40),
        name="adaln_mods",
    )(cvecs, w_mod, b_mod)


_TM_IN = 1024
_TN_IN = 512
_NORM_ROWS = 16
_NORM_SLICES = 8


def _inproj_body(x0_ref, xn_ref, mods0_ref, modsn_ref, nw_ref, w_ref, qn_ref, kn_ref, g_ref,
                 o_ref, ha_ref, hb_ref):
    i = pl.program_id(0)
    j = pl.program_id(1)
    tm = ha_ref.shape[0]
    nw = nw_ref[...]
    n_qk_tiles = 2 * ATT_W // _TN_IN

    def norm_rows(x_ref, mods_ref, h_ref, r0):
        m = mods_ref[0]
        rows = pl.ds(pl.multiple_of(r0, _NORM_ROWS), _NORM_ROWS)
        x = x_ref[rows, :]
        ms = jnp.mean(x * x, axis=-1, keepdims=True)
        y = x * lax.rsqrt(ms + EPS) * nw
        h_ref[rows, :] = (y * (1.0 + m[1:2]) + m[0:1]).astype(bf16)

    @pl.when((i == 0) & (j == 0))
    def _():
        def first(c, carry):
            norm_rows(x0_ref, mods0_ref, ha_ref, c * _NORM_ROWS)
            return carry

        lax.fori_loop(0, tm // _NORM_ROWS, first, 0, unroll=4)

    def column_step(cur_ref, nxt_ref):
        slice_rows = tm // _NORM_SLICES
        r_slice = jnp.minimum(j, _NORM_SLICES - 1) * slice_rows
        for c in range(slice_rows // _NORM_ROWS):
            norm_rows(xn_ref, modsn_ref, nxt_ref, r_slice + c * _NORM_ROWS)

        acc = jnp.dot(cur_ref[...], w_ref[...].astype(bf16), preferred_element_type=f32)

        @pl.when(j < n_qk_tiles)
        def _():
            sq = acc * acc
            hi = sq.astype(bf16)
            lo = (sq - hi.astype(f32)).astype(bf16)
            g = g_ref[...]
            ms = (jnp.dot(hi, g, preferred_element_type=f32)
                  + jnp.dot(lo, g, preferred_element_type=f32))
            head_w = jnp.where(j < n_qk_tiles // 2, qn_ref[...], kn_ref[...])
            o_ref[...] = acc * lax.rsqrt(ms + EPS) * head_w

        @pl.when(j >= n_qk_tiles)
        def _():
            o_ref[...] = acc

    @pl.when((i & 1) == 0)
    def _():
        column_step(ha_ref, hb_ref)

    @pl.when((i & 1) == 1)
    def _():
        column_step(hb_ref, ha_ref)


def _inproj(x2d, mods, norm_w, w_in, qn_t, kn_t, gmat, per_tile_rows):
    n = x2d.shape[0]
    tm = _TM_IN
    last = n // tm - 1
    row = (lambda i: 1 + i) if per_tile_rows else (lambda i: 0)
    nxt = lambda i: jnp.minimum(i + 1, last)
    return pl.pallas_call(
        _inproj_body,
        grid=(n // tm, IN_COLS // _TN_IN),
        in_specs=[pl.BlockSpec((tm, D), lambda i, j: (0, 0)),
                  pl.BlockSpec((tm, D), lambda i, j: (nxt(i), 0)),
                  pl.BlockSpec((1, 6, D), lambda i, j: (row(0), 0, 0)),
                  pl.BlockSpec((1, 6, D), lambda i, j: (row(nxt(i)), 0, 0)),
                  pl.BlockSpec((1, D), lambda i, j: (0, 0)),
                  pl.BlockSpec((D, _TN_IN), lambda i, j: (0, j)),
                  pl.BlockSpec((1, _TN_IN), lambda i, j: (0, 0)),
                  pl.BlockSpec((1, _TN_IN), lambda i, j: (0, 0)),
                  pl.BlockSpec((_TN_IN, _TN_IN), lambda i, j: (0, 0))],
        out_specs=pl.BlockSpec((tm, _TN_IN), lambda i, j: (i, j)),
        out_shape=jax.ShapeDtypeStruct((n, IN_COLS), f32),
        scratch_shapes=[pltpu.VMEM((tm, D), bf16), pltpu.VMEM((tm, D), bf16)],
        compiler_params=_cparams(("arbitrary", "arbitrary"), 56),
        name="in_projection",
    )(x2d, x2d, mods, mods, norm_w, w_in, qn_t, kn_t, gmat)


def _softmax_pv(s_list, v_list):
    m = s_list[0].max(axis=-1, keepdims=True)
    for s in s_list[1:]:
        m = jnp.maximum(m, s.max(axis=-1, keepdims=True))
    o = None
    for s, v in zip(s_list, v_list):
        os_ = jnp.dot(jnp.exp(s - m).astype(bf16), v, preferred_element_type=f32)
        o = os_ if o is None else o + os_
    return o


def _normalise_pair(o0, o1, lane):
    left = lane < HEAD_DIM
    den = pltpu.roll(jnp.where(left, o1, o0), HEAD_DIM, 1)
    return jnp.where(left, o0, o1) / den


def _nt_dot(a, b):
    return lax.dot_general(a, b, (((1,), (1,)), ((), ())), preferred_element_type=f32)


_QK_SCALE = HEAD_DIM ** -0.5


def _ctx_attn_body(q_ref, k_ref, v_ref, o_ref, nk_ref, nv_ref):
    lane = lax.broadcasted_iota(i32, (1, LANES), 1)
    for p in range(ATT_W // LANES):
        cols = slice(p * LANES, (p + 1) * LANES)
        q = q_ref[:, cols] * _QK_SCALE
        kf = k_ref[:, cols]
        k = kf.astype(bf16)
        v = v_ref[:, cols]
        nk_ref[0, 0, 2 * p] = kf[:, :HEAD_DIM]
        nk_ref[0, 0, 2 * p + 1] = pltpu.roll(kf, HEAD_DIM, 1)[:, :HEAD_DIM]
        nv_ref[0, 0, 2 * p] = v[:, :HEAD_DIM]
        nv_ref[0, 0, 2 * p + 1] = pltpu.roll(v, HEAD_DIM, 1)[:, :HEAD_DIM]
        heads = []
        for hh in range(2):
            hm = (lane >= hh * HEAD_DIM) & (lane < (hh + 1) * HEAD_DIM)
            qh = jnp.where(hm, q, 0.0).astype(bf16)
            vh = jnp.where(hm, v, 1.0).astype(bf16)
            heads.append(_softmax_pv([_nt_dot(qh, k)], [vh]))
        o_ref[:, cols] = _normalise_pair(heads[0], heads[1], lane).astype(bf16)


def _ctx_attention(proj, batch, seq):
    n = proj.shape[0]
    cache = pl.BlockSpec((1, 1, N_HEADS, seq, HEAD_DIM), lambda b: (b, 0, 0, 0, 0))
    cache_shape = jax.ShapeDtypeStruct((batch, 1, N_HEADS, seq, HEAD_DIM), f32)
    return pl.pallas_call(
        _ctx_attn_body,
        grid=(batch,),
        in_specs=[pl.BlockSpec((seq, ATT_W), lambda b: (b, 0)),
                  pl.BlockSpec((seq, ATT_W), lambda b: (b, 1)),
                  pl.BlockSpec((seq, ATT_W), lambda b: (b, 2))],
        out_specs=[pl.BlockSpec((seq, ATT_W), lambda b: (b, 0)), cache, cache],
        out_shape=[jax.ShapeDtypeStruct((n, ATT_W), bf16), cache_shape, cache_shape],
        compiler_params=_cparams(("arbitrary",), 32),
        name="context_attention",
    )(proj, proj, proj)


_NA_ROWS = 16
_NA_QROWS = 8
_NA_KROWS = 12


def _pad_rpb(rpb):
    nd, nc = 2 * WIN_H - 1, 2 * WIN_W - 1
    return jnp.pad(rpb.astype(f32), ((0, 0), (0, 2 * SUBLANES - nd), (0, LANES - nc)))


def _build_na_bias(rpb_ref, bias_ref):
    qc = lax.broadcasted_iota(i32, (GRID_W, LANES), 0)
    lane = lax.broadcasted_iota(i32, (GRID_W, LANES), 1)
    kc = lane & (GRID_W - 1)
    cs = jnp.clip(qc - WIN_W // 2, 0, GRID_W - WIN_W)
    col_ok = (kc >= cs) & (kc < cs + WIN_W)
    left = lane < GRID_W
    neg = jnp.full((GRID_W, LANES), NEG_INF, f32)
    for hh in range(2):
        tbl = rpb_ref[hh]
        tiles = {}

        def toeplitz(d, odd):
            if (d, odd) not in tiles:
                row = jnp.broadcast_to(tbl[d:d + 1, :], (GRID_W, LANES))
                shift = (LANES - (WIN_W - 1) + odd * GRID_W) % LANES
                t = pltpu.roll(row, shift, 1, stride=1, stride_axis=0)
                tiles[(d, odd)] = jnp.where(col_ok, t, NEG_INF)
            return tiles[(d, odd)]

        for j in range(2):
            for rq in range(_NA_QROWS):
                rq_abs = j * _NA_QROWS + rq
                rs = min(max(rq_abs - WIN_H // 2, 0), _NA_ROWS - WIN_H)
                for kp in range(_NA_KROWS // 2):
                    parts = []
                    for odd in range(2):
                        rk_abs = j * (_NA_ROWS - _NA_KROWS) + 2 * kp + odd
                        if rs <= rk_abs < rs + WIN_H:
                            parts.append(toeplitz(rk_abs - rq_abs + WIN_H - 1, odd))
                        else:
                            parts.append(neg)
                    bias_ref[hh, j, rq * GRID_W:(rq + 1) * GRID_W,
                             kp * LANES:(kp + 1) * LANES] = jnp.where(left, parts[0], parts[1])


def _lat_attn_body(q_ref, k_ref, v_ref, ck_ref, cv_ref, rpb_ref, o_ref, bias_ref):
    @pl.when(pl.program_id(1) == 0)
    def _():
        _build_na_bias(rpb_ref, bias_ref)

    lane = lax.broadcasted_iota(i32, (1, LANES), 1)
    nq = _NA_QROWS * GRID_W
    nk = _NA_KROWS * GRID_W
    kstep = (_NA_ROWS - _NA_KROWS) * GRID_W
    src = lax.broadcasted_iota(i32, (HEAD_DIM, LANES), 0)
    dst = lax.broadcasted_iota(i32, (HEAD_DIM, LANES), 1)
    ck = None
    cv_h = []
    for hh in range(2):
        place = jnp.where(dst == src + hh * HEAD_DIM, 1.0, 0.0).astype(bf16)
        part = jnp.dot(ck_ref[0, 0, hh].astype(bf16), place, preferred_element_type=f32)
        ck = part if ck is None else ck + part
        placed = jnp.dot(cv_ref[0, 0, hh].astype(bf16), place, preferred_element_type=f32)
        own = (lane >= hh * HEAD_DIM) & (lane < (hh + 1) * HEAD_DIM)
        cv_h.append(jnp.where(own, placed, 1.0).astype(bf16))
    ck = ck.astype(bf16)
    for j in range(2):
        q = q_ref[j * nq:(j + 1) * nq, :] * _QK_SCALE
        k = k_ref[j * kstep:j * kstep + nk, :].astype(bf16)
        v = v_ref[j * kstep:j * kstep + nk, :]
        heads = []
        for hh in range(2):
            hm = (lane >= hh * HEAD_DIM) & (lane < (hh + 1) * HEAD_DIM)
            qh = jnp.where(hm, q, 0.0).astype(bf16)
            s_win = _nt_dot(qh, k) + bias_ref[hh, j]
            s_ctx = _nt_dot(qh, ck)
            vh = jnp.where(hm, v, 1.0).astype(bf16)
            heads.append(_softmax_pv([s_win, s_ctx], [vh, cv_h[hh]]))
        o_ref[j * nq:(j + 1) * nq, :] = _normalise_pair(heads[0], heads[1], lane).astype(bf16)


def _lat_attention(proj, ck, cv, rpb_p, layer, batch, seq):
    n = proj.shape[0]
    past = ck.shape[3]
    cache = pl.BlockSpec((1, 1, 2, past, HEAD_DIM), lambda p, b: (b, layer, p, 0, 0))
    npair = ATT_W // LANES
    nq = _NA_QROWS * GRID_W
    nk = _NA_KROWS * GRID_W
    return pl.pallas_call(
        _lat_attn_body,
        grid=(npair, batch),
        in_specs=[pl.BlockSpec((seq, LANES), lambda p, b: (b, p)),
                  pl.BlockSpec((seq, LANES), lambda p, b: (b, npair + p)),
                  pl.BlockSpec((seq, LANES), lambda p, b: (b, 2 * npair + p)),
                  cache, cache,
                  pl.BlockSpec((2, 2 * SUBLANES, LANES), lambda p, b: (p, 0, 0))],
        out_specs=pl.BlockSpec((seq, LANES), lambda p, b: (b, p)),
        out_shape=jax.ShapeDtypeStruct((n, ATT_W), bf16),
        scratch_shapes=[pltpu.VMEM((2, 2, nq, nk), f32)],
        compiler_params=_cparams(("arbitrary", "arbitrary"), 48),
        name="neighbourhood_attention",
    )(proj, proj, proj, ck, cv, rpb_p)


_LRU_TC = 256
_GATE_ROWS = 256
_LRU_HALVES = _LRU_TC // LANES


def _gelu_tanh(x):
    c = 0.7978845608028654
    return x * (0.5 * (1.0 + jnp.tanh(c * (x + 0.044715 * (x * x * x)))))


def _lru_body(xr_ref, gr_ref, w_ref, cw_ref, cb_ref, ba_ref, bx_ref, lam_ref, h0f_ref, h0b_ref,
              y_ref, htf_ref, htb_ref, xc_ref, af_ref, uf_ref, ab_ref, ub_ref, *, T, nb):
    nslab = _LRU_HALVES * nb // SUBLANES
    R = nb * T
    x = xr_ref[...]
    t = lax.broadcasted_iota(i32, (R, 1), 0) & (T - 1)
    cw = cw_ref[...]
    xc = jnp.where(t >= 2, pltpu.roll(x, 2, 0), 0.0) * cw[0:1]
    xc = xc + jnp.where(t >= 1, pltpu.roll(x, 1, 0), 0.0) * cw[1:2]
    xc = xc + x * cw[2:3]
    xc = xc + jnp.where(t < T - 1, pltpu.roll(x, R - 1, 0), 0.0) * cw[3:4]
    xc_ref[...] = xc + cb_ref[...]

    nl = -lam_ref[...]
    sp = jnp.maximum(nl, 0.0) + jnp.log1p(jnp.exp(-jnp.abs(nl)))
    half_c_sp = (0.5 * LRU_C) * sp
    w = w_ref[0]
    dirs = ((af_ref, uf_ref), (ab_ref, ub_ref))
    for b in range(nb):
        for ci in range(T // _GATE_ROWS):
            r0 = b * T + ci * _GATE_ROWS
            xcc = xc_ref[r0:r0 + _GATE_ROWS, :]
            half_x = 0.5 * xcc
            z = jnp.dot(xcc.astype(bf16), w, preferred_element_type=f32)
            for d, (a_ref, u_ref) in enumerate(dirs):
                c0 = d * 2 * _LRU_TC
                tr = jnp.tanh(0.5 * (z[:, c0:c0 + _LRU_TC] + ba_ref[d:d + 1]))
                tg = jnp.tanh(0.5 * (z[:, c0 + _LRU_TC:c0 + 2 * _LRU_TC] + bx_ref[d:d + 1]))
                neg_log_a = (1.0 + tr) * half_c_sp[d:d + 1]
                a = jnp.exp(-neg_log_a)
                one_m_a2 = jnp.tanh(neg_log_a) * (1.0 + a * a)
                root = jnp.where(one_m_a2 > 0.0, one_m_a2 * lax.rsqrt(one_m_a2), 0.0)
                u = root * ((1.0 + tg) * half_x)
                for hf in range(_LRU_HALVES):
                    s, sub = divmod(hf * nb + b, SUBLANES)
                    rows = pl.ds(ci * _GATE_ROWS * SUBLANES + sub, _GATE_ROWS, stride=SUBLANES)
                    a_ref[s, rows, :] = a[:, hf * LANES:(hf + 1) * LANES]
                    u_ref[s, rows, :] = u[:, hf * LANES:(hf + 1) * LANES]

    def step(i, carry):
        rf = pl.ds(pl.multiple_of(i * SUBLANES, SUBLANES), SUBLANES)
        rb = pl.ds(pl.multiple_of((T - 1 - i) * SUBLANES, SUBLANES), SUBLANES)
        new = []
        for s in range(nslab):
            hf_ = af_ref[s, rf, :] * carry[2 * s] + uf_ref[s, rf, :]
            uf_ref[s, rf, :] = hf_
            hb_ = ab_ref[s, rb, :] * carry[2 * s + 1] + ub_ref[s, rb, :]
            ub_ref[s, rb, :] = hb_
            new += [hf_, hb_]
        return tuple(new)

    init = []
    for s in range(nslab):
        init += [h0f_ref[:, s * LANES:(s + 1) * LANES], h0b_ref[:, s * LANES:(s + 1) * LANES]]
    fin = lax.fori_loop(0, T, step, tuple(init), unroll=8)
    for s in range(nslab):
        htf_ref[:, s * LANES:(s + 1) * LANES] = fin[2 * s]
        htb_ref[:, s * LANES:(s + 1) * LANES] = fin[2 * s + 1]

    for b in range(nb):
        for hf in range(_LRU_HALVES):
            s, sub = divmod(hf * nb + b, SUBLANES)
            rows = pl.ds(sub, T, stride=SUBLANES)
            hs = uf_ref[s, rows, :] + ub_ref[s, rows, :]
            g = gr_ref[b * T:(b + 1) * T, hf * LANES:(hf + 1) * LANES]
            y_ref[b * T:(b + 1) * T, hf * LANES:(hf + 1) * LANES] = (
                hs * _gelu_tanh(g)).astype(bf16)


def _rglru(proj, wcat, conv_w, conv_b, b_a, b_x, lam, h0f, h0b, T, nb):
    n = proj.shape[0]
    nslab = _LRU_HALVES * nb // SUBLANES
    W = nslab * LANES
    R = nb * T
    nct = LRU_W // _LRU_TC
    xr0 = 3 * ATT_W // _LRU_TC
    gr0 = xr0 + nct
    chan = lambda k: pl.BlockSpec((k, _LRU_TC), lambda g, j: (0, j))
    slab = pl.BlockSpec((SUBLANES, W), lambda g, j: (g, j))
    body = functools.partial(_lru_body, T=T, nb=nb)
    return pl.pallas_call(
        body,
        grid=(n // R, nct),
        in_specs=[pl.BlockSpec((R, _LRU_TC), lambda g, j: (g, xr0 + j)),
                  pl.BlockSpec((R, _LRU_TC), lambda g, j: (g, gr0 + j)),
                  pl.BlockSpec((1, _LRU_TC, 4 * _LRU_TC), lambda g, j: (j, 0, 0)),
                  chan(4), chan(1), chan(2), chan(2), chan(2), slab, slab],
        out_specs=[pl.BlockSpec((R, _LRU_TC), lambda g, j: (g, j)), slab, slab],
        out_shape=[jax.ShapeDtypeStruct((n, LRU_W), bf16),
                   jax.ShapeDtypeStruct(h0f.shape, f32),
                   jax.ShapeDtypeStruct(h0b.shape, f32)],
        scratch_shapes=[pltpu.VMEM((R, _LRU_TC), f32)]
        + [pltpu.VMEM((nslab, T * SUBLANES, LANES), f32) for _ in range(4)],
        compiler_params=_cparams(("arbitrary", "arbitrary"), 56),
        name="rg_lru",
    )(proj, proj, wcat, conv_w, conv_b, b_a, b_x, lam, h0f, h0b)


def _lru_gate_weights(w_a, w_x):
    per_tile = _LRU_TC // LRU_BLOCK
    nct = LRU_W // _LRU_TC
    eye = jnp.eye(per_tile, dtype=f32)

    def bd(w):
        w = w.reshape(nct, per_tile, LRU_BLOCK, LRU_BLOCK)
        m = w[:, :, :, None, :] * eye[None, :, None, :, None]
        return m.reshape(nct, _LRU_TC, _LRU_TC)

    return jnp.concatenate([bd(w_a[0]), bd(w_x[0]), bd(w_a[1]), bd(w_x[1])], axis=-1).astype(bf16)


_TM_OUT = 512


def _outproj_body(att_ref, lru_ref, w_ref, x_ref, mods_ref, nw_ref, wr_ref,
                  x1_ref, h2_ref, aff_ref):
    mix = (jnp.dot(att_ref[...], w_ref[0:ATT_W, :], preferred_element_type=f32)
           + jnp.dot(lru_ref[...], w_ref[ATT_W:D, :], preferred_element_type=f32))
    m = mods_ref[0]
    x1 = x_ref[...] + m[2:3] * mix
    x1_ref[...] = x1
    ms = jnp.mean(x1 * x1, axis=-1, keepdims=True)
    h2 = (x1 * lax.rsqrt(ms + EPS) * nw_ref[...]) * (1.0 + m[4:5]) + m[3:4]
    h2_ref[...] = pltpu.pack_elementwise([h2[:, :D // 2], h2[:, D // 2:]], packed_dtype=bf16)
    logits = _nt_dot(wr_ref[...], h2.astype(bf16))
    e = jnp.exp(logits - logits.max(axis=0, keepdims=True))
    aff_ref[...] = e / e.sum(axis=0, keepdims=True)


def _outproj(att, lru, w_out_b, x2d, mods, norm_w, w_router_t, tiles_per_row):
    n = x2d.shape[0]
    tm = _TM_OUT
    row = (lambda i: 1 + i // tiles_per_row) if tiles_per_row else (lambda i: 0)
    return pl.pallas_call(
        _outproj_body,
        grid=(n // tm,),
        in_specs=[pl.BlockSpec((tm, ATT_W), lambda i: (i, 0)),
                  pl.BlockSpec((tm, LRU_W), lambda i: (i, 0)),
                  pl.BlockSpec((D, D), lambda i: (0, 0)),
                  pl.BlockSpec((tm, D), lambda i: (i, 0)),
                  pl.BlockSpec((1, 6, D), lambda i: (row(i), 0, 0)),
                  pl.BlockSpec((1, D), lambda i: (0, 0)),
                  pl.BlockSpec((N_EXPERTS, D), lambda i: (0, 0))],
        out_specs=[pl.BlockSpec((tm, D), lambda i: (i, 0)),
                   pl.BlockSpec((tm, D // 2), lambda i: (i, 0)),
                   pl.BlockSpec((N_EXPERTS, tm), lambda i: (0, i))],
        out_shape=[jax.ShapeDtypeStruct((n, D), f32),
                   jax.ShapeDtypeStruct((n, D // 2), jnp.uint32),
                   jax.ShapeDtypeStruct((N_EXPERTS, n), f32)],
        compiler_params=_cparams(("arbitrary",), 52),
        name="out_projection_router",
    )(att, lru, w_out_b, x2d, mods, norm_w, w_router_t)


def _prefix_sum_lanes(x, lane, n):
    s = 1
    while s < n:
        x = x + jnp.where(lane >= s, pltpu.roll(x, s, 1), 0)
        s *= 2
    return x


def _route_body(aff_ref, idx_ref, g_ref, pos_ref, sp_ref, sel_ref, rem_ref, *, n, cap):
    aff = aff_ref[...]
    lane = lax.broadcasted_iota(i32, (N_EXPERTS, n), 1)

    def count(mask):
        return jnp.sum(jnp.where(mask, 1.0, 0.0), axis=1, keepdims=True).astype(i32)

    def as_value(bits):
        return pltpu.bitcast(jnp.broadcast_to(bits, (N_EXPERTS, LANES)), f32)[:, :1]

    def search(i, thr):
        cand = thr | jnp.left_shift(jnp.int32(1), 30 - i)
        return jnp.where(count(aff >= as_value(cand)) >= cap, cand, thr)

    thr = lax.fori_loop(0, 31, search, jnp.zeros((N_EXPERTS, 1), i32))
    sure = aff >= as_value(thr + 1)
    sel_ref[...] = sure.astype(i32)
    rem_ref[...] = ((aff >= as_value(thr)) & jnp.logical_not(sure)).astype(i32)

    def more(need):
        return jnp.max(need) > 0

    def take_largest(need):
        rem = rem_ref[...] > 0
        top = jnp.max(jnp.where(rem, aff, -1.0), axis=1, keepdims=True)
        is_top = rem & (aff == top) & (need > 0)
        rank = _prefix_sum_lanes(is_top.astype(i32), lane, n)
        pick = is_top & ((count(is_top) <= need) | (rank <= need))
        sel_ref[...] = sel_ref[...] | pick.astype(i32)
        rem_ref[...] = (rem & jnp.logical_not(is_top)).astype(i32)
        return need - count(pick)

    lax.while_loop(more, take_largest, cap - count(sure))
    sel_i = sel_ref[...]
    sel = sel_i > 0
    pos = _prefix_sum_lanes(sel_i, lane, n) - sel_i
    pos_ref[...] = pos
    sp_ref[...] = jnp.where(sel, pos, UNSELECTED)

    key = jnp.where(sel, lane - pos, -1)
    idx = lane
    g = aff
    b = 0
    while (1 << b) < n:
        s = 1 << b
        k_sh = pltpu.roll(key, n - s, 1)
        take = (k_sh >= 0) & ((jnp.right_shift(k_sh, b) & 1) == 1)
        stay = (key >= 0) & ((jnp.right_shift(key, b) & 1) == 0)
        idx = jnp.where(take, pltpu.roll(idx, n - s, 1), idx)
        g = jnp.where(take, pltpu.roll(g, n - s, 1), g)
        key = jnp.where(take, k_sh, jnp.where(stay, key, -1))
        b += 1
    idx_ref[...] = idx[:, :cap]
    g_ref[...] = g[:, :cap]


def _route(aff_t, cap):
    n = aff_t.shape[1]
    body = functools.partial(_route_body, n=n, cap=cap)
    full = lambda shape: pl.BlockSpec(shape, lambda: (0, 0))
    return pl.pallas_call(
        body,
        in_specs=[full((N_EXPERTS, n))],
        out_specs=[full((N_EXPERTS, cap)), full((N_EXPERTS, cap)),
                   full((N_EXPERTS, n)), full((N_EXPERTS, n))],
        out_shape=[jax.ShapeDtypeStruct((N_EXPERTS, cap), i32),
                   jax.ShapeDtypeStruct((N_EXPERTS, cap), f32),
                   jax.ShapeDtypeStruct((N_EXPERTS, n), i32),
                   jax.ShapeDtypeStruct((N_EXPERTS, n), i32)],
        scratch_shapes=[pltpu.VMEM((N_EXPERTS, n), i32), pltpu.VMEM((N_EXPERTS, n), i32)],
        compiler_params=pltpu.CompilerParams(vmem_limit_bytes=32 * _MIB),
        name="expert_choice_routing",
    )(aff_t)


_TF = 512
_TN_DOWN = 512


def _expert_body(idx_ref, h2a_hbm, h2b_hbm, g_ref, wg_ref, wu_ref, wd_ref, oa_ref, ob_ref,
                 xe_ref, xb_ref, he_ref, sem, *, caps, na, nd):
    e = pl.program_id(0)
    s = pl.program_id(1)
    srcs = (h2a_hbm, h2b_hbm)
    starts = (0, caps[0])
    rows_total = caps[0] + caps[1]

    def row_copy(grp, tok, r):
        return pltpu.make_async_copy(srcs[grp].at[pl.ds(tok, 1), :], xe_ref.at[pl.ds(r, 1), :], sem)

    def wait_all_rows():
        for grp in range(2):
            pltpu.make_async_copy(srcs[grp].at[pl.ds(0, caps[grp]), :],
                                  xe_ref.at[pl.ds(starts[grp], caps[grp]), :], sem).wait()

    @pl.when((e == 0) & (s == 0))
    def _():
        for grp in range(2):
            def issue(r, c, grp=grp):
                row_copy(grp, idx_ref[starts[grp] + r], starts[grp] + r).start()
                return c

            lax.fori_loop(0, caps[grp], issue, 0, unroll=8)

    @pl.when(s == 0)
    def _():
        wait_all_rows()
        packed = xe_ref[...]
        for part in range(2):
            xb_ref[:, part * (D // 2):(part + 1) * (D // 2)] = pltpu.unpack_elementwise(
                packed, index=part, packed_dtype=bf16, unpacked_dtype=f32).astype(bf16)

    @pl.when(s < na)
    def _():
        wg = wg_ref[0].astype(bf16)
        wu = wu_ref[0].astype(bf16)
        half = rows_total // 2
        for rows in (slice(0, half), slice(half, rows_total)):
            xb = xb_ref[rows, :]
            gp = jnp.dot(xb, wg, preferred_element_type=f32)
            up = jnp.dot(xb, wu, preferred_element_type=f32)
            he_ref[s, rows, :] = ((gp * jax.nn.sigmoid(gp)) * up).astype(bf16)

    @pl.when(s >= na)
    def _():
        base = jnp.minimum(e + 1, N_EXPERTS - 1) * rows_total
        for k in range(nd):
            @pl.when(s - na == k)
            def _(k=k):
                for grp in range(2):
                    per = caps[grp] // nd
                    for r in range(starts[grp] + k * per, starts[grp] + (k + 1) * per):
                        row_copy(grp, idx_ref[base + r], r).start()

        wd = wd_ref[0].astype(bf16)
        acc = None
        for k in range(na):
            part = jnp.dot(he_ref[k], wd[k * _TF:(k + 1) * _TF, :], preferred_element_type=f32)
            acc = part if acc is None else acc + part
        scaled = acc * g_ref[0]
        oa_ref[0] = scaled[:caps[0]]
        ob_ref[0] = scaled[caps[0]:]

    @pl.when((e == N_EXPERTS - 1) & (s == na + nd - 1))
    def _():
        wait_all_rows()


def _expert_ffn(idx_a, idx_b, h2a, h2b, g_a, g_b, w_gate, w_up, w_down):
    caps = (idx_a.shape[1], idx_b.shape[1])
    rows = caps[0] + caps[1]
    na = D_FF // _TF
    nd = D // _TN_DOWN
    idx = jnp.concatenate([idx_a, idx_b], axis=1).reshape(N_EXPERTS * rows)
    g3 = jnp.concatenate([g_a, g_b], axis=1)[:, :, None]
    body = functools.partial(_expert_body, caps=caps, na=na, nd=nd)
    up_tile = lambda e, s, idx: (e, 0, jnp.minimum(s, na - 1))
    down_tile = lambda e, s, idx: (e, 0, jnp.maximum(s - na, 0))
    grid_spec = pltpu.PrefetchScalarGridSpec(
        num_scalar_prefetch=1,
        grid=(N_EXPERTS, na + nd),
        in_specs=[pl.BlockSpec(memory_space=pl.ANY),
                  pl.BlockSpec(memory_space=pl.ANY),
                  pl.BlockSpec((1, rows, 1), lambda e, s, idx: (e, 0, 0)),
                  pl.BlockSpec((1, D, _TF), up_tile),
                  pl.BlockSpec((1, D, _TF), up_tile),
                  pl.BlockSpec((1, D_FF, _TN_DOWN), down_tile)],
        out_specs=[pl.BlockSpec((1, caps[0], _TN_DOWN), down_tile),
                   pl.BlockSpec((1, caps[1], _TN_DOWN), down_tile)],
        scratch_shapes=[pltpu.VMEM((rows, D // 2), jnp.uint32), pltpu.VMEM((rows, D), bf16),
                        pltpu.VMEM((na, rows, _TF), bf16), pltpu.SemaphoreType.DMA(())],
    )
    return pl.pallas_call(
        body,
        grid_spec=grid_spec,
        out_shape=[jax.ShapeDtypeStruct((N_EXPERTS, caps[0], D), f32),
                   jax.ShapeDtypeStruct((N_EXPERTS, caps[1], D), f32)],
        compiler_params=_cparams(("arbitrary", "arbitrary"), 60),
        name="expert_ffn",
    )(idx, h2a, h2b, g3, w_gate, w_up, w_down)


_TT = 128
_KB = 512
_KB_TYPICAL = 416
_RUN_MAX = _TT + 2 * SUBLANES
_RUN_PIECES = tuple(SUBLANES << i for i in reversed(range((_RUN_MAX // SUBLANES).bit_length())))
_STAGE_ROWS = pl.cdiv(N_EXPERTS * _RUN_MAX, _KB) * _KB
_STAGE_PIECES = tuple(SUBLANES << i
                      for i in reversed(range((_STAGE_ROWS // SUBLANES).bit_length())))


def _combine_body(offs_ref, ye_hbm, sp_ref, x1_ref, mods_ref, o_ref, stage_ref, acc_ref, sem,
                  *, cap):
    j = pl.program_id(0)
    slot = j & 1

    def wait_rows(total, sl):
        for size in _STAGE_PIECES:
            @pl.when((total & size) != 0)
            def _(size=size):
                pltpu.make_async_copy(ye_hbm.at[pl.ds(0, size), :],
                                      stage_ref.at[sl, pl.ds(0, size), :], sem.at[sl]).wait()

    def tile_copies(jj, sl, start):
        bases, offs = [], []
        base = jnp.int32(0)
        for e in range(N_EXPERTS):
            off = offs_ref[e, jj]
            end = offs_ref[e, jj + 1]
            off8 = (off // SUBLANES) * SUBLANES
            length = jnp.where(end > off, ((end + SUBLANES - 1) // SUBLANES) * SUBLANES - off8, 0)
            bases.append(base)
            offs.append(off8)
            src = e * cap + off8
            dst = base
            for size in _RUN_PIECES if start else ():
                piece = length & size

                @pl.when(piece != 0)
                def _(src=src, dst=dst, size=size):
                    pltpu.make_async_copy(
                        ye_hbm.at[pl.ds(pl.multiple_of(src, SUBLANES), size), :],
                        stage_ref.at[sl, pl.ds(pl.multiple_of(dst, SUBLANES), size), :],
                        sem.at[sl]).start()

                src = src + piece
                dst = dst + piece
            base = base + length
        return bases, offs, base

    @pl.when(j == 0)
    def _():
        stage_ref[...] = jnp.zeros_like(stage_ref)
        tile_copies(0, 0, True)

    @pl.when(j + 1 < pl.num_programs(0))
    def _():
        tile_copies(j + 1, 1 - slot, True)

    bases, offs, total = tile_copies(j, slot, False)
    wait_rows(total, slot)

    sp = sp_ref[...]
    def scatter_rows(row0, nrows):
        kiota = lax.broadcasted_iota(i32, (1, nrows), 1)
        hit = None
        for e in range(N_EXPERTS):
            local = sp[:, e:e + 1] + (bases[e] - offs[e] - row0)
            h = local == kiota
            hit = h if hit is None else (hit | h)
        s = jnp.where(hit, 1.0, 0.0).astype(bf16)
        rows = stage_ref[slot, pl.ds(pl.multiple_of(row0, SUBLANES), nrows), :]
        hi = rows.astype(bf16)
        lo = (rows - hi.astype(f32)).astype(bf16)
        return (jnp.dot(s, hi, preferred_element_type=f32)
                + jnp.dot(s, lo, preferred_element_type=f32))

    gate = mods_ref[0][5:6]

    @pl.when(total <= _KB_TYPICAL)
    def _():
        o_ref[...] = x1_ref[...] + gate * scatter_rows(0, _KB_TYPICAL)

    @pl.when(total > _KB_TYPICAL)
    def _():
        acc_ref[...] = jnp.zeros_like(acc_ref)

        def kblock(kb, c):
            acc_ref[...] += scatter_rows(kb * _KB, _KB)
            return c

        lax.fori_loop(0, (total + _KB - 1) // _KB, kblock, 0)
        o_ref[...] = x1_ref[...] + gate * acc_ref[...]


def _combine(offs, ye_flat, sp_n, x1, mods, cap, tiles_per_row):
    n = x1.shape[0]
    row = (lambda j: 1 + j // tiles_per_row) if tiles_per_row else (lambda j: 0)
    body = functools.partial(_combine_body, cap=cap)
    grid_spec = pltpu.PrefetchScalarGridSpec(
        num_scalar_prefetch=1,
        grid=(n // _TT,),
        in_specs=[pl.BlockSpec(memory_space=pl.ANY),
                  pl.BlockSpec((_TT, N_EXPERTS), lambda j, offs: (j, 0)),
                  pl.BlockSpec((_TT, D), lambda j, offs: (j, 0)),
                  pl.BlockSpec((1, 6, D), lambda j, offs: (row(j), 0, 0))],
        out_specs=pl.BlockSpec((_TT, D), lambda j, offs: (j, 0)),
        scratch_shapes=[pltpu.VMEM((2, _STAGE_ROWS, D), f32), pltpu.VMEM((_TT, D), f32),
                        pltpu.SemaphoreType.DMA((2,))],
    )
    return pl.pallas_call(
        body,
        grid_spec=grid_spec,
        out_shape=jax.ShapeDtypeStruct((n, D), f32),
        compiler_params=_cparams(("arbitrary",), 52),
        name="expert_combine",
    )(offs, ye_flat, sp_n, x1, mods)


def _moe(x1s, h2s, affs, mods, w_gate, w_up, w_down, tiles_per_row):
    routes = [_route(aff_t, CAPACITY_FACTOR * aff_t.shape[1] // N_EXPERTS) for aff_t in affs]
    (idx_a, g_a, _, _), (idx_b, g_b, _, _) = routes
    yes = _expert_ffn(idx_a, idx_b, h2s[0], h2s[1], g_a, g_b, w_gate, w_up, w_down)
    outs = []
    for ye, (idx, _, pos, sp), x1, tpr in zip(yes, routes, x1s, tiles_per_row):
        cap = idx.shape[1]
        offs = jnp.concatenate([pos[:, ::_TT], jnp.full((N_EXPERTS, 1), cap, i32)], axis=1)
        outs.append(_combine(offs, ye.reshape(N_EXPERTS * cap, D), sp.T, x1, mods, cap, tpr))
    return outs


def kernel(x_prompt, x_sample, cache_k, cache_v, state_fwd, state_bwd, c, c_ctx, w_mod, b_mod,
           norm_mix, w_in, q_norm, k_norm, rpb, conv_w, conv_b, lru_w_a, lru_b_a, lru_w_x,
           lru_b_x, lru_lambda, w_out, norm_ffn, w_router, w_gate, w_up, w_down):
    B, S, _ = x_prompt.shape
    DB, T, _ = x_sample.shape
    assert x_prompt.shape[2] == D and w_mod.shape[0] == 1, "one layer of width D"
    assert B % SUBLANES == 0 and SUBLANES % DB == 0 and DB < SUBLANES
    assert T == _NA_ROWS * GRID_W == _TM_IN and (B * S) % _TM_IN == 0 and S % _GATE_ROWS == 0
    l = 0

    cvecs = jnp.concatenate([c_ctx[None, :], c, jnp.zeros((SUBLANES - 1 - DB, D), f32)], axis=0)
    mods = _adaln_mods(cvecs, w_mod[l], b_mod[l][None, :]).reshape(SUBLANES, 6, D)

    heads_per_tile = _TN_IN // HEAD_DIM
    qn_t = jnp.tile(q_norm[l], heads_per_tile)[None, :]
    kn_t = jnp.tile(k_norm[l], heads_per_tile)[None, :]
    gmat = (jnp.kron(jnp.eye(heads_per_tile, dtype=f32), jnp.ones((HEAD_DIM, HEAD_DIM), f32))
            / HEAD_DIM).astype(bf16)
    nmix = norm_mix[l][None, :]
    nffn = norm_ffn[l][None, :]
    w_in_l = w_in[l]
    w_out_b = w_out[l].astype(bf16)
    w_router_t = w_router[l].T.astype(bf16)
    wcat = _lru_gate_weights(lru_w_a[l], lru_w_x[l])
    cb = conv_b[l][None, :]

    xp = x_prompt.reshape(B * S, D)
    xs = x_sample.reshape(DB * T, D)

    proj_c = _inproj(xp, mods, nmix, w_in_l, qn_t, kn_t, gmat, per_tile_rows=False)
    att_c, new_k, new_v = _ctx_attention(proj_c, B, S)
    zeros_c = jnp.zeros((B, LRU_W), f32)
    lru_c, htf, htb = _rglru(proj_c, wcat, conv_w[l], cb, lru_b_a[l], lru_b_x[l], lru_lambda[l],
                             zeros_c, zeros_c, T=S, nb=SUBLANES)
    x1_c, h2_c, aff_c = _outproj(att_c, lru_c, w_out_b, xp, mods, nffn, w_router_t, 0)

    nct = LRU_W // _LRU_TC

    def slabs(h0):
        return (h0.reshape(DB, nct, _LRU_HALVES, LANES).transpose(2, 0, 1, 3)
                .reshape(_LRU_HALVES * DB, nct * LANES))

    proj_l = _inproj(xs, mods, nmix, w_in_l, qn_t, kn_t, gmat, per_tile_rows=True)
    att_l = _lat_attention(proj_l, cache_k, cache_v, _pad_rpb(rpb[l]), l, DB, T)
    h0f_l = slabs(state_fwd[:, l])
    h0b_l = slabs(state_bwd[:, l])
    lru_l, _, _ = _rglru(proj_l, wcat, conv_w[l], cb, lru_b_a[l], lru_b_x[l], lru_lambda[l],
                         h0f_l, h0b_l, T=T, nb=DB)
    x1_l, h2_l, aff_l = _outproj(att_l, lru_l, w_out_b, xs, mods, nffn, w_router_t, T // _TM_OUT)

    y_c, y_l = _moe((x1_c, x1_l), (h2_c, h2_l), (aff_c, aff_l), mods,
                    w_gate[l], w_up[l], w_down[l], (0, T // _TT))

    return (y_c.reshape(B, S, D), y_l.reshape(DB, T, D), new_k, new_v,
            htf[:, None, :], htb[:, None, :])
```

```python
import functools

import jax
import jax.numpy as jnp
from jax import lax
from jax.experimental import pallas as pl
from jax.experimental.pallas import tpu as pltpu

f32 = jnp.float32
bf16 = jnp.bfloat16
i32 = jnp.int32

D = 2048
N_HEADS = 16
HEAD_DIM = 64
ATT_W = N_HEADS * HEAD_DIM
LRU_W = D - ATT_W
LRU_BLOCK = 64
LRU_C = 8.0
GRID_W = 64
WIN_H = 8
WIN_W = 16
N_EXPERTS = 16
CAPACITY_FACTOR = 2
D_FF = 2048
EPS = 1e-6
NEG_INF = -1e30
IN_COLS = 3 * ATT_W + 2 * LRU_W

LANES = 128
SUBLANES = 8
MXU_DIM = 256
UNSELECTED = -(1 << 30)

_MIB = 1024 * 1024


def _cparams(sem, vmem_mib):
    return pltpu.CompilerParams(dimension_semantics=sem, vmem_limit_bytes=vmem_mib * _MIB)


_TN_MOD = 1024


def _mods_body(c_ref, w_ref, b_ref, o_ref):
    c = c_ref[...]
    s = (c * jax.nn.sigmoid(c)).astype(bf16)
    o_ref[...] = jnp.dot(s, w_ref[...].astype(bf16), preferred_element_type=f32) + b_ref[...]


def _adaln_mods(cvecs, w_mod, b_mod):
    tn = _TN_MOD
    return pl.pallas_call(
        _mods_body,
        grid=(6 * D // tn,),
        in_specs=[pl.BlockSpec((SUBLANES, D), lambda j: (0, 0)),
                  pl.BlockSpec((D, tn), lambda j: (0, j)),
                  pl.BlockSpec((1, tn), lambda j: (0, j))],
        out_specs=pl.BlockSpec((SUBLANES, tn), lambda j: (0, j)),
        out_shape=jax.ShapeDtypeStruct((SUBLANES, 6 * D), f32),
        compiler_params=_cparams(("arbitrary",), 40),
        name="adaln_mods",
    )(cvecs, w_mod, b_mod)


_TM_IN = 1024
_TN_IN = 512
_NORM_ROWS = 16
_NORM_SLICES = 8


def _inproj_body(x0_ref, xn_ref, mods0_ref, modsn_ref, nw_ref, w_ref, qn_ref, kn_ref, g_ref,
                 o_ref, ha_ref, hb_ref):
    i = pl.program_id(0)
    j = pl.program_id(1)
    tm = ha_ref.shape[0]
    nw = nw_ref[...]
    n_qk_tiles = 2 * ATT_W // _TN_IN

    def norm_rows(x_ref, mods_ref, h_ref, r0):
        m = mods_ref[0]
        rows = pl.ds(pl.multiple_of(r0, _NORM_ROWS), _NORM_ROWS)
        x = x_ref[rows, :]
        ms = jnp.mean(x * x, axis=-1, keepdims=True)
        y = x * lax.rsqrt(ms + EPS) * nw
        h_ref[rows, :] = (y * (1.0 + m[1:2]) + m[0:1]).astype(bf16)

    @pl.when((i == 0) & (j == 0))
    def _():
        def first(c, carry):
            norm_rows(x0_ref, mods0_ref, ha_ref, c * _NORM_ROWS)
            return carry

        lax.fori_loop(0, tm // _NORM_ROWS, first, 0, unroll=4)

    def column_step(cur_ref, nxt_ref):
        slice_rows = tm // _NORM_SLICES
        r_slice = jnp.minimum(j, _NORM_SLICES - 1) * slice_rows
        for c in range(slice_rows // _NORM_ROWS):
            norm_rows(xn_ref, modsn_ref, nxt_ref, r_slice + c * _NORM_ROWS)

        acc = jnp.dot(cur_ref[...], w_ref[...].astype(bf16), preferred_element_type=f32)

        @pl.when(j < n_qk_tiles)
        def _():
            sq = acc * acc
            hi = sq.astype(bf16)
            lo = (sq - hi.astype(f32)).astype(bf16)
            g = g_ref[...]
            ms = (jnp.dot(hi, g, preferred_element_type=f32)
                  + jnp.dot(lo, g, preferred_element_type=f32))
            head_w = jnp.where(j < n_qk_tiles // 2, qn_ref[...], kn_ref[...])
            o_ref[...] = acc * lax.rsqrt(ms + EPS) * head_w

        @pl.when(j >= n_qk_tiles)
        def _():
            o_ref[...] = acc

    @pl.when((i & 1) == 0)
    def _():
        column_step(ha_ref, hb_ref)

    @pl.when((i & 1) == 1)
    def _():
        column_step(hb_ref, ha_ref)


def _inproj(x2d, mods, norm_w, w_in, qn_t, kn_t, gmat, per_tile_rows):
    n = x2d.shape[0]
    tm = _TM_IN
    last = n // tm - 1
    row = (lambda i: 1 + i) if per_tile_rows else (lambda i: 0)
    nxt = lambda i: jnp.minimum(i + 1, last)
    return pl.pallas_call(
        _inproj_body,
        grid=(n // tm, IN_COLS // _TN_IN),
        in_specs=[pl.BlockSpec((tm, D), lambda i, j: (0, 0)),
                  pl.BlockSpec((tm, D), lambda i, j: (nxt(i), 0)),
                  pl.BlockSpec((1, 6, D), lambda i, j: (row(0), 0, 0)),
                  pl.BlockSpec((1, 6, D), lambda i, j: (row(nxt(i)), 0, 0)),
                  pl.BlockSpec((1, D), lambda i, j: (0, 0)),
                  pl.BlockSpec((D, _TN_IN), lambda i, j: (0, j)),
                  pl.BlockSpec((1, _TN_IN), lambda i, j: (0, 0)),
                  pl.BlockSpec((1, _TN_IN), lambda i, j: (0, 0)),
                  pl.BlockSpec((_TN_IN, _TN_IN), lambda i, j: (0, 0))],
        out_specs=pl.BlockSpec((tm, _TN_IN), lambda i, j: (i, j)),
        out_shape=jax.ShapeDtypeStruct((n, IN_COLS), f32),
        scratch_shapes=[pltpu.VMEM((tm, D), bf16), pltpu.VMEM((tm, D), bf16)],
        compiler_params=_cparams(("arbitrary", "arbitrary"), 56),
        name="in_projection",
    )(x2d, x2d, mods, mods, norm_w, w_in, qn_t, kn_t, gmat)


def _softmax_pv(s_list, v_list):
    m = s_list[0].max(axis=-1, keepdims=True)
    for s in s_list[1:]:
        m = jnp.maximum(m, s.max(axis=-1, keepdims=True))
    o = None
    for s, v in zip(s_list, v_list):
        os_ = jnp.dot(jnp.exp(s - m).astype(bf16), v, preferred_element_type=f32)
        o = os_ if o is None else o + os_
    return o


def _normalise_pair(o0, o1, lane):
    left = lane < HEAD_DIM
    den = pltpu.roll(jnp.where(left, o1, o0), HEAD_DIM, 1)
    return jnp.where(left, o0, o1) / den


def _nt_dot(a, b):
    return lax.dot_general(a, b, (((1,), (1,)), ((), ())), preferred_element_type=f32)


_QK_SCALE = HEAD_DIM ** -0.5


def _ctx_attn_body(q_ref, k_ref, v_ref, o_ref, nk_ref, nv_ref):
    lane = lax.broadcasted_iota(i32, (1, LANES), 1)
    for p in range(ATT_W // LANES):
        cols = slice(p * LANES, (p + 1) * LANES)
        q = q_ref[:, cols] * _QK_SCALE
        kf = k_ref[:, cols]
        k = kf.astype(bf16)
        v = v_ref[:, cols]
        nk_ref[0, 0, 2 * p] = kf[:, :HEAD_DIM]
        nk_ref[0, 0, 2 * p + 1] = pltpu.roll(kf, HEAD_DIM, 1)[:, :HEAD_DIM]
        nv_ref[0, 0, 2 * p] = v[:, :HEAD_DIM]
        nv_ref[0, 0, 2 * p + 1] = pltpu.roll(v, HEAD_DIM, 1)[:, :HEAD_DIM]
        heads = []
        for hh in range(2):
            hm = (lane >= hh * HEAD_DIM) & (lane < (hh + 1) * HEAD_DIM)
            qh = jnp.where(hm, q, 0.0).astype(bf16)
            vh = jnp.where(hm, v, 1.0).astype(bf16)
            heads.append(_softmax_pv([_nt_dot(qh, k)], [vh]))
        o_ref[:, cols] = _normalise_pair(heads[0], heads[1], lane).astype(bf16)


def _ctx_attention(proj, batch, seq):
    n = proj.shape[0]
    cache = pl.BlockSpec((1, 1, N_HEADS, seq, HEAD_DIM), lambda b: (b, 0, 0, 0, 0))
    cache_shape = jax.ShapeDtypeStruct((batch, 1, N_HEADS, seq, HEAD_DIM), f32)
    return pl.pallas_call(
        _ctx_attn_body,
        grid=(batch,),
        in_specs=[pl.BlockSpec((seq, ATT_W), lambda b: (b, 0)),
                  pl.BlockSpec((seq, ATT_W), lambda b: (b, 1)),
                  pl.BlockSpec((seq, ATT_W), lambda b: (b, 2))],
        out_specs=[pl.BlockSpec((seq, ATT_W), lambda b: (b, 0)), cache, cache],
        out_shape=[jax.ShapeDtypeStruct((n, ATT_W), bf16), cache_shape, cache_shape],
        compiler_params=_cparams(("arbitrary",), 32),
        name="context_attention",
    )(proj, proj, proj)


_NA_ROWS = 16
_NA_QROWS = 8
_NA_KROWS = 12


def _pad_rpb(rpb):
    nd, nc = 2 * WIN_H - 1, 2 * WIN_W - 1
    return jnp.pad(rpb.astype(f32), ((0, 0), (0, 2 * SUBLANES - nd), (0, LANES - nc)))


def _build_na_bias(rpb_ref, bias_ref):
    qc = lax.broadcasted_iota(i32, (GRID_W, LANES), 0)
    lane = lax.broadcasted_iota(i32, (GRID_W, LANES), 1)
    kc = lane & (GRID_W - 1)
    cs = jnp.clip(qc - WIN_W // 2, 0, GRID_W - WIN_W)
    col_ok = (kc >= cs) & (kc < cs + WIN_W)
    left = lane < GRID_W
    neg = jnp.full((GRID_W, LANES), NEG_INF, f32)
    for hh in range(2):
        tbl = rpb_ref[hh]
        tiles = {}

        def toeplitz(d, odd):
            if (d, odd) not in tiles:
                row = jnp.broadcast_to(tbl[d:d + 1, :], (GRID_W, LANES))
                shift = (LANES - (WIN_W - 1) + odd * GRID_W) % LANES
                t = pltpu.roll(row, shift, 1, stride=1, stride_axis=0)
                tiles[(d, odd)] = jnp.where(col_ok, t, NEG_INF)
            return tiles[(d, odd)]

        for j in range(2):
            for rq in range(_NA_QROWS):
                rq_abs = j * _NA_QROWS + rq
                rs = min(max(rq_abs - WIN_H // 2, 0), _NA_ROWS - WIN_H)
                for kp in range(_NA_KROWS // 2):
                    parts = []
                    for odd in range(2):
                        rk_abs = j * (_NA_ROWS - _NA_KROWS) + 2 * kp + odd
                        if rs <= rk_abs < rs + WIN_H:
                            parts.append(toeplitz(rk_abs - rq_abs + WIN_H - 1, odd))
                        else:
                            parts.append(neg)
                    bias_ref[hh, j, rq * GRID_W:(rq + 1) * GRID_W,
                             kp * LANES:(kp + 1) * LANES] = jnp.where(left, parts[0], parts[1])


def _lat_attn_body(q_ref, k_ref, v_ref, ck_ref, cv_ref, rpb_ref, o_ref, bias_ref):
    @pl.when(pl.program_id(1) == 0)
    def _():
        _build_na_bias(rpb_ref, bias_ref)

    lane = lax.broadcasted_iota(i32, (1, LANES), 1)
    nq = _NA_QROWS * GRID_W
    nk = _NA_KROWS * GRID_W
    kstep = (_NA_ROWS - _NA_KROWS) * GRID_W
    src = lax.broadcasted_iota(i32, (HEAD_DIM, LANES), 0)
    dst = lax.broadcasted_iota(i32, (HEAD_DIM, LANES), 1)
    ck = None
    cv_h = []
    for hh in range(2):
        place = jnp.where(dst == src + hh * HEAD_DIM, 1.0, 0.0).astype(bf16)
        part = jnp.dot(ck_ref[0, 0, hh].astype(bf16), place, preferred_element_type=f32)
        ck = part if ck is None else ck + part
        placed = jnp.dot(cv_ref[0, 0, hh].astype(bf16), place, preferred_element_type=f32)
        own = (lane >= hh * HEAD_DIM) & (lane < (hh + 1) * HEAD_DIM)
        cv_h.append(jnp.where(own, placed, 1.0).astype(bf16))
    ck = ck.astype(bf16)
    for j in range(2):
        q = q_ref[j * nq:(j + 1) * nq, :] * _QK_SCALE
        k = k_ref[j * kstep:j * kstep + nk, :].astype(bf16)
        v = v_ref[j * kstep:j * kstep + nk, :]
        heads = []
        for hh in range(2):
            hm = (lane >= hh * HEAD_DIM) & (lane < (hh + 1) * HEAD_DIM)
            qh = jnp.where(hm, q, 0.0).astype(bf16)
            s_win = _nt_dot(qh, k) + bias_ref[hh, j]
            s_ctx = _nt_dot(qh, ck)
            vh = jnp.where(hm, v, 1.0).astype(bf16)
            heads.append(_softmax_pv([s_win, s_ctx], [vh, cv_h[hh]]))
        o_ref[j * nq:(j + 1) * nq, :] = _normalise_pair(heads[0], heads[1], lane).astype(bf16)


def _lat_attention(proj, ck, cv, rpb_p, layer, batch, seq):
    n = proj.shape[0]
    past = ck.shape[3]
    cache = pl.BlockSpec((1, 1, 2, past, HEAD_DIM), lambda p, b: (b, layer, p, 0, 0))
    npair = ATT_W // LANES
    nq = _NA_QROWS * GRID_W
    nk = _NA_KROWS * GRID_W
    return pl.pallas_call(
        _lat_attn_body,
        grid=(npair, batch),
        in_specs=[pl.BlockSpec((seq, LANES), lambda p, b: (b, p)),
                  pl.BlockSpec((seq, LANES), lambda p, b: (b, npair + p)),
                  pl.BlockSpec((seq, LANES), lambda p, b: (b, 2 * npair + p)),
                  cache, cache,
                  pl.BlockSpec((2, 2 * SUBLANES, LANES), lambda p, b: (p, 0, 0))],
        out_specs=pl.BlockSpec((seq, LANES), lambda p, b: (b, p)),
        out_shape=jax.ShapeDtypeStruct((n, ATT_W), bf16),
        scratch_shapes=[pltpu.VMEM((2, 2, nq, nk), f32)],
        compiler_params=_cparams(("arbitrary", "arbitrary"), 48),
        name="neighbourhood_attention",
    )(proj, proj, proj, ck, cv, rpb_p)


_LRU_TC = 256
_GATE_ROWS = 256
_LRU_HALVES = _LRU_TC // LANES


def _gelu_tanh(x):
    c = 0.7978845608028654
    return x * (0.5 * (1.0 + jnp.tanh(c * (x + 0.044715 * (x * x * x)))))


def _lru_body(xr_ref, gr_ref, w_ref, cw_ref, cb_ref, ba_ref, bx_ref, lam_ref, h0f_ref, h0b_ref,
              y_ref, htf_ref, htb_ref, xc_ref, af_ref, uf_ref, ab_ref, ub_ref, *, T, nb):
    nslab = _LRU_HALVES * nb // SUBLANES
    R = nb * T
    x = xr_ref[...]
    t = lax.broadcasted_iota(i32, (R, 1), 0) & (T - 1)
    cw = cw_ref[...]
    xc = jnp.where(t >= 2, pltpu.roll(x, 2, 0), 0.0) * cw[0:1]
    xc = xc + jnp.where(t >= 1, pltpu.roll(x, 1, 0), 0.0) * cw[1:2]
    xc = xc + x * cw[2:3]
    xc = xc + jnp.where(t < T - 1, pltpu.roll(x, R - 1, 0), 0.0) * cw[3:4]
    xc_ref[...] = xc + cb_ref[...]

    nl = -lam_ref[...]
    sp = jnp.maximum(nl, 0.0) + jnp.log1p(jnp.exp(-jnp.abs(nl)))
    half_c_sp = (0.5 * LRU_C) * sp
    w = w_ref[0]
    dirs = ((af_ref, uf_ref), (ab_ref, ub_ref))
    for b in range(nb):
        for ci in range(T // _GATE_ROWS):
            r0 = b * T + ci * _GATE_ROWS
            xcc = xc_ref[r0:r0 + _GATE_ROWS, :]
            half_x = 0.5 * xcc
            z = jnp.dot(xcc.astype(bf16), w, preferred_element_type=f32)
            for d, (a_ref, u_ref) in enumerate(dirs):
                c0 = d * 2 * _LRU_TC
                tr = jnp.tanh(0.5 * (z[:, c0:c0 + _LRU_TC] + ba_ref[d:d + 1]))
                tg = jnp.tanh(0.5 * (z[:, c0 + _LRU_TC:c0 + 2 * _LRU_TC] + bx_ref[d:d + 1]))
                neg_log_a = (1.0 + tr) * half_c_sp[d:d + 1]
                a = jnp.exp(-neg_log_a)
                one_m_a2 = jnp.tanh(neg_log_a) * (1.0 + a * a)
                root = jnp.where(one_m_a2 > 0.0, one_m_a2 * lax.rsqrt(one_m_a2), 0.0)
                u = root * ((1.0 + tg) * half_x)
                for hf in range(_LRU_HALVES):
                    s, sub = divmod(hf * nb + b, SUBLANES)
                    rows = pl.ds(ci * _GATE_ROWS * SUBLANES + sub, _GATE_ROWS, stride=SUBLANES)
                    a_ref[s, rows, :] = a[:, hf * LANES:(hf + 1) * LANES]
                    u_ref[s, rows, :] = u[:, hf * LANES:(hf + 1) * LANES]

    def step(i, carry):
        rf = pl.ds(pl.multiple_of(i * SUBLANES, SUBLANES), SUBLANES)
        rb = pl.ds(pl.multiple_of((T - 1 - i) * SUBLANES, SUBLANES), SUBLANES)
        new = []
        for s in range(nslab):
            hf_ = af_ref[s, rf, :] * carry[2 * s] + uf_ref[s, rf, :]
            uf_ref[s, rf, :] = hf_
            hb_ = ab_ref[s, rb, :] * carry[2 * s + 1] + ub_ref[s, rb, :]
            ub_ref[s, rb, :] = hb_
            new += [hf_, hb_]
        return tuple(new)

    init = []
    for s in range(nslab):
        init += [h0f_ref[:, s * LANES:(s + 1) * LANES], h0b_ref[:, s * LANES:(s + 1) * LANES]]
    fin = lax.fori_loop(0, T, step, tuple(init), unroll=8)
    for s in range(nslab):
        htf_ref[:, s * LANES:(s + 1) * LANES] = fin[2 * s]
        htb_ref[:, s * LANES:(s + 1) * LANES] = fin[2 * s + 1]

    for b in range(nb):
        for hf in range(_LRU_HALVES):
            s, sub = divmod(hf * nb + b, SUBLANES)
            rows = pl.ds(sub, T, stride=SUBLANES)
            hs = uf_ref[s, rows, :] + ub_ref[s, rows, :]
            g = gr_ref[b * T:(b + 1) * T, hf * LANES:(hf + 1) * LANES]
            y_ref[b * T:(b + 1) * T, hf * LANES:(hf + 1) * LANES] = (
                hs * _gelu_tanh(g)).astype(bf16)


def _rglru(proj, wcat, conv_w, conv_b, b_a, b_x, lam, h0f, h0b, T, nb):
    n = proj.shape[0]
    nslab = _LRU_HALVES * nb // SUBLANES
    W = nslab * LANES
    R = nb * T
    nct = LRU_W // _LRU_TC
    xr0 = 3 * ATT_W // _LRU_TC
    gr0 = xr0 + nct
    chan = lambda k: pl.BlockSpec((k, _LRU_TC), lambda g, j: (0, j))
    slab = pl.BlockSpec((SUBLANES, W), lambda g, j: (g, j))
    body = functools.partial(_lru_body, T=T, nb=nb)
    return pl.pallas_call(
        body,
        grid=(n // R, nct),
        in_specs=[pl.BlockSpec((R, _LRU_TC), lambda g, j: (g, xr0 + j)),
                  pl.BlockSpec((R, _LRU_TC), lambda g, j: (g, gr0 + j)),
                  pl.BlockSpec((1, _LRU_TC, 4 * _LRU_TC), lambda g, j: (j, 0, 0)),
                  chan(4), chan(1), chan(2), chan(2), chan(2), slab, slab],
        out_specs=[pl.BlockSpec((R, _LRU_TC), lambda g, j: (g, j)), slab, slab],
        out_shape=[jax.ShapeDtypeStruct((n, LRU_W), bf16),
                   jax.ShapeDtypeStruct(h0f.shape, f32),
                   jax.ShapeDtypeStruct(h0b.shape, f32)],
        scratch_shapes=[pltpu.VMEM((R, _LRU_TC), f32)]
        + [pltpu.VMEM((nslab, T * SUBLANES, LANES), f32) for _ in range(4)],
        compiler_params=_cparams(("arbitrary", "arbitrary"), 56),
        name="rg_lru",
    )(proj, proj, wcat, conv_w, conv_b, b_a, b_x, lam, h0f, h0b)


def _lru_gate_weights(w_a, w_x):
    per_tile = _LRU_TC // LRU_BLOCK
    nct = LRU_W // _LRU_TC
    eye = jnp.eye(per_tile, dtype=f32)

    def bd(w):
        w = w.reshape(nct, per_tile, LRU_BLOCK, LRU_BLOCK)
        m = w[:, :, :, None, :] * eye[None, :, None, :, None]
        return m.reshape(nct, _LRU_TC, _LRU_TC)

    return jnp.concatenate([bd(w_a[0]), bd(w_x[0]), bd(w_a[1]), bd(w_x[1])], axis=-1).astype(bf16)


_TM_OUT = 512


def _outproj_body(att_ref, lru_ref, w_ref, x_ref, mods_ref, nw_ref, wr_ref,
                  x1_ref, h2_ref, aff_ref):
    mix = (jnp.dot(att_ref[...], w_ref[0:ATT_W, :], preferred_element_type=f32)
           + jnp.dot(lru_ref[...], w_ref[ATT_W:D, :], preferred_element_type=f32))
    m = mods_ref[0]
    x1 = x_ref[...] + m[2:3] * mix
    x1_ref[...] = x1
    ms = jnp.mean(x1 * x1, axis=-1, keepdims=True)
    h2 = (x1 * lax.rsqrt(ms + EPS) * nw_ref[...]) * (1.0 + m[4:5]) + m[3:4]
    h2_ref[...] = pltpu.pack_elementwise([h2[:, :D // 2], h2[:, D // 2:]], packed_dtype=bf16)
    logits = _nt_dot(wr_ref[...], h2.astype(bf16))
    e = jnp.exp(logits - logits.max(axis=0, keepdims=True))
    aff_ref[...] = e / e.sum(axis=0, keepdims=True)


def _outproj(att, lru, w_out_b, x2d, mods, norm_w, w_router_t, tiles_per_row):
    n = x2d.shape[0]
    tm = _TM_OUT
    row = (lambda i: 1 + i // tiles_per_row) if tiles_per_row else (lambda i: 0)
    return pl.pallas_call(
        _outproj_body,
        grid=(n // tm,),
        in_specs=[pl.BlockSpec((tm, ATT_W), lambda i: (i, 0)),
                  pl.BlockSpec((tm, LRU_W), lambda i: (i, 0)),
                  pl.BlockSpec((D, D), lambda i: (0, 0)),
                  pl.BlockSpec((tm, D), lambda i: (i, 0)),
                  pl.BlockSpec((1, 6, D), lambda i: (row(i), 0, 0)),
                  pl.BlockSpec((1, D), lambda i: (0, 0)),
                  pl.BlockSpec((N_EXPERTS, D), lambda i: (0, 0))],
        out_specs=[pl.BlockSpec((tm, D), lambda i: (i, 0)),
                   pl.BlockSpec((tm, D // 2), lambda i: (i, 0)),
                   pl.BlockSpec((N_EXPERTS, tm), lambda i: (0, i))],
        out_shape=[jax.ShapeDtypeStruct((n, D), f32),
                   jax.ShapeDtypeStruct((n, D // 2), jnp.uint32),
                   jax.ShapeDtypeStruct((N_EXPERTS, n), f32)],
        compiler_params=_cparams(("arbitrary",), 52),
        name="out_projection_router",
    )(att, lru, w_out_b, x2d, mods, norm_w, w_router_t)


def _prefix_sum_lanes(x, lane, n):
    s = 1
    while s < n:
        x = x + jnp.where(lane >= s, pltpu.roll(x, s, 1), 0)
        s *= 2
    return x


def _route_body(aff_ref, idx_ref, g_ref, pos_ref, sp_ref, sel_ref, rem_ref, *, n, cap):
    aff = aff_ref[...]
    lane = lax.broadcasted_iota(i32, (N_EXPERTS, n), 1)

    def count(mask):
        return jnp.sum(jnp.where(mask, 1.0, 0.0), axis=1, keepdims=True).astype(i32)

    def as_value(bits):
        return pltpu.bitcast(jnp.broadcast_to(bits, (N_EXPERTS, LANES)), f32)[:, :1]

    def search(i, thr):
        cand = thr | jnp.left_shift(jnp.int32(1), 30 - i)
        return jnp.where(count(aff >= as_value(cand)) >= cap, cand, thr)

    thr = lax.fori_loop(0, 31, search, jnp.zeros((N_EXPERTS, 1), i32))
    sure = aff >= as_value(thr + 1)
    sel_ref[...] = sure.astype(i32)
    rem_ref[...] = ((aff >= as_value(thr)) & jnp.logical_not(sure)).astype(i32)

    def more(need):
        return jnp.max(need) > 0

    def take_largest(need):
        rem = rem_ref[...] > 0
        top = jnp.max(jnp.where(rem, aff, -1.0), axis=1, keepdims=True)
        is_top = rem & (aff == top) & (need > 0)
        rank = _prefix_sum_lanes(is_top.astype(i32), lane, n)
        pick = is_top & ((count(is_top) <= need) | (rank <= need))
        sel_ref[...] = sel_ref[...] | pick.astype(i32)
        rem_ref[...] = (rem & jnp.logical_not(is_top)).astype(i32)
        return need - count(pick)

    lax.while_loop(more, take_largest, cap - count(sure))
    sel_i = sel_ref[...]
    sel = sel_i > 0
    pos = _prefix_sum_lanes(sel_i, lane, n) - sel_i
    pos_ref[...] = pos
    sp_ref[...] = jnp.where(sel, pos, UNSELECTED)

    key = jnp.where(sel, lane - pos, -1)
    idx = lane
    g = aff
    b = 0
    while (1 << b) < n:
        s = 1 << b
        k_sh = pltpu.roll(key, n - s, 1)
        take = (k_sh >= 0) & ((jnp.right_shift(k_sh, b) & 1) == 1)
        stay = (key >= 0) & ((jnp.right_shift(key, b) & 1) == 0)
        idx = jnp.where(take, pltpu.roll(idx, n - s, 1), idx)
        g = jnp.where(take, pltpu.roll(g, n - s, 1), g)
        key = jnp.where(take, k_sh, jnp.where(stay, key, -1))
        b += 1
    idx_ref[...] = idx[:, :cap]
    g_ref[...] = g[:, :cap]


def _route(aff_t, cap):
    n = aff_t.shape[1]
    body = functools.partial(_route_body, n=n, cap=cap)
    full = lambda shape: pl.BlockSpec(shape, lambda: (0, 0))
    return pl.pallas_call(
        body,
        in_specs=[full((N_EXPERTS, n))],
        out_specs=[full((N_EXPERTS, cap)), full((N_EXPERTS, cap)),
                   full((N_EXPERTS, n)), full((N_EXPERTS, n))],
        out_shape=[jax.ShapeDtypeStruct((N_EXPERTS, cap), i32),
                   jax.ShapeDtypeStruct((N_EXPERTS, cap), f32),
                   jax.ShapeDtypeStruct((N_EXPERTS, n), i32),
                   jax.ShapeDtypeStruct((N_EXPERTS, n), i32)],
        scratch_shapes=[pltpu.VMEM((N_EXPERTS, n), i32), pltpu.VMEM((N_EXPERTS, n), i32)],
        compiler_params=pltpu.CompilerParams(vmem_limit_bytes=32 * _MIB),
        name="expert_choice_routing",
    )(aff_t)


_TF = 512
_TN_DOWN = 512


def _expert_body(idx_ref, h2a_hbm, h2b_hbm, g_ref, wg_ref, wu_ref, wd_ref, oa_ref, ob_ref,
                 xe_ref, xb_ref, he_ref, sem, *, caps, na, nd):
    e = pl.program_id(0)
    s = pl.program_id(1)
    srcs = (h2a_hbm, h2b_hbm)
    starts = (0, caps[0])
    rows_total = caps[0] + caps[1]

    def row_copy(grp, tok, r):
        return pltpu.make_async_copy(srcs[grp].at[pl.ds(tok, 1), :], xe_ref.at[pl.ds(r, 1), :], sem)

    def wait_all_rows():
        for grp in range(2):
            pltpu.make_async_copy(srcs[grp].at[pl.ds(0, caps[grp]), :],
                                  xe_ref.at[pl.ds(starts[grp], caps[grp]), :], sem).wait()

    @pl.when((e == 0) & (s == 0))
    def _():
        for grp in range(2):
            def issue(r, c, grp=grp):
                row_copy(grp, idx_ref[starts[grp] + r], starts[grp] + r).start()
                return c

            lax.fori_loop(0, caps[grp], issue, 0, unroll=8)

    def gate_up(unpack_rows):
        wg = wg_ref[0].astype(bf16)
        wu = wu_ref[0].astype(bf16)
        half = rows_total // 2
        for rows in (slice(0, half), slice(half, rows_total)):
            if unpack_rows:
                packed = xe_ref[rows, :]
                for part in range(2):
                    xb_ref[rows, part * (D // 2):(part + 1) * (D // 2)] = (
                        pltpu.unpack_elementwise(packed, index=part, packed_dtype=bf16,
                                                 unpacked_dtype=f32).astype(bf16))
            xb = xb_ref[rows, :]
            gp = jnp.dot(xb, wg, preferred_element_type=f32)
            up = jnp.dot(xb, wu, preferred_element_type=f32)
            he_ref[s, rows, :] = ((gp * jax.nn.sigmoid(gp)) * up).astype(bf16)

    @pl.when(s == 0)
    def _():
        wait_all_rows()
        gate_up(True)

    @pl.when((s > 0) & (s < na))
    def _():
        gate_up(False)

    @pl.when(s >= na)
    def _():
        base = jnp.minimum(e + 1, N_EXPERTS - 1) * rows_total
        for k in range(nd):
            @pl.when(s - na == k)
            def _(k=k):
                for grp in range(2):
                    per = caps[grp] // nd
                    for r in range(starts[grp] + k * per, starts[grp] + (k + 1) * per):
                        row_copy(grp, idx_ref[base + r], r).start()

        wd = wd_ref[0].astype(bf16)
        acc = None
        for k in range(na):
            part = jnp.dot(he_ref[k], wd[k * _TF:(k + 1) * _TF, :], preferred_element_type=f32)
            acc = part if acc is None else acc + part
        scaled = acc * g_ref[0]
        oa_ref[0] = scaled[:caps[0]]
        ob_ref[0] = scaled[caps[0]:]

    @pl.when((e == N_EXPERTS - 1) & (s == na + nd - 1))
    def _():
        wait_all_rows()


def _expert_ffn(idx_a, idx_b, h2a, h2b, g_a, g_b, w_gate, w_up, w_down):
    caps = (idx_a.shape[1], idx_b.shape[1])
    rows = caps[0] + caps[1]
    na = D_FF // _TF
    nd = D // _TN_DOWN
    idx = jnp.concatenate([idx_a, idx_b], axis=1).reshape(N_EXPERTS * rows)
    g3 = jnp.concatenate([g_a, g_b], axis=1)[:, :, None]
    body = functools.partial(_expert_body, caps=caps, na=na, nd=nd)
    up_tile = lambda e, s, idx: (e, 0, jnp.minimum(s, na - 1))
    down_tile = lambda e, s, idx: (e, 0, jnp.maximum(s - na, 0))
    grid_spec = pltpu.PrefetchScalarGridSpec(
        num_scalar_prefetch=1,
        grid=(N_EXPERTS, na + nd),
        in_specs=[pl.BlockSpec(memory_space=pl.ANY),
                  pl.BlockSpec(memory_space=pl.ANY),
                  pl.BlockSpec((1, rows, 1), lambda e, s, idx: (e, 0, 0)),
                  pl.BlockSpec((1, D, _TF), up_tile),
                  pl.BlockSpec((1, D, _TF), up_tile),
                  pl.BlockSpec((1, D_FF, _TN_DOWN), down_tile)],
        out_specs=[pl.BlockSpec((1, caps[0], _TN_DOWN), down_tile),
                   pl.BlockSpec((1, caps[1], _TN_DOWN), down_tile)],
        scratch_shapes=[pltpu.VMEM((rows, D // 2), jnp.uint32), pltpu.VMEM((rows, D), bf16),
                        pltpu.VMEM((na, rows, _TF), bf16), pltpu.SemaphoreType.DMA(())],
    )
    return pl.pallas_call(
        body,
        grid_spec=grid_spec,
        out_shape=[jax.ShapeDtypeStruct((N_EXPERTS, caps[0], D), f32),
                   jax.ShapeDtypeStruct((N_EXPERTS, caps[1], D), f32)],
        compiler_params=_cparams(("arbitrary", "arbitrary"), 60),
        name="expert_ffn",
    )(idx, h2a, h2b, g3, w_gate, w_up, w_down)


_TT = 128
_KB = 512
_KB_TYPICAL = 416
_RUN_MAX = _TT + 2 * SUBLANES
_RUN_PIECES = tuple(SUBLANES << i for i in reversed(range((_RUN_MAX // SUBLANES).bit_length())))
_STAGE_ROWS = pl.cdiv(N_EXPERTS * _RUN_MAX, _KB) * _KB
_STAGE_PIECES = tuple(SUBLANES << i
                      for i in reversed(range((_STAGE_ROWS // SUBLANES).bit_length())))


def _combine_body(offs_ref, ye_hbm, sp_ref, x1_ref, mods_ref, o_ref, stage_ref, acc_ref, sem,
                  *, cap):
    j = pl.program_id(0)
    slot = j & 1

    def wait_rows(total, sl):
        for size in _STAGE_PIECES:
            @pl.when((total & size) != 0)
            def _(size=size):
                pltpu.make_async_copy(ye_hbm.at[pl.ds(0, size), :],
                                      stage_ref.at[sl, pl.ds(0, size), :], sem.at[sl]).wait()

    def tile_copies(jj, sl, start):
        bases, offs = [], []
        base = jnp.int32(0)
        for e in range(N_EXPERTS):
            off = offs_ref[e, jj]
            end = offs_ref[e, jj + 1]
            off8 = (off // SUBLANES) * SUBLANES
            length = jnp.where(end > off, ((end + SUBLANES - 1) // SUBLANES) * SUBLANES - off8, 0)
            bases.append(base)
            offs.append(off8)
            src = e * cap + off8
            dst = base
            for size in _RUN_PIECES if start else ():
                piece = length & size

                @pl.when(piece != 0)
                def _(src=src, dst=dst, size=size):
                    pltpu.make_async_copy(
                        ye_hbm.at[pl.ds(pl.multiple_of(src, SUBLANES), size), :],
                        stage_ref.at[sl, pl.ds(pl.multiple_of(dst, SUBLANES), size), :],
                        sem.at[sl]).start()

                src = src + piece
                dst = dst + piece
            base = base + length
        return bases, offs, base

    @pl.when(j == 0)
    def _():
        stage_ref[...] = jnp.zeros_like(stage_ref)
        tile_copies(0, 0, True)

    @pl.when(j + 1 < pl.num_programs(0))
    def _():
        tile_copies(j + 1, 1 - slot, True)

    bases, offs, total = tile_copies(j, slot, False)
    wait_rows(total, slot)

    sp = sp_ref[...]
    def scatter_rows(row0, nrows):
        kiota = lax.broadcasted_iota(i32, (1, nrows), 1)
        hit = None
        for e in range(N_EXPERTS):
            local = sp[:, e:e + 1] + (bases[e] - offs[e] - row0)
            h = local == kiota
            hit = h if hit is None else (hit | h)
        s = jnp.where(hit, 1.0, 0.0).astype(bf16)
        rows = stage_ref[slot, pl.ds(pl.multiple_of(row0, SUBLANES), nrows), :]
        hi = rows.astype(bf16)
        lo = (rows - hi.astype(f32)).astype(bf16)
        return (jnp.dot(s, hi, preferred_element_type=f32)
                + jnp.dot(s, lo, preferred_element_type=f32))

    gate = mods_ref[0][5:6]

    @pl.when(total <= _KB_TYPICAL)
    def _():
        o_ref[...] = x1_ref[...] + gate * scatter_rows(0, _KB_TYPICAL)

    @pl.when(total > _KB_TYPICAL)
    def _():
        acc_ref[...] = jnp.zeros_like(acc_ref)

        def kblock(kb, c):
            acc_ref[...] += scatter_rows(kb * _KB, _KB)
            return c

        lax.fori_loop(0, (total + _KB - 1) // _KB, kblock, 0)
        o_ref[...] = x1_ref[...] + gate * acc_ref[...]


def _combine(offs, ye_flat, sp_n, x1, mods, cap, tiles_per_row):
    n = x1.shape[0]
    row = (lambda j: 1 + j // tiles_per_row) if tiles_per_row else (lambda j: 0)
    body = functools.partial(_combine_body, cap=cap)
    grid_spec = pltpu.PrefetchScalarGridSpec(
        num_scalar_prefetch=1,
        grid=(n // _TT,),
        in_specs=[pl.BlockSpec(memory_space=pl.ANY),
                  pl.BlockSpec((_TT, N_EXPERTS), lambda j, offs: (j, 0)),
                  pl.BlockSpec((_TT, D), lambda j, offs: (j, 0)),
                  pl.BlockSpec((1, 6, D), lambda j, offs: (row(j), 0, 0))],
        out_specs=pl.BlockSpec((_TT, D), lambda j, offs: (j, 0)),
        scratch_shapes=[pltpu.VMEM((2, _STAGE_ROWS, D), f32), pltpu.VMEM((_TT, D), f32),
                        pltpu.SemaphoreType.DMA((2,))],
    )
    return pl.pallas_call(
        body,
        grid_spec=grid_spec,
        out_shape=jax.ShapeDtypeStruct((n, D), f32),
        compiler_params=_cparams(("arbitrary",), 52),
        name="expert_combine",
    )(offs, ye_flat, sp_n, x1, mods)


def _moe(x1s, h2s, affs, mods, w_gate, w_up, w_down, tiles_per_row):
    routes = [_route(aff_t, CAPACITY_FACTOR * aff_t.shape[1] // N_EXPERTS) for aff_t in affs]
    (idx_a, g_a, _, _), (idx_b, g_b, _, _) = routes
    yes = _expert_ffn(idx_a, idx_b, h2s[0], h2s[1], g_a, g_b, w_gate, w_up, w_down)
    outs = []
    for ye, (idx, _, pos, sp), x1, tpr in zip(yes, routes, x1s, tiles_per_row):
        cap = idx.shape[1]
        offs = jnp.concatenate([pos[:, ::_TT], jnp.full((N_EXPERTS, 1), cap, i32)], axis=1)
        outs.append(_combine(offs, ye.reshape(N_EXPERTS * cap, D), sp.T, x1, mods, cap, tpr))
    return outs


def kernel(x_prompt, x_sample, cache_k, cache_v, state_fwd, state_bwd, c, c_ctx, w_mod, b_mod,
           norm_mix, w_in, q_norm, k_norm, rpb, conv_w, conv_b, lru_w_a, lru_b_a, lru_w_x,
           lru_b_x, lru_lambda, w_out, norm_ffn, w_router, w_gate, w_up, w_down):
    B, S, _ = x_prompt.shape
    DB, T, _ = x_sample.shape
    assert x_prompt.shape[2] == D and w_mod.shape[0] == 1, "one layer of width D"
    assert B % SUBLANES == 0 and SUBLANES % DB == 0 and DB < SUBLANES
    assert T == _NA_ROWS * GRID_W == _TM_IN and (B * S) % _TM_IN == 0 and S % _GATE_ROWS == 0
    l = 0

    cvecs = jnp.concatenate([c_ctx[None, :], c, jnp.zeros((SUBLANES - 1 - DB, D), f32)], axis=0)
    mods = _adaln_mods(cvecs, w_mod[l], b_mod[l][None, :]).reshape(SUBLANES, 6, D)

    heads_per_tile = _TN_IN // HEAD_DIM
    qn_t = jnp.tile(q_norm[l], heads_per_tile)[None, :]
    kn_t = jnp.tile(k_norm[l], heads_per_tile)[None, :]
    gmat = (jnp.kron(jnp.eye(heads_per_tile, dtype=f32), jnp.ones((HEAD_DIM, HEAD_DIM), f32))
            / HEAD_DIM).astype(bf16)
    nmix = norm_mix[l][None, :]
    nffn = norm_ffn[l][None, :]
    w_in_l = w_in[l]
    w_out_b = w_out[l].astype(bf16)
    w_router_t = w_router[l].T.astype(bf16)
    wcat = _lru_gate_weights(lru_w_a[l], lru_w_x[l])
    cb = conv_b[l][None, :]

    xp = x_prompt.reshape(B * S, D)
    xs = x_sample.reshape(DB * T, D)

    proj_c = _inproj(xp, mods, nmix, w_in_l, qn_t, kn_t, gmat, per_tile_rows=False)
    att_c, new_k, new_v = _ctx_attention(proj_c, B, S)
    zeros_c = jnp.zeros((B, LRU_W), f32)
    lru_c, htf, htb = _rglru(proj_c, wcat, conv_w[l], cb, lru_b_a[l], lru_b_x[l], lru_lambda[l],
                             zeros_c, zeros_c, T=S, nb=SUBLANES)
    x1_c, h2_c, aff_c = _outproj(att_c, lru_c, w_out_b, xp, mods, nffn, w_router_t, 0)

    nct = LRU_W // _LRU_TC

    def slabs(h0):
        return (h0.reshape(DB, nct, _LRU_HALVES, LANES).transpose(2, 0, 1, 3)
                .reshape(_LRU_HALVES * DB, nct * LANES))

    proj_l = _inproj(xs, mods, nmix, w_in_l, qn_t, kn_t, gmat, per_tile_rows=True)
    att_l = _lat_attention(proj_l, cache_k, cache_v, _pad_rpb(rpb[l]), l, DB, T)
    h0f_l = slabs(state_fwd[:, l])
    h0b_l = slabs(state_bwd[:, l])
    lru_l, _, _ = _rglru(proj_l, wcat, conv_w[l], cb, lru_b_a[l], lru_b_x[l], lru_lambda[l],
                         h0f_l, h0b_l, T=T, nb=DB)
    x1_l, h2_l, aff_l = _outproj(att_l, lru_l, w_out_b, xs, mods, nffn, w_router_t, T // _TM_OUT)

    y_c, y_l = _moe((x1_c, x1_l), (h2_c, h2_l), (aff_c, aff_l), mods,
                    w_gate[l], w_up[l], w_down[l], (0, T // _TT))

    return (y_c.reshape(B, S, D), y_l.reshape(DB, T, D), new_k, new_v,
            htf[:, None, :], htb[:, None, :])
```

```python
import functools

import jax
import jax.numpy as jnp
from jax import lax
from jax.experimental import pallas as pl
from jax.experimental.pallas import tpu as pltpu

f32 = jnp.float32
bf16 = jnp.bfloat16
i32 = jnp.int32

D = 2048
N_HEADS = 16
HEAD_DIM = 64
ATT_W = N_HEADS * HEAD_DIM
LRU_W = D - ATT_W
LRU_BLOCK = 64
LRU_C = 8.0
GRID_W = 64
WIN_H = 8
WIN_W = 16
N_EXPERTS = 16
CAPACITY_FACTOR = 2
D_FF = 2048
EPS = 1e-6
NEG_INF = -1e30
IN_COLS = 3 * ATT_W + 2 * LRU_W

LANES = 128
SUBLANES = 8
MXU_DIM = 256
UNSELECTED = -(1 << 30)

_MIB = 1024 * 1024


def _cparams(sem, vmem_mib):
    return pltpu.CompilerParams(dimension_semantics=sem, vmem_limit_bytes=vmem_mib * _MIB)


_TN_MOD = 1024


def _mods_body(c_ref, w_ref, b_ref, o_ref):
    c = c_ref[...]
    s = (c * jax.nn.sigmoid(c)).astype(bf16)
    o_ref[...] = jnp.dot(s, w_ref[...].astype(bf16), preferred_element_type=f32) + b_ref[...]


def _adaln_mods(cvecs, w_mod, b_mod):
    tn = _TN_MOD
    return pl.pallas_call(
        _mods_body,
        grid=(6 * D // tn,),
        in_specs=[pl.BlockSpec((SUBLANES, D), lambda j: (0, 0)),
                  pl.BlockSpec((D, tn), lambda j: (0, j)),
                  pl.BlockSpec((1, tn), lambda j: (0, j))],
        out_specs=pl.BlockSpec((SUBLANES, tn), lambda j: (0, j)),
        out_shape=jax.ShapeDtypeStruct((SUBLANES, 6 * D), f32),
        compiler_params=_cparams(("arbitrary",), 40),
        name="adaln_mods",
    )(cvecs, w_mod, b_mod)


_TM_IN = 1024
_TN_IN = 512
_NORM_ROWS = 16
_NORM_SLICES = 8


def _inproj_body(x0_ref, xn_ref, mods0_ref, modsn_ref, nw_ref, w_ref, qn_ref, kn_ref, g_ref,
                 o_ref, ha_ref, hb_ref):
    i = pl.program_id(0)
    j = pl.program_id(1)
    tm = ha_ref.shape[0]
    nw = nw_ref[...]
    n_qk_tiles = 2 * ATT_W // _TN_IN

    def norm_rows(x_ref, mods_ref, h_ref, r0):
        m = mods_ref[0]
        rows = pl.ds(pl.multiple_of(r0, _NORM_ROWS), _NORM_ROWS)
        x = x_ref[rows, :]
        ms = jnp.mean(x * x, axis=-1, keepdims=True)
        y = x * lax.rsqrt(ms + EPS) * nw
        h_ref[rows, :] = (y * (1.0 + m[1:2]) + m[0:1]).astype(bf16)

    @pl.when((i == 0) & (j == 0))
    def _():
        def first(c, carry):
            norm_rows(x0_ref, mods0_ref, ha_ref, c * _NORM_ROWS)
            return carry

        lax.fori_loop(0, tm // _NORM_ROWS, first, 0, unroll=4)

    def column_step(cur_ref, nxt_ref):
        def project():
            slice_rows = tm // _NORM_SLICES
            r_slice = jnp.minimum(j, _NORM_SLICES - 1) * slice_rows
            for c in range(slice_rows // _NORM_ROWS):
                norm_rows(xn_ref, modsn_ref, nxt_ref, r_slice + c * _NORM_ROWS)
            return jnp.dot(cur_ref[...], w_ref[...].astype(bf16), preferred_element_type=f32)

        @pl.when(j < n_qk_tiles)
        def _():
            acc = project()
            sq = acc * acc
            hi = sq.astype(bf16)
            lo = (sq - hi.astype(f32)).astype(bf16)
            g = g_ref[...]
            ms = (jnp.dot(hi, g, preferred_element_type=f32)
                  + jnp.dot(lo, g, preferred_element_type=f32))
            head_w = jnp.where(j < n_qk_tiles // 2, qn_ref[...], kn_ref[...])
            o_ref[...] = acc * lax.rsqrt(ms + EPS) * head_w

        @pl.when(j >= n_qk_tiles)
        def _():
            o_ref[...] = project()

    @pl.when((i & 1) == 0)
    def _():
        column_step(ha_ref, hb_ref)

    @pl.when((i & 1) == 1)
    def _():
        column_step(hb_ref, ha_ref)


def _inproj(x2d, mods, norm_w, w_in, qn_t, kn_t, gmat, per_tile_rows):
    n = x2d.shape[0]
    tm = _TM_IN
    last = n // tm - 1
    row = (lambda i: 1 + i) if per_tile_rows else (lambda i: 0)
    nxt = lambda i: jnp.minimum(i + 1, last)
    return pl.pallas_call(
        _inproj_body,
        grid=(n // tm, IN_COLS // _TN_IN),
        in_specs=[pl.BlockSpec((tm, D), lambda i, j: (0, 0)),
                  pl.BlockSpec((tm, D), lambda i, j: (nxt(i), 0)),
                  pl.BlockSpec((1, 6, D), lambda i, j: (row(0), 0, 0)),
                  pl.BlockSpec((1, 6, D), lambda i, j: (row(nxt(i)), 0, 0)),
                  pl.BlockSpec((1, D), lambda i, j: (0, 0)),
                  pl.BlockSpec((D, _TN_IN), lambda i, j: (0, j)),
                  pl.BlockSpec((1, _TN_IN), lambda i, j: (0, 0)),
                  pl.BlockSpec((1, _TN_IN), lambda i, j: (0, 0)),
                  pl.BlockSpec((_TN_IN, _TN_IN), lambda i, j: (0, 0))],
        out_specs=pl.BlockSpec((tm, _TN_IN), lambda i, j: (i, j)),
        out_shape=jax.ShapeDtypeStruct((n, IN_COLS), f32),
        scratch_shapes=[pltpu.VMEM((tm, D), bf16), pltpu.VMEM((tm, D), bf16)],
        compiler_params=_cparams(("arbitrary", "arbitrary"), 56),
        name="in_projection",
    )(x2d, x2d, mods, mods, norm_w, w_in, qn_t, kn_t, gmat)


def _softmax_pv(s_list, v_list):
    m = s_list[0].max(axis=-1, keepdims=True)
    for s in s_list[1:]:
        m = jnp.maximum(m, s.max(axis=-1, keepdims=True))
    o = None
    for s, v in zip(s_list, v_list):
        os_ = jnp.dot(jnp.exp(s - m).astype(bf16), v, preferred_element_type=f32)
        o = os_ if o is None else o + os_
    return o


def _normalise_pair(o0, o1, lane):
    left = lane < HEAD_DIM
    den = pltpu.roll(jnp.where(left, o1, o0), HEAD_DIM, 1)
    return jnp.where(left, o0, o1) / den


def _nt_dot(a, b):
    return lax.dot_general(a, b, (((1,), (1,)), ((), ())), preferred_element_type=f32)


_QK_SCALE = HEAD_DIM ** -0.5


def _ctx_attn_body(q_ref, k_ref, v_ref, o_ref, nk_ref, nv_ref):
    lane = lax.broadcasted_iota(i32, (1, LANES), 1)
    for p in range(ATT_W // LANES):
        cols = slice(p * LANES, (p + 1) * LANES)
        q = q_ref[:, cols] * _QK_SCALE
        kf = k_ref[:, cols]
        k = kf.astype(bf16)
        v = v_ref[:, cols]
        nk_ref[0, 0, 2 * p] = kf[:, :HEAD_DIM]
        nk_ref[0, 0, 2 * p + 1] = pltpu.roll(kf, HEAD_DIM, 1)[:, :HEAD_DIM]
        nv_ref[0, 0, 2 * p] = v[:, :HEAD_DIM]
        nv_ref[0, 0, 2 * p + 1] = pltpu.roll(v, HEAD_DIM, 1)[:, :HEAD_DIM]
        heads = []
        for hh in range(2):
            hm = (lane >= hh * HEAD_DIM) & (lane < (hh + 1) * HEAD_DIM)
            qh = jnp.where(hm, q, 0.0).astype(bf16)
            vh = jnp.where(hm, v, 1.0).astype(bf16)
            heads.append(_softmax_pv([_nt_dot(qh, k)], [vh]))
        o_ref[:, cols] = _normalise_pair(heads[0], heads[1], lane).astype(bf16)


def _ctx_attention(proj, batch, seq):
    n = proj.shape[0]
    cache = pl.BlockSpec((1, 1, N_HEADS, seq, HEAD_DIM), lambda b: (b, 0, 0, 0, 0))
    cache_shape = jax.ShapeDtypeStruct((batch, 1, N_HEADS, seq, HEAD_DIM), f32)
    return pl.pallas_call(
        _ctx_attn_body,
        grid=(batch,),
        in_specs=[pl.BlockSpec((seq, ATT_W), lambda b: (b, 0)),
                  pl.BlockSpec((seq, ATT_W), lambda b: (b, 1)),
                  pl.BlockSpec((seq, ATT_W), lambda b: (b, 2))],
        out_specs=[pl.BlockSpec((seq, ATT_W), lambda b: (b, 0)), cache, cache],
        out_shape=[jax.ShapeDtypeStruct((n, ATT_W), bf16), cache_shape, cache_shape],
        compiler_params=_cparams(("arbitrary",), 32),
        name="context_attention",
    )(proj, proj, proj)


_NA_ROWS = 16
_NA_QROWS = 8
_NA_KROWS = 12


def _pad_rpb(rpb):
    nd, nc = 2 * WIN_H - 1, 2 * WIN_W - 1
    return jnp.pad(rpb.astype(f32), ((0, 0), (0, 2 * SUBLANES - nd), (0, LANES - nc)))


def _build_na_bias(rpb_ref, bias_ref):
    qc = lax.broadcasted_iota(i32, (GRID_W, LANES), 0)
    lane = lax.broadcasted_iota(i32, (GRID_W, LANES), 1)
    kc = lane & (GRID_W - 1)
    cs = jnp.clip(qc - WIN_W // 2, 0, GRID_W - WIN_W)
    col_ok = (kc >= cs) & (kc < cs + WIN_W)
    left = lane < GRID_W
    neg = jnp.full((GRID_W, LANES), NEG_INF, f32)
    for hh in range(2):
        tbl = rpb_ref[hh]
        tiles = {}

        def toeplitz(d, odd):
            if (d, odd) not in tiles:
                row = jnp.broadcast_to(tbl[d:d + 1, :], (GRID_W, LANES))
                shift = (LANES - (WIN_W - 1) + odd * GRID_W) % LANES
                t = pltpu.roll(row, shift, 1, stride=1, stride_axis=0)
                tiles[(d, odd)] = jnp.where(col_ok, t, NEG_INF)
            return tiles[(d, odd)]

        for j in range(2):
            for rq in range(_NA_QROWS):
                rq_abs = j * _NA_QROWS + rq
                rs = min(max(rq_abs - WIN_H // 2, 0), _NA_ROWS - WIN_H)
                for kp in range(_NA_KROWS // 2):
                    parts = []
                    for odd in range(2):
                        rk_abs = j * (_NA_ROWS - _NA_KROWS) + 2 * kp + odd
                        if rs <= rk_abs < rs + WIN_H:
                            parts.append(toeplitz(rk_abs - rq_abs + WIN_H - 1, odd))
                        else:
                            parts.append(neg)
                    bias_ref[hh, j, rq * GRID_W:(rq + 1) * GRID_W,
                             kp * LANES:(kp + 1) * LANES] = jnp.where(left, parts[0], parts[1])


def _lat_attn_body(q_ref, k_ref, v_ref, ck_ref, cv_ref, rpb_ref, o_ref, bias_ref):
    @pl.when(pl.program_id(1) == 0)
    def _():
        _build_na_bias(rpb_ref, bias_ref)

    lane = lax.broadcasted_iota(i32, (1, LANES), 1)
    nq = _NA_QROWS * GRID_W
    nk = _NA_KROWS * GRID_W
    kstep = (_NA_ROWS - _NA_KROWS) * GRID_W
    src = lax.broadcasted_iota(i32, (HEAD_DIM, LANES), 0)
    dst = lax.broadcasted_iota(i32, (HEAD_DIM, LANES), 1)
    ck = None
    cv_h = []
    for hh in range(2):
        place = jnp.where(dst == src + hh * HEAD_DIM, 1.0, 0.0).astype(bf16)
        part = jnp.dot(ck_ref[0, 0, hh].astype(bf16), place, preferred_element_type=f32)
        ck = part if ck is None else ck + part
        placed = jnp.dot(cv_ref[0, 0, hh].astype(bf16), place, preferred_element_type=f32)
        own = (lane >= hh * HEAD_DIM) & (lane < (hh + 1) * HEAD_DIM)
        cv_h.append(jnp.where(own, placed, 1.0).astype(bf16))
    ck = ck.astype(bf16)
    for j in range(2):
        q = q_ref[j * nq:(j + 1) * nq, :] * _QK_SCALE
        k = k_ref[j * kstep:j * kstep + nk, :].astype(bf16)
        v = v_ref[j * kstep:j * kstep + nk, :]
        heads = []
        for hh in range(2):
            hm = (lane >= hh * HEAD_DIM) & (lane < (hh + 1) * HEAD_DIM)
            qh = jnp.where(hm, q, 0.0).astype(bf16)
            s_win = _nt_dot(qh, k) + bias_ref[hh, j]
            s_ctx = _nt_dot(qh, ck)
            vh = jnp.where(hm, v, 1.0).astype(bf16)
            heads.append(_softmax_pv([s_win, s_ctx], [vh, cv_h[hh]]))
        o_ref[j * nq:(j + 1) * nq, :] = _normalise_pair(heads[0], heads[1], lane).astype(bf16)


def _lat_attention(proj, ck, cv, rpb_p, layer, batch, seq):
    n = proj.shape[0]
    past = ck.shape[3]
    cache = pl.BlockSpec((1, 1, 2, past, HEAD_DIM), lambda p, b: (b, layer, p, 0, 0))
    npair = ATT_W // LANES
    nq = _NA_QROWS * GRID_W
    nk = _NA_KROWS * GRID_W
    return pl.pallas_call(
        _lat_attn_body,
        grid=(npair, batch),
        in_specs=[pl.BlockSpec((seq, LANES), lambda p, b: (b, p)),
                  pl.BlockSpec((seq, LANES), lambda p, b: (b, npair + p)),
                  pl.BlockSpec((seq, LANES), lambda p, b: (b, 2 * npair + p)),
                  cache, cache,
                  pl.BlockSpec((2, 2 * SUBLANES, LANES), lambda p, b: (p, 0, 0))],
        out_specs=pl.BlockSpec((seq, LANES), lambda p, b: (b, p)),
        out_shape=jax.ShapeDtypeStruct((n, ATT_W), bf16),
        scratch_shapes=[pltpu.VMEM((2, 2, nq, nk), f32)],
        compiler_params=_cparams(("arbitrary", "arbitrary"), 48),
        name="neighbourhood_attention",
    )(proj, proj, proj, ck, cv, rpb_p)


_LRU_TC = 256
_GATE_ROWS = 256
_LRU_HALVES = _LRU_TC // LANES


def _gelu_tanh(x):
    c = 0.7978845608028654
    return x * (0.5 * (1.0 + jnp.tanh(c * (x + 0.044715 * (x * x * x)))))


def _lru_body(xr_ref, gr_ref, w_ref, cw_ref, cb_ref, ba_ref, bx_ref, lam_ref, h0f_ref, h0b_ref,
              y_ref, htf_ref, htb_ref, xc_ref, af_ref, uf_ref, ab_ref, ub_ref, *, T, nb):
    nslab = _LRU_HALVES * nb // SUBLANES
    R = nb * T
    x = xr_ref[...]
    t = lax.broadcasted_iota(i32, (R, 1), 0) & (T - 1)
    cw = cw_ref[...]
    xc = jnp.where(t >= 2, pltpu.roll(x, 2, 0), 0.0) * cw[0:1]
    xc = xc + jnp.where(t >= 1, pltpu.roll(x, 1, 0), 0.0) * cw[1:2]
    xc = xc + x * cw[2:3]
    xc = xc + jnp.where(t < T - 1, pltpu.roll(x, R - 1, 0), 0.0) * cw[3:4]
    xc_ref[...] = xc + cb_ref[...]

    nl = -lam_ref[...]
    sp = jnp.maximum(nl, 0.0) + jnp.log1p(jnp.exp(-jnp.abs(nl)))
    half_c_sp = (0.5 * LRU_C) * sp
    w = w_ref[0]
    dirs = ((af_ref, uf_ref), (ab_ref, ub_ref))
    for b in range(nb):
        for ci in range(T // _GATE_ROWS):
            r0 = b * T + ci * _GATE_ROWS
            xcc = xc_ref[r0:r0 + _GATE_ROWS, :]
            half_x = 0.5 * xcc
            z = jnp.dot(xcc.astype(bf16), w, preferred_element_type=f32)
            for d, (a_ref, u_ref) in enumerate(dirs):
                c0 = d * 2 * _LRU_TC
                tr = jnp.tanh(0.5 * (z[:, c0:c0 + _LRU_TC] + ba_ref[d:d + 1]))
                tg = jnp.tanh(0.5 * (z[:, c0 + _LRU_TC:c0 + 2 * _LRU_TC] + bx_ref[d:d + 1]))
                neg_log_a = (1.0 + tr) * half_c_sp[d:d + 1]
                a = jnp.exp(-neg_log_a)
                one_m_a2 = jnp.tanh(neg_log_a) * (1.0 + a * a)
                root = jnp.where(one_m_a2 > 0.0, one_m_a2 * lax.rsqrt(one_m_a2), 0.0)
                u = root * ((1.0 + tg) * half_x)
                for hf in range(_LRU_HALVES):
                    s, sub = divmod(hf * nb + b, SUBLANES)
                    rows = pl.ds(ci * _GATE_ROWS * SUBLANES + sub, _GATE_ROWS, stride=SUBLANES)
                    a_ref[s, rows, :] = a[:, hf * LANES:(hf + 1) * LANES]
                    u_ref[s, rows, :] = u[:, hf * LANES:(hf + 1) * LANES]

    def step(i, carry):
        rf = pl.ds(pl.multiple_of(i * SUBLANES, SUBLANES), SUBLANES)
        rb = pl.ds(pl.multiple_of((T - 1 - i) * SUBLANES, SUBLANES), SUBLANES)
        new = []
        for s in range(nslab):
            hf_ = af_ref[s, rf, :] * carry[2 * s] + uf_ref[s, rf, :]
            uf_ref[s, rf, :] = hf_
            hb_ = ab_ref[s, rb, :] * carry[2 * s + 1] + ub_ref[s, rb, :]
            ub_ref[s, rb, :] = hb_
            new += [hf_, hb_]
        return tuple(new)

    init = []
    for s in range(nslab):
        init += [h0f_ref[:, s * LANES:(s + 1) * LANES], h0b_ref[:, s * LANES:(s + 1) * LANES]]
    fin = lax.fori_loop(0, T, step, tuple(init), unroll=8)
    for s in range(nslab):
        htf_ref[:, s * LANES:(s + 1) * LANES] = fin[2 * s]
        htb_ref[:, s * LANES:(s + 1) * LANES] = fin[2 * s + 1]

    for b in range(nb):
        for hf in range(_LRU_HALVES):
            s, sub = divmod(hf * nb + b, SUBLANES)
            rows = pl.ds(sub, T, stride=SUBLANES)
            hs = uf_ref[s, rows, :] + ub_ref[s, rows, :]
            g = gr_ref[b * T:(b + 1) * T, hf * LANES:(hf + 1) * LANES]
            y_ref[b * T:(b + 1) * T, hf * LANES:(hf + 1) * LANES] = (
                hs * _gelu_tanh(g)).astype(bf16)


def _rglru(proj, wcat, conv_w, conv_b, b_a, b_x, lam, h0f, h0b, T, nb):
    n = proj.shape[0]
    nslab = _LRU_HALVES * nb // SUBLANES
    W = nslab * LANES
    R = nb * T
    nct = LRU_W // _LRU_TC
    xr0 = 3 * ATT_W // _LRU_TC
    gr0 = xr0 + nct
    chan = lambda k: pl.BlockSpec((k, _LRU_TC), lambda g, j: (0, j))
    slab = pl.BlockSpec((SUBLANES, W), lambda g, j: (g, j))
    body = functools.partial(_lru_body, T=T, nb=nb)
    return pl.pallas_call(
        body,
        grid=(n // R, nct),
        in_specs=[pl.BlockSpec((R, _LRU_TC), lambda g, j: (g, xr0 + j)),
                  pl.BlockSpec((R, _LRU_TC), lambda g, j: (g, gr0 + j)),
                  pl.BlockSpec((1, _LRU_TC, 4 * _LRU_TC), lambda g, j: (j, 0, 0)),
                  chan(4), chan(1), chan(2), chan(2), chan(2), slab, slab],
        out_specs=[pl.BlockSpec((R, _LRU_TC), lambda g, j: (g, j)), slab, slab],
        out_shape=[jax.ShapeDtypeStruct((n, LRU_W), bf16),
                   jax.ShapeDtypeStruct(h0f.shape, f32),
                   jax.ShapeDtypeStruct(h0b.shape, f32)],
        scratch_shapes=[pltpu.VMEM((R, _LRU_TC), f32)]
        + [pltpu.VMEM((nslab, T * SUBLANES, LANES), f32) for _ in range(4)],
        compiler_params=_cparams(("arbitrary", "arbitrary"), 56),
        name="rg_lru",
    )(proj, proj, wcat, conv_w, conv_b, b_a, b_x, lam, h0f, h0b)


def _lru_gate_weights(w_a, w_x):
    per_tile = _LRU_TC // LRU_BLOCK
    nct = LRU_W // _LRU_TC
    eye = jnp.eye(per_tile, dtype=f32)

    def bd(w):
        w = w.reshape(nct, per_tile, LRU_BLOCK, LRU_BLOCK)
        m = w[:, :, :, None, :] * eye[None, :, None, :, None]
        return m.reshape(nct, _LRU_TC, _LRU_TC)

    return jnp.concatenate([bd(w_a[0]), bd(w_x[0]), bd(w_a[1]), bd(w_x[1])], axis=-1).astype(bf16)


_TM_OUT = 512


def _outproj_body(att_ref, lru_ref, w_ref, x_ref, mods_ref, nw_ref, wr_ref,
                  x1_ref, h2_ref, aff_ref):
    mix = (jnp.dot(att_ref[...], w_ref[0:ATT_W, :], preferred_element_type=f32)
           + jnp.dot(lru_ref[...], w_ref[ATT_W:D, :], preferred_element_type=f32))
    m = mods_ref[0]
    x1 = x_ref[...] + m[2:3] * mix
    x1_ref[...] = x1
    ms = jnp.mean(x1 * x1, axis=-1, keepdims=True)
    h2 = (x1 * lax.rsqrt(ms + EPS) * nw_ref[...]) * (1.0 + m[4:5]) + m[3:4]
    h2_ref[...] = pltpu.pack_elementwise([h2[:, :D // 2], h2[:, D // 2:]], packed_dtype=bf16)
    logits = _nt_dot(wr_ref[...], h2.astype(bf16))
    e = jnp.exp(logits - logits.max(axis=0, keepdims=True))
    aff_ref[...] = e / e.sum(axis=0, keepdims=True)


def _outproj(att, lru, w_out_b, x2d, mods, norm_w, w_router_t, tiles_per_row):
    n = x2d.shape[0]
    tm = _TM_OUT
    row = (lambda i: 1 + i // tiles_per_row) if tiles_per_row else (lambda i: 0)
    return pl.pallas_call(
        _outproj_body,
        grid=(n // tm,),
        in_specs=[pl.BlockSpec((tm, ATT_W), lambda i: (i, 0)),
                  pl.BlockSpec((tm, LRU_W), lambda i: (i, 0)),
                  pl.BlockSpec((D, D), lambda i: (0, 0)),
                  pl.BlockSpec((tm, D), lambda i: (i, 0)),
                  pl.BlockSpec((1, 6, D), lambda i: (row(i), 0, 0)),
                  pl.BlockSpec((1, D), lambda i: (0, 0)),
                  pl.BlockSpec((N_EXPERTS, D), lambda i: (0, 0))],
        out_specs=[pl.BlockSpec((tm, D), lambda i: (i, 0)),
                   pl.BlockSpec((tm, D // 2), lambda i: (i, 0)),
                   pl.BlockSpec((N_EXPERTS, tm), lambda i: (0, i))],
        out_shape=[jax.ShapeDtypeStruct((n, D), f32),
                   jax.ShapeDtypeStruct((n, D // 2), jnp.uint32),
                   jax.ShapeDtypeStruct((N_EXPERTS, n), f32)],
        compiler_params=_cparams(("arbitrary",), 52),
        name="out_projection_router",
    )(att, lru, w_out_b, x2d, mods, norm_w, w_router_t)


def _prefix_sum_lanes(x, lane, n):
    s = 1
    while s < n:
        x = x + jnp.where(lane >= s, pltpu.roll(x, s, 1), 0)
        s *= 2
    return x


def _route_body(aff_ref, idx_ref, g_ref, pos_ref, sp_ref, sel_ref, rem_ref, *, n, cap):
    aff = aff_ref[...]
    lane = lax.broadcasted_iota(i32, (N_EXPERTS, n), 1)

    def count(mask):
        return jnp.sum(jnp.where(mask, 1.0, 0.0), axis=1, keepdims=True).astype(i32)

    def as_value(bits):
        return pltpu.bitcast(jnp.broadcast_to(bits, (N_EXPERTS, LANES)), f32)[:, :1]

    def search(i, thr):
        cand = thr | jnp.left_shift(jnp.int32(1), 30 - i)
        return jnp.where(count(aff >= as_value(cand)) >= cap, cand, thr)

    thr = lax.fori_loop(0, 31, search, jnp.zeros((N_EXPERTS, 1), i32))
    sure = aff >= as_value(thr + 1)
    sel_ref[...] = sure.astype(i32)
    rem_ref[...] = ((aff >= as_value(thr)) & jnp.logical_not(sure)).astype(i32)

    def more(need):
        return jnp.max(need) > 0

    def take_largest(need):
        rem = rem_ref[...] > 0
        top = jnp.max(jnp.where(rem, aff, -1.0), axis=1, keepdims=True)
        is_top = rem & (aff == top) & (need > 0)
        rank = _prefix_sum_lanes(is_top.astype(i32), lane, n)
        pick = is_top & ((count(is_top) <= need) | (rank <= need))
        sel_ref[...] = sel_ref[...] | pick.astype(i32)
        rem_ref[...] = (rem & jnp.logical_not(is_top)).astype(i32)
        return need - count(pick)

    lax.while_loop(more, take_largest, cap - count(sure))
    sel_i = sel_ref[...]
    sel = sel_i > 0
    pos = _prefix_sum_lanes(sel_i, lane, n) - sel_i
    pos_ref[...] = pos
    sp_ref[...] = jnp.where(sel, pos, UNSELECTED)

    key = jnp.where(sel, lane - pos, -1)
    idx = lane
    g = aff
    b = 0
    while (1 << b) < n:
        s = 1 << b
        k_sh = pltpu.roll(key, n - s, 1)
        take = (k_sh >= 0) & ((jnp.right_shift(k_sh, b) & 1) == 1)
        stay = (key >= 0) & ((jnp.right_shift(key, b) & 1) == 0)
        idx = jnp.where(take, pltpu.roll(idx, n - s, 1), idx)
        g = jnp.where(take, pltpu.roll(g, n - s, 1), g)
        key = jnp.where(take, k_sh, jnp.where(stay, key, -1))
        b += 1
    idx_ref[...] = idx[:, :cap]
    g_ref[...] = g[:, :cap]


def _route(aff_t, cap):
    n = aff_t.shape[1]
    body = functools.partial(_route_body, n=n, cap=cap)
    full = lambda shape: pl.BlockSpec(shape, lambda: (0, 0))
    return pl.pallas_call(
        body,
        in_specs=[full((N_EXPERTS, n))],
        out_specs=[full((N_EXPERTS, cap)), full((N_EXPERTS, cap)),
                   full((N_EXPERTS, n)), full((N_EXPERTS, n))],
        out_shape=[jax.ShapeDtypeStruct((N_EXPERTS, cap), i32),
                   jax.ShapeDtypeStruct((N_EXPERTS, cap), f32),
                   jax.ShapeDtypeStruct((N_EXPERTS, n), i32),
                   jax.ShapeDtypeStruct((N_EXPERTS, n), i32)],
        scratch_shapes=[pltpu.VMEM((N_EXPERTS, n), i32), pltpu.VMEM((N_EXPERTS, n), i32)],
        compiler_params=pltpu.CompilerParams(vmem_limit_bytes=32 * _MIB),
        name="expert_choice_routing",
    )(aff_t)


_TF = 512
_TN_DOWN = 512


def _expert_body(idx_ref, h2a_hbm, h2b_hbm, g_ref, wg_ref, wu_ref, wd_ref, oa_ref, ob_ref,
                 xe_ref, xb_ref, he_ref, sem, *, caps, na, nd):
    e = pl.program_id(0)
    s = pl.program_id(1)
    srcs = (h2a_hbm, h2b_hbm)
    starts = (0, caps[0])
    rows_total = caps[0] + caps[1]

    def row_copy(grp, tok, r):
        return pltpu.make_async_copy(srcs[grp].at[pl.ds(tok, 1), :], xe_ref.at[pl.ds(r, 1), :], sem)

    def wait_all_rows():
        for grp in range(2):
            pltpu.make_async_copy(srcs[grp].at[pl.ds(0, caps[grp]), :],
                                  xe_ref.at[pl.ds(starts[grp], caps[grp]), :], sem).wait()

    @pl.when((e == 0) & (s == 0))
    def _():
        for grp in range(2):
            def issue(r, c, grp=grp):
                row_copy(grp, idx_ref[starts[grp] + r], starts[grp] + r).start()
                return c

            lax.fori_loop(0, caps[grp], issue, 0, unroll=8)

    def gate_up(unpack_rows):
        wg = wg_ref[0].astype(bf16)
        wu = wu_ref[0].astype(bf16)
        half = rows_total // 2
        for rows in (slice(0, half), slice(half, rows_total)):
            if unpack_rows:
                packed = xe_ref[rows, :]
                for part in range(2):
                    xb_ref[rows, part * (D // 2):(part + 1) * (D // 2)] = (
                        pltpu.unpack_elementwise(packed, index=part, packed_dtype=bf16,
                                                 unpacked_dtype=f32).astype(bf16))
            xb = xb_ref[rows, :]
            gp = jnp.dot(xb, wg, preferred_element_type=f32)
            up = jnp.dot(xb, wu, preferred_element_type=f32)
            he_ref[s, rows, :] = ((gp * jax.nn.sigmoid(gp)) * up).astype(bf16)

    @pl.when(s == 0)
    def _():
        wait_all_rows()
        gate_up(True)

    @pl.when((s > 0) & (s < na))
    def _():
        gate_up(False)

    @pl.when(s >= na)
    def _():
        base = jnp.minimum(e + 1, N_EXPERTS - 1) * rows_total
        for k in range(nd):
            @pl.when(s - na == k)
            def _(k=k):
                for grp in range(2):
                    per = caps[grp] // nd
                    for r in range(starts[grp] + k * per, starts[grp] + (k + 1) * per):
                        row_copy(grp, idx_ref[base + r], r).start()

        wd = wd_ref[0].astype(bf16)
        acc = None
        for k in range(na):
            part = jnp.dot(he_ref[k], wd[k * _TF:(k + 1) * _TF, :], preferred_element_type=f32)
            acc = part if acc is None else acc + part
        scaled = acc * g_ref[0]
        oa_ref[0] = scaled[:caps[0]]
        ob_ref[0] = scaled[caps[0]:]

    @pl.when((e == N_EXPERTS - 1) & (s == na + nd - 1))
    def _():
        wait_all_rows()


def _expert_ffn(idx_a, idx_b, h2a, h2b, g_a, g_b, w_gate, w_up, w_down):
    caps = (idx_a.shape[1], idx_b.shape[1])
    rows = caps[0] + caps[1]
    na = D_FF // _TF
    nd = D // _TN_DOWN
    idx = jnp.concatenate([idx_a, idx_b], axis=1).reshape(N_EXPERTS * rows)
    g3 = jnp.concatenate([g_a, g_b], axis=1)[:, :, None]
    body = functools.partial(_expert_body, caps=caps, na=na, nd=nd)
    up_tile = lambda e, s, idx: (e, 0, jnp.minimum(s, na - 1))
    down_tile = lambda e, s, idx: (e, 0, jnp.maximum(s - na, 0))
    grid_spec = pltpu.PrefetchScalarGridSpec(
        num_scalar_prefetch=1,
        grid=(N_EXPERTS, na + nd),
        in_specs=[pl.BlockSpec(memory_space=pl.ANY),
                  pl.BlockSpec(memory_space=pl.ANY),
                  pl.BlockSpec((1, rows, 1), lambda e, s, idx: (e, 0, 0)),
                  pl.BlockSpec((1, D, _TF), up_tile),
                  pl.BlockSpec((1, D, _TF), up_tile),
                  pl.BlockSpec((1, D_FF, _TN_DOWN), down_tile)],
        out_specs=[pl.BlockSpec((1, caps[0], _TN_DOWN), down_tile),
                   pl.BlockSpec((1, caps[1], _TN_DOWN), down_tile)],
        scratch_shapes=[pltpu.VMEM((rows, D // 2), jnp.uint32), pltpu.VMEM((rows, D), bf16),
                        pltpu.VMEM((na, rows, _TF), bf16), pltpu.SemaphoreType.DMA(())],
    )
    return pl.pallas_call(
        body,
        grid_spec=grid_spec,
        out_shape=[jax.ShapeDtypeStruct((N_EXPERTS, caps[0], D), f32),
                   jax.ShapeDtypeStruct((N_EXPERTS, caps[1], D), f32)],
        compiler_params=_cparams(("arbitrary", "arbitrary"), 60),
        name="expert_ffn",
    )(idx, h2a, h2b, g3, w_gate, w_up, w_down)


_TT = 128
_KB = 512
_KB_TYPICAL = 416
_RUN_MAX = _TT + 2 * SUBLANES
_RUN_PIECES = tuple(SUBLANES << i for i in reversed(range((_RUN_MAX // SUBLANES).bit_length())))
_STAGE_ROWS = pl.cdiv(N_EXPERTS * _RUN_MAX, _KB) * _KB
_STAGE_PIECES = tuple(SUBLANES << i
                      for i in reversed(range((_STAGE_ROWS // SUBLANES).bit_length())))


def _combine_body(offs_ref, ye_hbm, sp_ref, x1_ref, mods_ref, o_ref, stage_ref, acc_ref, sem,
                  *, cap):
    j = pl.program_id(0)
    slot = j & 1

    def wait_rows(total, sl):
        for size in _STAGE_PIECES:
            @pl.when((total & size) != 0)
            def _(size=size):
                pltpu.make_async_copy(ye_hbm.at[pl.ds(0, size), :],
                                      stage_ref.at[sl, pl.ds(0, size), :], sem.at[sl]).wait()

    def tile_copies(jj, sl, start):
        bases, offs = [], []
        base = jnp.int32(0)
        for e in range(N_EXPERTS):
            off = offs_ref[e, jj]
            end = offs_ref[e, jj + 1]
            off8 = (off // SUBLANES) * SUBLANES
            length = jnp.where(end > off, ((end + SUBLANES - 1) // SUBLANES) * SUBLANES - off8, 0)
            bases.append(base)
            offs.append(off8)
            src = e * cap + off8
            dst = base
            for size in _RUN_PIECES if start else ():
                piece = length & size

                @pl.when(piece != 0)
                def _(src=src, dst=dst, size=size):
                    pltpu.make_async_copy(
                        ye_hbm.at[pl.ds(pl.multiple_of(src, SUBLANES), size), :],
                        stage_ref.at[sl, pl.ds(pl.multiple_of(dst, SUBLANES), size), :],
                        sem.at[sl]).start()

                src = src + piece
                dst = dst + piece
            base = base + length
        return bases, offs, base

    @pl.when(j == 0)
    def _():
        stage_ref[...] = jnp.zeros_like(stage_ref)
        tile_copies(0, 0, True)

    @pl.when(j + 1 < pl.num_programs(0))
    def _():
        tile_copies(j + 1, 1 - slot, True)

    bases, offs, total = tile_copies(j, slot, False)
    wait_rows(total, slot)

    sp = sp_ref[...]
    def scatter_rows(row0, nrows):
        kiota = lax.broadcasted_iota(i32, (1, nrows), 1)
        hit = None
        for e in range(N_EXPERTS):
            local = sp[:, e:e + 1] + (bases[e] - offs[e] - row0)
            h = local == kiota
            hit = h if hit is None else (hit | h)
        s = jnp.where(hit, 1.0, 0.0).astype(bf16)
        rows = stage_ref[slot, pl.ds(pl.multiple_of(row0, SUBLANES), nrows), :]
        hi = rows.astype(bf16)
        lo = (rows - hi.astype(f32)).astype(bf16)
        return (jnp.dot(s, hi, preferred_element_type=f32)
                + jnp.dot(s, lo, preferred_element_type=f32))

    gate = mods_ref[0][5:6]

    @pl.when(total <= _KB_TYPICAL)
    def _():
        o_ref[...] = x1_ref[...] + gate * scatter_rows(0, _KB_TYPICAL)

    @pl.when(total > _KB_TYPICAL)
    def _():
        acc_ref[...] = jnp.zeros_like(acc_ref)

        def kblock(kb, c):
            acc_ref[...] += scatter_rows(kb * _KB, _KB)
            return c

        lax.fori_loop(0, (total + _KB - 1) // _KB, kblock, 0)
        o_ref[...] = x1_ref[...] + gate * acc_ref[...]


def _combine(offs, ye_flat, sp_n, x1, mods, cap, tiles_per_row):
    n = x1.shape[0]
    row = (lambda j: 1 + j // tiles_per_row) if tiles_per_row else (lambda j: 0)
    body = functools.partial(_combine_body, cap=cap)
    grid_spec = pltpu.PrefetchScalarGridSpec(
        num_scalar_prefetch=1,
        grid=(n // _TT,),
        in_specs=[pl.BlockSpec(memory_space=pl.ANY),
                  pl.BlockSpec((_TT, N_EXPERTS), lambda j, offs: (j, 0)),
                  pl.BlockSpec((_TT, D), lambda j, offs: (j, 0)),
                  pl.BlockSpec((1, 6, D), lambda j, offs: (row(j), 0, 0))],
        out_specs=pl.BlockSpec((_TT, D), lambda j, offs: (j, 0)),
        scratch_shapes=[pltpu.VMEM((2, _STAGE_ROWS, D), f32), pltpu.VMEM((_TT, D), f32),
                        pltpu.SemaphoreType.DMA((2,))],
    )
    return pl.pallas_call(
        body,
        grid_spec=grid_spec,
        out_shape=jax.ShapeDtypeStruct((n, D), f32),
        compiler_params=_cparams(("arbitrary",), 52),
        name="expert_combine",
    )(offs, ye_flat, sp_n, x1, mods)


def _moe(x1s, h2s, affs, mods, w_gate, w_up, w_down, tiles_per_row):
    routes = [_route(aff_t, CAPACITY_FACTOR * aff_t.shape[1] // N_EXPERTS) for aff_t in affs]
    (idx_a, g_a, _, _), (idx_b, g_b, _, _) = routes
    yes = _expert_ffn(idx_a, idx_b, h2s[0], h2s[1], g_a, g_b, w_gate, w_up, w_down)
    outs = []
    for ye, (idx, _, pos, sp), x1, tpr in zip(yes, routes, x1s, tiles_per_row):
        cap = idx.shape[1]
        offs = jnp.concatenate([pos[:, ::_TT], jnp.full((N_EXPERTS, 1), cap, i32)], axis=1)
        outs.append(_combine(offs, ye.reshape(N_EXPERTS * cap, D), sp.T, x1, mods, cap, tpr))
    return outs


def kernel(x_prompt, x_sample, cache_k, cache_v, state_fwd, state_bwd, c, c_ctx, w_mod, b_mod,
           norm_mix, w_in, q_norm, k_norm, rpb, conv_w, conv_b, lru_w_a, lru_b_a, lru_w_x,
           lru_b_x, lru_lambda, w_out, norm_ffn, w_router, w_gate, w_up, w_down):
    B, S, _ = x_prompt.shape
    DB, T, _ = x_sample.shape
    assert x_prompt.shape[2] == D and w_mod.shape[0] == 1, "one layer of width D"
    assert B % SUBLANES == 0 and SUBLANES % DB == 0 and DB < SUBLANES
    assert T == _NA_ROWS * GRID_W == _TM_IN and (B * S) % _TM_IN == 0 and S % _GATE_ROWS == 0
    l = 0

    cvecs = jnp.concatenate([c_ctx[None, :], c, jnp.zeros((SUBLANES - 1 - DB, D), f32)], axis=0)
    mods = _adaln_mods(cvecs, w_mod[l], b_mod[l][None, :]).reshape(SUBLANES, 6, D)

    heads_per_tile = _TN_IN // HEAD_DIM
    qn_t = jnp.tile(q_norm[l], heads_per_tile)[None, :]
    kn_t = jnp.tile(k_norm[l], heads_per_tile)[None, :]
    gmat = (jnp.kron(jnp.eye(heads_per_tile, dtype=f32), jnp.ones((HEAD_DIM, HEAD_DIM), f32))
            / HEAD_DIM).astype(bf16)
    nmix = norm_mix[l][None, :]
    nffn = norm_ffn[l][None, :]
    w_in_l = w_in[l]
    w_out_b = w_out[l].astype(bf16)
    w_router_t = w_router[l].T.astype(bf16)
    wcat = _lru_gate_weights(lru_w_a[l], lru_w_x[l])
    cb = conv_b[l][None, :]

    xp = x_prompt.reshape(B * S, D)
    xs = x_sample.reshape(DB * T, D)

    proj_c = _inproj(xp, mods, nmix, w_in_l, qn_t, kn_t, gmat, per_tile_rows=False)
    att_c, new_k, new_v = _ctx_attention(proj_c, B, S)
    zeros_c = jnp.zeros((B, LRU_W), f32)
    lru_c, htf, htb = _rglru(proj_c, wcat, conv_w[l], cb, lru_b_a[l], lru_b_x[l], lru_lambda[l],
                             zeros_c, zeros_c, T=S, nb=SUBLANES)
    x1_c, h2_c, aff_c = _outproj(att_c, lru_c, w_out_b, xp, mods, nffn, w_router_t, 0)

    nct = LRU_W // _LRU_TC

    def slabs(h0):
        return (h0.reshape(DB, nct, _LRU_HALVES, LANES).transpose(2, 0, 1, 3)
                .reshape(_LRU_HALVES * DB, nct * LANES))

    proj_l = _inproj(xs, mods, nmix, w_in_l, qn_t, kn_t, gmat, per_tile_rows=True)
    att_l = _lat_attention(proj_l, cache_k, cache_v, _pad_rpb(rpb[l]), l, DB, T)
    h0f_l = slabs(state_fwd[:, l])
    h0b_l = slabs(state_bwd[:, l])
    lru_l, _, _ = _rglru(proj_l, wcat, conv_w[l], cb, lru_b_a[l], lru_b_x[l], lru_lambda[l],
                         h0f_l, h0b_l, T=T, nb=DB)
    x1_l, h2_l, aff_l = _outproj(att_l, lru_l, w_out_b, xs, mods, nffn, w_router_t, T // _TM_OUT)

    y_c, y_l = _moe((x1_c, x1_l), (h2_c, h2_l), (aff_c, aff_l), mods,
                    w_gate[l], w_up[l], w_down[l], (0, T // _TT))

    return (y_c.reshape(B, S, D), y_l.reshape(DB, T, D), new_k, new_v,
            htf[:, None, :], htb[:, None, :])
```
